```python
import math
import jax, jax.numpy as jnp
from jax import lax
import numpy as np

D_MODEL = 1024
BATCH = 4
SEQ = 8192
DEPTH = 1

GLA_HEADS = 4
GLA_DK = 64
GLA_DV = 128
GLA_LOWRANK = 16
GLA_TAU = 16.0
GLA_CHUNK = 64
MOBA_HEADS = 8
MOBA_DH = 64
MOBA_BLOCK = 256
MOBA_TOPK = 3
MOBA_QCHUNK = 128
REL_BUCKETS = 32
REL_MAX_DIST = 128
MEM_LEN = 256
MEM_HEADS = 4
MEM_DH = 128
N_GROUPS = 4
EXPERTS_PER_GROUP = 8
N_EXPERTS = N_GROUPS * EXPERTS_PER_GROUP
TOPK_IN_GROUP = 2
D_EXPERT = 512
MOE_BLOCK = 128
EPS = 1e-6

GLA_QK = GLA_HEADS * GLA_DK
GLA_V = GLA_HEADS * GLA_DV
MOBA_W = MOBA_HEADS * MOBA_DH
MEM_W = MEM_HEADS * MEM_DH
IN_SPLITS = (GLA_QK, GLA_QK, GLA_V, GLA_V, GLA_LOWRANK, MOBA_W, MOBA_W, MOBA_W, D_MODEL, D_MODEL)
D_IN = sum(IN_SPLITS)
SPLIT_POINTS = tuple(sum(IN_SPLITS[:i + 1]) for i in range(len(IN_SPLITS) - 1))

kernel_name = 'hybrid_gla_moba_hmoe_block'


def rmsnorm(x, g):
    xf = x.astype(jnp.float32)
    y = xf * lax.rsqrt(jnp.mean(xf * xf, axis=-1, keepdims=True) + EPS)
    return (y * g.astype(jnp.float32)).astype(x.dtype)


def gla_chunked(q, k, v, log_a):
    B, S = q.shape[0], q.shape[1]
    C = GLA_CHUNK
    N = S // C

    def chunks(t):
        return t.astype(jnp.float32).reshape(B, N, C, GLA_HEADS, t.shape[-1]).transpose(0, 3, 1, 2, 4)

    q, k, v, log_a = chunks(q) * (GLA_DK ** -0.5), chunks(k), chunks(v), chunks(log_a)
    b = jnp.cumsum(log_a, axis=3)
    b_last = b[:, :, :, C - 1:]
    b_mid = b[:, :, :, C // 2 - 1:C // 2]
    att = jnp.einsum('bhncd,bhnsd->bhncs', q * jnp.exp(b - b_mid), k * jnp.exp(b_mid - b))
    causal = jnp.tril(jnp.ones((C, C), dtype=bool))
    att = jnp.where(causal, att, 0.0)
    o_intra = jnp.einsum('bhncs,bhnsv->bhncv', att, v)
    kv = jnp.einsum('bhncd,bhncv->bhndv', k * jnp.exp(b_last - b), v)
    decay = jnp.exp(b_last[:, :, :, 0])

    def step(state, inp):
        dec, kv_c = inp
        return dec[..., None] * state + kv_c, state

    init = jnp.zeros((B, GLA_HEADS, GLA_DK, GLA_DV), jnp.float32)
    _, states = lax.scan(step, init, (jnp.moveaxis(decay, 2, 0), jnp.moveaxis(kv, 2, 0)))
    states = jnp.moveaxis(states, 0, 2)
    o_inter = jnp.einsum('bhncd,bhndv->bhncv', q * jnp.exp(b), states)
    o = o_intra + o_inter
    return o.transpose(0, 2, 3, 1, 4).reshape(B, S, GLA_HEADS, GLA_DV)


def t5_bucket(dist):
    n = jnp.maximum(dist, 0)
    max_exact = REL_BUCKETS // 2
    nf = jnp.maximum(n, 1).astype(jnp.float32)
    large = max_exact + (jnp.log(nf / max_exact) / math.log(REL_MAX_DIST / max_exact)
                         * (REL_BUCKETS - max_exact)).astype(jnp.int32)
    large = jnp.minimum(large, REL_BUCKETS - 1)
    return jnp.where(n < max_exact, n, large)


def moba_attention(q, k, v, rel_bias):
    B, S = q.shape[0], q.shape[1]
    H, BS, QC = MOBA_HEADS, MOBA_BLOCK, MOBA_QCHUNK
    Sp = -(-S // BS) * BS
    NB = Sp // BS
    NC = Sp // QC
    K = min(MOBA_TOPK, NB)

    def heads(t):
        t = t.reshape(B, S, H, MOBA_DH).transpose(0, 2, 1, 3)
        return jnp.pad(t, ((0, 0), (0, 0), (0, Sp - S), (0, 0)))

    q = heads(q) * (MOBA_DH ** -0.5)
    kb = heads(k).reshape(B, H, NB, BS, MOBA_DH)
    vb = heads(v).reshape(B, H, NB, BS, MOBA_DH)
    k_mean = jnp.mean(kb.astype(jnp.float32), axis=3)
    gate = jnp.einsum('bhsd,bhnd->bhsn', q.astype(jnp.float32), k_mean)
    q_blk = jnp.arange(Sp) // BS
    fully_past = jnp.arange(NB)[None, :] < q_blk[:, None]
    gate = jnp.where(fully_past, gate, -jnp.inf)
    _, sel = lax.top_k(gate, K)
    sel_valid = sel < q_blk[:, None]

    def to_chunks(t):
        return jnp.moveaxis(t.reshape(B, H, NC, QC, t.shape[-1]), 2, 0)

    bias_hb = rel_bias.T.astype(jnp.float32)
    b_idx = jnp.arange(B)[:, None, None, None]
    h_idx = jnp.arange(H)[None, :, None, None]
    offs = jnp.arange(BS)

    def one_chunk(args):
        c, qc, selc, validc = args
        q_pos = c * QC + jnp.arange(QC)
        own = (c * QC) // BS
        k_own = lax.dynamic_index_in_dim(kb, own, axis=2, keepdims=False)
        v_own = lax.dynamic_index_in_dim(vb, own, axis=2, keepdims=False)
        d_own = q_pos[:, None] - (own * BS + offs)[None, :]
        s_own = jnp.einsum('bhqd,bhkd->bhqk', qc, k_own).astype(jnp.float32) + bias_hb[:, t5_bucket(d_own)]
        s_own = jnp.where(d_own >= 0, s_own, -jnp.inf)
        k_sel = kb[b_idx, h_idx, selc]
        v_sel = vb[b_idx, h_idx, selc]
        d_sel = q_pos[:, None, None] - (selc[..., None] * BS + offs)
        s_sel = (jnp.einsum('bhqd,bhqjkd->bhqjk', qc, k_sel).astype(jnp.float32)
                 + bias_hb[h_idx[..., None], t5_bucket(d_sel)])
        s_sel = jnp.where(validc[..., None], s_sel, -jnp.inf)
        logits = jnp.concatenate([s_own, s_sel.reshape(B, H, QC, K * BS)], axis=-1)
        p = jax.nn.softmax(logits, axis=-1)
        p_own = p[..., :BS].astype(v_own.dtype)
        p_sel = p[..., BS:].reshape(B, H, QC, K, BS).astype(v_sel.dtype)
        return (jnp.einsum('bhqk,bhkd->bhqd', p_own, v_own)
                + jnp.einsum('bhqjk,bhqjkd->bhqd', p_sel, v_sel))

    out = lax.map(one_chunk, (jnp.arange(NC), to_chunks(q), to_chunks(sel), to_chunks(sel_valid)))
    out = out.transpose(1, 0, 3, 2, 4).reshape(B, Sp, H * MOBA_DH)
    return out[:, :S]


def mixer(h, rel_bias, w_in, w_alpha_up, b_alpha, g_gla_head, w_proj_gla, w_proj_moba, w_out):
    B, S, _ = h.shape
    proj = h @ w_in
    q_g, k_g, v_g, r_g, a_lr, q_m, k_m, v_m, z_g, z_m = jnp.split(proj, SPLIT_POINTS, axis=-1)
    log_a = jax.nn.log_sigmoid((a_lr @ w_alpha_up + b_alpha).astype(jnp.float32)) / GLA_TAU
    o_g = gla_chunked(q_g.reshape(B, S, GLA_HEADS, GLA_DK), k_g.reshape(B, S, GLA_HEADS, GLA_DK),
                      v_g.reshape(B, S, GLA_HEADS, GLA_DV), log_a.reshape(B, S, GLA_HEADS, GLA_DK))
    o_g = rmsnorm(o_g, g_gla_head).reshape(B, S, GLA_V).astype(h.dtype) * jax.nn.silu(r_g)
    o_m = moba_attention(q_m, k_m, v_m, rel_bias)
    merged = jax.nn.sigmoid(z_g) * (o_g @ w_proj_gla) + jax.nn.sigmoid(z_m) * (o_m @ w_proj_moba)
    return merged @ w_out


def memory_attention(h, mem_n, w_cq, w_ckv, w_co):
    B, S, _ = h.shape
    M = mem_n.shape[1]
    q = (h @ w_cq).reshape(B, S, MEM_HEADS, MEM_DH)
    k, v = jnp.split((mem_n @ w_ckv).reshape(B, M, 2, MEM_HEADS, MEM_DH), 2, axis=2)
    k, v = k[:, :, 0], v[:, :, 0]
    s = jnp.einsum('bqhd,bkhd->bhqk', q, k).astype(jnp.float32) * (MEM_DH ** -0.5)
    p = jax.nn.softmax(s, axis=-1).astype(v.dtype)
    o = jnp.einsum('bhqk,bkhd->bqhd', p, v).reshape(B, S, MEM_W)
    return o @ w_co


def hier_moe(h, w_rg, b_rg, w_re, b_re, w_gate, w_up, w_down):
    B, S, D = h.shape
    T = B * S
    xt = h.reshape(T, D)
    g_prob = jax.nn.softmax((xt @ w_rg).astype(jnp.float32) + b_rg, axis=-1)
    p_grp, grp = lax.top_k(g_prob, 1)
    e_logits = ((xt @ w_re).astype(jnp.float32) + b_re).reshape(T, N_GROUPS, EXPERTS_PER_GROUP)
    e_in = jnp.take_along_axis(e_logits, grp[:, :, None], axis=1)[:, 0]
    top_p, top_i = lax.top_k(jax.nn.softmax(e_in, axis=-1), TOPK_IN_GROUP)
    weights = p_grp * top_p / jnp.sum(top_p, axis=-1, keepdims=True)
    expert = grp * EXPERTS_PER_GROUP + top_i
    A = T * TOPK_IN_GROUP
    flat_e = expert.reshape(A)
    flat_t = jnp.repeat(jnp.arange(T, dtype=jnp.int32), TOPK_IN_GROUP)
    flat_w = weights.reshape(A)
    order = jnp.argsort(flat_e)
    se = flat_e[order]
    counts = jnp.zeros((N_EXPERTS,), jnp.int32).at[flat_e].add(1)
    padded = (counts + MOE_BLOCK - 1) // MOE_BLOCK * MOE_BLOCK
    start = jnp.cumsum(counts) - counts
    pend = jnp.cumsum(padded)
    pstart = pend - padded
    dest = pstart[se] + (jnp.arange(A) - start[se])
    cap = (A + N_EXPERTS * (MOE_BLOCK - 1) + MOE_BLOCK - 1) // MOE_BLOCK * MOE_BLOCK
    n_blk = cap // MOE_BLOCK
    buf_t = jnp.zeros((cap,), jnp.int32).at[dest].set(flat_t[order])
    buf_w = jnp.zeros((cap,), jnp.float32).at[dest].set(flat_w[order])
    blk_e = jnp.minimum(jnp.searchsorted(pend, jnp.arange(n_blk) * MOE_BLOCK, side='right'), N_EXPERTS - 1)

    def run_block(args):
        e, tok, w = args
        xb = xt[tok]
        hid = jax.nn.silu(xb @ w_gate[e]) * (xb @ w_up[e])
        return (hid @ w_down[e]) * w[:, None].astype(xb.dtype)

    yb = lax.map(run_block, (blk_e, buf_t.reshape(n_blk, MOE_BLOCK), buf_w.reshape(n_blk, MOE_BLOCK)))
    out = jnp.zeros((T, D), h.dtype).at[buf_t].add(yb.reshape(cap, D).astype(h.dtype))
    return out.reshape(B, S, D)


def setup_inputs(seed: int = 0) -> dict:
    key = jax.random.key(seed)
    ks = jax.random.split(key, 26)
    f32 = jnp.float32
    L = DEPTH

    def nrm(k, shape, scale):
        return jax.random.normal(k, shape, f32) * scale

    def gain(k, shape):
        return 1.0 + 0.02 * jax.random.normal(k, shape, f32)

    return {
        'x': nrm(ks[0], (BATCH, SEQ, D_MODEL), 1.0),
        'mem': nrm(ks[1], (BATCH, MEM_LEN, D_MODEL), 1.0),
        'g_mem': gain(ks[2], (D_MODEL,)),
        'rel_bias': nrm(ks[3], (REL_BUCKETS, MOBA_HEADS), 0.5),
        'g_mix': gain(ks[4], (L, D_MODEL)),
        'w_in': nrm(ks[5], (L, D_MODEL, D_IN), D_MODEL ** -0.5),
        'w_alpha_up': nrm(ks[6], (L, GLA_LOWRANK, GLA_QK), GLA_LOWRANK ** -0.5),
        'b_alpha': nrm(ks[7], (L, GLA_QK), 0.1),
        'g_gla_head': gain(ks[8], (L, GLA_HEADS, GLA_DV)),
        'w_proj_gla': nrm(ks[9], (L, GLA_V, D_MODEL), GLA_V ** -0.5),
        'w_proj_moba': nrm(ks[10], (L, MOBA_W, D_MODEL), MOBA_W ** -0.5),
        'w_out': nrm(ks[11], (L, D_MODEL, D_MODEL), D_MODEL ** -0.5),
        'g_cross': gain(ks[12], (L, D_MODEL)),
        'w_cq': nrm(ks[13], (L, D_MODEL, MEM_W), D_MODEL ** -0.5),
        'w_ckv': nrm(ks[14], (L, D_MODEL, 2 * MEM_W), D_MODEL ** -0.5),
        'w_co': nrm(ks[15], (L, MEM_W, D_MODEL), MEM_W ** -0.5),
        'g_moe': gain(ks[16], (L, D_MODEL)),
        'w_router_group': nrm(ks[17], (L, D_MODEL, N_GROUPS), D_MODEL ** -0.5),
        'b_router_group': nrm(ks[18], (L, N_GROUPS), 0.01),
        'w_router_expert': nrm(ks[19], (L, D_MODEL, N_EXPERTS), D_MODEL ** -0.5),
        'b_router_expert': nrm(ks[20], (L, N_EXPERTS), 0.01),
        'w_exp_gate': nrm(ks[21], (L, N_EXPERTS, D_MODEL, D_EXPERT), D_MODEL ** -0.5),
        'w_exp_up': nrm(ks[22], (L, N_EXPERTS, D_MODEL, D_EXPERT), D_MODEL ** -0.5),
        'w_exp_down': nrm(ks[23], (L, N_EXPERTS, D_EXPERT, D_MODEL), D_EXPERT ** -0.5),
        'g_final': gain(ks[24], (D_MODEL,)),
    }


def reference(x, mem, g_mem, rel_bias, g_mix, w_in, w_alpha_up, b_alpha, g_gla_head, w_proj_gla,
              w_proj_moba, w_out, g_cross, w_cq, w_ckv, w_co, g_moe, w_router_group, b_router_group,
              w_router_expert, b_router_expert, w_exp_gate, w_exp_up, w_exp_down, g_final):
    mem_n = rmsnorm(mem, g_mem)
    for l in range(DEPTH):
        x = x + mixer(rmsnorm(x, g_mix[l]), rel_bias, w_in[l], w_alpha_up[l], b_alpha[l], g_gla_head[l],
                      w_proj_gla[l], w_proj_moba[l], w_out[l])
        x = x + memory_attention(rmsnorm(x, g_cross[l]), mem_n, w_cq[l], w_ckv[l], w_co[l])
        x = x + hier_moe(rmsnorm(x, g_moe[l]), w_router_group[l], b_router_group[l], w_router_expert[l],
                         b_router_expert[l], w_exp_gate[l], w_exp_up[l], w_exp_down[l])
    return rmsnorm(x, g_final)
```

```python
import functools
import math

import jax
import jax.numpy as jnp
from jax import lax
from jax.experimental import pallas as pl
from jax.experimental.pallas import tpu as pltpu

F32 = jnp.float32
BF16 = jnp.bfloat16
NEG_INF = float("-inf")

EPS = 1e-6
GLA_HEADS, GLA_DK, GLA_DV, GLA_LOWRANK, GLA_TAU, GLA_CHUNK = 4, 64, 128, 16, 16.0, 64
GLA_QK, GLA_V = GLA_HEADS * GLA_DK, GLA_HEADS * GLA_DV
MOBA_HEADS, MOBA_DH, MOBA_BLOCK, MOBA_TOPK = 8, 64, 256, 3
MOBA_W = MOBA_HEADS * MOBA_DH
REL_BUCKETS, REL_MAX_DIST = 32, 128
MEM_HEADS, MEM_DH = 4, 128
MEM_W = MEM_HEADS * MEM_DH
N_GROUPS, EXPERTS_PER_GROUP, TOPK_IN_GROUP = 4, 8, 2
N_EXPERTS = N_GROUPS * EXPERTS_PER_GROUP

LANES = 128
VMEM_LIMIT_BYTES = 56 * 1024 * 1024

PROJ_ROWS = 512
GLA_ROWS = 512
MOBA_GROUP = 4
MIX_ROWS = 512
EXPERT_ROWS = 128
MOVE_ROWS = 256


def _params(*semantics):
    return pltpu.CompilerParams(dimension_semantics=semantics, vmem_limit_bytes=VMEM_LIMIT_BYTES)


def _full(shape):
    return pl.BlockSpec(shape, lambda *_: (0,) * len(shape))


def _rms(x, g):
    return x * lax.rsqrt(jnp.mean(x * x, axis=-1, keepdims=True) + EPS) * g


def _proj_kernel(x_ref, g_ref, w_qk, w_v, w_r, w_a, w_up, b_a, w_qm, w_km, w_vm, w_zg, w_zm,
                 o_qk, o_v, o_r, o_la, o_qm, o_km, o_vm, o_zg, o_zm):
    h = _rms(x_ref[...], g_ref[...]).astype(BF16)

    def mm(w_ref):
        return jnp.dot(h, w_ref[...], preferred_element_type=F32)

    o_qk[...] = mm(w_qk)
    o_v[...] = mm(w_v).astype(BF16)
    o_r[...] = mm(w_r)
    a_lr = mm(w_a).astype(BF16)
    pre = jnp.dot(a_lr, w_up[...], preferred_element_type=F32) + b_a[...]
    o_la[...] = jax.nn.log_sigmoid(pre) * (1.0 / GLA_TAU)
    o_qm[...] = (mm(w_qm) * (MOBA_DH ** -0.5)).astype(BF16)
    o_km[...] = mm(w_km).astype(BF16)
    o_vm[...] = mm(w_vm).astype(BF16)
    o_zg[...] = mm(w_zg)
    o_zm[...] = mm(w_zm)


def _project(x2d, g_mix, w_in, w_alpha_up, b_alpha):
    T, D = x2d.shape
    rows = min(PROJ_ROWS, T)
    assert T % rows == 0
    splits = (GLA_QK, GLA_QK, GLA_V, GLA_V, GLA_LOWRANK, MOBA_W, MOBA_W, MOBA_W, D, D)
    offs = [0]
    for s in splits:
        offs.append(offs[-1] + s)
    wb = w_in.astype(BF16)
    sec = lambda i, j: wb[:, offs[i]:offs[j]]
    w_a = jnp.pad(sec(4, 5), ((0, 0), (0, LANES - GLA_LOWRANK)))
    w_up = jnp.pad(w_alpha_up.astype(BF16), ((0, LANES - GLA_LOWRANK), (0, 0)))
    weights = [sec(0, 2), sec(2, 3), sec(3, 4), w_a, w_up, b_alpha.reshape(1, GLA_QK).astype(F32),
               sec(5, 6), sec(6, 7), sec(7, 8), sec(8, 9), sec(9, 10)]
    out_defs = [(2 * GLA_QK, F32), (GLA_V, BF16), (GLA_V, F32), (GLA_QK, F32),
                (MOBA_W, BF16), (MOBA_W, BF16), (MOBA_W, BF16), (D, F32), (D, F32)]
    row_spec = lambda n: pl.BlockSpec((rows, n), lambda i: (i, 0))
    return pl.pallas_call(
        _proj_kernel,
        grid=(T // rows,),
        in_specs=[row_spec(D), _full((1, D))] + [_full(w.shape) for w in weights],
        out_specs=[row_spec(n) for n, _ in out_defs],
        out_shape=[jax.ShapeDtypeStruct((T, n), dt) for n, dt in out_defs],
        compiler_params=_params("parallel"),
        name="norm_in_proj",
    )(x2d, g_mix.reshape(1, D).astype(F32), *weights)


def _gla_kernel(qk_ref, la_ref, v_ref, r_ref, g_ref, o_ref, state_ref, obuf_ref):
    C, H, DK, DV = GLA_CHUNK, GLA_HEADS, GLA_DK, GLA_DV
    rows = qk_ref.shape[1]
    hi = lax.Precision.HIGHEST

    @pl.when(pl.program_id(1) == 0)
    def _():
        state_ref[...] = jnp.zeros_like(state_ref)

    tri = (lax.broadcasted_iota(jnp.int32, (C, C), 0) >= lax.broadcasted_iota(jnp.int32, (C, C), 1)).astype(F32)
    ones = jnp.ones((C, DV), F32)
    lane_head = lax.broadcasted_iota(jnp.int32, (1, H * DK), 1) // DK
    head_masks = [(lane_head == h).astype(F32) for h in range(H)]
    stack_row = lax.broadcasted_iota(jnp.int32, (H * C, C), 0) % C
    stack_col = lax.broadcasted_iota(jnp.int32, (H * C, C), 1)
    causal = stack_col <= stack_row
    scale = DK ** -0.5

    def stack(m):
        return jnp.concatenate([m * head_masks[h] for h in range(H)], axis=0).astype(BF16)

    def diag(m, rb, cb):
        return [m[h * rb:(h + 1) * rb, h * cb:(h + 1) * cb] for h in range(H)]

    for c in range(rows // C):
        sl = slice(c * C, (c + 1) * C)
        q = qk_ref[0, sl, 0:H * DK]
        k = qk_ref[0, sl, H * DK:2 * H * DK]
        la = la_ref[0, sl, :]
        v = v_ref[0, sl, :]
        b = jnp.dot(tri, la, precision=hi, preferred_element_type=F32)
        b_last = b[C - 1:C, :]
        b_mid = b[C // 2 - 1:C // 2, :]
        b_last_rows = lax.dot_general(la, ones, (((0,), (0,)), ((), ())), precision=hi,
                                      preferred_element_type=F32)
        qe = stack(q * scale * jnp.exp(b - b_mid))
        ke = (k * jnp.exp(b_mid - b)).astype(BF16)
        kd = (k * jnp.exp(b_last - b)).astype(BF16)
        qb = stack(q * scale * jnp.exp(b))
        att = lax.dot_general(qe, ke, (((1,), (1,)), ((), ())), preferred_element_type=F32)
        att = jnp.where(causal, att, 0.0).astype(BF16)
        o_intra = diag(jnp.dot(att, v, preferred_element_type=F32), C, DV)
        state = state_ref[...]
        o_inter = jnp.dot(qb, state.astype(BF16), preferred_element_type=F32)
        kv = lax.dot_general(kd, v, (((0,), (0,)), ((), ())), preferred_element_type=F32)
        state_ref[...] = jnp.exp(b_last_rows) * state + jnp.concatenate(diag(kv, DK, DV), axis=0)
        for h in range(H):
            obuf_ref[sl, h * DV:(h + 1) * DV] = o_intra[h] + o_inter[h * C:(h + 1) * C, :]

    r = r_ref[0]
    for h in range(H):
        cs = slice(h * DV, (h + 1) * DV)
        y = _rms(obuf_ref[:, cs], g_ref[:, cs])
        rh = r[:, cs]
        o_ref[0, :, cs] = (y * (rh * jax.nn.sigmoid(rh))).astype(BF16)


def _gla(qk, la, v, r, g_head, B, S):
    rows = min(GLA_ROWS, S)
    assert S % rows == 0 and rows % GLA_CHUNK == 0
    spec = lambda n: pl.BlockSpec((1, rows, n), lambda b, i: (b, i, 0))
    return pl.pallas_call(
        _gla_kernel,
        grid=(B, S // rows),
        in_specs=[spec(2 * GLA_QK), spec(GLA_QK), spec(GLA_V), spec(GLA_V), _full((1, GLA_V))],
        out_specs=spec(GLA_V),
        out_shape=jax.ShapeDtypeStruct((B, S, GLA_V), BF16),
        scratch_shapes=[pltpu.VMEM((GLA_HEADS * GLA_DK, GLA_DV), F32), pltpu.VMEM((rows, GLA_V), F32)],
        compiler_params=_params("parallel", "arbitrary"),
        name="gla_chunked",
    )(qk.reshape(B, S, -1), la.reshape(B, S, -1), v.reshape(B, S, -1), r.reshape(B, S, -1),
      g_head.reshape(1, GLA_V).astype(F32))


def _t5_bucket(dist):
    n = jnp.maximum(dist, 0)
    max_exact = REL_BUCKETS // 2
    nf = jnp.maximum(n, 1).astype(F32)
    large = max_exact + (jnp.log(nf / max_exact) / math.log(REL_MAX_DIST / max_exact)
                         * (REL_BUCKETS - max_exact)).astype(jnp.int32)
    large = jnp.minimum(large, REL_BUCKETS - 1)
    return jnp.where(n < max_exact, n, large)


def _moba_bias_kernel(rb_ref, o_ref):
    BS, G = MOBA_BLOCK, MOBA_GROUP
    grp, kind = pl.program_id(0), pl.program_id(1)
    d = (lax.broadcasted_iota(jnp.int32, (BS, BS), 0) - lax.broadcasted_iota(jnp.int32, (BS, BS), 1)
         + kind * BS)
    bucket = _t5_bucket(d)
    for h in range(G):
        val = jnp.zeros((BS, BS), F32)
        for bkt in range(REL_BUCKETS):
            val = jnp.where(bucket == bkt, rb_ref[bkt, grp * G + h], val)
        o_ref[0, 0, h * BS:(h + 1) * BS, :] = jnp.where(d >= 0, val, NEG_INF)


def _moba_bias(rel_bias):
    BS, G = MOBA_BLOCK, MOBA_GROUP
    n_grp = MOBA_HEADS // G
    return pl.pallas_call(
        _moba_bias_kernel,
        grid=(n_grp, 2),
        in_specs=[pl.BlockSpec(memory_space=pltpu.SMEM)],
        out_specs=pl.BlockSpec((1, 1, G * BS, BS), lambda g, k: (g, k, 0, 0)),
        out_shape=jax.ShapeDtypeStruct((n_grp, 2, G * BS, BS), F32),
        compiler_params=_params("parallel", "parallel"),
        name="moba_bias_tables",
    )(rel_bias.astype(F32))


def _moba_kernel(rb_ref, q_ref, k_ref, v_ref, bias_ref, o_ref,
                 qs_ref, kmean_ref, sel_ref, m_ref, l_ref, acc_ref):
    BS, G, DH = MOBA_BLOCK, MOBA_GROUP, MOBA_DH
    W = G * DH
    S = k_ref.shape[1]
    NB = S // BS
    grp, i = pl.program_id(1), pl.program_id(2)

    @pl.when(i == 0)
    def _():
        blk_of_key = lax.broadcasted_iota(jnp.int32, (LANES, S), 1) // BS
        ind = (blk_of_key == lax.broadcasted_iota(jnp.int32, (LANES, S), 0)).astype(BF16)
        kmean_ref[...] = jnp.dot(ind, k_ref[0], preferred_element_type=F32) * (1.0 / BS)

    q = q_ref[0]
    lane_head = lax.broadcasted_iota(jnp.int32, (1, W), 1) // DH
    for h in range(G):
        qs_ref[h * BS:(h + 1) * BS, :] = jnp.where(lane_head == h, q, jnp.zeros_like(q))
    qs = qs_ref[...]

    gate = lax.dot_general(qs, kmean_ref[...].astype(BF16), (((1,), (1,)), ((), ())),
                           preferred_element_type=F32)
    blk = lax.broadcasted_iota(jnp.int32, gate.shape, 1)
    gate = jnp.where(blk < i, gate, NEG_INF)
    for t in range(MOBA_TOPK):
        mx = jnp.max(gate, axis=-1, keepdims=True)
        hit = (gate == mx) & (mx > NEG_INF)
        idx = jnp.min(jnp.where(hit, blk, LANES), axis=-1, keepdims=True)
        sel_ref[t] = idx
        gate = jnp.where(blk == idx, NEG_INF, gate)

    def chosen(j):
        return (sel_ref[0] == j) | (sel_ref[1] == j) | (sel_ref[2] == j)

    def scores(j):
        kj = k_ref[0, pl.ds(pl.multiple_of(j * BS, BS), BS), :]
        return lax.dot_general(qs, kj, (((1,), (1,)), ((), ())), preferred_element_type=F32)

    def accumulate(j, s):
        m_old = m_ref[...]
        m_new = jnp.maximum(m_old, jnp.max(s, axis=-1, keepdims=True))
        alpha = jnp.exp(m_old - m_new)
        p = jnp.exp(s - m_new)
        l_ref[...] = alpha * l_ref[...] + jnp.sum(p, axis=-1, keepdims=True)
        vj = v_ref[0, pl.ds(pl.multiple_of(j * BS, BS), BS), :]
        acc_ref[...] = alpha * acc_ref[...] + jnp.dot(p.astype(BF16), vj, preferred_element_type=F32)
        m_ref[...] = m_new

    s = scores(i) + bias_ref[0, 0]
    m0 = jnp.max(s, axis=-1, keepdims=True)
    p = jnp.exp(s - m0)
    m_ref[...] = m0
    l_ref[...] = jnp.sum(p, axis=-1, keepdims=True)
    vi = v_ref[0, pl.ds(pl.multiple_of(i * BS, BS), BS), :]
    acc_ref[...] = jnp.dot(p.astype(BF16), vi, preferred_element_type=F32)

    @pl.when(i >= 1)
    def _():
        accumulate(i - 1, jnp.where(chosen(i - 1), scores(i - 1) + bias_ref[0, 1], NEG_INF))

    row_head = lax.broadcasted_iota(jnp.int32, (G * BS, 1), 0) // BS
    far_bias = jnp.zeros((G * BS, 1), F32)
    for h in range(G):
        far_bias = jnp.where(row_head == h, rb_ref[REL_BUCKETS - 1, grp * G + h], far_bias)

    def far(j, carry):
        accumulate(j, jnp.where(chosen(j), scores(j) + far_bias, NEG_INF))
        return carry

    lax.fori_loop(0, jnp.maximum(i - 1, 0), far, 0)

    out = jnp.zeros((BS, W), F32)
    for h in range(G):
        rs = slice(h * BS, (h + 1) * BS)
        out = jnp.where(lane_head == h, acc_ref[rs, :] / l_ref[rs, :], out)
    o_ref[0] = out.astype(o_ref.dtype)


def _moba(qm, km, vm, bias, rel_bias, B, S):
    BS, G = MOBA_BLOCK, MOBA_GROUP
    W = G * MOBA_DH
    n_grp = MOBA_HEADS // G
    assert S % BS == 0 and S // BS <= LANES
    blk = pl.BlockSpec((1, BS, W), lambda b, g, i: (b, i, g))
    seq = pl.BlockSpec((1, S, W), lambda b, g, i: (b, 0, g))
    return pl.pallas_call(
        _moba_kernel,
        grid=(B, n_grp, S // BS),
        in_specs=[pl.BlockSpec(memory_space=pltpu.SMEM), blk, seq, seq,
                  pl.BlockSpec((1, 2, G * BS, BS), lambda b, g, i: (g, 0, 0, 0))],
        out_specs=blk,
        out_shape=jax.ShapeDtypeStruct((B, S, MOBA_W), BF16),
        scratch_shapes=[pltpu.VMEM((G * BS, W), BF16), pltpu.VMEM((LANES, W), F32),
                        pltpu.VMEM((MOBA_TOPK, G * BS, 1), jnp.int32),
                        pltpu.VMEM((G * BS, 1), F32), pltpu.VMEM((G * BS, 1), F32),
                        pltpu.VMEM((G * BS, W), F32)],
        compiler_params=_params("parallel", "parallel", "arbitrary"),
        name="moba_attention",
    )(rel_bias.astype(F32), qm.reshape(B, S, -1), km.reshape(B, S, -1), vm.reshape(B, S, -1), bias)


def _mem_kv_kernel(mem_ref, g_ref, w_ref, k_ref, v_ref):
    kv = jnp.dot(_rms(mem_ref[0], g_ref[...]).astype(BF16), w_ref[...], preferred_element_type=F32)
    k_ref[0] = kv[:, :MEM_W].astype(BF16)
    v_ref[0] = kv[:, MEM_W:].astype(BF16)


def _mem_kv(mem, g_mem, w_ckv):
    B, M, D = mem.shape
    spec = pl.BlockSpec((1, M, MEM_W), lambda b: (b, 0, 0))
    return pl.pallas_call(
        _mem_kv_kernel,
        grid=(B,),
        in_specs=[pl.BlockSpec((1, M, D), lambda b: (b, 0, 0)), _full((1, D)), _full((D, 2 * MEM_W))],
        out_specs=[spec, spec],
        out_shape=[jax.ShapeDtypeStruct((B, M, MEM_W), BF16)] * 2,
        compiler_params=_params("parallel"),
        name="memory_kv",
    )(mem, g_mem.reshape(1, D).astype(F32), w_ckv.astype(BF16))


INFO_W0, INFO_W1, INFO_E0, INFO_E1, INFO_R0, INFO_R1 = range(6)
ROUTER_GROUP_LANE0, ROUTER_EXPERT_LANE0 = 0, N_GROUPS


def _mix_kernel(x_ref, og_ref, om_ref, zg_ref, zm_ref, mk_ref, mv_ref, gc_ref, gm_ref,
                wpg, wpm, wout, wcq, wco, wr, br,
                x2_ref, info_ref, cnt_ref, base_ref):
    first = (pl.program_id(0) == 0) & (pl.program_id(1) == 0)

    @pl.when(first)
    def _():
        base_ref[...] = jnp.zeros_like(base_ref)

    def mm(a, w_ref):
        return jnp.dot(a.astype(BF16), w_ref[...], preferred_element_type=F32)

    merged = (jax.nn.sigmoid(zg_ref[0]) * mm(og_ref[0], wpg) + jax.nn.sigmoid(zm_ref[0]) * mm(om_ref[0], wpm))
    x1 = x_ref[0] + mm(merged, wout)

    qc = mm(_rms(x1, gc_ref[...]), wcq).astype(BF16)
    heads = []
    for h in range(MEM_HEADS):
        cs = slice(h * MEM_DH, (h + 1) * MEM_DH)
        s = lax.dot_general(qc[:, cs], mk_ref[0, :, cs], (((1,), (1,)), ((), ())),
                            preferred_element_type=F32) * (MEM_DH ** -0.5)
        p = jnp.exp(s - jnp.max(s, axis=-1, keepdims=True))
        o = jnp.dot(p.astype(BF16), mv_ref[0, :, cs], preferred_element_type=F32)
        heads.append(o / jnp.sum(p, axis=-1, keepdims=True))
    x2 = x1 + mm(jnp.concatenate(heads, axis=-1), wco)
    x2_ref[0] = x2

    logits = mm(_rms(x2, gm_ref[...]), wr) + br[...]
    rows = logits.shape[0]
    lane = lax.broadcasted_iota(jnp.int32, (rows, LANES), 1)
    is_grp = lane < N_GROUPS
    gl = jnp.where(is_grp, logits, NEG_INF)
    ge = jnp.exp(gl - jnp.max(gl, axis=-1, keepdims=True))
    g_prob = ge / jnp.sum(ge, axis=-1, keepdims=True)
    p_grp = jnp.max(g_prob, axis=-1, keepdims=True)
    grp = jnp.min(jnp.where((g_prob == p_grp) & is_grp, lane, LANES), axis=-1, keepdims=True)
    e_id = lane - ROUTER_EXPERT_LANE0
    in_grp = (e_id >= grp * EXPERTS_PER_GROUP) & (e_id < (grp + 1) * EXPERTS_PER_GROUP)
    el = jnp.where(in_grp, logits, NEG_INF)
    ee = jnp.exp(el - jnp.max(el, axis=-1, keepdims=True))
    e_prob = jnp.where(in_grp, ee / jnp.sum(ee, axis=-1, keepdims=True), -1.0)
    p0 = jnp.max(e_prob, axis=-1, keepdims=True)
    e0 = jnp.min(jnp.where(e_prob == p0, e_id, LANES), axis=-1, keepdims=True)
    e_rest = jnp.where(e_id == e0, -1.0, e_prob)
    p1 = jnp.max(e_rest, axis=-1, keepdims=True)
    e1 = jnp.min(jnp.where(e_rest == p1, e_id, LANES), axis=-1, keepdims=True)
    w0 = p_grp * p0 / (p0 + p1)
    w1 = p_grp * p1 / (p0 + p1)

    onehot = ((lane == e0) | (lane == e1)).astype(F32)
    before = (lax.broadcasted_iota(jnp.int32, (rows, rows), 1)
              < lax.broadcasted_iota(jnp.int32, (rows, rows), 0)).astype(BF16)
    seen = base_ref[...] + jnp.dot(before, onehot.astype(BF16), preferred_element_type=F32)
    r0 = jnp.sum(jnp.where(lane == e0, seen, 0.0), axis=-1, keepdims=True)
    r1 = jnp.sum(jnp.where(lane == e1, seen, 0.0), axis=-1, keepdims=True)
    base_ref[...] = base_ref[...] + jnp.sum(onehot, axis=0, keepdims=True)
    cnt_ref[...] = base_ref[...]

    info = jnp.zeros((rows, LANES), F32)
    for ln, val in ((INFO_W0, w0), (INFO_W1, w1), (INFO_E0, e0.astype(F32)), (INFO_E1, e1.astype(F32)),
                    (INFO_R0, r0), (INFO_R1, r1)):
        info = jnp.where(lane == ln, val, info)
    info_ref[0] = info


def _mix(x, o_g, o_m, z_g, z_m, mem_k, mem_v, g_cross, g_moe, w_proj_gla, w_proj_moba, w_out, w_cq, w_co,
         w_rg, b_rg, w_re, b_re):
    B, S, D = x.shape
    M = mem_k.shape[1]
    rows = min(MIX_ROWS, S)
    assert S % rows == 0
    pad = LANES - N_GROUPS - N_EXPERTS
    wr = jnp.pad(jnp.concatenate([w_rg, w_re], axis=1), ((0, 0), (0, pad))).astype(BF16)
    br = jnp.pad(jnp.concatenate([b_rg, b_re]), (0, pad)).reshape(1, LANES).astype(F32)
    weights = [w_proj_gla.astype(BF16), w_proj_moba.astype(BF16), w_out.astype(BF16), w_cq.astype(BF16),
               w_co.astype(BF16), wr, br]
    tile = lambda n: pl.BlockSpec((1, rows, n), lambda b, i: (b, i, 0))
    memspec = pl.BlockSpec((1, M, MEM_W), lambda b, i: (b, 0, 0))
    return pl.pallas_call(
        _mix_kernel,
        grid=(B, S // rows),
        in_specs=[tile(D), tile(GLA_V), tile(MOBA_W), tile(D), tile(D), memspec, memspec,
                  _full((1, D)), _full((1, D))] + [_full(w.shape) for w in weights],
        out_specs=[tile(D), tile(LANES), _full((1, LANES))],
        out_shape=[jax.ShapeDtypeStruct((B, S, D), F32), jax.ShapeDtypeStruct((B, S, LANES), F32),
                   jax.ShapeDtypeStruct((1, LANES), F32)],
        scratch_shapes=[pltpu.VMEM((1, LANES), F32)],
        compiler_params=_params("arbitrary", "arbitrary"),
        name="merge_memattn_router",
    )(x, o_g, o_m, z_g.reshape(B, S, D), z_m.reshape(B, S, D), mem_k, mem_v,
      g_cross.reshape(1, D).astype(F32), g_moe.reshape(1, D).astype(F32), *weights)


def _plan_kernel(cnt_ref, info_ref, dest_ref, blk_ref):
    rows = info_ref.shape[0]
    lane1 = lax.broadcasted_iota(jnp.int32, (1, LANES), 1)
    nblk = jnp.floor((cnt_ref[...] + (EXPERT_ROWS - 1)) * (1.0 / EXPERT_ROWS))
    nblk = jnp.where(lane1 < N_EXPERTS, nblk, 0.0)
    hi = jnp.floor(nblk * (1.0 / 256.0))
    lo = nblk - 256.0 * hi
    upto = (lax.broadcasted_iota(jnp.int32, (LANES, LANES), 0)
            <= lax.broadcasted_iota(jnp.int32, (LANES, LANES), 1)).astype(BF16)
    digits = jnp.concatenate([jnp.broadcast_to(hi, (8, LANES)), jnp.broadcast_to(lo, (8, LANES))], axis=0)
    sums = jnp.dot(digits.astype(BF16), upto, preferred_element_type=F32)
    pend = sums[0:1] * 256.0 + sums[8:9]
    pstart_rows = (pend - nblk) * EXPERT_ROWS

    info = info_ref[...]
    lane = lax.broadcasted_iota(jnp.int32, (rows, LANES), 1)

    def field(ln):
        return jnp.sum(jnp.where(lane == ln, info, 0.0), axis=-1, keepdims=True)

    def dest(e, r):
        return jnp.sum(jnp.where(lane == e.astype(jnp.int32), pstart_rows, 0.0), axis=-1, keepdims=True) + r

    d0 = dest(field(INFO_E0), field(INFO_R0))
    d1 = dest(field(INFO_E1), field(INFO_R1))
    cols = jnp.where(lane == 0, d0, jnp.where(lane == 1, d1, 0.0))
    dest_ref[0] = cols.T[0:8, :].astype(jnp.int32)

    @pl.when(pl.program_id(0) == 0)
    def _():
        n = lax.broadcasted_iota(jnp.int32, (blk_ref.shape[0], LANES), 0).astype(F32)
        done = jnp.where((pend <= n) & (lane1 < N_EXPERTS), 1.0, 0.0)
        e = jnp.minimum(jnp.sum(done, axis=-1, keepdims=True), N_EXPERTS - 1.0)
        blk_ref[...] = e.astype(jnp.int32)


def _plan(counts, info2d, n_blk):
    T = info2d.shape[0]
    rows = min(MOVE_ROWS, T)
    assert T % rows == 0
    n_blk_pad = -(-n_blk // 8) * 8
    return pl.pallas_call(
        _plan_kernel,
        grid=(T // rows,),
        in_specs=[_full((1, LANES)), pl.BlockSpec((rows, LANES), lambda i: (i, 0))],
        out_specs=[pl.BlockSpec((1, 8, rows), lambda i: (i, 0, 0)), _full((n_blk_pad, 1))],
        out_shape=[jax.ShapeDtypeStruct((T // rows, 8, rows), jnp.int32),
                   jax.ShapeDtypeStruct((n_blk_pad, 1), jnp.int32)],
        compiler_params=_params("arbitrary"),
        name="dispatch_plan",
    )(counts, info2d)


def _dispatch_kernel(dest_ref, x_ref, g_ref, xs_in_ref, xs_ref, hbuf_ref, sem):
    del xs_in_ref
    rows = x_ref.shape[0]
    hbuf_ref[...] = _rms(x_ref[...], g_ref[...])

    def row_copy(r, slot):
        return pltpu.make_async_copy(hbuf_ref.at[pl.ds(r, 1)], xs_ref.at[pl.ds(dest_ref[0, slot, r], 1)], sem)

    def start(r, c):
        row_copy(r, 0).start()
        row_copy(r, 1).start()
        return c

    def wait(r, c):
        row_copy(r, 0).wait()
        row_copy(r, 1).wait()
        return c

    lax.fori_loop(0, rows, start, 0)
    lax.fori_loop(0, rows, wait, 0)


def _dispatch(dest, x2d, g_moe, cap):
    T, D = x2d.shape
    rows = dest.shape[2]
    xs0 = jnp.zeros((cap, D), F32)
    return pl.pallas_call(
        _dispatch_kernel,
        grid=(T // rows,),
        in_specs=[pl.BlockSpec((1, 8, rows), lambda i: (i, 0, 0), memory_space=pltpu.SMEM),
                  pl.BlockSpec((rows, D), lambda i: (i, 0)), _full((1, D)),
                  pl.BlockSpec(memory_space=pl.ANY)],
        out_specs=pl.BlockSpec(memory_space=pl.ANY),
        out_shape=jax.ShapeDtypeStruct((cap, D), F32),
        scratch_shapes=[pltpu.VMEM((rows, D), F32), pltpu.SemaphoreType.DMA(())],
        input_output_aliases={3: 0},
        compiler_params=_params("arbitrary"),
        name="moe_dispatch",
    )(dest, x2d, g_moe.reshape(1, D).astype(F32), xs0)


def _expert_kernel(blk_e_ref, xs_ref, wg_ref, wu_ref, wd_ref, y_ref):
    del blk_e_ref
    xb = xs_ref[...].astype(BF16)
    gate = jnp.dot(xb, wg_ref[0], preferred_element_type=F32)
    up = jnp.dot(xb, wu_ref[0], preferred_element_type=F32)
    hid = (gate * jax.nn.sigmoid(gate) * up).astype(BF16)
    y_ref[...] = jnp.dot(hid, wd_ref[0], preferred_element_type=F32)


def _experts(blk_e, xs, w_gate, w_up, w_down):
    cap, D = xs.shape
    DE = w_gate.shape[-1]
    n_blk = cap // EXPERT_ROWS
    rows_spec = pl.BlockSpec((EXPERT_ROWS, D), lambda n, e: (n, 0))
    return pl.pallas_call(
        _expert_kernel,
        grid_spec=pltpu.PrefetchScalarGridSpec(
            num_scalar_prefetch=1,
            grid=(n_blk,),
            in_specs=[rows_spec,
                      pl.BlockSpec((1, D, DE), lambda n, e: (e[n], 0, 0)),
                      pl.BlockSpec((1, D, DE), lambda n, e: (e[n], 0, 0)),
                      pl.BlockSpec((1, DE, D), lambda n, e: (e[n], 0, 0))],
            out_specs=rows_spec),
        out_shape=jax.ShapeDtypeStruct((cap, D), F32),
        compiler_params=_params("arbitrary"),
        name="moe_experts",
    )(blk_e, xs, w_gate.astype(BF16), w_up.astype(BF16), w_down.astype(BF16))


def _combine_kernel(final_norm, dest_ref, x_ref, info_ref, g_ref, y_ref, o_ref, ybuf_ref, sem):
    rows = x_ref.shape[0]

    def row_copy(r, slot):
        return pltpu.make_async_copy(y_ref.at[pl.ds(dest_ref[0, slot, r], 1)], ybuf_ref.at[slot, pl.ds(r, 1)], sem)

    def start(r, c):
        row_copy(r, 0).start()
        row_copy(r, 1).start()
        return c

    def wait(r, c):
        row_copy(r, 0).wait()
        row_copy(r, 1).wait()
        return c

    lax.fori_loop(0, rows, start, 0)
    lax.fori_loop(0, rows, wait, 0)
    info = info_ref[...]
    w0 = info[:, INFO_W0:INFO_W0 + 1]
    w1 = info[:, INFO_W1:INFO_W1 + 1]
    out = x_ref[...] + (w0 * ybuf_ref[0] + w1 * ybuf_ref[1])
    o_ref[...] = _rms(out, g_ref[...]) if final_norm else out


def _combine(dest, x2d, info2d, g_final, y, final_norm):
    T, D = x2d.shape
    rows = dest.shape[2]
    return pl.pallas_call(
        functools.partial(_combine_kernel, final_norm),
        grid=(T // rows,),
        in_specs=[pl.BlockSpec((1, 8, rows), lambda i: (i, 0, 0), memory_space=pltpu.SMEM),
                  pl.BlockSpec((rows, D), lambda i: (i, 0)), pl.BlockSpec((rows, LANES), lambda i: (i, 0)),
                  _full((1, D)), pl.BlockSpec(memory_space=pl.ANY)],
        out_specs=pl.BlockSpec((rows, D), lambda i: (i, 0)),
        out_shape=jax.ShapeDtypeStruct((T, D), F32),
        scratch_shapes=[pltpu.VMEM((2, rows, D), F32), pltpu.SemaphoreType.DMA(())],
        compiler_params=_params("arbitrary"),
        name="moe_combine_final_norm",
    )(dest, x2d, info2d, g_final.reshape(1, D).astype(F32), y)


def kernel(x, mem, g_mem, rel_bias, g_mix, w_in, w_alpha_up, b_alpha, g_gla_head, w_proj_gla, w_proj_moba,
           w_out, g_cross, w_cq, w_ckv, w_co, g_moe, w_router_group, b_router_group, w_router_expert,
           b_router_expert, w_exp_gate, w_exp_up, w_exp_down, g_final):
    B, S, D = x.shape
    T = B * S
    depth = g_mix.shape[0]
    n_assign = T * TOPK_IN_GROUP
    n_blk = -(-(n_assign + N_EXPERTS * (EXPERT_ROWS - 1)) // EXPERT_ROWS)
    cap = n_blk * EXPERT_ROWS

    mem_bias = _moba_bias(rel_bias)
    for l in range(depth):
        qk, v_g, r_g, la, q_m, k_m, v_m, z_g, z_m = _project(x.reshape(T, D), g_mix[l], w_in[l], w_alpha_up[l],
                                                             b_alpha[l])
        o_g = _gla(qk, la, v_g, r_g, g_gla_head[l], B, S)
        o_m = _moba(q_m, k_m, v_m, mem_bias, rel_bias, B, S)
        mem_k, mem_v = _mem_kv(mem, g_mem, w_ckv[l])
        x2, info, counts = _mix(x, o_g, o_m, z_g, z_m, mem_k, mem_v, g_cross[l], g_moe[l], w_proj_gla[l],
                                w_proj_moba[l], w_out[l], w_cq[l], w_co[l], w_router_group[l], b_router_group[l],
                                w_router_expert[l], b_router_expert[l])
        x2d, info2d = x2.reshape(T, D), info.reshape(T, LANES)
        dest, blk_e = _plan(counts, info2d, n_blk)
        xs = _dispatch(dest, x2d, g_moe[l], cap)
        y = _experts(blk_e[:n_blk, 0], xs, w_exp_gate[l], w_exp_up[l], w_exp_down[l])
        x = _combine(dest, x2d, info2d, g_final, y, final_norm=(l == depth - 1)).reshape(B, S, D)
    return x
```

```python
import functools
import math

import jax
import jax.numpy as jnp
from jax import lax
from jax.experimental import pallas as pl
from jax.experimental.pallas import tpu as pltpu

F32 = jnp.float32
BF16 = jnp.bfloat16
NEG_INF = float("-inf")

EPS = 1e-6
GLA_HEADS, GLA_DK, GLA_DV, GLA_LOWRANK, GLA_TAU, GLA_CHUNK = 4, 64, 128, 16, 16.0, 64
GLA_QK, GLA_V = GLA_HEADS * GLA_DK, GLA_HEADS * GLA_DV
MOBA_HEADS, MOBA_DH, MOBA_BLOCK, MOBA_TOPK = 8, 64, 256, 3
MOBA_W = MOBA_HEADS * MOBA_DH
LOG2E = math.log2(math.e)
MOBA_Q_SCALE = MOBA_DH ** -0.5 * LOG2E
REL_BUCKETS, REL_MAX_DIST = 32, 128
MEM_HEADS, MEM_DH = 4, 128
MEM_W = MEM_HEADS * MEM_DH
N_GROUPS, EXPERTS_PER_GROUP, TOPK_IN_GROUP = 4, 8, 2
N_EXPERTS = N_GROUPS * EXPERTS_PER_GROUP

LANES = 128
VMEM_LIMIT_BYTES = 56 * 1024 * 1024

PROJ_ROWS = 512
GLA_ROWS = 512
MOBA_GROUP = 4
MIX_ROWS = 512
EXPERT_ROWS = 128
MOVE_ROWS = 256


def _params(*semantics):
    return pltpu.CompilerParams(dimension_semantics=semantics, vmem_limit_bytes=VMEM_LIMIT_BYTES)


def _full(shape):
    return pl.BlockSpec(shape, lambda *_: (0,) * len(shape))


def _rms(x, g):
    return x * lax.rsqrt(jnp.mean(x * x, axis=-1, keepdims=True) + EPS) * g


def _proj_kernel(x_ref, g_ref, w_qk, w_v, w_r, w_a, w_up, b_a, w_qm, w_km, w_vm, w_zg, w_zm,
                 o_qk, o_v, o_r, o_la, o_qm, o_km, o_vm, o_zg, o_zm):
    h = _rms(x_ref[...], g_ref[...]).astype(BF16)

    def mm(w_ref):
        return jnp.dot(h, w_ref[...], preferred_element_type=F32)

    o_qk[...] = mm(w_qk)
    o_v[...] = mm(w_v).astype(BF16)
    o_r[...] = mm(w_r)
    a_lr = mm(w_a).astype(BF16)
    pre = jnp.dot(a_lr, w_up[...], preferred_element_type=F32) + b_a[...]
    o_la[...] = jax.nn.log_sigmoid(pre) * (1.0 / GLA_TAU)
    def mm_t(wt_ref):
        return lax.dot_general(wt_ref[...], h, (((1,), (1,)), ((), ())), preferred_element_type=F32)

    def store_blocks(o_ref, val_t):
        for c in range(o_ref.shape[0]):
            o_ref[c] = val_t[:, c * MOBA_BLOCK:(c + 1) * MOBA_BLOCK]

    store_blocks(o_qm, (mm_t(w_qm) * MOBA_Q_SCALE).astype(BF16))
    o_km[...] = mm(w_km).astype(BF16)
    store_blocks(o_vm, mm_t(w_vm).astype(BF16))
    o_zg[...] = mm(w_zg)
    o_zm[...] = mm(w_zm)


def _project(x2d, g_mix, w_in, w_alpha_up, b_alpha):
    T, D = x2d.shape
    rows = min(PROJ_ROWS, T)
    assert T % rows == 0
    splits = (GLA_QK, GLA_QK, GLA_V, GLA_V, GLA_LOWRANK, MOBA_W, MOBA_W, MOBA_W, D, D)
    offs = [0]
    for s in splits:
        offs.append(offs[-1] + s)
    wb = w_in.astype(BF16)
    sec = lambda i, j: wb[:, offs[i]:offs[j]]
    w_a = jnp.pad(sec(4, 5), ((0, 0), (0, LANES - GLA_LOWRANK)))
    w_up = jnp.pad(w_alpha_up.astype(BF16), ((0, LANES - GLA_LOWRANK), (0, 0)))
    weights = [sec(0, 2), sec(2, 3), sec(3, 4), w_a, w_up, b_alpha.reshape(1, GLA_QK).astype(F32),
               sec(5, 6).T, sec(6, 7), sec(7, 8).T, sec(8, 9), sec(9, 10)]
    out_defs = [(2 * GLA_QK, F32, False), (GLA_V, BF16, False), (GLA_V, F32, False), (GLA_QK, F32, False),
                (MOBA_W, BF16, True), (MOBA_W, BF16, False), (MOBA_W, BF16, True), (D, F32, False),
                (D, F32, False)]
    BS = MOBA_BLOCK
    assert rows % BS == 0
    row_spec = lambda n: pl.BlockSpec((rows, n), lambda i: (i, 0))
    blk_spec = lambda n: pl.BlockSpec((rows // BS, n, BS), lambda i: (i, 0, 0))
    return pl.pallas_call(
        _proj_kernel,
        grid=(T // rows,),
        in_specs=[row_spec(D), _full((1, D))] + [_full(w.shape) for w in weights],
        out_specs=[blk_spec(n) if t else row_spec(n) for n, _, t in out_defs],
        out_shape=[jax.ShapeDtypeStruct((T // BS, n, BS) if t else (T, n), dt) for n, dt, t in out_defs],
        compiler_params=_params("parallel"),
        name="norm_in_proj",
    )(x2d, g_mix.reshape(1, D).astype(F32), *weights)


def _gla_kernel(qk_ref, la_ref, v_ref, r_ref, g_ref, o_ref, state_ref, obuf_ref):
    C, H, DK, DV = GLA_CHUNK, GLA_HEADS, GLA_DK, GLA_DV
    rows = qk_ref.shape[1]
    hi = lax.Precision.HIGHEST

    @pl.when(pl.program_id(1) == 0)
    def _():
        state_ref[...] = jnp.zeros_like(state_ref)

    tri = (lax.broadcasted_iota(jnp.int32, (C, C), 0) >= lax.broadcasted_iota(jnp.int32, (C, C), 1)).astype(F32)
    ones = jnp.ones((C, DV), F32)
    lane_head = lax.broadcasted_iota(jnp.int32, (1, H * DK), 1) // DK
    head_masks = [(lane_head == h).astype(F32) for h in range(H)]
    stack_row = lax.broadcasted_iota(jnp.int32, (H * C, C), 0) % C
    stack_col = lax.broadcasted_iota(jnp.int32, (H * C, C), 1)
    causal = stack_col <= stack_row
    scale = DK ** -0.5

    def stack(m):
        return jnp.concatenate([m * head_masks[h] for h in range(H)], axis=0).astype(BF16)

    def diag(m, rb, cb):
        return [m[h * rb:(h + 1) * rb, h * cb:(h + 1) * cb] for h in range(H)]

    for c in range(rows // C):
        sl = slice(c * C, (c + 1) * C)
        q = qk_ref[0, sl, 0:H * DK]
        k = qk_ref[0, sl, H * DK:2 * H * DK]
        la = la_ref[0, sl, :]
        v = v_ref[0, sl, :]
        b = jnp.dot(tri, la, precision=hi, preferred_element_type=F32)
        b_last = b[C - 1:C, :]
        b_mid = b[C // 2 - 1:C // 2, :]
        b_last_rows = lax.dot_general(la, ones, (((0,), (0,)), ((), ())), precision=hi,
                                      preferred_element_type=F32)
        qe = stack(q * scale * jnp.exp(b - b_mid))
        ke = (k * jnp.exp(b_mid - b)).astype(BF16)
        kd = (k * jnp.exp(b_last - b)).astype(BF16)
        qb = stack(q * scale * jnp.exp(b))
        att = lax.dot_general(qe, ke, (((1,), (1,)), ((), ())), preferred_element_type=F32)
        att = jnp.where(causal, att, 0.0).astype(BF16)
        o_intra = diag(jnp.dot(att, v, preferred_element_type=F32), C, DV)
        state = state_ref[...]
        o_inter = jnp.dot(qb, state.astype(BF16), preferred_element_type=F32)
        kv = lax.dot_general(kd, v, (((0,), (0,)), ((), ())), preferred_element_type=F32)
        state_ref[...] = jnp.exp(b_last_rows) * state + jnp.concatenate(diag(kv, DK, DV), axis=0)
        for h in range(H):
            obuf_ref[sl, h * DV:(h + 1) * DV] = o_intra[h] + o_inter[h * C:(h + 1) * C, :]

    r = r_ref[0]
    for h in range(H):
        cs = slice(h * DV, (h + 1) * DV)
        y = _rms(obuf_ref[:, cs], g_ref[:, cs])
        rh = r[:, cs]
        o_ref[0, :, cs] = (y * (rh * jax.nn.sigmoid(rh))).astype(BF16)


def _gla(qk, la, v, r, g_head, B, S):
    rows = min(GLA_ROWS, S)
    assert S % rows == 0 and rows % GLA_CHUNK == 0
    spec = lambda n: pl.BlockSpec((1, rows, n), lambda b, i: (b, i, 0))
    return pl.pallas_call(
        _gla_kernel,
        grid=(B, S // rows),
        in_specs=[spec(2 * GLA_QK), spec(GLA_QK), spec(GLA_V), spec(GLA_V), _full((1, GLA_V))],
        out_specs=spec(GLA_V),
        out_shape=jax.ShapeDtypeStruct((B, S, GLA_V), BF16),
        scratch_shapes=[pltpu.VMEM((GLA_HEADS * GLA_DK, GLA_DV), F32), pltpu.VMEM((rows, GLA_V), F32)],
        compiler_params=_params("parallel", "arbitrary"),
        name="gla_chunked",
    )(qk.reshape(B, S, -1), la.reshape(B, S, -1), v.reshape(B, S, -1), r.reshape(B, S, -1),
      g_head.reshape(1, GLA_V).astype(F32))


def _t5_bucket(dist):
    n = jnp.maximum(dist, 0)
    max_exact = REL_BUCKETS // 2
    nf = jnp.maximum(n, 1).astype(F32)
    large = max_exact + (jnp.log(nf / max_exact) / math.log(REL_MAX_DIST / max_exact)
                         * (REL_BUCKETS - max_exact)).astype(jnp.int32)
    large = jnp.minimum(large, REL_BUCKETS - 1)
    return jnp.where(n < max_exact, n, large)


def _moba_bias_kernel(rb_ref, o_ref):
    BS, G = MOBA_BLOCK, MOBA_GROUP
    grp, kind = pl.program_id(0), pl.program_id(1)
    d = (lax.broadcasted_iota(jnp.int32, (BS, BS), 1) - lax.broadcasted_iota(jnp.int32, (BS, BS), 0)
         + kind * BS)
    bucket = _t5_bucket(d)
    for h in range(G):
        val = jnp.zeros((BS, BS), F32)
        for bkt in range(REL_BUCKETS):
            val = jnp.where(bucket == bkt, rb_ref[bkt, grp * G + h] * LOG2E, val)
        o_ref[0, 0, :, h * BS:(h + 1) * BS] = jnp.where(d >= 0, val, NEG_INF)


def _moba_bias(rel_bias):
    BS, G = MOBA_BLOCK, MOBA_GROUP
    n_grp = MOBA_HEADS // G
    return pl.pallas_call(
        _moba_bias_kernel,
        grid=(n_grp, 2),
        in_specs=[pl.BlockSpec(memory_space=pltpu.SMEM)],
        out_specs=pl.BlockSpec((1, 1, BS, G * BS), lambda g, k: (g, k, 0, 0)),
        out_shape=jax.ShapeDtypeStruct((n_grp, 2, BS, G * BS), F32),
        compiler_params=_params("parallel", "parallel"),
        name="moba_bias_tables",
    )(rel_bias.astype(F32))


def _moba_kernel(rb_ref, q_ref, k_ref, v_ref, bias_ref, o_ref,
                 qs_ref, kmean_ref, sel_ref, m_ref, l_ref, acc_ref, sbuf_ref):
    BS, G, DH = MOBA_BLOCK, MOBA_GROUP, MOBA_DH
    NBP = kmean_ref.shape[0]
    grp, i = pl.program_id(1), pl.program_id(2)

    @pl.when(i == 0)
    def _():
        S = k_ref.shape[1]
        blk_of_key = lax.broadcasted_iota(jnp.int32, (NBP, S), 1) // BS
        ind = (blk_of_key == lax.broadcasted_iota(jnp.int32, (NBP, S), 0)).astype(BF16)
        kmean_ref[...] = jnp.dot(ind, k_ref[0], preferred_element_type=F32) * (1.0 / BS)

    qt = q_ref[0]
    sub_head = lax.broadcasted_iota(jnp.int32, (G * DH, 1), 0) // DH
    for h in range(G):
        qs_ref[:, h * BS:(h + 1) * BS] = jnp.where(sub_head == h, qt, jnp.zeros_like(qt))

    gate = jnp.dot(kmean_ref[...].astype(BF16), qs_ref[...], preferred_element_type=F32)
    blk = lax.broadcasted_iota(jnp.int32, gate.shape, 0)
    gate = jnp.where(blk < i, gate, NEG_INF)
    for t in range(MOBA_TOPK):
        mx = jnp.max(gate, axis=0, keepdims=True)
        hit = (gate == mx) & (mx > NEG_INF)
        idx = jnp.min(jnp.where(hit, blk, NBP), axis=0, keepdims=True)
        sel_ref[t:t + 1, :] = idx
        gate = jnp.where(blk == idx, NEG_INF, gate)

    def mask_row(j):
        hit = (sel_ref[0:1, :] == j) | (sel_ref[1:2, :] == j) | (sel_ref[2:3, :] == j)
        return jnp.where(hit, 0.0, NEG_INF)

    def visit(j, addend, first=False):
        kj = k_ref[0, pl.ds(pl.multiple_of(j * BS, BS), BS), :]
        vjt = v_ref[j]
        s = jnp.dot(kj, qs_ref[...], preferred_element_type=F32) + addend
        mx = jnp.max(s, axis=0, keepdims=True)
        m_new = mx if first else jnp.maximum(m_ref[...], mx)
        p = jnp.exp2(s - m_new)
        pb = p.astype(BF16)
        pv = jnp.concatenate(
            [jnp.dot(vjt[h * DH:(h + 1) * DH, :], pb[:, h * BS:(h + 1) * BS], preferred_element_type=F32)
             for h in range(G)], axis=1)
        if first:
            l_ref[...] = jnp.sum(p, axis=0, keepdims=True)
            acc_ref[...] = pv
        else:
            alpha = jnp.exp2(m_ref[...] - m_new)
            l_ref[...] = alpha * l_ref[...] + jnp.sum(p, axis=0, keepdims=True)
            acc_ref[...] = alpha * acc_ref[...] + pv
        m_ref[...] = m_new

    visit(i, bias_ref[0, 0], first=True)

    @pl.when(i >= 1)
    def _():
        visit(i - 1, bias_ref[0, 1] + mask_row(i - 1))

    lane_head = lax.broadcasted_iota(jnp.int32, (1, G * BS), 1) // BS
    far_bias = jnp.zeros((1, G * BS), F32)
    for h in range(G):
        far_bias = jnp.where(lane_head == h, rb_ref[REL_BUCKETS - 1, grp * G + h] * LOG2E, far_bias)

    def far_scores(j0, n, slot):
        kj = k_ref[0, pl.ds(pl.multiple_of(j0 * BS, BS), n * BS), :]
        for h in range(G):
            cs = slice(h * BS, (h + 1) * BS)
            sbuf_ref[slot, 0:n * BS, cs] = jnp.dot(kj, qs_ref[:, cs], preferred_element_type=F32)

    def visit_far(j0, n, slot):
        vjt = jnp.concatenate([v_ref[j0 + t] for t in range(n)], axis=1)
        addend = [far_bias + mask_row(j0 + t) for t in range(n)]
        m_old, l_old, acc_old = m_ref[...], l_ref[...], acc_ref[...]
        ones = jnp.ones((8, n * BS), BF16)
        m_out, l_out, acc_out = [], [], []
        for h in range(G):
            cs = slice(h * BS, (h + 1) * BS)
            s = sbuf_ref[slot, 0:n * BS, cs]
            mx = jnp.max(s[0:BS], axis=0, keepdims=True) + addend[0][:, cs]
            for t in range(1, n):
                mx = jnp.maximum(mx, jnp.max(s[t * BS:(t + 1) * BS], axis=0, keepdims=True) + addend[t][:, cs])
            m_new = jnp.maximum(m_old[:, cs], mx)
            pb = jnp.concatenate([jnp.exp2(s[t * BS:(t + 1) * BS] - (m_new - addend[t][:, cs])).astype(BF16)
                                  for t in range(n)], axis=0)
            pv = jnp.dot(jnp.concatenate([vjt[h * DH:(h + 1) * DH, :], ones], axis=0), pb,
                         preferred_element_type=F32)
            alpha = jnp.exp2(m_old[:, cs] - m_new)
            m_out.append(m_new)
            l_out.append(alpha * l_old[:, cs] + pv[DH:DH + 1, :])
            acc_out.append(alpha * acc_old[:, cs] + pv[0:DH, :])
        m_ref[...] = jnp.concatenate(m_out, axis=1)
        l_ref[...] = jnp.concatenate(l_out, axis=1)
        acc_ref[...] = jnp.concatenate(acc_out, axis=1)

    n_far = jnp.maximum(i - 1, 0)
    n_pairs = lax.shift_right_logical(n_far, 1)

    @pl.when(n_pairs > 0)
    def _():
        far_scores(0, 2, 0)

    def far_quad(u, carry):
        far_scores(4 * u + 2, 2, 1)
        visit_far(4 * u, 2, 0)
        far_scores(2 * jnp.minimum(2 * u + 2, n_pairs - 1), 2, 0)
        visit_far(4 * u + 2, 2, 1)
        return carry

    lax.fori_loop(0, lax.shift_right_logical(n_pairs, 1), far_quad, 0)

    @pl.when(n_pairs % 2 == 1)
    def _():
        visit_far(2 * (n_pairs - 1), 2, 0)

    @pl.when(n_far % 2 == 1)
    def _():
        far_scores(n_far - 1, 1, 1)
        visit_far(n_far - 1, 1, 1)

    out_t = jnp.concatenate([acc_ref[:, h * BS:(h + 1) * BS] / l_ref[:, h * BS:(h + 1) * BS] for h in range(G)],
                            axis=0)
    o_ref[0] = out_t.T.astype(o_ref.dtype)


def _moba(qm_t, km, vm_t, bias, rel_bias, B, S):
    BS, G, DH = MOBA_BLOCK, MOBA_GROUP, MOBA_DH
    W = G * DH
    n_grp = MOBA_HEADS // G
    assert S % BS == 0
    NB = S // BS
    NBP = -(-NB // 8) * 8
    return pl.pallas_call(
        _moba_kernel,
        grid=(B, n_grp, NB),
        in_specs=[pl.BlockSpec(memory_space=pltpu.SMEM),
                  pl.BlockSpec((1, W, BS), lambda b, g, i: (b * NB + i, g, 0)),
                  pl.BlockSpec((1, S, W), lambda b, g, i: (b, 0, g)),
                  pl.BlockSpec((NB, W, BS), lambda b, g, i: (b, g, 0)),
                  pl.BlockSpec((1, 2, BS, G * BS), lambda b, g, i: (g, 0, 0, 0))],
        out_specs=pl.BlockSpec((1, BS, W), lambda b, g, i: (b, i, g)),
        out_shape=jax.ShapeDtypeStruct((B, S, MOBA_W), BF16),
        scratch_shapes=[pltpu.VMEM((W, G * BS), BF16), pltpu.VMEM((NBP, W), F32),
                        pltpu.VMEM((8, G * BS), jnp.int32),
                        pltpu.VMEM((1, G * BS), F32), pltpu.VMEM((1, G * BS), F32),
                        pltpu.VMEM((DH, G * BS), F32), pltpu.VMEM((2, 2 * BS, G * BS), F32)],
        compiler_params=_params("parallel", "parallel", "arbitrary"),
        name="moba_attention",
    )(rel_bias.astype(F32), qm_t, km.reshape(B, S, -1), vm_t, bias)


def _mem_kv_kernel(mem_ref, g_ref, w_ref, k_ref, v_ref):
    kv = jnp.dot(_rms(mem_ref[0], g_ref[...]).astype(BF16), w_ref[...], preferred_element_type=F32)
    k_ref[0] = kv[:, :MEM_W].astype(BF16)
    v_ref[0] = kv[:, MEM_W:].astype(BF16)


def _mem_kv(mem, g_mem, w_ckv):
    B, M, D = mem.shape
    spec = pl.BlockSpec((1, M, MEM_W), lambda b: (b, 0, 0))
    return pl.pallas_call(
        _mem_kv_kernel,
        grid=(B,),
        in_specs=[pl.BlockSpec((1, M, D), lambda b: (b, 0, 0)), _full((1, D)), _full((D, 2 * MEM_W))],
        out_specs=[spec, spec],
        out_shape=[jax.ShapeDtypeStruct((B, M, MEM_W), BF16)] * 2,
        compiler_params=_params("parallel"),
        name="memory_kv",
    )(mem, g_mem.reshape(1, D).astype(F32), w_ckv.astype(BF16))


INFO_W0, INFO_W1, INFO_E0, INFO_E1, INFO_R0, INFO_R1 = range(6)
ROUTER_GROUP_LANE0, ROUTER_EXPERT_LANE0 = 0, N_GROUPS


def _mix_kernel(x_ref, og_ref, om_ref, zg_ref, zm_ref, mk_ref, mv_ref, gc_ref, gm_ref,
                wpg, wpm, wout, wcq, wco, wr, br,
                x2_ref, info_ref, cnt_ref, base_ref):
    first = (pl.program_id(0) == 0) & (pl.program_id(1) == 0)

    @pl.when(first)
    def _():
        base_ref[...] = jnp.zeros_like(base_ref)

    def mm(a, w_ref):
        return jnp.dot(a.astype(BF16), w_ref[...], preferred_element_type=F32)

    merged = (jax.nn.sigmoid(zg_ref[0]) * mm(og_ref[0], wpg) + jax.nn.sigmoid(zm_ref[0]) * mm(om_ref[0], wpm))
    x1 = x_ref[0] + mm(merged, wout)

    qc = mm(_rms(x1, gc_ref[...]), wcq).astype(BF16)
    heads = []
    for h in range(MEM_HEADS):
        cs = slice(h * MEM_DH, (h + 1) * MEM_DH)
        s = lax.dot_general(qc[:, cs], mk_ref[0, :, cs], (((1,), (1,)), ((), ())),
                            preferred_element_type=F32) * (MEM_DH ** -0.5)
        p = jnp.exp(s - jnp.max(s, axis=-1, keepdims=True))
        o = jnp.dot(p.astype(BF16), mv_ref[0, :, cs], preferred_element_type=F32)
        heads.append(o / jnp.sum(p, axis=-1, keepdims=True))
    x2 = x1 + mm(jnp.concatenate(heads, axis=-1), wco)
    x2_ref[0] = x2

    logits = mm(_rms(x2, gm_ref[...]), wr) + br[...]
    rows = logits.shape[0]
    lane = lax.broadcasted_iota(jnp.int32, (rows, LANES), 1)
    is_grp = lane < N_GROUPS
    gl = jnp.where(is_grp, logits, NEG_INF)
    ge = jnp.exp(gl - jnp.max(gl, axis=-1, keepdims=True))
    g_prob = ge / jnp.sum(ge, axis=-1, keepdims=True)
    p_grp = jnp.max(g_prob, axis=-1, keepdims=True)
    grp = jnp.min(jnp.where((g_prob == p_grp) & is_grp, lane, LANES), axis=-1, keepdims=True)
    e_id = lane - ROUTER_EXPERT_LANE0
    in_grp = (e_id >= grp * EXPERTS_PER_GROUP) & (e_id < (grp + 1) * EXPERTS_PER_GROUP)
    el = jnp.where(in_grp, logits, NEG_INF)
    ee = jnp.exp(el - jnp.max(el, axis=-1, keepdims=True))
    e_prob = jnp.where(in_grp, ee / jnp.sum(ee, axis=-1, keepdims=True), -1.0)
    p0 = jnp.max(e_prob, axis=-1, keepdims=True)
    e0 = jnp.min(jnp.where(e_prob == p0, e_id, LANES), axis=-1, keepdims=True)
    e_rest = jnp.where(e_id == e0, -1.0, e_prob)
    p1 = jnp.max(e_rest, axis=-1, keepdims=True)
    e1 = jnp.min(jnp.where(e_rest == p1, e_id, LANES), axis=-1, keepdims=True)
    w0 = p_grp * p0 / (p0 + p1)
    w1 = p_grp * p1 / (p0 + p1)

    onehot = ((lane == e0) | (lane == e1)).astype(F32)
    before = (lax.broadcasted_iota(jnp.int32, (rows, rows), 1)
              < lax.broadcasted_iota(jnp.int32, (rows, rows), 0)).astype(BF16)
    seen = base_ref[...] + jnp.dot(before, onehot.astype(BF16), preferred_element_type=F32)
    r0 = jnp.sum(jnp.where(lane == e0, seen, 0.0), axis=-1, keepdims=True)
    r1 = jnp.sum(jnp.where(lane == e1, seen, 0.0), axis=-1, keepdims=True)
    base_ref[...] = base_ref[...] + jnp.sum(onehot, axis=0, keepdims=True)
    cnt_ref[...] = base_ref[...]

    info = jnp.zeros((rows, LANES), F32)
    for ln, val in ((INFO_W0, w0), (INFO_W1, w1), (INFO_E0, e0.astype(F32)), (INFO_E1, e1.astype(F32)),
                    (INFO_R0, r0), (INFO_R1, r1)):
        info = jnp.where(lane == ln, val, info)
    info_ref[0] = info


def _mix(x, o_g, o_m, z_g, z_m, mem_k, mem_v, g_cross, g_moe, w_proj_gla, w_proj_moba, w_out, w_cq, w_co,
         w_rg, b_rg, w_re, b_re):
    B, S, D = x.shape
    M = mem_k.shape[1]
    rows = min(MIX_ROWS, S)
    assert S % rows == 0
    pad = LANES - N_GROUPS - N_EXPERTS
    wr = jnp.pad(jnp.concatenate([w_rg, w_re], axis=1), ((0, 0), (0, pad))).astype(BF16)
    br = jnp.pad(jnp.concatenate([b_rg, b_re]), (0, pad)).reshape(1, LANES).astype(F32)
    weights = [w_proj_gla.astype(BF16), w_proj_moba.astype(BF16), w_out.astype(BF16), w_cq.astype(BF16),
               w_co.astype(BF16), wr, br]
    tile = lambda n: pl.BlockSpec((1, rows, n), lambda b, i: (b, i, 0))
    memspec = pl.BlockSpec((1, M, MEM_W), lambda b, i: (b, 0, 0))
    return pl.pallas_call(
        _mix_kernel,
        grid=(B, S // rows),
        in_specs=[tile(D), tile(GLA_V), tile(MOBA_W), tile(D), tile(D), memspec, memspec,
                  _full((1, D)), _full((1, D))] + [_full(w.shape) for w in weights],
        out_specs=[tile(D), tile(LANES), _full((1, LANES))],
        out_shape=[jax.ShapeDtypeStruct((B, S, D), F32), jax.ShapeDtypeStruct((B, S, LANES), F32),
                   jax.ShapeDtypeStruct((1, LANES), F32)],
        scratch_shapes=[pltpu.VMEM((1, LANES), F32)],
        compiler_params=_params("arbitrary", "arbitrary"),
        name="merge_memattn_router",
    )(x, o_g, o_m, z_g.reshape(B, S, D), z_m.reshape(B, S, D), mem_k, mem_v,
      g_cross.reshape(1, D).astype(F32), g_moe.reshape(1, D).astype(F32), *weights)


def _plan_kernel(cnt_ref, info_ref, dest_ref, blk_ref):
    rows = info_ref.shape[0]
    lane1 = lax.broadcasted_iota(jnp.int32, (1, LANES), 1)
    nblk = jnp.floor((cnt_ref[...] + (EXPERT_ROWS - 1)) * (1.0 / EXPERT_ROWS))
    nblk = jnp.where(lane1 < N_EXPERTS, nblk, 0.0)
    hi = jnp.floor(nblk * (1.0 / 256.0))
    lo = nblk - 256.0 * hi
    upto = (lax.broadcasted_iota(jnp.int32, (LANES, LANES), 0)
            <= lax.broadcasted_iota(jnp.int32, (LANES, LANES), 1)).astype(BF16)
    digits = jnp.concatenate([jnp.broadcast_to(hi, (8, LANES)), jnp.broadcast_to(lo, (8, LANES))], axis=0)
    sums = jnp.dot(digits.astype(BF16), upto, preferred_element_type=F32)
    pend = sums[0:1] * 256.0 + sums[8:9]
    pstart_rows = (pend - nblk) * EXPERT_ROWS

    info = info_ref[...]
    lane = lax.broadcasted_iota(jnp.int32, (rows, LANES), 1)

    def field(ln):
        return jnp.sum(jnp.where(lane == ln, info, 0.0), axis=-1, keepdims=True)

    def dest(e, r):
        return jnp.sum(jnp.where(lane == e.astype(jnp.int32), pstart_rows, 0.0), axis=-1, keepdims=True) + r

    d0 = dest(field(INFO_E0), field(INFO_R0))
    d1 = dest(field(INFO_E1), field(INFO_R1))
    cols = jnp.where(lane == 0, d0, jnp.where(lane == 1, d1, 0.0))
    dest_ref[0] = cols.T[0:8, :].astype(jnp.int32)

    @pl.when(pl.program_id(0) == 0)
    def _():
        n = lax.broadcasted_iota(jnp.int32, (blk_ref.shape[0], LANES), 0).astype(F32)
        done = jnp.where((pend <= n) & (lane1 < N_EXPERTS), 1.0, 0.0)
        e = jnp.minimum(jnp.sum(done, axis=-1, keepdims=True), N_EXPERTS - 1.0)
        blk_ref[...] = e.astype(jnp.int32)


def _plan(counts, info2d, n_blk):
    T = info2d.shape[0]
    rows = min(MOVE_ROWS, T)
    assert T % rows == 0
    n_blk_pad = -(-n_blk // 8) * 8
    return pl.pallas_call(
        _plan_kernel,
        grid=(T // rows,),
        in_specs=[_full((1, LANES)), pl.BlockSpec((rows, LANES), lambda i: (i, 0))],
        out_specs=[pl.BlockSpec((1, 8, rows), lambda i: (i, 0, 0)), _full((n_blk_pad, 1))],
        out_shape=[jax.ShapeDtypeStruct((T // rows, 8, rows), jnp.int32),
                   jax.ShapeDtypeStruct((n_blk_pad, 1), jnp.int32)],
        compiler_params=_params("arbitrary"),
        name="dispatch_plan",
    )(counts, info2d)


def _dispatch_kernel(dest_ref, x_ref, g_ref, xs_in_ref, xs_ref, hbuf_ref, sem):
    del xs_in_ref
    rows = x_ref.shape[0]
    hbuf_ref[...] = _rms(x_ref[...], g_ref[...])

    def row_copy(r, slot):
        return pltpu.make_async_copy(hbuf_ref.at[pl.ds(r, 1)], xs_ref.at[pl.ds(dest_ref[0, slot, r], 1)], sem)

    def start(r, c):
        row_copy(r, 0).start()
        row_copy(r, 1).start()
        return c

    def wait(r, c):
        row_copy(r, 0).wait()
        row_copy(r, 1).wait()
        return c

    lax.fori_loop(0, rows, start, 0)
    lax.fori_loop(0, rows, wait, 0)


def _dispatch(dest, x2d, g_moe, cap):
    T, D = x2d.shape
    rows = dest.shape[2]
    xs0 = jnp.zeros((cap, D), F32)
    return pl.pallas_call(
        _dispatch_kernel,
        grid=(T // rows,),
        in_specs=[pl.BlockSpec((1, 8, rows), lambda i: (i, 0, 0), memory_space=pltpu.SMEM),
                  pl.BlockSpec((rows, D), lambda i: (i, 0)), _full((1, D)),
                  pl.BlockSpec(memory_space=pl.ANY)],
        out_specs=pl.BlockSpec(memory_space=pl.ANY),
        out_shape=jax.ShapeDtypeStruct((cap, D), F32),
        scratch_shapes=[pltpu.VMEM((rows, D), F32), pltpu.SemaphoreType.DMA(())],
        input_output_aliases={3: 0},
        compiler_params=_params("arbitrary"),
        name="moe_dispatch",
    )(dest, x2d, g_moe.reshape(1, D).astype(F32), xs0)


def _expert_kernel(blk_e_ref, xs_ref, wg_ref, wu_ref, wd_ref, y_ref):
    del blk_e_ref
    xb = xs_ref[...].astype(BF16)
    gate = jnp.dot(xb, wg_ref[0], preferred_element_type=F32)
    up = jnp.dot(xb, wu_ref[0], preferred_element_type=F32)
    hid = (gate * jax.nn.sigmoid(gate) * up).astype(BF16)
    y_ref[...] = jnp.dot(hid, wd_ref[0], preferred_element_type=F32)


def _experts(blk_e, xs, w_gate, w_up, w_down):
    cap, D = xs.shape
    DE = w_gate.shape[-1]
    n_blk = cap // EXPERT_ROWS
    rows_spec = pl.BlockSpec((EXPERT_ROWS, D), lambda n, e: (n, 0))
    return pl.pallas_call(
        _expert_kernel,
        grid_spec=pltpu.PrefetchScalarGridSpec(
            num_scalar_prefetch=1,
            grid=(n_blk,),
            in_specs=[rows_spec,
                      pl.BlockSpec((1, D, DE), lambda n, e: (e[n], 0, 0)),
                      pl.BlockSpec((1, D, DE), lambda n, e: (e[n], 0, 0)),
                      pl.BlockSpec((1, DE, D), lambda n, e: (e[n], 0, 0))],
            out_specs=rows_spec),
        out_shape=jax.ShapeDtypeStruct((cap, D), F32),
        compiler_params=_params("arbitrary"),
        name="moe_experts",
    )(blk_e, xs, w_gate.astype(BF16), w_up.astype(BF16), w_down.astype(BF16))


def _combine_kernel(final_norm, dest_ref, x_ref, info_ref, g_ref, y_ref, o_ref, ybuf_ref, sem):
    rows = x_ref.shape[0]

    def row_copy(r, slot):
        return pltpu.make_async_copy(y_ref.at[pl.ds(dest_ref[0, slot, r], 1)], ybuf_ref.at[slot, pl.ds(r, 1)], sem)

    def start(r, c):
        row_copy(r, 0).start()
        row_copy(r, 1).start()
        return c

    def wait(r, c):
        row_copy(r, 0).wait()
        row_copy(r, 1).wait()
        return c

    lax.fori_loop(0, rows, start, 0)
    lax.fori_loop(0, rows, wait, 0)
    info = info_ref[...]
    w0 = info[:, INFO_W0:INFO_W0 + 1]
    w1 = info[:, INFO_W1:INFO_W1 + 1]
    out = x_ref[...] + (w0 * ybuf_ref[0] + w1 * ybuf_ref[1])
    o_ref[...] = _rms(out, g_ref[...]) if final_norm else out


def _combine(dest, x2d, info2d, g_final, y, final_norm):
    T, D = x2d.shape
    rows = dest.shape[2]
    return pl.pallas_call(
        functools.partial(_combine_kernel, final_norm),
        grid=(T // rows,),
        in_specs=[pl.BlockSpec((1, 8, rows), lambda i: (i, 0, 0), memory_space=pltpu.SMEM),
                  pl.BlockSpec((rows, D), lambda i: (i, 0)), pl.BlockSpec((rows, LANES), lambda i: (i, 0)),
                  _full((1, D)), pl.BlockSpec(memory_space=pl.ANY)],
        out_specs=pl.BlockSpec((rows, D), lambda i: (i, 0)),
        out_shape=jax.ShapeDtypeStruct((T, D), F32),
        scratch_shapes=[pltpu.VMEM((2, rows, D), F32), pltpu.SemaphoreType.DMA(())],
        compiler_params=_params("arbitrary"),
        name="moe_combine_final_norm",
    )(dest, x2d, info2d, g_final.reshape(1, D).astype(F32), y)


def kernel(x, mem, g_mem, rel_bias, g_mix, w_in, w_alpha_up, b_alpha, g_gla_head, w_proj_gla, w_proj_moba,
           w_out, g_cross, w_cq, w_ckv, w_co, g_moe, w_router_group, b_router_group, w_router_expert,
           b_router_expert, w_exp_gate, w_exp_up, w_exp_down, g_final):
    B, S, D = x.shape
    T = B * S
    depth = g_mix.shape[0]
    n_assign = T * TOPK_IN_GROUP
    n_blk = -(-(n_assign + N_EXPERTS * (EXPERT_ROWS - 1)) // EXPERT_ROWS)
    cap = n_blk * EXPERT_ROWS

    mem_bias = _moba_bias(rel_bias)
    for l in range(depth):
        qk, v_g, r_g, la, q_m, k_m, v_m, z_g, z_m = _project(x.reshape(T, D), g_mix[l], w_in[l], w_alpha_up[l],
                                                             b_alpha[l])
        o_g = _gla(qk, la, v_g, r_g, g_gla_head[l], B, S)
        o_m = _moba(q_m, k_m, v_m, mem_bias, rel_bias, B, S)
        mem_k, mem_v = _mem_kv(mem, g_mem, w_ckv[l])
        x2, info, counts = _mix(x, o_g, o_m, z_g, z_m, mem_k, mem_v, g_cross[l], g_moe[l], w_proj_gla[l],
                                w_proj_moba[l], w_out[l], w_cq[l], w_co[l], w_router_group[l], b_router_group[l],
                                w_router_expert[l], b_router_expert[l])
        x2d, info2d = x2.reshape(T, D), info.reshape(T, LANES)
        dest, blk_e = _plan(counts, info2d, n_blk)
        xs = _dispatch(dest, x2d, g_moe[l], cap)
        y = _experts(blk_e[:n_blk, 0], xs, w_exp_gate[l], w_exp_up[l], w_exp_down[l])
        x = _combine(dest, x2d, info2d, g_final, y, final_norm=(l == depth - 1)).reshape(B, S, D)
    return x
```

```python
import functools
import math

import jax
import jax.numpy as jnp
from jax import lax
from jax.experimental import pallas as pl
from jax.experimental.pallas import tpu as pltpu

F32 = jnp.float32
BF16 = jnp.bfloat16
NEG_INF = float("-inf")

EPS = 1e-6
GLA_HEADS, GLA_DK, GLA_DV, GLA_LOWRANK, GLA_TAU, GLA_CHUNK = 4, 64, 128, 16, 16.0, 64
GLA_QK, GLA_V = GLA_HEADS * GLA_DK, GLA_HEADS * GLA_DV
MOBA_HEADS, MOBA_DH, MOBA_BLOCK, MOBA_TOPK = 8, 64, 256, 3
MOBA_W = MOBA_HEADS * MOBA_DH
LOG2E = math.log2(math.e)
MOBA_Q_SCALE = MOBA_DH ** -0.5 * LOG2E
REL_BUCKETS, REL_MAX_DIST = 32, 128
MEM_HEADS, MEM_DH = 4, 128
MEM_W = MEM_HEADS * MEM_DH
N_GROUPS, EXPERTS_PER_GROUP, TOPK_IN_GROUP = 4, 8, 2
N_EXPERTS = N_GROUPS * EXPERTS_PER_GROUP

LANES = 128
VMEM_LIMIT_BYTES = 56 * 1024 * 1024

PROJ_ROWS = 512
GLA_ROWS = 512
MOBA_GROUP = 4
MIX_ROWS = 512
EXPERT_ROWS = 256
MOVE_ROWS = 256
DMA_UNROLL = 8


def _params(*semantics):
    return pltpu.CompilerParams(dimension_semantics=semantics, vmem_limit_bytes=VMEM_LIMIT_BYTES)


def _full(shape):
    return pl.BlockSpec(shape, lambda *_: (0,) * len(shape))


def _rms(x, g):
    return x * lax.rsqrt(jnp.mean(x * x, axis=-1, keepdims=True) + EPS) * g


def _proj_kernel(x_ref, g_ref, w_qk, w_v, w_r, w_a, w_up, b_a, w_qm, w_km, w_vm, w_zg, w_zm,
                 o_qk, o_v, o_r, o_la, o_qm, o_km, o_vm, o_zg, o_zm):
    h = _rms(x_ref[...], g_ref[...]).astype(BF16)

    def mm(w_ref):
        return jnp.dot(h, w_ref[...], preferred_element_type=F32)

    o_qk[...] = mm(w_qk)
    o_v[...] = mm(w_v).astype(BF16)
    o_r[...] = mm(w_r)
    a_lr = mm(w_a).astype(BF16)
    pre = jnp.dot(a_lr, w_up[...], preferred_element_type=F32) + b_a[...]
    o_la[...] = jax.nn.log_sigmoid(pre) * (1.0 / GLA_TAU)

    def mm_t(wt_ref):
        return lax.dot_general(wt_ref[...], h, (((1,), (1,)), ((), ())), preferred_element_type=F32)

    def store_blocks(o_ref, val_t):
        for c in range(o_ref.shape[0]):
            o_ref[c] = val_t[:, c * MOBA_BLOCK:(c + 1) * MOBA_BLOCK]

    store_blocks(o_qm, (mm_t(w_qm) * MOBA_Q_SCALE).astype(BF16))
    o_km[...] = mm(w_km).astype(BF16)
    store_blocks(o_vm, mm_t(w_vm).astype(BF16))
    o_zg[...] = mm(w_zg)
    o_zm[...] = mm(w_zm)


def _project(x2d, g_mix, w_in, w_alpha_up, b_alpha):
    T, D = x2d.shape
    rows = min(PROJ_ROWS, T)
    assert T % rows == 0
    splits = (GLA_QK, GLA_QK, GLA_V, GLA_V, GLA_LOWRANK, MOBA_W, MOBA_W, MOBA_W, D, D)
    offs = [0]
    for s in splits:
        offs.append(offs[-1] + s)
    wb = w_in.astype(BF16)
    sec = lambda i, j: wb[:, offs[i]:offs[j]]
    w_a = jnp.pad(sec(4, 5), ((0, 0), (0, LANES - GLA_LOWRANK)))
    w_up = jnp.pad(w_alpha_up.astype(BF16), ((0, LANES - GLA_LOWRANK), (0, 0)))
    weights = [sec(0, 2), sec(2, 3), sec(3, 4), w_a, w_up, b_alpha.reshape(1, GLA_QK).astype(F32),
               sec(5, 6).T, sec(6, 7), sec(7, 8).T, sec(8, 9), sec(9, 10)]
    out_defs = [(2 * GLA_QK, F32, False), (GLA_V, BF16, False), (GLA_V, F32, False), (GLA_QK, F32, False),
                (MOBA_W, BF16, True), (MOBA_W, BF16, False), (MOBA_W, BF16, True), (D, F32, False),
                (D, F32, False)]
    BS = MOBA_BLOCK
    assert rows % BS == 0
    row_spec = lambda n: pl.BlockSpec((rows, n), lambda i: (i, 0))
    blk_spec = lambda n: pl.BlockSpec((rows // BS, n, BS), lambda i: (i, 0, 0))
    return pl.pallas_call(
        _proj_kernel,
        grid=(T // rows,),
        in_specs=[row_spec(D), _full((1, D))] + [_full(w.shape) for w in weights],
        out_specs=[blk_spec(n) if t else row_spec(n) for n, _, t in out_defs],
        out_shape=[jax.ShapeDtypeStruct((T // BS, n, BS) if t else (T, n), dt) for n, dt, t in out_defs],
        compiler_params=_params("parallel"),
        name="norm_in_proj",
    )(x2d, g_mix.reshape(1, D).astype(F32), *weights)


def _gla_kernel(qk_ref, la_ref, v_ref, r_ref, g_ref, o_ref, state_ref, obuf_ref):
    C, H, DK, DV = GLA_CHUNK, GLA_HEADS, GLA_DK, GLA_DV
    rows = qk_ref.shape[1]
    hi = lax.Precision.HIGHEST

    @pl.when(pl.program_id(1) == 0)
    def _():
        state_ref[...] = jnp.zeros_like(state_ref)

    tri = (lax.broadcasted_iota(jnp.int32, (C, C), 0) >= lax.broadcasted_iota(jnp.int32, (C, C), 1)).astype(F32)
    ones = jnp.ones((C, DV), F32)
    lane_head = lax.broadcasted_iota(jnp.int32, (1, H * DK), 1) // DK
    head_masks = [(lane_head == h).astype(F32) for h in range(H)]
    stack_row = lax.broadcasted_iota(jnp.int32, (H * C, C), 0) % C
    stack_col = lax.broadcasted_iota(jnp.int32, (H * C, C), 1)
    causal = stack_col <= stack_row
    scale = DK ** -0.5

    def stack(m):
        return jnp.concatenate([m * head_masks[h] for h in range(H)], axis=0).astype(BF16)

    def diag(m, rb, cb):
        return [m[h * rb:(h + 1) * rb, h * cb:(h + 1) * cb] for h in range(H)]

    for c in range(rows // C):
        sl = slice(c * C, (c + 1) * C)
        q = qk_ref[0, sl, 0:H * DK]
        k = qk_ref[0, sl, H * DK:2 * H * DK]
        la = la_ref[0, sl, :]
        v = v_ref[0, sl, :]
        b = jnp.dot(tri, la, precision=hi, preferred_element_type=F32)
        b_last = b[C - 1:C, :]
        b_mid = b[C // 2 - 1:C // 2, :]
        b_last_rows = lax.dot_general(la, ones, (((0,), (0,)), ((), ())), precision=hi,
                                      preferred_element_type=F32)
        qe = stack(q * scale * jnp.exp(b - b_mid))
        ke = (k * jnp.exp(b_mid - b)).astype(BF16)
        kd = (k * jnp.exp(b_last - b)).astype(BF16)
        qb = stack(q * scale * jnp.exp(b))
        att = lax.dot_general(qe, ke, (((1,), (1,)), ((), ())), preferred_element_type=F32)
        att = jnp.where(causal, att, 0.0).astype(BF16)
        o_intra = diag(jnp.dot(att, v, preferred_element_type=F32), C, DV)
        state = state_ref[...]
        o_inter = jnp.dot(qb, state.astype(BF16), preferred_element_type=F32)
        kv = lax.dot_general(kd, v, (((0,), (0,)), ((), ())), preferred_element_type=F32)
        state_ref[...] = jnp.exp(b_last_rows) * state + jnp.concatenate(diag(kv, DK, DV), axis=0)
        for h in range(H):
            obuf_ref[sl, h * DV:(h + 1) * DV] = o_intra[h] + o_inter[h * C:(h + 1) * C, :]

    r = r_ref[0]
    for h in range(H):
        cs = slice(h * DV, (h + 1) * DV)
        y = _rms(obuf_ref[:, cs], g_ref[:, cs])
        rh = r[:, cs]
        o_ref[0, :, cs] = (y * (rh * jax.nn.sigmoid(rh))).astype(BF16)


def _gla(qk, la, v, r, g_head, B, S):
    rows = min(GLA_ROWS, S)
    assert S % rows == 0 and rows % GLA_CHUNK == 0
    spec = lambda n: pl.BlockSpec((1, rows, n), lambda b, i: (b, i, 0))
    return pl.pallas_call(
        _gla_kernel,
        grid=(B, S // rows),
        in_specs=[spec(2 * GLA_QK), spec(GLA_QK), spec(GLA_V), spec(GLA_V), _full((1, GLA_V))],
        out_specs=spec(GLA_V),
        out_shape=jax.ShapeDtypeStruct((B, S, GLA_V), BF16),
        scratch_shapes=[pltpu.VMEM((GLA_HEADS * GLA_DK, GLA_DV), F32), pltpu.VMEM((rows, GLA_V), F32)],
        compiler_params=_params("parallel", "arbitrary"),
        name="gla_chunked",
    )(qk.reshape(B, S, -1), la.reshape(B, S, -1), v.reshape(B, S, -1), r.reshape(B, S, -1),
      g_head.reshape(1, GLA_V).astype(F32))


def _t5_bucket(dist):
    n = jnp.maximum(dist, 0)
    max_exact = REL_BUCKETS // 2
    nf = jnp.maximum(n, 1).astype(F32)
    large = max_exact + (jnp.log(nf / max_exact) / math.log(REL_MAX_DIST / max_exact)
                         * (REL_BUCKETS - max_exact)).astype(jnp.int32)
    large = jnp.minimum(large, REL_BUCKETS - 1)
    return jnp.where(n < max_exact, n, large)


def _moba_bias_kernel(rb_ref, o_ref):
    BS, G = MOBA_BLOCK, MOBA_GROUP
    grp, kind = pl.program_id(0), pl.program_id(1)
    d = (lax.broadcasted_iota(jnp.int32, (BS, BS), 1) - lax.broadcasted_iota(jnp.int32, (BS, BS), 0)
         + kind * BS)
    bucket = _t5_bucket(d)
    for h in range(G):
        val = jnp.zeros((BS, BS), F32)
        for bkt in range(REL_BUCKETS):
            val = jnp.where(bucket == bkt, rb_ref[bkt, grp * G + h] * LOG2E, val)
        o_ref[0, 0, :, h * BS:(h + 1) * BS] = jnp.where(d >= 0, val, NEG_INF)


def _moba_bias(rel_bias):
    BS, G = MOBA_BLOCK, MOBA_GROUP
    n_grp = MOBA_HEADS // G
    return pl.pallas_call(
        _moba_bias_kernel,
        grid=(n_grp, 2),
        in_specs=[pl.BlockSpec(memory_space=pltpu.SMEM)],
        out_specs=pl.BlockSpec((1, 1, BS, G * BS), lambda g, k: (g, k, 0, 0)),
        out_shape=jax.ShapeDtypeStruct((n_grp, 2, BS, G * BS), F32),
        compiler_params=_params("parallel", "parallel"),
        name="moba_bias_tables",
    )(rel_bias.astype(F32))


def _moba_kernel(rb_ref, q_ref, k_ref, v_ref, bias_ref, o_ref,
                 qs_ref, kmean_ref, sel_ref, m_ref, l_ref, acc_ref, sbuf_ref):
    BS, G, DH = MOBA_BLOCK, MOBA_GROUP, MOBA_DH
    NBP = kmean_ref.shape[0]
    grp, i = pl.program_id(1), pl.program_id(2)

    @pl.when(i == 0)
    def _():
        S = k_ref.shape[1]
        blk_of_key = lax.broadcasted_iota(jnp.int32, (NBP, S), 1) // BS
        ind = (blk_of_key == lax.broadcasted_iota(jnp.int32, (NBP, S), 0)).astype(BF16)
        kmean_ref[...] = jnp.dot(ind, k_ref[0], preferred_element_type=F32) * (1.0 / BS)

    qt = q_ref[0]
    sub_head = lax.broadcasted_iota(jnp.int32, (G * DH, 1), 0) // DH
    for h in range(G):
        qs_ref[:, h * BS:(h + 1) * BS] = jnp.where(sub_head == h, qt, jnp.zeros_like(qt))

    gate = jnp.dot(kmean_ref[...].astype(BF16), qs_ref[...], preferred_element_type=F32)
    blk = lax.broadcasted_iota(jnp.int32, gate.shape, 0)
    gate = jnp.where(blk < i, gate, NEG_INF)
    for t in range(MOBA_TOPK):
        mx = jnp.max(gate, axis=0, keepdims=True)
        hit = (gate == mx) & (mx > NEG_INF)
        idx = jnp.min(jnp.where(hit, blk, NBP), axis=0, keepdims=True)
        sel_ref[t:t + 1, :] = idx
        gate = jnp.where(blk == idx, NEG_INF, gate)

    def mask_row(j):
        hit = (sel_ref[0:1, :] == j) | (sel_ref[1:2, :] == j) | (sel_ref[2:3, :] == j)
        return jnp.where(hit, 0.0, NEG_INF)

    def visit(j, addend, first=False):
        kj = k_ref[0, pl.ds(pl.multiple_of(j * BS, BS), BS), :]
        vjt = v_ref[j]
        s = jnp.dot(kj, qs_ref[...], preferred_element_type=F32) + addend
        mx = jnp.max(s, axis=0, keepdims=True)
        m_new = mx if first else jnp.maximum(m_ref[...], mx)
        p = jnp.exp2(s - m_new)
        pb = p.astype(BF16)
        pv = jnp.concatenate(
            [jnp.dot(vjt[h * DH:(h + 1) * DH, :], pb[:, h * BS:(h + 1) * BS], preferred_element_type=F32)
             for h in range(G)], axis=1)
        if first:
            l_ref[...] = jnp.sum(p, axis=0, keepdims=True)
            acc_ref[...] = pv
        else:
            alpha = jnp.exp2(m_ref[...] - m_new)
            l_ref[...] = alpha * l_ref[...] + jnp.sum(p, axis=0, keepdims=True)
            acc_ref[...] = alpha * acc_ref[...] + pv
        m_ref[...] = m_new

    visit(i, bias_ref[0, 0], first=True)

    @pl.when(i >= 1)
    def _():
        visit(i - 1, bias_ref[0, 1] + mask_row(i - 1))

    lane_head = lax.broadcasted_iota(jnp.int32, (1, G * BS), 1) // BS
    far_bias = jnp.zeros((1, G * BS), F32)
    for h in range(G):
        far_bias = jnp.where(lane_head == h, rb_ref[REL_BUCKETS - 1, grp * G + h] * LOG2E, far_bias)

    def far_scores(j0, n, slot):
        kj = k_ref[0, pl.ds(pl.multiple_of(j0 * BS, BS), n * BS), :]
        for h in range(G):
            cs = slice(h * BS, (h + 1) * BS)
            sbuf_ref[slot, 0:n * BS, cs] = jnp.dot(kj, qs_ref[:, cs], preferred_element_type=F32)

    def visit_far(j0, n, slot):
        vjt = jnp.concatenate([v_ref[j0 + t] for t in range(n)], axis=1)
        addend = [far_bias + mask_row(j0 + t) for t in range(n)]
        m_old, l_old, acc_old = m_ref[...], l_ref[...], acc_ref[...]
        ones = jnp.ones((8, n * BS), BF16)
        m_out, l_out, acc_out = [], [], []
        for h in range(G):
            cs = slice(h * BS, (h + 1) * BS)
            s = sbuf_ref[slot, 0:n * BS, cs]
            mx = jnp.max(s[0:BS], axis=0, keepdims=True) + addend[0][:, cs]
            for t in range(1, n):
                mx = jnp.maximum(mx, jnp.max(s[t * BS:(t + 1) * BS], axis=0, keepdims=True) + addend[t][:, cs])
            m_new = jnp.maximum(m_old[:, cs], mx)
            pb = jnp.concatenate([jnp.exp2(s[t * BS:(t + 1) * BS] - (m_new - addend[t][:, cs])).astype(BF16)
                                  for t in range(n)], axis=0)
            pv = jnp.dot(jnp.concatenate([vjt[h * DH:(h + 1) * DH, :], ones], axis=0), pb,
                         preferred_element_type=F32)
            alpha = jnp.exp2(m_old[:, cs] - m_new)
            m_out.append(m_new)
            l_out.append(alpha * l_old[:, cs] + pv[DH:DH + 1, :])
            acc_out.append(alpha * acc_old[:, cs] + pv[0:DH, :])
        m_ref[...] = jnp.concatenate(m_out, axis=1)
        l_ref[...] = jnp.concatenate(l_out, axis=1)
        acc_ref[...] = jnp.concatenate(acc_out, axis=1)

    n_far = jnp.maximum(i - 1, 0)
    n_pairs = lax.shift_right_logical(n_far, 1)

    @pl.when(n_pairs > 0)
    def _():
        far_scores(0, 2, 0)

    def far_quad(u, carry):
        far_scores(4 * u + 2, 2, 1)
        visit_far(4 * u, 2, 0)
        far_scores(2 * jnp.minimum(2 * u + 2, n_pairs - 1), 2, 0)
        visit_far(4 * u + 2, 2, 1)
        return carry

    lax.fori_loop(0, lax.shift_right_logical(n_pairs, 1), far_quad, 0)

    @pl.when(n_pairs % 2 == 1)
    def _():
        visit_far(2 * (n_pairs - 1), 2, 0)

    @pl.when(n_far % 2 == 1)
    def _():
        far_scores(n_far - 1, 1, 1)
        visit_far(n_far - 1, 1, 1)

    out_t = jnp.concatenate([acc_ref[:, h * BS:(h + 1) * BS] / l_ref[:, h * BS:(h + 1) * BS] for h in range(G)],
                            axis=0)
    o_ref[0] = out_t.T.astype(o_ref.dtype)


def _moba(qm_t, km, vm_t, bias, rel_bias, B, S):
    BS, G, DH = MOBA_BLOCK, MOBA_GROUP, MOBA_DH
    W = G * DH
    n_grp = MOBA_HEADS // G
    assert S % BS == 0
    NB = S // BS
    NBP = -(-NB // 8) * 8
    return pl.pallas_call(
        _moba_kernel,
        grid=(B, n_grp, NB),
        in_specs=[pl.BlockSpec(memory_space=pltpu.SMEM),
                  pl.BlockSpec((1, W, BS), lambda b, g, i: (b * NB + i, g, 0)),
                  pl.BlockSpec((1, S, W), lambda b, g, i: (b, 0, g)),
                  pl.BlockSpec((NB, W, BS), lambda b, g, i: (b, g, 0)),
                  pl.BlockSpec((1, 2, BS, G * BS), lambda b, g, i: (g, 0, 0, 0))],
        out_specs=pl.BlockSpec((1, BS, W), lambda b, g, i: (b, i, g)),
        out_shape=jax.ShapeDtypeStruct((B, S, MOBA_W), BF16),
        scratch_shapes=[pltpu.VMEM((W, G * BS), BF16), pltpu.VMEM((NBP, W), F32),
                        pltpu.VMEM((8, G * BS), jnp.int32),
                        pltpu.VMEM((1, G * BS), F32), pltpu.VMEM((1, G * BS), F32),
                        pltpu.VMEM((DH, G * BS), F32), pltpu.VMEM((2, 2 * BS, G * BS), F32)],
        compiler_params=_params("parallel", "parallel", "arbitrary"),
        name="moba_attention",
    )(rel_bias.astype(F32), qm_t, km.reshape(B, S, -1), vm_t, bias)


def _mem_kv_kernel(mem_ref, g_ref, w_ref, k_ref, v_ref):
    kv = jnp.dot(_rms(mem_ref[0], g_ref[...]).astype(BF16), w_ref[...], preferred_element_type=F32)
    k_ref[0] = kv[:, :MEM_W].astype(BF16)
    v_ref[0] = kv[:, MEM_W:].astype(BF16)


def _mem_kv(mem, g_mem, w_ckv):
    B, M, D = mem.shape
    spec = pl.BlockSpec((1, M, MEM_W), lambda b: (b, 0, 0))
    return pl.pallas_call(
        _mem_kv_kernel,
        grid=(B,),
        in_specs=[pl.BlockSpec((1, M, D), lambda b: (b, 0, 0)), _full((1, D)), _full((D, 2 * MEM_W))],
        out_specs=[spec, spec],
        out_shape=[jax.ShapeDtypeStruct((B, M, MEM_W), BF16)] * 2,
        compiler_params=_params("parallel"),
        name="memory_kv",
    )(mem, g_mem.reshape(1, D).astype(F32), w_ckv.astype(BF16))


INFO_W0, INFO_W1, INFO_E0, INFO_E1, INFO_R0, INFO_R1 = range(6)
ROUTER_GROUP_LANE0, ROUTER_EXPERT_LANE0 = 0, N_GROUPS


def _mix_kernel(x_ref, og_ref, om_ref, zg_ref, zm_ref, mk_ref, mv_ref, gc_ref, gm_ref,
                wpg, wpm, wout, wcq, wco, wr, br,
                x2_ref, info_ref, cnt_ref, base_ref):
    first = (pl.program_id(0) == 0) & (pl.program_id(1) == 0)

    @pl.when(first)
    def _():
        base_ref[...] = jnp.zeros_like(base_ref)

    def mm(a, w_ref):
        return jnp.dot(a.astype(BF16), w_ref[...], preferred_element_type=F32)

    merged = (jax.nn.sigmoid(zg_ref[0]) * mm(og_ref[0], wpg) + jax.nn.sigmoid(zm_ref[0]) * mm(om_ref[0], wpm))
    x1 = x_ref[0] + mm(merged, wout)

    qc = mm(_rms(x1, gc_ref[...]), wcq).astype(BF16)
    heads = []
    for h in range(MEM_HEADS):
        cs = slice(h * MEM_DH, (h + 1) * MEM_DH)
        s = lax.dot_general(qc[:, cs], mk_ref[0, :, cs], (((1,), (1,)), ((), ())),
                            preferred_element_type=F32) * (MEM_DH ** -0.5)
        p = jnp.exp(s - jnp.max(s, axis=-1, keepdims=True))
        o = jnp.dot(p.astype(BF16), mv_ref[0, :, cs], preferred_element_type=F32)
        heads.append(o / jnp.sum(p, axis=-1, keepdims=True))
    x2 = x1 + mm(jnp.concatenate(heads, axis=-1), wco)
    x2_ref[0] = x2

    logits = mm(_rms(x2, gm_ref[...]), wr) + br[...]
    rows = logits.shape[0]
    lane = lax.broadcasted_iota(jnp.int32, (rows, LANES), 1)
    is_grp = lane < N_GROUPS
    gl = jnp.where(is_grp, logits, NEG_INF)
    ge = jnp.exp(gl - jnp.max(gl, axis=-1, keepdims=True))
    g_prob = ge / jnp.sum(ge, axis=-1, keepdims=True)
    p_grp = jnp.max(g_prob, axis=-1, keepdims=True)
    grp = jnp.min(jnp.where((g_prob == p_grp) & is_grp, lane, LANES), axis=-1, keepdims=True)
    e_id = lane - ROUTER_EXPERT_LANE0
    in_grp = (e_id >= grp * EXPERTS_PER_GROUP) & (e_id < (grp + 1) * EXPERTS_PER_GROUP)
    el = jnp.where(in_grp, logits, NEG_INF)
    ee = jnp.exp(el - jnp.max(el, axis=-1, keepdims=True))
    e_prob = jnp.where(in_grp, ee / jnp.sum(ee, axis=-1, keepdims=True), -1.0)
    p0 = jnp.max(e_prob, axis=-1, keepdims=True)
    e0 = jnp.min(jnp.where(e_prob == p0, e_id, LANES), axis=-1, keepdims=True)
    e_rest = jnp.where(e_id == e0, -1.0, e_prob)
    p1 = jnp.max(e_rest, axis=-1, keepdims=True)
    e1 = jnp.min(jnp.where(e_rest == p1, e_id, LANES), axis=-1, keepdims=True)
    w0 = p_grp * p0 / (p0 + p1)
    w1 = p_grp * p1 / (p0 + p1)

    onehot = ((lane == e0) | (lane == e1)).astype(F32)
    before = (lax.broadcasted_iota(jnp.int32, (rows, rows), 1)
              < lax.broadcasted_iota(jnp.int32, (rows, rows), 0)).astype(BF16)
    seen = base_ref[...] + jnp.dot(before, onehot.astype(BF16), preferred_element_type=F32)
    r0 = jnp.sum(jnp.where(lane == e0, seen, 0.0), axis=-1, keepdims=True)
    r1 = jnp.sum(jnp.where(lane == e1, seen, 0.0), axis=-1, keepdims=True)
    base_ref[...] = base_ref[...] + jnp.sum(onehot, axis=0, keepdims=True)
    cnt_ref[...] = base_ref[...]

    info = jnp.zeros((rows, LANES), F32)
    for ln, val in ((INFO_W0, w0), (INFO_W1, w1), (INFO_E0, e0.astype(F32)), (INFO_E1, e1.astype(F32)),
                    (INFO_R0, r0), (INFO_R1, r1)):
        info = jnp.where(lane == ln, val, info)
    info_ref[0] = info


def _mix(x, o_g, o_m, z_g, z_m, mem_k, mem_v, g_cross, g_moe, w_proj_gla, w_proj_moba, w_out, w_cq, w_co,
         w_rg, b_rg, w_re, b_re):
    B, S, D = x.shape
    M = mem_k.shape[1]
    rows = min(MIX_ROWS, S)
    assert S % rows == 0
    pad = LANES - N_GROUPS - N_EXPERTS
    wr = jnp.pad(jnp.concatenate([w_rg, w_re], axis=1), ((0, 0), (0, pad))).astype(BF16)
    br = jnp.pad(jnp.concatenate([b_rg, b_re]), (0, pad)).reshape(1, LANES).astype(F32)
    weights = [w_proj_gla.astype(BF16), w_proj_moba.astype(BF16), w_out.astype(BF16), w_cq.astype(BF16),
               w_co.astype(BF16), wr, br]
    tile = lambda n: pl.BlockSpec((1, rows, n), lambda b, i: (b, i, 0))
    memspec = pl.BlockSpec((1, M, MEM_W), lambda b, i: (b, 0, 0))
    return pl.pallas_call(
        _mix_kernel,
        grid=(B, S // rows),
        in_specs=[tile(D), tile(GLA_V), tile(MOBA_W), tile(D), tile(D), memspec, memspec,
                  _full((1, D)), _full((1, D))] + [_full(w.shape) for w in weights],
        out_specs=[tile(D), tile(LANES), _full((1, LANES))],
        out_shape=[jax.ShapeDtypeStruct((B, S, D), F32), jax.ShapeDtypeStruct((B, S, LANES), F32),
                   jax.ShapeDtypeStruct((1, LANES), F32)],
        scratch_shapes=[pltpu.VMEM((1, LANES), F32)],
        compiler_params=_params("arbitrary", "arbitrary"),
        name="merge_memattn_router",
    )(x, o_g, o_m, z_g.reshape(B, S, D), z_m.reshape(B, S, D), mem_k, mem_v,
      g_cross.reshape(1, D).astype(F32), g_moe.reshape(1, D).astype(F32), *weights)


def _plan_kernel(cnt_ref, info_ref, dest_ref, blk_ref):
    rows = info_ref.shape[0]
    lane1 = lax.broadcasted_iota(jnp.int32, (1, LANES), 1)
    nblk = jnp.floor((cnt_ref[...] + (EXPERT_ROWS - 1)) * (1.0 / EXPERT_ROWS))
    nblk = jnp.where(lane1 < N_EXPERTS, nblk, 0.0)
    hi = jnp.floor(nblk * (1.0 / 256.0))
    lo = nblk - 256.0 * hi
    upto = (lax.broadcasted_iota(jnp.int32, (LANES, LANES), 0)
            <= lax.broadcasted_iota(jnp.int32, (LANES, LANES), 1)).astype(BF16)
    digits = jnp.concatenate([jnp.broadcast_to(hi, (8, LANES)), jnp.broadcast_to(lo, (8, LANES))], axis=0)
    sums = jnp.dot(digits.astype(BF16), upto, preferred_element_type=F32)
    pend = sums[0:1] * 256.0 + sums[8:9]
    pstart_rows = (pend - nblk) * EXPERT_ROWS

    info = info_ref[...]
    lane = lax.broadcasted_iota(jnp.int32, (rows, LANES), 1)

    def field(ln):
        return jnp.sum(jnp.where(lane == ln, info, 0.0), axis=-1, keepdims=True)

    def dest(e, r):
        return jnp.sum(jnp.where(lane == e.astype(jnp.int32), pstart_rows, 0.0), axis=-1, keepdims=True) + r

    d0 = dest(field(INFO_E0), field(INFO_R0))
    d1 = dest(field(INFO_E1), field(INFO_R1))
    cols = jnp.where(lane == 0, d0, jnp.where(lane == 1, d1, 0.0))
    dest_ref[0] = cols.T[0:8, :].astype(jnp.int32)

    @pl.when(pl.program_id(0) == 0)
    def _():
        n = lax.broadcasted_iota(jnp.int32, (blk_ref.shape[0], LANES), 0).astype(F32)
        done = jnp.where((pend <= n) & (lane1 < N_EXPERTS), 1.0, 0.0)
        e = jnp.minimum(jnp.sum(done, axis=-1, keepdims=True), N_EXPERTS - 1.0)
        blk_ref[...] = e.astype(jnp.int32)


def _plan(counts, info2d, n_blk):
    T = info2d.shape[0]
    rows = min(MOVE_ROWS, T)
    assert T % rows == 0
    n_blk_pad = -(-n_blk // 8) * 8
    return pl.pallas_call(
        _plan_kernel,
        grid=(T // rows,),
        in_specs=[_full((1, LANES)), pl.BlockSpec((rows, LANES), lambda i: (i, 0))],
        out_specs=[pl.BlockSpec((1, 8, rows), lambda i: (i, 0, 0)), _full((n_blk_pad, 1))],
        out_shape=[jax.ShapeDtypeStruct((T // rows, 8, rows), jnp.int32),
                   jax.ShapeDtypeStruct((n_blk_pad, 1), jnp.int32)],
        compiler_params=_params("arbitrary"),
        name="dispatch_plan",
    )(counts, info2d)


def _dispatch_kernel(dest_ref, x_ref, g_ref, xs_in_ref, xs_ref, hbuf_ref, sems):
    del xs_in_ref
    rows = x_ref.shape[0]
    i, n = pl.program_id(0), pl.num_programs(0)
    buf = i % 2
    hbuf_ref[buf] = _rms(x_ref[...], g_ref[...])

    def row_copy(b, r, dst):
        return pltpu.make_async_copy(hbuf_ref.at[b, pl.ds(r, 1)], xs_ref.at[pl.ds(dst, 1)], sems.at[b])

    def start(r, c):
        row_copy(buf, r, dest_ref[0, 0, r]).start()
        row_copy(buf, r, dest_ref[0, 1, r]).start()
        return c

    def wait_all(b):
        def wait(r, c):
            row_copy(b, r, 0).wait()
            row_copy(b, r, 0).wait()
            return c
        lax.fori_loop(0, rows, wait, 0, unroll=DMA_UNROLL)

    lax.fori_loop(0, rows, start, 0, unroll=DMA_UNROLL)

    @pl.when(i > 0)
    def _():
        wait_all(1 - buf)

    @pl.when(i == n - 1)
    def _():
        wait_all(buf)


def _dispatch(dest, x2d, g_moe, cap):
    T, D = x2d.shape
    rows = dest.shape[2]
    xs0 = jnp.zeros((cap, D), F32)
    return pl.pallas_call(
        _dispatch_kernel,
        grid=(T // rows,),
        in_specs=[pl.BlockSpec((1, 8, rows), lambda i: (i, 0, 0), memory_space=pltpu.SMEM),
                  pl.BlockSpec((rows, D), lambda i: (i, 0)), _full((1, D)),
                  pl.BlockSpec(memory_space=pl.ANY)],
        out_specs=pl.BlockSpec(memory_space=pl.ANY),
        out_shape=jax.ShapeDtypeStruct((cap, D), F32),
        scratch_shapes=[pltpu.VMEM((2, rows, D), F32), pltpu.SemaphoreType.DMA((2,))],
        input_output_aliases={3: 0},
        compiler_params=_params("arbitrary"),
        name="moe_dispatch",
    )(dest, x2d, g_moe.reshape(1, D).astype(F32), xs0)


def _expert_kernel(blk_e_ref, xs_ref, wg_ref, wu_ref, wd_ref, y_ref, wg_bf, wu_bf, wd_bf):
    n = pl.program_id(0)
    prev = blk_e_ref[jnp.maximum(n - 1, 0)]

    @pl.when((n == 0) | (blk_e_ref[n] != prev))
    def _():
        wg_bf[...] = wg_ref[0].astype(BF16)
        wu_bf[...] = wu_ref[0].astype(BF16)
        wd_bf[...] = wd_ref[0].astype(BF16)

    xb = xs_ref[...].astype(BF16)
    gate = jnp.dot(xb, wg_bf[...], preferred_element_type=F32)
    up = jnp.dot(xb, wu_bf[...], preferred_element_type=F32)
    hid = (gate * jax.nn.sigmoid(gate) * up).astype(BF16)
    y_ref[...] = jnp.dot(hid, wd_bf[...], preferred_element_type=F32)


def _experts(blk_e, xs, w_gate, w_up, w_down):
    cap, D = xs.shape
    DE = w_gate.shape[-1]
    n_blk = cap // EXPERT_ROWS
    rows_spec = pl.BlockSpec((EXPERT_ROWS, D), lambda n, e: (n, 0))
    return pl.pallas_call(
        _expert_kernel,
        grid_spec=pltpu.PrefetchScalarGridSpec(
            num_scalar_prefetch=1,
            grid=(n_blk,),
            in_specs=[rows_spec,
                      pl.BlockSpec((1, D, DE), lambda n, e: (e[n], 0, 0)),
                      pl.BlockSpec((1, D, DE), lambda n, e: (e[n], 0, 0)),
                      pl.BlockSpec((1, DE, D), lambda n, e: (e[n], 0, 0))],
            out_specs=rows_spec,
            scratch_shapes=[pltpu.VMEM((D, DE), BF16), pltpu.VMEM((D, DE), BF16), pltpu.VMEM((DE, D), BF16)]),
        out_shape=jax.ShapeDtypeStruct((cap, D), F32),
        compiler_params=_params("arbitrary"),
        name="moe_experts",
    )(blk_e, xs, w_gate, w_up, w_down)


def _combine_kernel(final_norm, dest_ref, dest_next_ref, x_ref, info_ref, g_ref, y_ref, o_ref, ybuf_ref, sems):
    rows = x_ref.shape[0]
    i, n = pl.program_id(0), pl.num_programs(0)
    buf = i % 2

    def row_copy(b, r, slot, src):
        return pltpu.make_async_copy(y_ref.at[pl.ds(src, 1)], ybuf_ref.at[b, slot, pl.ds(r, 1)], sems.at[b])

    def fetch(b, d_ref):
        def start(r, c):
            row_copy(b, r, 0, d_ref[0, 0, r]).start()
            row_copy(b, r, 1, d_ref[0, 1, r]).start()
            return c
        lax.fori_loop(0, rows, start, 0, unroll=DMA_UNROLL)

    @pl.when(i == 0)
    def _():
        fetch(0, dest_ref)

    @pl.when(i + 1 < n)
    def _():
        fetch(1 - buf, dest_next_ref)

    def wait(r, c):
        row_copy(buf, r, 0, 0).wait()
        row_copy(buf, r, 1, 0).wait()
        return c

    lax.fori_loop(0, rows, wait, 0, unroll=DMA_UNROLL)
    info = info_ref[...]
    w0 = info[:, INFO_W0:INFO_W0 + 1]
    w1 = info[:, INFO_W1:INFO_W1 + 1]
    out = x_ref[...] + (w0 * ybuf_ref[buf, 0] + w1 * ybuf_ref[buf, 1])
    o_ref[...] = _rms(out, g_ref[...]) if final_norm else out


def _combine(dest, x2d, info2d, g_final, y, final_norm):
    T, D = x2d.shape
    rows = dest.shape[2]
    n = T // rows
    return pl.pallas_call(
        functools.partial(_combine_kernel, final_norm),
        grid=(n,),
        in_specs=[pl.BlockSpec((1, 8, rows), lambda i: (i, 0, 0), memory_space=pltpu.SMEM),
                  pl.BlockSpec((1, 8, rows), lambda i: (jnp.minimum(i + 1, n - 1), 0, 0), memory_space=pltpu.SMEM),
                  pl.BlockSpec((rows, D), lambda i: (i, 0)), pl.BlockSpec((rows, LANES), lambda i: (i, 0)),
                  _full((1, D)), pl.BlockSpec(memory_space=pl.ANY)],
        out_specs=pl.BlockSpec((rows, D), lambda i: (i, 0)),
        out_shape=jax.ShapeDtypeStruct((T, D), F32),
        scratch_shapes=[pltpu.VMEM((2, 2, rows, D), F32), pltpu.SemaphoreType.DMA((2,))],
        compiler_params=_params("arbitrary"),
        name="moe_combine_final_norm",
    )(dest, dest, x2d, info2d, g_final.reshape(1, D).astype(F32), y)


def kernel(x, mem, g_mem, rel_bias, g_mix, w_in, w_alpha_up, b_alpha, g_gla_head, w_proj_gla, w_proj_moba,
           w_out, g_cross, w_cq, w_ckv, w_co, g_moe, w_router_group, b_router_group, w_router_expert,
           b_router_expert, w_exp_gate, w_exp_up, w_exp_down, g_final):
    B, S, D = x.shape
    T = B * S
    depth = g_mix.shape[0]
    n_assign = T * TOPK_IN_GROUP
    n_blk = -(-(n_assign + N_EXPERTS * (EXPERT_ROWS - 1)) // EXPERT_ROWS)
    cap = n_blk * EXPERT_ROWS

    mem_bias = _moba_bias(rel_bias)
    for l in range(depth):
        qk, v_g, r_g, la, q_m, k_m, v_m, z_g, z_m = _project(x.reshape(T, D), g_mix[l], w_in[l], w_alpha_up[l],
                                                             b_alpha[l])
        o_g = _gla(qk, la, v_g, r_g, g_gla_head[l], B, S)
        o_m = _moba(q_m, k_m, v_m, mem_bias, rel_bias, B, S)
        mem_k, mem_v = _mem_kv(mem, g_mem, w_ckv[l])
        x2, info, counts = _mix(x, o_g, o_m, z_g, z_m, mem_k, mem_v, g_cross[l], g_moe[l], w_proj_gla[l],
                                w_proj_moba[l], w_out[l], w_cq[l], w_co[l], w_router_group[l], b_router_group[l],
                                w_router_expert[l], b_router_expert[l])
        x2d, info2d = x2.reshape(T, D), info.reshape(T, LANES)
        dest, blk_e = _plan(counts, info2d, n_blk)
        xs = _dispatch(dest, x2d, g_moe[l], cap)
        y = _experts(blk_e[:n_blk, 0], xs, w_exp_gate[l], w_exp_up[l], w_exp_down[l])
        x = _combine(dest, x2d, info2d, g_final, y, final_norm=(l == depth - 1)).reshape(B, S, D)
    return x
```

```python
import functools
import math

import jax
import jax.numpy as jnp
from jax import lax
from jax.experimental import pallas as pl
from jax.experimental.pallas import tpu as pltpu

F32 = jnp.float32
BF16 = jnp.bfloat16
NEG_INF = float("-inf")

EPS = 1e-6
GLA_HEADS, GLA_DK, GLA_DV, GLA_LOWRANK, GLA_TAU, GLA_CHUNK = 4, 64, 128, 16, 16.0, 64
GLA_QK, GLA_V = GLA_HEADS * GLA_DK, GLA_HEADS * GLA_DV
MOBA_HEADS, MOBA_DH, MOBA_BLOCK, MOBA_TOPK = 8, 64, 256, 3
MOBA_W = MOBA_HEADS * MOBA_DH
LOG2E = math.log2(math.e)
MOBA_Q_SCALE = MOBA_DH ** -0.5 * LOG2E
REL_BUCKETS, REL_MAX_DIST = 32, 128
MEM_HEADS, MEM_DH = 4, 128
MEM_W = MEM_HEADS * MEM_DH
N_GROUPS, EXPERTS_PER_GROUP, TOPK_IN_GROUP = 4, 8, 2
N_EXPERTS = N_GROUPS * EXPERTS_PER_GROUP

LANES = 128
VMEM_LIMIT_BYTES = 56 * 1024 * 1024

PROJ_ROWS = 512
GLA_ROWS = 512
MOBA_GROUP = 4
MIX_ROWS = 512
MIX_PARTS = 2
EXPERT_ROWS = 256
MOVE_ROWS = 256
PLAN_ROWS = 2048
DMA_UNROLL = 8


def _params(*semantics):
    return pltpu.CompilerParams(dimension_semantics=semantics, vmem_limit_bytes=VMEM_LIMIT_BYTES)


def _full(shape):
    return pl.BlockSpec(shape, lambda *_: (0,) * len(shape))


def _rms(x, g):
    return x * lax.rsqrt(jnp.mean(x * x, axis=-1, keepdims=True) + EPS) * g


def _proj_kernel(x_ref, g_ref, w_qk, w_v, w_r, w_a, w_up, b_a, w_qm, w_km, w_vm, w_zg, w_zm,
                 o_qk, o_v, o_r, o_la, o_qm, o_km, o_vm, o_zg, o_zm):
    h = _rms(x_ref[...], g_ref[...]).astype(BF16)

    def mm(w_ref):
        return jnp.dot(h, w_ref[...], preferred_element_type=F32)

    o_qk[...] = mm(w_qk)
    o_v[...] = mm(w_v).astype(BF16)
    o_r[...] = mm(w_r)
    a_lr = mm(w_a).astype(BF16)
    pre = jnp.dot(a_lr, w_up[...], preferred_element_type=F32) + b_a[...]
    o_la[...] = jax.nn.log_sigmoid(pre) * (1.0 / GLA_TAU)

    def mm_t(wt_ref):
        return lax.dot_general(wt_ref[...], h, (((1,), (1,)), ((), ())), preferred_element_type=F32)

    def store_blocks(o_ref, val_t):
        for c in range(o_ref.shape[0]):
            o_ref[c] = val_t[:, c * MOBA_BLOCK:(c + 1) * MOBA_BLOCK]

    store_blocks(o_qm, (mm_t(w_qm) * MOBA_Q_SCALE).astype(BF16))
    o_km[...] = mm(w_km).astype(BF16)
    store_blocks(o_vm, mm_t(w_vm).astype(BF16))
    o_zg[...] = mm(w_zg)
    o_zm[...] = mm(w_zm)


def _project(x2d, g_mix, w_in, w_alpha_up, b_alpha):
    T, D = x2d.shape
    rows = min(PROJ_ROWS, T)
    assert T % rows == 0
    splits = (GLA_QK, GLA_QK, GLA_V, GLA_V, GLA_LOWRANK, MOBA_W, MOBA_W, MOBA_W, D, D)
    offs = [0]
    for s in splits:
        offs.append(offs[-1] + s)
    wb = w_in.astype(BF16)
    sec = lambda i, j: wb[:, offs[i]:offs[j]]
    w_a = jnp.pad(sec(4, 5), ((0, 0), (0, LANES - GLA_LOWRANK)))
    w_up = jnp.pad(w_alpha_up.astype(BF16), ((0, LANES - GLA_LOWRANK), (0, 0)))
    weights = [sec(0, 2), sec(2, 3), sec(3, 4), w_a, w_up, b_alpha.reshape(1, GLA_QK).astype(F32),
               sec(5, 6).T, sec(6, 7), sec(7, 8).T, sec(8, 9), sec(9, 10)]
    out_defs = [(2 * GLA_QK, F32, False), (GLA_V, BF16, False), (GLA_V, F32, False), (GLA_QK, F32, False),
                (MOBA_W, BF16, True), (MOBA_W, BF16, False), (MOBA_W, BF16, True), (D, F32, False),
                (D, F32, False)]
    BS = MOBA_BLOCK
    assert rows % BS == 0
    row_spec = lambda n: pl.BlockSpec((rows, n), lambda i: (i, 0))
    blk_spec = lambda n: pl.BlockSpec((rows // BS, n, BS), lambda i: (i, 0, 0))
    return pl.pallas_call(
        _proj_kernel,
        grid=(T // rows,),
        in_specs=[row_spec(D), _full((1, D))] + [_full(w.shape) for w in weights],
        out_specs=[blk_spec(n) if t else row_spec(n) for n, _, t in out_defs],
        out_shape=[jax.ShapeDtypeStruct((T // BS, n, BS) if t else (T, n), dt) for n, dt, t in out_defs],
        compiler_params=_params("parallel"),
        name="norm_in_proj",
    )(x2d, g_mix.reshape(1, D).astype(F32), *weights)


def _gla_kernel(qk_ref, la_ref, v_ref, r_ref, g_ref, o_ref, state_ref, obuf_ref):
    C, H, DK, DV = GLA_CHUNK, GLA_HEADS, GLA_DK, GLA_DV
    rows = qk_ref.shape[1]

    @pl.when(pl.program_id(1) == 0)
    def _():
        state_ref[...] = jnp.zeros_like(state_ref)

    tri = (lax.broadcasted_iota(jnp.int32, (C, C), 0) >= lax.broadcasted_iota(jnp.int32, (C, C), 1)).astype(BF16)
    lane_head = lax.broadcasted_iota(jnp.int32, (1, H * DK), 1) // DK
    head_masks = [(lane_head == h).astype(F32) for h in range(H)]
    stack_row = lax.broadcasted_iota(jnp.int32, (H * C, C), 0) % C
    stack_col = lax.broadcasted_iota(jnp.int32, (H * C, C), 1)
    causal = stack_col <= stack_row
    same_head = (lax.broadcasted_iota(jnp.int32, (H * DV, H * DK), 0) // DV
                 == lax.broadcasted_iota(jnp.int32, (H * DV, H * DK), 1) // DK)
    scale = DK ** -0.5

    def stack(m):
        return jnp.concatenate([m * head_masks[h] for h in range(H)], axis=0).astype(BF16)

    chunks = [slice(c * C, (c + 1) * C) for c in range(rows // C)]

    def cum_log_decay(sl):
        la = la_ref[0, sl, :]
        p1 = la.astype(BF16)
        r1 = la - p1.astype(F32)
        p2 = r1.astype(BF16)
        p3 = (r1 - p2.astype(F32)).astype(BF16)
        s3 = jnp.dot(tri, jnp.concatenate([p1, p2, p3], axis=1), preferred_element_type=F32)
        w = H * DK
        return (s3[:, 0:w] + s3[:, w:2 * w]) + s3[:, 2 * w:3 * w]

    b_all = [cum_log_decay(sl) for sl in chunks]

    qe_all, ke_all, kd_all, qb_all, decay_all = [], [], [], [], []
    for sl, b in zip(chunks, b_all):
        q = qk_ref[0, sl, 0:H * DK] * scale
        k = qk_ref[0, sl, H * DK:2 * H * DK]
        b_last = b[C - 1:C, :]
        b_mid = b[C // 2 - 1:C // 2, :]
        qe_all.append(stack(q * jnp.exp(b - b_mid)))
        ke_all.append((k * jnp.exp(b_mid - b)).astype(BF16))
        kd_all.append((k * jnp.exp(b_last - b)).astype(BF16))
        qb_all.append((q * jnp.exp(b)).astype(BF16))
        decay_all.append(jnp.exp(b_last))

    att_all = [jnp.where(causal, lax.dot_general(qe, ke, (((1,), (1,)), ((), ())), preferred_element_type=F32),
                         0.0).astype(BF16) for qe, ke in zip(qe_all, ke_all)]

    o_intra_all, kv_all = [], []
    for sl, att, kd in zip(chunks, att_all, kd_all):
        v = v_ref[0, sl, :]
        o_intra_all.append(jnp.concatenate(
            [jnp.dot(att[h * C:(h + 1) * C, :], v[:, h * DV:(h + 1) * DV], preferred_element_type=F32)
             for h in range(H)], axis=1))
        kv_t = lax.dot_general(v, kd, (((0,), (0,)), ((), ())), preferred_element_type=F32)
        kv_all.append(jnp.where(same_head, kv_t, 0.0))

    state_t = state_ref[...]
    for sl, qb, decay, kv_t, o_intra in zip(chunks, qb_all, decay_all, kv_all, o_intra_all):
        o_inter = lax.dot_general(qb, state_t.astype(BF16), (((1,), (1,)), ((), ())),
                                  preferred_element_type=F32)
        obuf_ref[sl, :] = o_intra + o_inter
        state_t = decay * state_t + kv_t
    state_ref[...] = state_t

    r = r_ref[0]
    for h in range(H):
        cs = slice(h * DV, (h + 1) * DV)
        y = _rms(obuf_ref[:, cs], g_ref[:, cs])
        rh = r[:, cs]
        o_ref[0, :, cs] = (y * (rh * jax.nn.sigmoid(rh))).astype(BF16)


def _gla(qk, la, v, r, g_head, B, S):
    rows = min(GLA_ROWS, S)
    assert S % rows == 0 and rows % GLA_CHUNK == 0
    spec = lambda n: pl.BlockSpec((1, rows, n), lambda b, i: (b, i, 0))
    return pl.pallas_call(
        _gla_kernel,
        grid=(B, S // rows),
        in_specs=[spec(2 * GLA_QK), spec(GLA_QK), spec(GLA_V), spec(GLA_V), _full((1, GLA_V))],
        out_specs=spec(GLA_V),
        out_shape=jax.ShapeDtypeStruct((B, S, GLA_V), BF16),
        scratch_shapes=[pltpu.VMEM((GLA_V, GLA_QK), F32), pltpu.VMEM((rows, GLA_V), F32)],
        compiler_params=_params("parallel", "arbitrary"),
        name="gla_chunked",
    )(qk.reshape(B, S, -1), la.reshape(B, S, -1), v.reshape(B, S, -1), r.reshape(B, S, -1),
      g_head.reshape(1, GLA_V).astype(F32))


def _t5_bucket(dist):
    n = jnp.maximum(dist, 0)
    max_exact = REL_BUCKETS // 2
    nf = jnp.maximum(n, 1).astype(F32)
    large = max_exact + (jnp.log(nf / max_exact) / math.log(REL_MAX_DIST / max_exact)
                         * (REL_BUCKETS - max_exact)).astype(jnp.int32)
    large = jnp.minimum(large, REL_BUCKETS - 1)
    return jnp.where(n < max_exact, n, large)


def _moba_bias_kernel(rb_ref, o_ref):
    BS, G = MOBA_BLOCK, MOBA_GROUP
    grp, kind = pl.program_id(0), pl.program_id(1)
    d = (lax.broadcasted_iota(jnp.int32, (BS, BS), 1) - lax.broadcasted_iota(jnp.int32, (BS, BS), 0)
         + kind * BS)
    bucket = _t5_bucket(d)
    for h in range(G):
        val = jnp.zeros((BS, BS), F32)
        for bkt in range(REL_BUCKETS):
            val = jnp.where(bucket == bkt, rb_ref[bkt, grp * G + h] * LOG2E, val)
        o_ref[0, 0, :, h * BS:(h + 1) * BS] = jnp.where(d >= 0, val, NEG_INF)


def _moba_bias(rel_bias):
    BS, G = MOBA_BLOCK, MOBA_GROUP
    n_grp = MOBA_HEADS // G
    return pl.pallas_call(
        _moba_bias_kernel,
        grid=(n_grp, 2),
        in_specs=[pl.BlockSpec(memory_space=pltpu.SMEM)],
        out_specs=pl.BlockSpec((1, 1, BS, G * BS), lambda g, k: (g, k, 0, 0)),
        out_shape=jax.ShapeDtypeStruct((n_grp, 2, BS, G * BS), F32),
        compiler_params=_params("parallel", "parallel"),
        name="moba_bias_tables",
    )(rel_bias.astype(F32))


def _moba_kernel(rb_ref, q_ref, k_ref, v_ref, bias_ref, o_ref,
                 qs_ref, kmean_ref, sel_ref, m_ref, l_ref, acc_ref, sbuf_ref):
    BS, G, DH = MOBA_BLOCK, MOBA_GROUP, MOBA_DH
    NBP = kmean_ref.shape[0]
    grp, i = pl.program_id(1), pl.program_id(2)

    @pl.when(i == 0)
    def _():
        S = k_ref.shape[1]
        blk_of_key = lax.broadcasted_iota(jnp.int32, (NBP, S), 1) // BS
        ind = (blk_of_key == lax.broadcasted_iota(jnp.int32, (NBP, S), 0)).astype(BF16)
        kmean_ref[...] = jnp.dot(ind, k_ref[0], preferred_element_type=F32) * (1.0 / BS)

    qt = q_ref[0]
    sub_head = lax.broadcasted_iota(jnp.int32, (G * DH, 1), 0) // DH
    for h in range(G):
        qs_ref[:, h * BS:(h + 1) * BS] = jnp.where(sub_head == h, qt, jnp.zeros_like(qt))

    gate = jnp.dot(kmean_ref[...].astype(BF16), qs_ref[...], preferred_element_type=F32)
    blk = lax.broadcasted_iota(jnp.int32, gate.shape, 0)
    gate = jnp.where(blk < i, gate, NEG_INF)
    for t in range(MOBA_TOPK):
        mx = jnp.max(gate, axis=0, keepdims=True)
        hit = (gate == mx) & (mx > NEG_INF)
        idx = jnp.min(jnp.where(hit, blk, NBP), axis=0, keepdims=True)
        sel_ref[t:t + 1, :] = idx
        gate = jnp.where(blk == idx, NEG_INF, gate)

    def mask_row(j):
        hit = (sel_ref[0:1, :] == j) | (sel_ref[1:2, :] == j) | (sel_ref[2:3, :] == j)
        return jnp.where(hit, 0.0, NEG_INF)

    def with_ones(vt):
        return jnp.concatenate([vt, jnp.ones((8, vt.shape[1]), BF16)], axis=0)

    def far_scores(j0, n, slot):
        kj = k_ref[0, pl.ds(pl.multiple_of(j0 * BS, BS), n * BS), :]
        for h in range(G):
            cs = slice(h * BS, (h + 1) * BS)
            sbuf_ref[slot, 0:n * BS, cs] = jnp.dot(kj, qs_ref[:, cs], preferred_element_type=F32)

    j_prev = jnp.maximum(i - 1, 0)
    k_own = k_ref[0, pl.ds(pl.multiple_of(i * BS, BS), BS), :]
    k_prev = k_ref[0, pl.ds(pl.multiple_of(j_prev * BS, BS), BS), :]
    vt_near = jnp.concatenate([v_ref[i], v_ref[j_prev]], axis=1)
    prev_mask = mask_row(i - 1)
    for h in range(G):
        cs = slice(h * BS, (h + 1) * BS)
        sbuf_ref[1, 0:BS, cs] = jnp.dot(k_own, qs_ref[:, cs], preferred_element_type=F32)
        sbuf_ref[1, BS:2 * BS, cs] = jnp.dot(k_prev, qs_ref[:, cs], preferred_element_type=F32)
    far_scores(0, 2, 0)
    m_out, l_out, acc_out = [], [], []
    for h in range(G):
        cs = slice(h * BS, (h + 1) * BS)
        s_own = sbuf_ref[1, 0:BS, cs] + bias_ref[0, 0, :, cs]
        s_prev = sbuf_ref[1, BS:2 * BS, cs] + (bias_ref[0, 1, :, cs] + prev_mask[:, cs])
        m0 = jnp.maximum(jnp.max(s_own, axis=0, keepdims=True), jnp.max(s_prev, axis=0, keepdims=True))
        pb = jnp.concatenate([jnp.exp2(s_own - m0).astype(BF16), jnp.exp2(s_prev - m0).astype(BF16)], axis=0)
        pv = jnp.dot(with_ones(vt_near[h * DH:(h + 1) * DH, :]), pb, preferred_element_type=F32)
        m_out.append(m0)
        l_out.append(pv[DH:DH + 1, :])
        acc_out.append(pv[0:DH, :])
    m_ref[...] = jnp.concatenate(m_out, axis=1)
    l_ref[...] = jnp.concatenate(l_out, axis=1)
    acc_ref[...] = jnp.concatenate(acc_out, axis=1)

    lane_head = lax.broadcasted_iota(jnp.int32, (1, G * BS), 1) // BS
    far_bias = jnp.zeros((1, G * BS), F32)
    for h in range(G):
        far_bias = jnp.where(lane_head == h, rb_ref[REL_BUCKETS - 1, grp * G + h] * LOG2E, far_bias)

    def visit_far(j0, n, slot):
        vjt = jnp.concatenate([v_ref[j0 + t] for t in range(n)], axis=1)
        addend = [far_bias + mask_row(j0 + t) for t in range(n)]
        m_old, l_old, acc_old = m_ref[...], l_ref[...], acc_ref[...]
        m_out, l_out, acc_out = [], [], []
        for h in range(G):
            cs = slice(h * BS, (h + 1) * BS)
            s = sbuf_ref[slot, 0:n * BS, cs]
            mx = jnp.max(s[0:BS], axis=0, keepdims=True) + addend[0][:, cs]
            for t in range(1, n):
                mx = jnp.maximum(mx, jnp.max(s[t * BS:(t + 1) * BS], axis=0, keepdims=True) + addend[t][:, cs])
            m_new = jnp.maximum(m_old[:, cs], mx)
            pb = jnp.concatenate([jnp.exp2(s[t * BS:(t + 1) * BS] - (m_new - addend[t][:, cs])).astype(BF16)
                                  for t in range(n)], axis=0)
            pv = jnp.dot(with_ones(vjt[h * DH:(h + 1) * DH, :]), pb, preferred_element_type=F32)
            alpha = jnp.exp2(m_old[:, cs] - m_new)
            m_out.append(m_new)
            l_out.append(alpha * l_old[:, cs] + pv[DH:DH + 1, :])
            acc_out.append(alpha * acc_old[:, cs] + pv[0:DH, :])
        m_ref[...] = jnp.concatenate(m_out, axis=1)
        l_ref[...] = jnp.concatenate(l_out, axis=1)
        acc_ref[...] = jnp.concatenate(acc_out, axis=1)

    n_far = jnp.maximum(i - 1, 0)
    n_pairs = lax.shift_right_logical(n_far, 1)

    def far_quad(u, carry):
        far_scores(4 * u + 2, 2, 1)
        visit_far(4 * u, 2, 0)
        far_scores(2 * jnp.minimum(2 * u + 2, n_pairs - 1), 2, 0)
        visit_far(4 * u + 2, 2, 1)
        return carry

    lax.fori_loop(0, lax.shift_right_logical(n_pairs, 1), far_quad, 0)

    @pl.when(n_pairs % 2 == 1)
    def _():
        visit_far(2 * (n_pairs - 1), 2, 0)

    @pl.when(n_far % 2 == 1)
    def _():
        far_scores(n_far - 1, 1, 1)
        visit_far(n_far - 1, 1, 1)

    out_t = jnp.concatenate([acc_ref[:, h * BS:(h + 1) * BS] / l_ref[:, h * BS:(h + 1) * BS] for h in range(G)],
                            axis=0)
    o_ref[0] = out_t.T.astype(o_ref.dtype)


def _moba(qm_t, km, vm_t, bias, rel_bias, B, S):
    BS, G, DH = MOBA_BLOCK, MOBA_GROUP, MOBA_DH
    W = G * DH
    n_grp = MOBA_HEADS // G
    assert S % BS == 0 and S >= 2 * BS
    NB = S // BS
    NBP = -(-NB // 8) * 8
    return pl.pallas_call(
        _moba_kernel,
        grid=(B, n_grp, NB),
        in_specs=[pl.BlockSpec(memory_space=pltpu.SMEM),
                  pl.BlockSpec((1, W, BS), lambda b, g, i: (b * NB + i, g, 0)),
                  pl.BlockSpec((1, S, W), lambda b, g, i: (b, 0, g)),
                  pl.BlockSpec((NB, W, BS), lambda b, g, i: (b, g, 0)),
                  pl.BlockSpec((1, 2, BS, G * BS), lambda b, g, i: (g, 0, 0, 0))],
        out_specs=pl.BlockSpec((1, BS, W), lambda b, g, i: (b, i, g)),
        out_shape=jax.ShapeDtypeStruct((B, S, MOBA_W), BF16),
        scratch_shapes=[pltpu.VMEM((W, G * BS), BF16), pltpu.VMEM((NBP, W), F32),
                        pltpu.VMEM((8, G * BS), jnp.int32),
                        pltpu.VMEM((1, G * BS), F32), pltpu.VMEM((1, G * BS), F32),
                        pltpu.VMEM((DH, G * BS), F32), pltpu.VMEM((2, 2 * BS, G * BS), F32)],
        compiler_params=_params("parallel", "parallel", "arbitrary"),
        name="moba_attention",
    )(rel_bias.astype(F32), qm_t, km.reshape(B, S, -1), vm_t, bias)


def _mem_kv_kernel(mem_ref, g_ref, w_ref, k_ref, v_ref):
    kv = jnp.dot(_rms(mem_ref[0], g_ref[...]).astype(BF16), w_ref[...], preferred_element_type=F32)
    k_ref[0] = kv[:, :MEM_W].astype(BF16)
    v_ref[0] = kv[:, MEM_W:].astype(BF16)


def _mem_kv(mem, g_mem, w_ckv):
    B, M, D = mem.shape
    spec = pl.BlockSpec((1, M, MEM_W), lambda b: (b, 0, 0))
    return pl.pallas_call(
        _mem_kv_kernel,
        grid=(B,),
        in_specs=[pl.BlockSpec((1, M, D), lambda b: (b, 0, 0)), _full((1, D)), _full((D, 2 * MEM_W))],
        out_specs=[spec, spec],
        out_shape=[jax.ShapeDtypeStruct((B, M, MEM_W), BF16)] * 2,
        compiler_params=_params("parallel"),
        name="memory_kv",
    )(mem, g_mem.reshape(1, D).astype(F32), w_ckv.astype(BF16))


INFO_W0, INFO_W1, INFO_E0, INFO_E1, INFO_R0, INFO_R1 = range(6)
ROUTER_GROUP_LANE0, ROUTER_EXPERT_LANE0 = 0, N_GROUPS


def _mix_kernel(x_ref, og_ref, om_ref, zg_ref, zm_ref, mk_ref, mv_ref, gc_ref, gm_ref,
                wpg, wpm, wout, wcq, wco, wr, br,
                x2_ref, info_ref, cnt_ref, base_ref):
    first = (pl.program_id(0) == 0) & (pl.program_id(1) == 0)

    @pl.when(first)
    def _():
        base_ref[...] = jnp.zeros_like(base_ref)

    def mm(a, w_ref):
        return jnp.dot(a.astype(BF16), w_ref[...], preferred_element_type=F32)

    n_rows = x_ref.shape[1]
    sub = n_rows // MIX_PARTS
    parts = [slice(p * sub, (p + 1) * sub) for p in range(MIX_PARTS)]

    merged = [jax.nn.sigmoid(zg_ref[0, rs, :]) * mm(og_ref[0, rs, :], wpg)
              + jax.nn.sigmoid(zm_ref[0, rs, :]) * mm(om_ref[0, rs, :], wpm) for rs in parts]
    x1 = [x_ref[0, rs, :] + mm(m, wout) for rs, m in zip(parts, merged)]

    qc = [mm(_rms(v, gc_ref[...]), wcq).astype(BF16) for v in x1]

    def mem_attention(q):
        heads = []
        for h in range(MEM_HEADS):
            cs = slice(h * MEM_DH, (h + 1) * MEM_DH)
            s = lax.dot_general(q[:, cs], mk_ref[0, :, cs], (((1,), (1,)), ((), ())),
                                preferred_element_type=F32) * (MEM_DH ** -0.5)
            p = jnp.exp(s - jnp.max(s, axis=-1, keepdims=True))
            o = jnp.dot(p.astype(BF16), mv_ref[0, :, cs], preferred_element_type=F32)
            heads.append(o / jnp.sum(p, axis=-1, keepdims=True))
        return jnp.concatenate(heads, axis=-1)

    attn = [mem_attention(q) for q in qc]
    x2 = [v + mm(a, wco) for v, a in zip(x1, attn)]
    for rs, v in zip(parts, x2):
        x2_ref[0, rs, :] = v

    logits = [mm(_rms(v, gm_ref[...]), wr) + br[...] for v in x2]
    lane = lax.broadcasted_iota(jnp.int32, (sub, LANES), 1)
    is_grp = lane < N_GROUPS
    e_id = lane - ROUTER_EXPERT_LANE0

    def route(lg):
        gl = jnp.where(is_grp, lg, NEG_INF)
        ge = jnp.exp(gl - jnp.max(gl, axis=-1, keepdims=True))
        g_prob = ge / jnp.sum(ge, axis=-1, keepdims=True)
        p_grp = jnp.max(g_prob, axis=-1, keepdims=True)
        grp = jnp.min(jnp.where((g_prob == p_grp) & is_grp, lane, LANES), axis=-1, keepdims=True)
        in_grp = (e_id >= grp * EXPERTS_PER_GROUP) & (e_id < (grp + 1) * EXPERTS_PER_GROUP)
        el = jnp.where(in_grp, lg, NEG_INF)
        ee = jnp.exp(el - jnp.max(el, axis=-1, keepdims=True))
        e_prob = jnp.where(in_grp, ee / jnp.sum(ee, axis=-1, keepdims=True), -1.0)
        p0 = jnp.max(e_prob, axis=-1, keepdims=True)
        e0 = jnp.min(jnp.where(e_prob == p0, e_id, LANES), axis=-1, keepdims=True)
        e_rest = jnp.where(e_id == e0, -1.0, e_prob)
        p1 = jnp.max(e_rest, axis=-1, keepdims=True)
        e1 = jnp.min(jnp.where(e_rest == p1, e_id, LANES), axis=-1, keepdims=True)
        return e0, e1, p_grp * p0 / (p0 + p1), p_grp * p1 / (p0 + p1)

    routed = [route(lg) for lg in logits]

    before = (lax.broadcasted_iota(jnp.int32, (sub, sub), 1)
              < lax.broadcasted_iota(jnp.int32, (sub, sub), 0)).astype(BF16)
    base = base_ref[...]
    for rs, (e0, e1, w0, w1) in zip(parts, routed):
        onehot = ((lane == e0) | (lane == e1)).astype(F32)
        seen = base + jnp.dot(before, onehot.astype(BF16), preferred_element_type=F32)
        r0 = jnp.sum(jnp.where(lane == e0, seen, 0.0), axis=-1, keepdims=True)
        r1 = jnp.sum(jnp.where(lane == e1, seen, 0.0), axis=-1, keepdims=True)
        base = base + jnp.sum(onehot, axis=0, keepdims=True)
        info = jnp.zeros((sub, LANES), F32)
        for ln, val in ((INFO_W0, w0), (INFO_W1, w1), (INFO_E0, e0.astype(F32)), (INFO_E1, e1.astype(F32)),
                        (INFO_R0, r0), (INFO_R1, r1)):
            info = jnp.where(lane == ln, val, info)
        info_ref[0, rs, :] = info
    base_ref[...] = base
    cnt_ref[...] = base


def _mix(x, o_g, o_m, z_g, z_m, mem_k, mem_v, g_cross, g_moe, w_proj_gla, w_proj_moba, w_out, w_cq, w_co,
         w_rg, b_rg, w_re, b_re):
    B, S, D = x.shape
    M = mem_k.shape[1]
    rows = min(MIX_ROWS, S)
    assert S % rows == 0
    pad = LANES - N_GROUPS - N_EXPERTS
    wr = jnp.pad(jnp.concatenate([w_rg, w_re], axis=1), ((0, 0), (0, pad))).astype(BF16)
    br = jnp.pad(jnp.concatenate([b_rg, b_re]), (0, pad)).reshape(1, LANES).astype(F32)
    weights = [w_proj_gla.astype(BF16), w_proj_moba.astype(BF16), w_out.astype(BF16), w_cq.astype(BF16),
               w_co.astype(BF16), wr, br]
    tile = lambda n: pl.BlockSpec((1, rows, n), lambda b, i: (b, i, 0))
    memspec = pl.BlockSpec((1, M, MEM_W), lambda b, i: (b, 0, 0))
    return pl.pallas_call(
        _mix_kernel,
        grid=(B, S // rows),
        in_specs=[tile(D), tile(GLA_V), tile(MOBA_W), tile(D), tile(D), memspec, memspec,
                  _full((1, D)), _full((1, D))] + [_full(w.shape) for w in weights],
        out_specs=[tile(D), tile(LANES), _full((1, LANES))],
        out_shape=[jax.ShapeDtypeStruct((B, S, D), F32), jax.ShapeDtypeStruct((B, S, LANES), F32),
                   jax.ShapeDtypeStruct((1, LANES), F32)],
        scratch_shapes=[pltpu.VMEM((1, LANES), F32)],
        compiler_params=_params("arbitrary", "arbitrary"),
        name="merge_memattn_router",
    )(x, o_g, o_m, z_g.reshape(B, S, D), z_m.reshape(B, S, D), mem_k, mem_v,
      g_cross.reshape(1, D).astype(F32), g_moe.reshape(1, D).astype(F32), *weights)


def _plan_kernel(cnt_ref, info_ref, dest_ref, blk_ref):
    rows = info_ref.shape[0]
    lane1 = lax.broadcasted_iota(jnp.int32, (1, LANES), 1)
    nblk = jnp.floor((cnt_ref[...] + (EXPERT_ROWS - 1)) * (1.0 / EXPERT_ROWS))
    nblk = jnp.where(lane1 < N_EXPERTS, nblk, 0.0)
    hi = jnp.floor(nblk * (1.0 / 256.0))
    lo = nblk - 256.0 * hi
    upto = (lax.broadcasted_iota(jnp.int32, (LANES, LANES), 0)
            <= lax.broadcasted_iota(jnp.int32, (LANES, LANES), 1)).astype(BF16)
    digits = jnp.concatenate([jnp.broadcast_to(hi, (8, LANES)), jnp.broadcast_to(lo, (8, LANES))], axis=0)
    sums = jnp.dot(digits.astype(BF16), upto, preferred_element_type=F32)
    pend = sums[0:1] * 256.0 + sums[8:9]
    pstart_rows = (pend - nblk) * EXPERT_ROWS

    info = info_ref[...]
    lane = lax.broadcasted_iota(jnp.int32, (rows, LANES), 1)

    def field(ln):
        return jnp.sum(jnp.where(lane == ln, info, 0.0), axis=-1, keepdims=True)

    def dest(e, r):
        return jnp.sum(jnp.where(lane == e.astype(jnp.int32), pstart_rows, 0.0), axis=-1, keepdims=True) + r

    d0 = dest(field(INFO_E0), field(INFO_R0))
    d1 = dest(field(INFO_E1), field(INFO_R1))
    cols = jnp.where(lane == 0, d0, jnp.where(lane == 1, d1, 0.0))
    move = dest_ref.shape[2]
    for c in range(dest_ref.shape[0]):
        dest_ref[c] = cols[c * move:(c + 1) * move, :].T[0:8, :].astype(jnp.int32)

    @pl.when(pl.program_id(0) == 0)
    def _():
        n = lax.broadcasted_iota(jnp.int32, (blk_ref.shape[0], LANES), 0).astype(F32)
        done = jnp.where((pend <= n) & (lane1 < N_EXPERTS), 1.0, 0.0)
        e = jnp.minimum(jnp.sum(done, axis=-1, keepdims=True), N_EXPERTS - 1.0)
        blk_ref[...] = e.astype(jnp.int32)


def _plan(counts, info2d, n_blk):
    T = info2d.shape[0]
    move = min(MOVE_ROWS, T)
    rows = min(PLAN_ROWS, T)
    assert T % rows == 0 and rows % move == 0
    n_blk_pad = -(-n_blk // 8) * 8
    return pl.pallas_call(
        _plan_kernel,
        grid=(T // rows,),
        in_specs=[_full((1, LANES)), pl.BlockSpec((rows, LANES), lambda i: (i, 0))],
        out_specs=[pl.BlockSpec((rows // move, 8, move), lambda i: (i, 0, 0)), _full((n_blk_pad, 1))],
        out_shape=[jax.ShapeDtypeStruct((T // move, 8, move), jnp.int32),
                   jax.ShapeDtypeStruct((n_blk_pad, 1), jnp.int32)],
        compiler_params=_params("arbitrary"),
        name="dispatch_plan",
    )(counts, info2d)


def _dispatch_kernel(dest_ref, x_ref, g_ref, xs_in_ref, xs_ref, hbuf_ref, sems):
    del xs_in_ref
    rows = x_ref.shape[0]
    i, n = pl.program_id(0), pl.num_programs(0)
    buf = i % 2
    hbuf_ref[buf] = _rms(x_ref[...], g_ref[...])

    def row_copy(b, r, dst):
        return pltpu.make_async_copy(hbuf_ref.at[b, pl.ds(r, 1)], xs_ref.at[pl.ds(dst, 1)], sems.at[b])

    def start(r, c):
        row_copy(buf, r, dest_ref[0, 0, r]).start()
        row_copy(buf, r, dest_ref[0, 1, r]).start()
        return c

    def wait_all(b):
        def wait(r, c):
            row_copy(b, r, 0).wait()
            row_copy(b, r, 0).wait()
            return c
        lax.fori_loop(0, rows, wait, 0, unroll=DMA_UNROLL)

    lax.fori_loop(0, rows, start, 0, unroll=DMA_UNROLL)

    @pl.when(i > 0)
    def _():
        wait_all(1 - buf)

    @pl.when(i == n - 1)
    def _():
        wait_all(buf)


def _dispatch(dest, x2d, g_moe, cap):
    T, D = x2d.shape
    rows = dest.shape[2]
    xs0 = jnp.zeros((cap, D), F32)
    return pl.pallas_call(
        _dispatch_kernel,
        grid=(T // rows,),
        in_specs=[pl.BlockSpec((1, 8, rows), lambda i: (i, 0, 0), memory_space=pltpu.SMEM),
                  pl.BlockSpec((rows, D), lambda i: (i, 0)), _full((1, D)),
                  pl.BlockSpec(memory_space=pl.ANY)],
        out_specs=pl.BlockSpec(memory_space=pl.ANY),
        out_shape=jax.ShapeDtypeStruct((cap, D), F32),
        scratch_shapes=[pltpu.VMEM((2, rows, D), F32), pltpu.SemaphoreType.DMA((2,))],
        input_output_aliases={3: 0},
        compiler_params=_params("arbitrary"),
        name="moe_dispatch",
    )(dest, x2d, g_moe.reshape(1, D).astype(F32), xs0)


def _expert_kernel(blk_e_ref, xs_ref, wg_ref, wu_ref, wd_ref, y_ref, wg_bf, wu_bf, wd_bf):
    n = pl.program_id(0)
    prev = blk_e_ref[jnp.maximum(n - 1, 0)]

    @pl.when((n == 0) | (blk_e_ref[n] != prev))
    def _():
        wg_bf[...] = wg_ref[0].astype(BF16)
        wu_bf[...] = wu_ref[0].astype(BF16)
        wd_bf[...] = wd_ref[0].astype(BF16)

    xb = xs_ref[...].astype(BF16)
    gate = jnp.dot(xb, wg_bf[...], preferred_element_type=F32)
    up = jnp.dot(xb, wu_bf[...], preferred_element_type=F32)
    hid = (gate * jax.nn.sigmoid(gate) * up).astype(BF16)
    y_ref[...] = jnp.dot(hid, wd_bf[...], preferred_element_type=F32)


def _experts(blk_e, xs, w_gate, w_up, w_down):
    cap, D = xs.shape
    DE = w_gate.shape[-1]
    n_blk = cap // EXPERT_ROWS
    rows_spec = pl.BlockSpec((EXPERT_ROWS, D), lambda n, e: (n, 0))
    return pl.pallas_call(
        _expert_kernel,
        grid_spec=pltpu.PrefetchScalarGridSpec(
            num_scalar_prefetch=1,
            grid=(n_blk,),
            in_specs=[rows_spec,
                      pl.BlockSpec((1, D, DE), lambda n, e: (e[n], 0, 0)),
                      pl.BlockSpec((1, D, DE), lambda n, e: (e[n], 0, 0)),
                      pl.BlockSpec((1, DE, D), lambda n, e: (e[n], 0, 0))],
            out_specs=rows_spec,
            scratch_shapes=[pltpu.VMEM((D, DE), BF16), pltpu.VMEM((D, DE), BF16), pltpu.VMEM((DE, D), BF16)]),
        out_shape=jax.ShapeDtypeStruct((cap, D), F32),
        compiler_params=_params("arbitrary"),
        name="moe_experts",
    )(blk_e, xs, w_gate, w_up, w_down)


def _combine_kernel(final_norm, dest_ref, dest_next_ref, x_ref, info_ref, g_ref, y_ref, o_ref, ybuf_ref, sems):
    rows = x_ref.shape[0]
    i, n = pl.program_id(0), pl.num_programs(0)
    buf = i % 2

    def row_copy(b, r, slot, src):
        return pltpu.make_async_copy(y_ref.at[pl.ds(src, 1)], ybuf_ref.at[b, slot, pl.ds(r, 1)], sems.at[b])

    def fetch(b, d_ref):
        def start(r, c):
            row_copy(b, r, 0, d_ref[0, 0, r]).start()
            row_copy(b, r, 1, d_ref[0, 1, r]).start()
            return c
        lax.fori_loop(0, rows, start, 0, unroll=DMA_UNROLL)

    @pl.when(i == 0)
    def _():
        fetch(0, dest_ref)

    @pl.when(i + 1 < n)
    def _():
        fetch(1 - buf, dest_next_ref)

    def wait(r, c):
        row_copy(buf, r, 0, 0).wait()
        row_copy(buf, r, 1, 0).wait()
        return c

    lax.fori_loop(0, rows, wait, 0, unroll=DMA_UNROLL)
    info = info_ref[...]
    w0 = info[:, INFO_W0:INFO_W0 + 1]
    w1 = info[:, INFO_W1:INFO_W1 + 1]
    out = x_ref[...] + (w0 * ybuf_ref[buf, 0] + w1 * ybuf_ref[buf, 1])
    o_ref[...] = _rms(out, g_ref[...]) if final_norm else out


def _combine(dest, x2d, info2d, g_final, y, final_norm):
    T, D = x2d.shape
    rows = dest.shape[2]
    n = T // rows
    return pl.pallas_call(
        functools.partial(_combine_kernel, final_norm),
        grid=(n,),
        in_specs=[pl.BlockSpec((1, 8, rows), lambda i: (i, 0, 0), memory_space=pltpu.SMEM),
                  pl.BlockSpec((1, 8, rows), lambda i: (jnp.minimum(i + 1, n - 1), 0, 0), memory_space=pltpu.SMEM),
                  pl.BlockSpec((rows, D), lambda i: (i, 0)), pl.BlockSpec((rows, LANES), lambda i: (i, 0)),
                  _full((1, D)), pl.BlockSpec(memory_space=pl.ANY)],
        out_specs=pl.BlockSpec((rows, D), lambda i: (i, 0)),
        out_shape=jax.ShapeDtypeStruct((T, D), F32),
        scratch_shapes=[pltpu.VMEM((2, 2, rows, D), F32), pltpu.SemaphoreType.DMA((2,))],
        compiler_params=_params("arbitrary"),
        name="moe_combine_final_norm",
    )(dest, dest, x2d, info2d, g_final.reshape(1, D).astype(F32), y)


def kernel(x, mem, g_mem, rel_bias, g_mix, w_in, w_alpha_up, b_alpha, g_gla_head, w_proj_gla, w_proj_moba,
           w_out, g_cross, w_cq, w_ckv, w_co, g_moe, w_router_group, b_router_group, w_router_expert,
           b_router_expert, w_exp_gate, w_exp_up, w_exp_down, g_final):
    B, S, D = x.shape
    T = B * S
    depth = g_mix.shape[0]
    n_assign = T * TOPK_IN_GROUP
    n_blk = -(-(n_assign + N_EXPERTS * (EXPERT_ROWS - 1)) // EXPERT_ROWS)
    cap = n_blk * EXPERT_ROWS

    mem_bias = _moba_bias(rel_bias)
    for l in range(depth):
        qk, v_g, r_g, la, q_m, k_m, v_m, z_g, z_m = _project(x.reshape(T, D), g_mix[l], w_in[l], w_alpha_up[l],
                                                             b_alpha[l])
        o_g = _gla(qk, la, v_g, r_g, g_gla_head[l], B, S)
        o_m = _moba(q_m, k_m, v_m, mem_bias, rel_bias, B, S)
        mem_k, mem_v = _mem_kv(mem, g_mem, w_ckv[l])
        x2, info, counts = _mix(x, o_g, o_m, z_g, z_m, mem_k, mem_v, g_cross[l], g_moe[l], w_proj_gla[l],
                                w_proj_moba[l], w_out[l], w_cq[l], w_co[l], w_router_group[l], b_router_group[l],
                                w_router_expert[l], b_router_expert[l])
        x2d, info2d = x2.reshape(T, D), info.reshape(T, LANES)
        dest, blk_e = _plan(counts, info2d, n_blk)
        xs = _dispatch(dest, x2d, g_moe[l], cap)
        y = _experts(blk_e[:n_blk, 0], xs, w_exp_gate[l], w_exp_up[l], w_exp_down[l])
        x = _combine(dest, x2d, info2d, g_final, y, final_norm=(l == depth - 1)).reshape(B, S, D)
    return x
```

```python
import functools
import math

import jax
import jax.numpy as jnp
from jax import lax
from jax.experimental import pallas as pl
from jax.experimental.pallas import tpu as pltpu

F32 = jnp.float32
BF16 = jnp.bfloat16
NEG_INF = float("-inf")

EPS = 1e-6
GLA_HEADS, GLA_DK, GLA_DV, GLA_LOWRANK, GLA_TAU, GLA_CHUNK = 4, 64, 128, 16, 16.0, 64
GLA_QK, GLA_V = GLA_HEADS * GLA_DK, GLA_HEADS * GLA_DV
MOBA_HEADS, MOBA_DH, MOBA_BLOCK, MOBA_TOPK = 8, 64, 256, 3
MOBA_W = MOBA_HEADS * MOBA_DH
LOG2E = math.log2(math.e)
MOBA_Q_SCALE = MOBA_DH ** -0.5 * LOG2E
REL_BUCKETS, REL_MAX_DIST = 32, 128
MEM_HEADS, MEM_DH = 4, 128
MEM_W = MEM_HEADS * MEM_DH
N_GROUPS, EXPERTS_PER_GROUP, TOPK_IN_GROUP = 4, 8, 2
N_EXPERTS = N_GROUPS * EXPERTS_PER_GROUP

LANES = 128
SUBLANES = 8
VMEM_LIMIT_BYTES = 56 * 1024 * 1024

PROJ_ROWS = 512
GLA_ROWS = 512
MOBA_GROUP = 4
MIX_ROWS = 512
MIX_PARTS = 2
EXPERT_ROWS = 256
MOVE_ROWS = 256
PLAN_ROWS = 2048
EXPERT_PARTS = 2


def _params(*semantics):
    return pltpu.CompilerParams(dimension_semantics=semantics, vmem_limit_bytes=VMEM_LIMIT_BYTES)


def _full(shape):
    return pl.BlockSpec(shape, lambda *_: (0,) * len(shape))


def _rms(x, g):
    return x * lax.rsqrt(jnp.mean(x * x, axis=-1, keepdims=True) + EPS) * g


def _for_each_row(rows, body):
    def trip(g, carry):
        for k in range(SUBLANES):
            body(g, k)
        return carry
    lax.fori_loop(0, rows // SUBLANES, trip, 0)


def _pack_bf16_pairs(x):
    n = x.shape[1] // 2
    bits = pltpu.bitcast(x.astype(BF16).astype(F32), jnp.uint32)
    return bits[:, n:] | (bits[:, :n] >> 16)


def _unpack_bf16_pairs(w):
    lo = pltpu.bitcast(w << 16, F32)
    hi = pltpu.bitcast(w & jnp.uint32(0xFFFF0000), F32)
    return jnp.concatenate([lo, hi], axis=1)


def _proj_kernel(x_ref, g_ref, w_qk, w_v, w_r, w_a, w_up, b_a, w_qm, w_km, w_vm, w_zg, w_zm,
                 o_qk, o_v, o_r, o_la, o_qm, o_km, o_vm, o_zg, o_zm):
    h = _rms(x_ref[...], g_ref[...]).astype(BF16)

    def mm(w_ref):
        return jnp.dot(h, w_ref[...], preferred_element_type=F32)

    o_qk[...] = mm(w_qk)
    o_v[...] = mm(w_v).astype(BF16)
    o_r[...] = mm(w_r)
    a_lr = mm(w_a).astype(BF16)
    pre = jnp.dot(a_lr, w_up[...], preferred_element_type=F32) + b_a[...]
    o_la[...] = jax.nn.log_sigmoid(pre) * (1.0 / GLA_TAU)

    def mm_t(wt_ref):
        return lax.dot_general(wt_ref[...], h, (((1,), (1,)), ((), ())), preferred_element_type=F32)

    def store_blocks(o_ref, val_t):
        for c in range(o_ref.shape[0]):
            o_ref[c] = val_t[:, c * MOBA_BLOCK:(c + 1) * MOBA_BLOCK]

    store_blocks(o_qm, (mm_t(w_qm) * MOBA_Q_SCALE).astype(BF16))
    o_km[...] = mm(w_km).astype(BF16)
    store_blocks(o_vm, mm_t(w_vm).astype(BF16))
    o_zg[...] = mm(w_zg)
    o_zm[...] = mm(w_zm)


def _project(x2d, g_mix, w_in, w_alpha_up, b_alpha):
    T, D = x2d.shape
    rows = min(PROJ_ROWS, T)
    assert T % rows == 0
    splits = (GLA_QK, GLA_QK, GLA_V, GLA_V, GLA_LOWRANK, MOBA_W, MOBA_W, MOBA_W, D, D)
    offs = [0]
    for s in splits:
        offs.append(offs[-1] + s)
    wb = w_in.astype(BF16)
    sec = lambda i, j: wb[:, offs[i]:offs[j]]
    w_a = jnp.pad(sec(4, 5), ((0, 0), (0, LANES - GLA_LOWRANK)))
    w_up = jnp.pad(w_alpha_up.astype(BF16), ((0, LANES - GLA_LOWRANK), (0, 0)))
    weights = [sec(0, 2), sec(2, 3), sec(3, 4), w_a, w_up, b_alpha.reshape(1, GLA_QK).astype(F32),
               sec(5, 6).T, sec(6, 7), sec(7, 8).T, sec(8, 9), sec(9, 10)]
    out_defs = [(2 * GLA_QK, F32, False), (GLA_V, BF16, False), (GLA_V, F32, False), (GLA_QK, F32, False),
                (MOBA_W, BF16, True), (MOBA_W, BF16, False), (MOBA_W, BF16, True), (D, F32, False),
                (D, F32, False)]
    BS = MOBA_BLOCK
    assert rows % BS == 0
    row_spec = lambda n: pl.BlockSpec((rows, n), lambda i: (i, 0))
    blk_spec = lambda n: pl.BlockSpec((rows // BS, n, BS), lambda i: (i, 0, 0))
    return pl.pallas_call(
        _proj_kernel,
        grid=(T // rows,),
        in_specs=[row_spec(D), _full((1, D))] + [_full(w.shape) for w in weights],
        out_specs=[blk_spec(n) if t else row_spec(n) for n, _, t in out_defs],
        out_shape=[jax.ShapeDtypeStruct((T // BS, n, BS) if t else (T, n), dt) for n, dt, t in out_defs],
        compiler_params=_params("parallel"),
        name="norm_in_proj",
    )(x2d, g_mix.reshape(1, D).astype(F32), *weights)


def _gla_kernel(qk_ref, la_ref, v_ref, r_ref, g_ref, o_ref, state_ref, obuf_ref):
    C, H, DK, DV = GLA_CHUNK, GLA_HEADS, GLA_DK, GLA_DV
    rows = qk_ref.shape[1]

    @pl.when(pl.program_id(1) == 0)
    def _():
        state_ref[...] = jnp.zeros_like(state_ref)

    tri = (lax.broadcasted_iota(jnp.int32, (C, C), 0) >= lax.broadcasted_iota(jnp.int32, (C, C), 1)).astype(BF16)
    lane_head = lax.broadcasted_iota(jnp.int32, (1, H * DK), 1) // DK
    head_masks = [(lane_head == h).astype(F32) for h in range(H)]
    stack_row = lax.broadcasted_iota(jnp.int32, (H * C, C), 0) % C
    stack_col = lax.broadcasted_iota(jnp.int32, (H * C, C), 1)
    causal = stack_col <= stack_row
    same_head = (lax.broadcasted_iota(jnp.int32, (H * DV, H * DK), 0) // DV
                 == lax.broadcasted_iota(jnp.int32, (H * DV, H * DK), 1) // DK)
    scale = DK ** -0.5

    def stack(m):
        return jnp.concatenate([m * head_masks[h] for h in range(H)], axis=0).astype(BF16)

    chunks = [slice(c * C, (c + 1) * C) for c in range(rows // C)]

    def cum_log_decay(sl):
        la = la_ref[0, sl, :]
        p1 = la.astype(BF16)
        r1 = la - p1.astype(F32)
        p2 = r1.astype(BF16)
        p3 = (r1 - p2.astype(F32)).astype(BF16)
        s3 = jnp.dot(tri, jnp.concatenate([p1, p2, p3], axis=1), preferred_element_type=F32)
        w = H * DK
        return (s3[:, 0:w] + s3[:, w:2 * w]) + s3[:, 2 * w:3 * w]

    b_all = [cum_log_decay(sl) for sl in chunks]

    qe_all, ke_all, kd_all, qb_all, decay_all = [], [], [], [], []
    for sl, b in zip(chunks, b_all):
        q = qk_ref[0, sl, 0:H * DK] * scale
        k = qk_ref[0, sl, H * DK:2 * H * DK]
        b_last = b[C - 1:C, :]
        b_mid = b[C // 2 - 1:C // 2, :]
        qe_all.append(stack(q * jnp.exp(b - b_mid)))
        ke_all.append((k * jnp.exp(b_mid - b)).astype(BF16))
        kd_all.append((k * jnp.exp(b_last - b)).astype(BF16))
        qb_all.append((q * jnp.exp(b)).astype(BF16))
        decay_all.append(jnp.exp(b_last))

    att_all = [jnp.where(causal, lax.dot_general(qe, ke, (((1,), (1,)), ((), ())), preferred_element_type=F32),
                         0.0).astype(BF16) for qe, ke in zip(qe_all, ke_all)]

    o_intra_all, kv_all = [], []
    for sl, att, kd in zip(chunks, att_all, kd_all):
        v = v_ref[0, sl, :]
        o_intra_all.append(jnp.concatenate(
            [jnp.dot(att[h * C:(h + 1) * C, :], v[:, h * DV:(h + 1) * DV], preferred_element_type=F32)
             for h in range(H)], axis=1))
        kv_t = lax.dot_general(v, kd, (((0,), (0,)), ((), ())), preferred_element_type=F32)
        kv_all.append(jnp.where(same_head, kv_t, 0.0))

    state_t = state_ref[...]
    for sl, qb, decay, kv_t, o_intra in zip(chunks, qb_all, decay_all, kv_all, o_intra_all):
        o_inter = lax.dot_general(qb, state_t.astype(BF16), (((1,), (1,)), ((), ())),
                                  preferred_element_type=F32)
        obuf_ref[sl, :] = o_intra + o_inter
        state_t = decay * state_t + kv_t
    state_ref[...] = state_t

    r = r_ref[0]
    for h in range(H):
        cs = slice(h * DV, (h + 1) * DV)
        y = _rms(obuf_ref[:, cs], g_ref[:, cs])
        rh = r[:, cs]
        o_ref[0, :, cs] = (y * (rh * jax.nn.sigmoid(rh))).astype(BF16)


def _gla(qk, la, v, r, g_head, B, S):
    rows = min(GLA_ROWS, S)
    assert S % rows == 0 and rows % GLA_CHUNK == 0
    spec = lambda n: pl.BlockSpec((1, rows, n), lambda b, i: (b, i, 0))
    return pl.pallas_call(
        _gla_kernel,
        grid=(B, S // rows),
        in_specs=[spec(2 * GLA_QK), spec(GLA_QK), spec(GLA_V), spec(GLA_V), _full((1, GLA_V))],
        out_specs=spec(GLA_V),
        out_shape=jax.ShapeDtypeStruct((B, S, GLA_V), BF16),
        scratch_shapes=[pltpu.VMEM((GLA_V, GLA_QK), F32), pltpu.VMEM((rows, GLA_V), F32)],
        compiler_params=_params("parallel", "arbitrary"),
        name="gla_chunked",
    )(qk.reshape(B, S, -1), la.reshape(B, S, -1), v.reshape(B, S, -1), r.reshape(B, S, -1),
      g_head.reshape(1, GLA_V).astype(F32))


def _t5_bucket(dist):
    n = jnp.maximum(dist, 0)
    max_exact = REL_BUCKETS // 2
    nf = jnp.maximum(n, 1).astype(F32)
    large = max_exact + (jnp.log(nf / max_exact) / math.log(REL_MAX_DIST / max_exact)
                         * (REL_BUCKETS - max_exact)).astype(jnp.int32)
    large = jnp.minimum(large, REL_BUCKETS - 1)
    return jnp.where(n < max_exact, n, large)


def _moba_bias_kernel(rb_ref, o_ref):
    BS, G = MOBA_BLOCK, MOBA_GROUP
    grp, kind = pl.program_id(0), pl.program_id(1)
    d = (lax.broadcasted_iota(jnp.int32, (BS, BS), 1) - lax.broadcasted_iota(jnp.int32, (BS, BS), 0)
         + kind * BS)
    bucket = _t5_bucket(d)
    for h in range(G):
        val = jnp.zeros((BS, BS), F32)
        for bkt in range(REL_BUCKETS):
            val = jnp.where(bucket == bkt, rb_ref[bkt, grp * G + h] * LOG2E, val)
        o_ref[0, 0, :, h * BS:(h + 1) * BS] = jnp.where(d >= 0, val, NEG_INF)


def _moba_bias(rel_bias):
    BS, G = MOBA_BLOCK, MOBA_GROUP
    n_grp = MOBA_HEADS // G
    return pl.pallas_call(
        _moba_bias_kernel,
        grid=(n_grp, 2),
        in_specs=[pl.BlockSpec(memory_space=pltpu.SMEM)],
        out_specs=pl.BlockSpec((1, 1, BS, G * BS), lambda g, k: (g, k, 0, 0)),
        out_shape=jax.ShapeDtypeStruct((n_grp, 2, BS, G * BS), F32),
        compiler_params=_params("parallel", "parallel"),
        name="moba_bias_tables",
    )(rel_bias.astype(F32))


def _moba_kernel(rb_ref, q_ref, k_ref, v_ref, bias_ref, o_ref,
                 qs_ref, kmean_ref, sel_ref, m_ref, l_ref, acc_ref, sbuf_ref):
    BS, G, DH = MOBA_BLOCK, MOBA_GROUP, MOBA_DH
    NBP = kmean_ref.shape[0]
    grp, i = pl.program_id(1), pl.program_id(2)

    @pl.when(i == 0)
    def _():
        S = k_ref.shape[1]
        blk_of_key = lax.broadcasted_iota(jnp.int32, (NBP, S), 1) // BS
        ind = (blk_of_key == lax.broadcasted_iota(jnp.int32, (NBP, S), 0)).astype(BF16)
        kmean_ref[...] = jnp.dot(ind, k_ref[0], preferred_element_type=F32) * (1.0 / BS)

    qt = q_ref[0]
    sub_head = lax.broadcasted_iota(jnp.int32, (G * DH, 1), 0) // DH
    for h in range(G):
        qs_ref[:, h * BS:(h + 1) * BS] = jnp.where(sub_head == h, qt, jnp.zeros_like(qt))

    gate = jnp.dot(kmean_ref[...].astype(BF16), qs_ref[...], preferred_element_type=F32)
    blk = lax.broadcasted_iota(jnp.int32, gate.shape, 0)
    gate = jnp.where(blk < i, gate, NEG_INF)
    for t in range(MOBA_TOPK):
        mx = jnp.max(gate, axis=0, keepdims=True)
        hit = (gate == mx) & (mx > NEG_INF)
        idx = jnp.min(jnp.where(hit, blk, NBP), axis=0, keepdims=True)
        sel_ref[t:t + 1, :] = idx
        gate = jnp.where(blk == idx, NEG_INF, gate)

    def mask_row(j):
        hit = (sel_ref[0:1, :] == j) | (sel_ref[1:2, :] == j) | (sel_ref[2:3, :] == j)
        return jnp.where(hit, 0.0, NEG_INF)

    def with_ones(vt):
        return jnp.concatenate([vt, jnp.ones((8, vt.shape[1]), BF16)], axis=0)

    def far_scores(j0, n, slot):
        kj = k_ref[0, pl.ds(pl.multiple_of(j0 * BS, BS), n * BS), :]
        for h in range(G):
            cs = slice(h * BS, (h + 1) * BS)
            sbuf_ref[slot, 0:n * BS, cs] = jnp.dot(kj, qs_ref[:, cs], preferred_element_type=F32)

    j_prev = jnp.maximum(i - 1, 0)
    k_own = k_ref[0, pl.ds(pl.multiple_of(i * BS, BS), BS), :]
    k_prev = k_ref[0, pl.ds(pl.multiple_of(j_prev * BS, BS), BS), :]
    vt_near = jnp.concatenate([v_ref[i], v_ref[j_prev]], axis=1)
    prev_mask = mask_row(i - 1)
    for h in range(G):
        cs = slice(h * BS, (h + 1) * BS)
        sbuf_ref[1, 0:BS, cs] = jnp.dot(k_own, qs_ref[:, cs], preferred_element_type=F32)
        sbuf_ref[1, BS:2 * BS, cs] = jnp.dot(k_prev, qs_ref[:, cs], preferred_element_type=F32)
    far_scores(0, 2, 0)
    m_out, l_out, acc_out = [], [], []
    for h in range(G):
        cs = slice(h * BS, (h + 1) * BS)
        s_own = sbuf_ref[1, 0:BS, cs] + bias_ref[0, 0, :, cs]
        s_prev = sbuf_ref[1, BS:2 * BS, cs] + (bias_ref[0, 1, :, cs] + prev_mask[:, cs])
        m0 = jnp.maximum(jnp.max(s_own, axis=0, keepdims=True), jnp.max(s_prev, axis=0, keepdims=True))
        pb = jnp.concatenate([jnp.exp2(s_own - m0).astype(BF16), jnp.exp2(s_prev - m0).astype(BF16)], axis=0)
        pv = jnp.dot(with_ones(vt_near[h * DH:(h + 1) * DH, :]), pb, preferred_element_type=F32)
        m_out.append(m0)
        l_out.append(pv[DH:DH + 1, :])
        acc_out.append(pv[0:DH, :])
    m_ref[...] = jnp.concatenate(m_out, axis=1)
    l_ref[...] = jnp.concatenate(l_out, axis=1)
    acc_ref[...] = jnp.concatenate(acc_out, axis=1)

    lane_head = lax.broadcasted_iota(jnp.int32, (1, G * BS), 1) // BS
    far_bias = jnp.zeros((1, G * BS), F32)
    for h in range(G):
        far_bias = jnp.where(lane_head == h, rb_ref[REL_BUCKETS - 1, grp * G + h] * LOG2E, far_bias)

    def visit_far(j0, n, slot):
        vjt = jnp.concatenate([v_ref[j0 + t] for t in range(n)], axis=1)
        addend = [far_bias + mask_row(j0 + t) for t in range(n)]
        m_old, l_old, acc_old = m_ref[...], l_ref[...], acc_ref[...]
        m_out, l_out, acc_out = [], [], []
        for h in range(G):
            cs = slice(h * BS, (h + 1) * BS)
            s = sbuf_ref[slot, 0:n * BS, cs]
            mx = jnp.max(s[0:BS], axis=0, keepdims=True) + addend[0][:, cs]
            for t in range(1, n):
                mx = jnp.maximum(mx, jnp.max(s[t * BS:(t + 1) * BS], axis=0, keepdims=True) + addend[t][:, cs])
            m_new = jnp.maximum(m_old[:, cs], mx)
            pb = jnp.concatenate([jnp.exp2(s[t * BS:(t + 1) * BS] - (m_new - addend[t][:, cs])).astype(BF16)
                                  for t in range(n)], axis=0)
            pv = jnp.dot(with_ones(vjt[h * DH:(h + 1) * DH, :]), pb, preferred_element_type=F32)
            alpha = jnp.exp2(m_old[:, cs] - m_new)
            m_out.append(m_new)
            l_out.append(alpha * l_old[:, cs] + pv[DH:DH + 1, :])
            acc_out.append(alpha * acc_old[:, cs] + pv[0:DH, :])
        m_ref[...] = jnp.concatenate(m_out, axis=1)
        l_ref[...] = jnp.concatenate(l_out, axis=1)
        acc_ref[...] = jnp.concatenate(acc_out, axis=1)

    n_far = jnp.maximum(i - 1, 0)
    n_pairs = lax.shift_right_logical(n_far, 1)

    def far_quad(u, carry):
        far_scores(4 * u + 2, 2, 1)
        visit_far(4 * u, 2, 0)
        far_scores(2 * jnp.minimum(2 * u + 2, n_pairs - 1), 2, 0)
        visit_far(4 * u + 2, 2, 1)
        return carry

    lax.fori_loop(0, lax.shift_right_logical(n_pairs, 1), far_quad, 0)

    @pl.when(n_pairs % 2 == 1)
    def _():
        visit_far(2 * (n_pairs - 1), 2, 0)

    @pl.when(n_far % 2 == 1)
    def _():
        far_scores(n_far - 1, 1, 1)
        visit_far(n_far - 1, 1, 1)

    out_t = jnp.concatenate([acc_ref[:, h * BS:(h + 1) * BS] / l_ref[:, h * BS:(h + 1) * BS] for h in range(G)],
                            axis=0)
    o_ref[0] = out_t.T.astype(o_ref.dtype)


def _moba(qm_t, km, vm_t, bias, rel_bias, B, S):
    BS, G, DH = MOBA_BLOCK, MOBA_GROUP, MOBA_DH
    W = G * DH
    n_grp = MOBA_HEADS // G
    assert S % BS == 0 and S >= 2 * BS
    NB = S // BS
    NBP = -(-NB // 8) * 8
    return pl.pallas_call(
        _moba_kernel,
        grid=(B, n_grp, NB),
        in_specs=[pl.BlockSpec(memory_space=pltpu.SMEM),
                  pl.BlockSpec((1, W, BS), lambda b, g, i: (b * NB + i, g, 0)),
                  pl.BlockSpec((1, S, W), lambda b, g, i: (b, 0, g)),
                  pl.BlockSpec((NB, W, BS), lambda b, g, i: (b, g, 0)),
                  pl.BlockSpec((1, 2, BS, G * BS), lambda b, g, i: (g, 0, 0, 0))],
        out_specs=pl.BlockSpec((1, BS, W), lambda b, g, i: (b, i, g)),
        out_shape=jax.ShapeDtypeStruct((B, S, MOBA_W), BF16),
        scratch_shapes=[pltpu.VMEM((W, G * BS), BF16), pltpu.VMEM((NBP, W), F32),
                        pltpu.VMEM((8, G * BS), jnp.int32),
                        pltpu.VMEM((1, G * BS), F32), pltpu.VMEM((1, G * BS), F32),
                        pltpu.VMEM((DH, G * BS), F32), pltpu.VMEM((2, 2 * BS, G * BS), F32)],
        compiler_params=_params("parallel", "parallel", "arbitrary"),
        name="moba_attention",
    )(rel_bias.astype(F32), qm_t, km.reshape(B, S, -1), vm_t, bias)


def _mem_kv_kernel(mem_ref, g_ref, w_ref, k_ref, v_ref):
    kv = jnp.dot(_rms(mem_ref[0], g_ref[...]).astype(BF16), w_ref[...], preferred_element_type=F32)
    k_ref[0] = kv[:, :MEM_W].astype(BF16)
    v_ref[0] = kv[:, MEM_W:].astype(BF16)


def _mem_kv(mem, g_mem, w_ckv):
    B, M, D = mem.shape
    spec = pl.BlockSpec((1, M, MEM_W), lambda b: (b, 0, 0))
    return pl.pallas_call(
        _mem_kv_kernel,
        grid=(B,),
        in_specs=[pl.BlockSpec((1, M, D), lambda b: (b, 0, 0)), _full((1, D)), _full((D, 2 * MEM_W))],
        out_specs=[spec, spec],
        out_shape=[jax.ShapeDtypeStruct((B, M, MEM_W), BF16)] * 2,
        compiler_params=_params("parallel"),
        name="memory_kv",
    )(mem, g_mem.reshape(1, D).astype(F32), w_ckv.astype(BF16))


INFO_W0, INFO_W1, INFO_E0, INFO_E1, INFO_R0, INFO_R1 = range(6)
ROUTER_GROUP_LANE0, ROUTER_EXPERT_LANE0 = 0, N_GROUPS


def _mix_kernel(x_ref, og_ref, om_ref, zg_ref, zm_ref, mk_ref, mv_ref, gc_ref, gm_ref,
                wpg, wpm, wout, wcq, wco, wr, br,
                x2_ref, info_ref, cnt_ref, base_ref):
    first = (pl.program_id(0) == 0) & (pl.program_id(1) == 0)

    @pl.when(first)
    def _():
        base_ref[...] = jnp.zeros_like(base_ref)

    def mm(a, w_ref):
        return jnp.dot(a.astype(BF16), w_ref[...], preferred_element_type=F32)

    n_rows = x_ref.shape[1]
    sub = n_rows // MIX_PARTS
    parts = [slice(p * sub, (p + 1) * sub) for p in range(MIX_PARTS)]

    merged = [jax.nn.sigmoid(zg_ref[0, rs, :]) * mm(og_ref[0, rs, :], wpg)
              + jax.nn.sigmoid(zm_ref[0, rs, :]) * mm(om_ref[0, rs, :], wpm) for rs in parts]
    x1 = [x_ref[0, rs, :] + mm(m, wout) for rs, m in zip(parts, merged)]

    qc = [mm(_rms(v, gc_ref[...]), wcq).astype(BF16) for v in x1]

    def mem_attention(q):
        heads = []
        for h in range(MEM_HEADS):
            cs = slice(h * MEM_DH, (h + 1) * MEM_DH)
            s = lax.dot_general(q[:, cs], mk_ref[0, :, cs], (((1,), (1,)), ((), ())),
                                preferred_element_type=F32) * (MEM_DH ** -0.5)
            p = jnp.exp(s - jnp.max(s, axis=-1, keepdims=True))
            o = jnp.dot(p.astype(BF16), mv_ref[0, :, cs], preferred_element_type=F32)
            heads.append(o / jnp.sum(p, axis=-1, keepdims=True))
        return jnp.concatenate(heads, axis=-1)

    attn = [mem_attention(q) for q in qc]
    x2 = [v + mm(a, wco) for v, a in zip(x1, attn)]
    for rs, v in zip(parts, x2):
        x2_ref[0, rs, :] = v

    logits = [mm(_rms(v, gm_ref[...]), wr) + br[...] for v in x2]
    lane = lax.broadcasted_iota(jnp.int32, (sub, LANES), 1)
    is_grp = lane < N_GROUPS
    e_id = lane - ROUTER_EXPERT_LANE0

    def route(lg):
        gl = jnp.where(is_grp, lg, NEG_INF)
        ge = jnp.exp(gl - jnp.max(gl, axis=-1, keepdims=True))
        g_prob = ge / jnp.sum(ge, axis=-1, keepdims=True)
        p_grp = jnp.max(g_prob, axis=-1, keepdims=True)
        grp = jnp.min(jnp.where((g_prob == p_grp) & is_grp, lane, LANES), axis=-1, keepdims=True)
        in_grp = (e_id >= grp * EXPERTS_PER_GROUP) & (e_id < (grp + 1) * EXPERTS_PER_GROUP)
        el = jnp.where(in_grp, lg, NEG_INF)
        ee = jnp.exp(el - jnp.max(el, axis=-1, keepdims=True))
        e_prob = jnp.where(in_grp, ee / jnp.sum(ee, axis=-1, keepdims=True), -1.0)
        p0 = jnp.max(e_prob, axis=-1, keepdims=True)
        e0 = jnp.min(jnp.where(e_prob == p0, e_id, LANES), axis=-1, keepdims=True)
        e_rest = jnp.where(e_id == e0, -1.0, e_prob)
        p1 = jnp.max(e_rest, axis=-1, keepdims=True)
        e1 = jnp.min(jnp.where(e_rest == p1, e_id, LANES), axis=-1, keepdims=True)
        return e0, e1, p_grp * p0 / (p0 + p1), p_grp * p1 / (p0 + p1)

    routed = [route(lg) for lg in logits]

    before = (lax.broadcasted_iota(jnp.int32, (sub, sub), 1)
              < lax.broadcasted_iota(jnp.int32, (sub, sub), 0)).astype(BF16)
    base = base_ref[...]
    for rs, (e0, e1, w0, w1) in zip(parts, routed):
        onehot = ((lane == e0) | (lane == e1)).astype(F32)
        seen = base + jnp.dot(before, onehot.astype(BF16), preferred_element_type=F32)
        r0 = jnp.sum(jnp.where(lane == e0, seen, 0.0), axis=-1, keepdims=True)
        r1 = jnp.sum(jnp.where(lane == e1, seen, 0.0), axis=-1, keepdims=True)
        base = base + jnp.sum(onehot, axis=0, keepdims=True)
        info = jnp.zeros((sub, LANES), F32)
        for ln, val in ((INFO_W0, w0), (INFO_W1, w1), (INFO_E0, e0.astype(F32)), (INFO_E1, e1.astype(F32)),
                        (INFO_R0, r0), (INFO_R1, r1)):
            info = jnp.where(lane == ln, val, info)
        info_ref[0, rs, :] = info
    base_ref[...] = base
    cnt_ref[...] = base


def _mix(x, o_g, o_m, z_g, z_m, mem_k, mem_v, g_cross, g_moe, w_proj_gla, w_proj_moba, w_out, w_cq, w_co,
         w_rg, b_rg, w_re, b_re):
    B, S, D = x.shape
    M = mem_k.shape[1]
    rows = min(MIX_ROWS, S)
    assert S % rows == 0
    pad = LANES - N_GROUPS - N_EXPERTS
    wr = jnp.pad(jnp.concatenate([w_rg, w_re], axis=1), ((0, 0), (0, pad))).astype(BF16)
    br = jnp.pad(jnp.concatenate([b_rg, b_re]), (0, pad)).reshape(1, LANES).astype(F32)
    weights = [w_proj_gla.astype(BF16), w_proj_moba.astype(BF16), w_out.astype(BF16), w_cq.astype(BF16),
               w_co.astype(BF16), wr, br]
    tile = lambda n: pl.BlockSpec((1, rows, n), lambda b, i: (b, i, 0))
    memspec = pl.BlockSpec((1, M, MEM_W), lambda b, i: (b, 0, 0))
    return pl.pallas_call(
        _mix_kernel,
        grid=(B, S // rows),
        in_specs=[tile(D), tile(GLA_V), tile(MOBA_W), tile(D), tile(D), memspec, memspec,
                  _full((1, D)), _full((1, D))] + [_full(w.shape) for w in weights],
        out_specs=[tile(D), tile(LANES), _full((1, LANES))],
        out_shape=[jax.ShapeDtypeStruct((B, S, D), F32), jax.ShapeDtypeStruct((B, S, LANES), F32),
                   jax.ShapeDtypeStruct((1, LANES), F32)],
        scratch_shapes=[pltpu.VMEM((1, LANES), F32)],
        compiler_params=_params("arbitrary", "arbitrary"),
        name="merge_memattn_router",
    )(x, o_g, o_m, z_g.reshape(B, S, D), z_m.reshape(B, S, D), mem_k, mem_v,
      g_cross.reshape(1, D).astype(F32), g_moe.reshape(1, D).astype(F32), *weights)


def _plan_kernel(cnt_ref, info_ref, dest_ref, blk_ref):
    rows = info_ref.shape[0]
    lane1 = lax.broadcasted_iota(jnp.int32, (1, LANES), 1)
    nblk = jnp.floor((cnt_ref[...] + (EXPERT_ROWS - 1)) * (1.0 / EXPERT_ROWS))
    nblk = jnp.where(lane1 < N_EXPERTS, nblk, 0.0)
    hi = jnp.floor(nblk * (1.0 / 256.0))
    lo = nblk - 256.0 * hi
    upto = (lax.broadcasted_iota(jnp.int32, (LANES, LANES), 0)
            <= lax.broadcasted_iota(jnp.int32, (LANES, LANES), 1)).astype(BF16)
    digits = jnp.concatenate([jnp.broadcast_to(hi, (8, LANES)), jnp.broadcast_to(lo, (8, LANES))], axis=0)
    sums = jnp.dot(digits.astype(BF16), upto, preferred_element_type=F32)
    pend = sums[0:1] * 256.0 + sums[8:9]
    pstart_rows = (pend - nblk) * EXPERT_ROWS

    info = info_ref[...]
    lane = lax.broadcasted_iota(jnp.int32, (rows, LANES), 1)

    def field(ln):
        return jnp.sum(jnp.where(lane == ln, info, 0.0), axis=-1, keepdims=True)

    def dest(e, r):
        return jnp.sum(jnp.where(lane == e.astype(jnp.int32), pstart_rows, 0.0), axis=-1, keepdims=True) + r

    d0 = dest(field(INFO_E0), field(INFO_R0))
    d1 = dest(field(INFO_E1), field(INFO_R1))
    cols = jnp.where(lane == 0, d0, jnp.where(lane == 1, d1, 0.0))
    move = dest_ref.shape[2]
    for c in range(dest_ref.shape[0]):
        dest_ref[c] = cols[c * move:(c + 1) * move, :].T[0:8, :].astype(jnp.int32)

    @pl.when(pl.program_id(0) == 0)
    def _():
        n = lax.broadcasted_iota(jnp.int32, (blk_ref.shape[0], LANES), 0).astype(F32)
        done = jnp.where((pend <= n) & (lane1 < N_EXPERTS), 1.0, 0.0)
        e = jnp.minimum(jnp.sum(done, axis=-1, keepdims=True), N_EXPERTS - 1.0)
        blk_ref[...] = e.astype(jnp.int32)


def _plan(counts, info2d, n_blk):
    T = info2d.shape[0]
    move = min(MOVE_ROWS, T)
    rows = min(PLAN_ROWS, T)
    assert T % rows == 0 and rows % move == 0
    n_blk_pad = -(-n_blk // 8) * 8
    return pl.pallas_call(
        _plan_kernel,
        grid=(T // rows,),
        in_specs=[_full((1, LANES)), pl.BlockSpec((rows, LANES), lambda i: (i, 0))],
        out_specs=[pl.BlockSpec((rows // move, 8, move), lambda i: (i, 0, 0)), _full((n_blk_pad, 1))],
        out_shape=[jax.ShapeDtypeStruct((T // move, 8, move), jnp.int32),
                   jax.ShapeDtypeStruct((n_blk_pad, 1), jnp.int32)],
        compiler_params=_params("arbitrary"),
        name="dispatch_plan",
    )(counts, info2d)


def _dispatch_kernel(dest_ref, x_ref, g_ref, xs_in_ref, xs_ref, hbuf_ref, sems):
    del xs_in_ref
    rows, D = x_ref.shape
    i, n = pl.program_id(0), pl.num_programs(0)
    buf = i % 2
    packed = _pack_bf16_pairs(_rms(x_ref[...], g_ref[...]))
    hbuf_ref[buf] = packed.reshape(rows // SUBLANES, SUBLANES, D // 2)

    def row_copy(b, g, k, dst):
        return pltpu.make_async_copy(hbuf_ref.at[b, g, pl.ds(k, 1)], xs_ref.at[pl.ds(dst, 1)], sems.at[b])

    def start(g, k):
        r = g * SUBLANES + k
        row_copy(buf, g, k, dest_ref[0, 0, r]).start()
        row_copy(buf, g, k, dest_ref[0, 1, r]).start()

    def wait_all(b):
        def wait(g, k):
            row_copy(b, g, k, 0).wait()
            row_copy(b, g, k, 0).wait()
        _for_each_row(rows, wait)

    _for_each_row(rows, start)

    @pl.when(i > 0)
    def _():
        wait_all(1 - buf)

    @pl.when(i == n - 1)
    def _():
        wait_all(buf)


def _dispatch(dest, x2d, g_moe, cap):
    T, D = x2d.shape
    rows = dest.shape[2]
    xs0 = jnp.zeros((cap, D // 2), jnp.uint32)
    return pl.pallas_call(
        _dispatch_kernel,
        grid=(T // rows,),
        in_specs=[pl.BlockSpec((1, 8, rows), lambda i: (i, 0, 0), memory_space=pltpu.SMEM),
                  pl.BlockSpec((rows, D), lambda i: (i, 0)), _full((1, D)),
                  pl.BlockSpec(memory_space=pl.ANY)],
        out_specs=pl.BlockSpec(memory_space=pl.ANY),
        out_shape=jax.ShapeDtypeStruct((cap, D // 2), jnp.uint32),
        scratch_shapes=[pltpu.VMEM((2, rows // SUBLANES, SUBLANES, D // 2), jnp.uint32),
                        pltpu.SemaphoreType.DMA((2,))],
        input_output_aliases={3: 0},
        compiler_params=_params("arbitrary"),
        name="moe_dispatch",
    )(dest, x2d, g_moe.reshape(1, D).astype(F32), xs0)


def _expert_kernel(blk_e_ref, xs_ref, wg_ref, wu_ref, wd_ref, y_ref, wg_bf, wu_bf, wd_bf):
    n = pl.program_id(0)
    prev = blk_e_ref[jnp.maximum(n - 1, 0)]

    @pl.when((n == 0) | (blk_e_ref[n] != prev))
    def _():
        wg_bf[...] = wg_ref[0].astype(BF16)
        wu_bf[...] = wu_ref[0].astype(BF16)
        wd_bf[...] = wd_ref[0].astype(BF16)

    sub = xs_ref.shape[0] // EXPERT_PARTS
    parts = [slice(p * sub, (p + 1) * sub) for p in range(EXPERT_PARTS)]
    xb = [_unpack_bf16_pairs(xs_ref[rs, :]).astype(BF16) for rs in parts]
    gate = [jnp.dot(v, wg_bf[...], preferred_element_type=F32) for v in xb]
    up = [jnp.dot(v, wu_bf[...], preferred_element_type=F32) for v in xb]
    hid = [(g * jax.nn.sigmoid(g) * u).astype(BF16) for g, u in zip(gate, up)]
    for rs, hv in zip(parts, hid):
        y_ref[rs, :] = _pack_bf16_pairs(jnp.dot(hv, wd_bf[...], preferred_element_type=F32))


def _experts(blk_e, xs, w_gate, w_up, w_down):
    cap = xs.shape[0]
    _, D, DE = w_gate.shape
    n_blk = cap // EXPERT_ROWS
    rows_spec = pl.BlockSpec((EXPERT_ROWS, D // 2), lambda n, e: (n, 0))
    return pl.pallas_call(
        _expert_kernel,
        grid_spec=pltpu.PrefetchScalarGridSpec(
            num_scalar_prefetch=1,
            grid=(n_blk,),
            in_specs=[rows_spec,
                      pl.BlockSpec((1, D, DE), lambda n, e: (e[n], 0, 0)),
                      pl.BlockSpec((1, D, DE), lambda n, e: (e[n], 0, 0)),
                      pl.BlockSpec((1, DE, D), lambda n, e: (e[n], 0, 0))],
            out_specs=rows_spec,
            scratch_shapes=[pltpu.VMEM((D, DE), BF16), pltpu.VMEM((D, DE), BF16), pltpu.VMEM((DE, D), BF16)]),
        out_shape=jax.ShapeDtypeStruct((cap, D // 2), jnp.uint32),
        compiler_params=_params("arbitrary"),
        name="moe_experts",
    )(blk_e, xs, w_gate, w_up, w_down)


def _combine_kernel(final_norm, dest_ref, dest_next_ref, x_ref, info_ref, g_ref, y_ref, o_ref, ybuf_ref, sems):
    rows, D = x_ref.shape
    i, n = pl.program_id(0), pl.num_programs(0)
    buf = i % 2

    def row_copy(b, g, k, slot, src):
        return pltpu.make_async_copy(y_ref.at[pl.ds(src, 1)], ybuf_ref.at[b, slot, g, pl.ds(k, 1)], sems.at[b])

    def fetch(b, d_ref):
        def start(g, k):
            r = g * SUBLANES + k
            row_copy(b, g, k, 0, d_ref[0, 0, r]).start()
            row_copy(b, g, k, 1, d_ref[0, 1, r]).start()
        _for_each_row(rows, start)

    @pl.when(i == 0)
    def _():
        fetch(0, dest_ref)

    @pl.when(i + 1 < n)
    def _():
        fetch(1 - buf, dest_next_ref)

    def wait(g, k):
        row_copy(buf, g, k, 0, 0).wait()
        row_copy(buf, g, k, 1, 0).wait()

    _for_each_row(rows, wait)
    info = info_ref[...]
    w0 = info[:, INFO_W0:INFO_W0 + 1]
    w1 = info[:, INFO_W1:INFO_W1 + 1]
    y0 = _unpack_bf16_pairs(ybuf_ref[buf, 0].reshape(rows, D // 2))
    y1 = _unpack_bf16_pairs(ybuf_ref[buf, 1].reshape(rows, D // 2))
    out = x_ref[...] + (w0 * y0 + w1 * y1)
    o_ref[...] = _rms(out, g_ref[...]) if final_norm else out


def _combine(dest, x2d, info2d, g_final, y, final_norm):
    T, D = x2d.shape
    rows = dest.shape[2]
    n = T // rows
    return pl.pallas_call(
        functools.partial(_combine_kernel, final_norm),
        grid=(n,),
        in_specs=[pl.BlockSpec((1, 8, rows), lambda i: (i, 0, 0), memory_space=pltpu.SMEM),
                  pl.BlockSpec((1, 8, rows), lambda i: (jnp.minimum(i + 1, n - 1), 0, 0), memory_space=pltpu.SMEM),
                  pl.BlockSpec((rows, D), lambda i: (i, 0)), pl.BlockSpec((rows, LANES), lambda i: (i, 0)),
                  _full((1, D)), pl.BlockSpec(memory_space=pl.ANY)],
        out_specs=pl.BlockSpec((rows, D), lambda i: (i, 0)),
        out_shape=jax.ShapeDtypeStruct((T, D), F32),
        scratch_shapes=[pltpu.VMEM((2, 2, rows // SUBLANES, SUBLANES, D // 2), jnp.uint32),
                        pltpu.SemaphoreType.DMA((2,))],
        compiler_params=_params("arbitrary"),
        name="moe_combine_final_norm",
    )(dest, dest, x2d, info2d, g_final.reshape(1, D).astype(F32), y)


def kernel(x, mem, g_mem, rel_bias, g_mix, w_in, w_alpha_up, b_alpha, g_gla_head, w_proj_gla, w_proj_moba,
           w_out, g_cross, w_cq, w_ckv, w_co, g_moe, w_router_group, b_router_group, w_router_expert,
           b_router_expert, w_exp_gate, w_exp_up, w_exp_down, g_final):
    B, S, D = x.shape
    T = B * S
    depth = g_mix.shape[0]
    n_assign = T * TOPK_IN_GROUP
    n_blk = -(-(n_assign + N_EXPERTS * (EXPERT_ROWS - 1)) // EXPERT_ROWS)
    cap = n_blk * EXPERT_ROWS

    mem_bias = _moba_bias(rel_bias)
    for l in range(depth):
        qk, v_g, r_g, la, q_m, k_m, v_m, z_g, z_m = _project(x.reshape(T, D), g_mix[l], w_in[l], w_alpha_up[l],
                                                             b_alpha[l])
        o_g = _gla(qk, la, v_g, r_g, g_gla_head[l], B, S)
        o_m = _moba(q_m, k_m, v_m, mem_bias, rel_bias, B, S)
        mem_k, mem_v = _mem_kv(mem, g_mem, w_ckv[l])
        x2, info, counts = _mix(x, o_g, o_m, z_g, z_m, mem_k, mem_v, g_cross[l], g_moe[l], w_proj_gla[l],
                                w_proj_moba[l], w_out[l], w_cq[l], w_co[l], w_router_group[l], b_router_group[l],
                                w_router_expert[l], b_router_expert[l])
        x2d, info2d = x2.reshape(T, D), info.reshape(T, LANES)
        dest, blk_e = _plan(counts, info2d, n_blk)
        xs = _dispatch(dest, x2d, g_moe[l], cap)
        y = _experts(blk_e[:n_blk, 0], xs, w_exp_gate[l], w_exp_up[l], w_exp_down[l])
        x = _combine(dest, x2d, info2d, g_final, y, final_norm=(l == depth - 1)).reshape(B, S, D)
    return x
```

```python
import functools
import math

import jax
import jax.numpy as jnp
from jax import lax
from jax.experimental import pallas as pl
from jax.experimental.pallas import tpu as pltpu

F32 = jnp.float32
BF16 = jnp.bfloat16
NEG_INF = float("-inf")

EPS = 1e-6
GLA_HEADS, GLA_DK, GLA_DV, GLA_LOWRANK, GLA_TAU, GLA_CHUNK = 4, 64, 128, 16, 16.0, 64
GLA_QK, GLA_V = GLA_HEADS * GLA_DK, GLA_HEADS * GLA_DV
MOBA_HEADS, MOBA_DH, MOBA_BLOCK, MOBA_TOPK = 8, 64, 256, 3
MOBA_W = MOBA_HEADS * MOBA_DH
LOG2E = math.log2(math.e)
MOBA_Q_SCALE = MOBA_DH ** -0.5 * LOG2E
REL_BUCKETS, REL_MAX_DIST = 32, 128
MEM_HEADS, MEM_DH = 4, 128
MEM_W = MEM_HEADS * MEM_DH
N_GROUPS, EXPERTS_PER_GROUP, TOPK_IN_GROUP = 4, 8, 2
N_EXPERTS = N_GROUPS * EXPERTS_PER_GROUP

LANES = 128
SUBLANES = 8
VMEM_LIMIT_BYTES = 56 * 1024 * 1024

PROJ_ROWS = 512
GLA_ROWS = 512
MOBA_GROUP = 4
MIX_ROWS = 512
MIX_PARTS = 2
EXPERT_ROWS = 256
MOVE_ROWS = 512
PLAN_ROWS = 2048
EXPERT_PARTS = 2


def _params(*semantics):
    return pltpu.CompilerParams(dimension_semantics=semantics, vmem_limit_bytes=VMEM_LIMIT_BYTES)


def _full(shape):
    return pl.BlockSpec(shape, lambda *_: (0,) * len(shape))


def _rms(x, g):
    return x * lax.rsqrt(jnp.mean(x * x, axis=-1, keepdims=True) + EPS) * g


def _for_each_row(rows, body):
    def trip(g, carry):
        for k in range(SUBLANES):
            body(g, k)
        return carry
    lax.fori_loop(0, rows // SUBLANES, trip, 0)


def _pack_bf16_pairs(x):
    n = x.shape[1] // 2
    bits = pltpu.bitcast(x.astype(BF16).astype(F32), jnp.uint32)
    return bits[:, n:] | (bits[:, :n] >> 16)


def _unpack_bf16_pairs(w):
    lo = pltpu.bitcast(w << 16, F32)
    hi = pltpu.bitcast(w & jnp.uint32(0xFFFF0000), F32)
    return jnp.concatenate([lo, hi], axis=1)


def _proj_kernel(x_ref, g_ref, w_qk, w_v, w_r, w_a, w_up, b_a, w_qm, w_km, w_vm, w_zg, w_zm,
                 o_qk, o_v, o_r, o_la, o_qm, o_km, o_vm, o_zg, o_zm):
    h = _rms(x_ref[...], g_ref[...]).astype(BF16)

    def mm(w_ref):
        return jnp.dot(h, w_ref[...], preferred_element_type=F32)

    o_qk[...] = mm(w_qk)
    o_v[...] = mm(w_v).astype(BF16)
    o_r[...] = mm(w_r)
    a_lr = mm(w_a).astype(BF16)
    pre = jnp.dot(a_lr, w_up[...], preferred_element_type=F32) + b_a[...]
    o_la[...] = jax.nn.log_sigmoid(pre) * (1.0 / GLA_TAU)

    def mm_t(wt_ref):
        return lax.dot_general(wt_ref[...], h, (((1,), (1,)), ((), ())), preferred_element_type=F32)

    def store_blocks(o_ref, val_t):
        for c in range(o_ref.shape[0]):
            o_ref[c] = val_t[:, c * MOBA_BLOCK:(c + 1) * MOBA_BLOCK]

    store_blocks(o_qm, (mm_t(w_qm) * MOBA_Q_SCALE).astype(BF16))
    o_km[...] = mm(w_km).astype(BF16)
    store_blocks(o_vm, mm_t(w_vm).astype(BF16))
    o_zg[...] = mm(w_zg)
    o_zm[...] = mm(w_zm)


def _project(x2d, g_mix, w_in, w_alpha_up, b_alpha):
    T, D = x2d.shape
    rows = min(PROJ_ROWS, T)
    assert T % rows == 0
    splits = (GLA_QK, GLA_QK, GLA_V, GLA_V, GLA_LOWRANK, MOBA_W, MOBA_W, MOBA_W, D, D)
    offs = [0]
    for s in splits:
        offs.append(offs[-1] + s)
    wb = w_in.astype(BF16)
    sec = lambda i, j: wb[:, offs[i]:offs[j]]
    w_a = jnp.pad(sec(4, 5), ((0, 0), (0, LANES - GLA_LOWRANK)))
    w_up = jnp.pad(w_alpha_up.astype(BF16), ((0, LANES - GLA_LOWRANK), (0, 0)))
    weights = [sec(0, 2), sec(2, 3), sec(3, 4), w_a, w_up, b_alpha.reshape(1, GLA_QK).astype(F32),
               sec(5, 6).T, sec(6, 7), sec(7, 8).T, sec(8, 9), sec(9, 10)]
    out_defs = [(2 * GLA_QK, F32, False), (GLA_V, BF16, False), (GLA_V, F32, False), (GLA_QK, F32, False),
                (MOBA_W, BF16, True), (MOBA_W, BF16, False), (MOBA_W, BF16, True), (D, F32, False),
                (D, F32, False)]
    BS = MOBA_BLOCK
    assert rows % BS == 0
    row_spec = lambda n: pl.BlockSpec((rows, n), lambda i: (i, 0))
    blk_spec = lambda n: pl.BlockSpec((rows // BS, n, BS), lambda i: (i, 0, 0))
    return pl.pallas_call(
        _proj_kernel,
        grid=(T // rows,),
        in_specs=[row_spec(D), _full((1, D))] + [_full(w.shape) for w in weights],
        out_specs=[blk_spec(n) if t else row_spec(n) for n, _, t in out_defs],
        out_shape=[jax.ShapeDtypeStruct((T // BS, n, BS) if t else (T, n), dt) for n, dt, t in out_defs],
        compiler_params=_params("parallel"),
        name="norm_in_proj",
    )(x2d, g_mix.reshape(1, D).astype(F32), *weights)


def _gla_kernel(qk_ref, la_ref, v_ref, r_ref, g_ref, o_ref, state_ref, obuf_ref):
    C, H, DK, DV = GLA_CHUNK, GLA_HEADS, GLA_DK, GLA_DV
    rows = qk_ref.shape[1]

    @pl.when(pl.program_id(1) == 0)
    def _():
        state_ref[...] = jnp.zeros_like(state_ref)

    tri = (lax.broadcasted_iota(jnp.int32, (C, C), 0) >= lax.broadcasted_iota(jnp.int32, (C, C), 1)).astype(BF16)
    lane_head = lax.broadcasted_iota(jnp.int32, (1, H * DK), 1) // DK
    head_masks = [(lane_head == h).astype(F32) for h in range(H)]
    stack_row = lax.broadcasted_iota(jnp.int32, (H * C, C), 0) % C
    stack_col = lax.broadcasted_iota(jnp.int32, (H * C, C), 1)
    causal = stack_col <= stack_row
    same_head = (lax.broadcasted_iota(jnp.int32, (H * DV, H * DK), 0) // DV
                 == lax.broadcasted_iota(jnp.int32, (H * DV, H * DK), 1) // DK)
    scale = DK ** -0.5

    def stack(m):
        return jnp.concatenate([m * head_masks[h] for h in range(H)], axis=0).astype(BF16)

    chunks = [slice(c * C, (c + 1) * C) for c in range(rows // C)]

    def cum_log_decay(sl):
        la = la_ref[0, sl, :]
        p1 = la.astype(BF16)
        r1 = la - p1.astype(F32)
        p2 = r1.astype(BF16)
        p3 = (r1 - p2.astype(F32)).astype(BF16)
        s3 = jnp.dot(tri, jnp.concatenate([p1, p2, p3], axis=1), preferred_element_type=F32)
        w = H * DK
        return (s3[:, 0:w] + s3[:, w:2 * w]) + s3[:, 2 * w:3 * w]

    b_all = [cum_log_decay(sl) for sl in chunks]

    qe_all, ke_all, kd_all, qb_all, decay_all = [], [], [], [], []
    for sl, b in zip(chunks, b_all):
        q = qk_ref[0, sl, 0:H * DK] * scale
        k = qk_ref[0, sl, H * DK:2 * H * DK]
        b_last = b[C - 1:C, :]
        b_mid = b[C // 2 - 1:C // 2, :]
        qe_all.append(stack(q * jnp.exp(b - b_mid)))
        ke_all.append((k * jnp.exp(b_mid - b)).astype(BF16))
        kd_all.append((k * jnp.exp(b_last - b)).astype(BF16))
        qb_all.append((q * jnp.exp(b)).astype(BF16))
        decay_all.append(jnp.exp(b_last))

    att_all = [jnp.where(causal, lax.dot_general(qe, ke, (((1,), (1,)), ((), ())), preferred_element_type=F32),
                         0.0).astype(BF16) for qe, ke in zip(qe_all, ke_all)]

    o_intra_all, kv_all = [], []
    for sl, att, kd in zip(chunks, att_all, kd_all):
        v = v_ref[0, sl, :]
        o_intra_all.append(jnp.concatenate(
            [jnp.dot(att[h * C:(h + 1) * C, :], v[:, h * DV:(h + 1) * DV], preferred_element_type=F32)
             for h in range(H)], axis=1))
        kv_t = lax.dot_general(v, kd, (((0,), (0,)), ((), ())), preferred_element_type=F32)
        kv_all.append(jnp.where(same_head, kv_t, 0.0))

    state_t = state_ref[...]
    for sl, qb, decay, kv_t, o_intra in zip(chunks, qb_all, decay_all, kv_all, o_intra_all):
        o_inter = lax.dot_general(qb, state_t.astype(BF16), (((1,), (1,)), ((), ())),
                                  preferred_element_type=F32)
        obuf_ref[sl, :] = o_intra + o_inter
        state_t = decay * state_t + kv_t
    state_ref[...] = state_t

    r = r_ref[0]
    for h in range(H):
        cs = slice(h * DV, (h + 1) * DV)
        y = _rms(obuf_ref[:, cs], g_ref[:, cs])
        rh = r[:, cs]
        o_ref[0, :, cs] = (y * (rh * jax.nn.sigmoid(rh))).astype(BF16)


def _gla(qk, la, v, r, g_head, B, S):
    rows = min(GLA_ROWS, S)
    assert S % rows == 0 and rows % GLA_CHUNK == 0
    spec = lambda n: pl.BlockSpec((1, rows, n), lambda b, i: (b, i, 0))
    return pl.pallas_call(
        _gla_kernel,
        grid=(B, S // rows),
        in_specs=[spec(2 * GLA_QK), spec(GLA_QK), spec(GLA_V), spec(GLA_V), _full((1, GLA_V))],
        out_specs=spec(GLA_V),
        out_shape=jax.ShapeDtypeStruct((B, S, GLA_V), BF16),
        scratch_shapes=[pltpu.VMEM((GLA_V, GLA_QK), F32), pltpu.VMEM((rows, GLA_V), F32)],
        compiler_params=_params("parallel", "arbitrary"),
        name="gla_chunked",
    )(qk.reshape(B, S, -1), la.reshape(B, S, -1), v.reshape(B, S, -1), r.reshape(B, S, -1),
      g_head.reshape(1, GLA_V).astype(F32))


def _t5_bucket(dist):
    n = jnp.maximum(dist, 0)
    max_exact = REL_BUCKETS // 2
    nf = jnp.maximum(n, 1).astype(F32)
    large = max_exact + (jnp.log(nf / max_exact) / math.log(REL_MAX_DIST / max_exact)
                         * (REL_BUCKETS - max_exact)).astype(jnp.int32)
    large = jnp.minimum(large, REL_BUCKETS - 1)
    return jnp.where(n < max_exact, n, large)


def _moba_bias_kernel(rb_ref, o_ref):
    BS, G = MOBA_BLOCK, MOBA_GROUP
    grp, kind = pl.program_id(0), pl.program_id(1)
    d = (lax.broadcasted_iota(jnp.int32, (BS, BS), 1) - lax.broadcasted_iota(jnp.int32, (BS, BS), 0)
         + kind * BS)
    bucket = _t5_bucket(d)
    for h in range(G):
        val = jnp.zeros((BS, BS), F32)
        for bkt in range(REL_BUCKETS):
            val = jnp.where(bucket == bkt, rb_ref[bkt, grp * G + h] * LOG2E, val)
        o_ref[0, 0, :, h * BS:(h + 1) * BS] = jnp.where(d >= 0, val, NEG_INF)


def _moba_bias(rel_bias):
    BS, G = MOBA_BLOCK, MOBA_GROUP
    n_grp = MOBA_HEADS // G
    return pl.pallas_call(
        _moba_bias_kernel,
        grid=(n_grp, 2),
        in_specs=[pl.BlockSpec(memory_space=pltpu.SMEM)],
        out_specs=pl.BlockSpec((1, 1, BS, G * BS), lambda g, k: (g, k, 0, 0)),
        out_shape=jax.ShapeDtypeStruct((n_grp, 2, BS, G * BS), F32),
        compiler_params=_params("parallel", "parallel"),
        name="moba_bias_tables",
    )(rel_bias.astype(F32))


def _moba_kernel(rb_ref, q_ref, k_ref, v_ref, bias_ref, o_ref,
                 qs_ref, kmean_ref, sel_ref, m_ref, l_ref, acc_ref, sbuf_ref):
    BS, G, DH = MOBA_BLOCK, MOBA_GROUP, MOBA_DH
    NBP = kmean_ref.shape[0]
    grp, i = pl.program_id(1), pl.program_id(2)

    @pl.when(i == 0)
    def _():
        S = k_ref.shape[1]
        blk_of_key = lax.broadcasted_iota(jnp.int32, (NBP, S), 1) // BS
        ind = (blk_of_key == lax.broadcasted_iota(jnp.int32, (NBP, S), 0)).astype(BF16)
        kmean_ref[...] = jnp.dot(ind, k_ref[0], preferred_element_type=F32) * (1.0 / BS)

    qt = q_ref[0]
    sub_head = lax.broadcasted_iota(jnp.int32, (G * DH, 1), 0) // DH
    for h in range(G):
        qs_ref[:, h * BS:(h + 1) * BS] = jnp.where(sub_head == h, qt, jnp.zeros_like(qt))

    gate = jnp.dot(kmean_ref[...].astype(BF16), qs_ref[...], preferred_element_type=F32)
    blk = lax.broadcasted_iota(jnp.int32, gate.shape, 0)
    gate = jnp.where(blk < i, gate, NEG_INF)
    for t in range(MOBA_TOPK):
        mx = jnp.max(gate, axis=0, keepdims=True)
        hit = (gate == mx) & (mx > NEG_INF)
        idx = jnp.min(jnp.where(hit, blk, NBP), axis=0, keepdims=True)
        sel_ref[t:t + 1, :] = idx
        gate = jnp.where(blk == idx, NEG_INF, gate)

    def mask_row(j):
        hit = (sel_ref[0:1, :] == j) | (sel_ref[1:2, :] == j) | (sel_ref[2:3, :] == j)
        return jnp.where(hit, 0.0, NEG_INF)

    def with_ones(vt):
        return jnp.concatenate([vt, jnp.ones((8, vt.shape[1]), BF16)], axis=0)

    def far_scores(j0, n, slot):
        kj = k_ref[0, pl.ds(pl.multiple_of(j0 * BS, BS), n * BS), :]
        for h in range(G):
            cs = slice(h * BS, (h + 1) * BS)
            sbuf_ref[slot, 0:n * BS, cs] = jnp.dot(kj, qs_ref[:, cs], preferred_element_type=F32)

    j_prev = jnp.maximum(i - 1, 0)
    k_own = k_ref[0, pl.ds(pl.multiple_of(i * BS, BS), BS), :]
    k_prev = k_ref[0, pl.ds(pl.multiple_of(j_prev * BS, BS), BS), :]
    vt_near = jnp.concatenate([v_ref[i], v_ref[j_prev]], axis=1)
    prev_mask = mask_row(i - 1)
    for h in range(G):
        cs = slice(h * BS, (h + 1) * BS)
        sbuf_ref[1, 0:BS, cs] = jnp.dot(k_own, qs_ref[:, cs], preferred_element_type=F32)
        sbuf_ref[1, BS:2 * BS, cs] = jnp.dot(k_prev, qs_ref[:, cs], preferred_element_type=F32)
    far_scores(0, 2, 0)
    m_out, l_out, acc_out = [], [], []
    for h in range(G):
        cs = slice(h * BS, (h + 1) * BS)
        s_own = sbuf_ref[1, 0:BS, cs] + bias_ref[0, 0, :, cs]
        s_prev = sbuf_ref[1, BS:2 * BS, cs] + (bias_ref[0, 1, :, cs] + prev_mask[:, cs])
        m0 = jnp.maximum(jnp.max(s_own, axis=0, keepdims=True), jnp.max(s_prev, axis=0, keepdims=True))
        pb = jnp.concatenate([jnp.exp2(s_own - m0).astype(BF16), jnp.exp2(s_prev - m0).astype(BF16)], axis=0)
        pv = jnp.dot(with_ones(vt_near[h * DH:(h + 1) * DH, :]), pb, preferred_element_type=F32)
        m_out.append(m0)
        l_out.append(pv[DH:DH + 1, :])
        acc_out.append(pv[0:DH, :])
    m_ref[...] = jnp.concatenate(m_out, axis=1)
    l_ref[...] = jnp.concatenate(l_out, axis=1)
    acc_ref[...] = jnp.concatenate(acc_out, axis=1)

    lane_head = lax.broadcasted_iota(jnp.int32, (1, G * BS), 1) // BS
    far_bias = jnp.zeros((1, G * BS), F32)
    for h in range(G):
        far_bias = jnp.where(lane_head == h, rb_ref[REL_BUCKETS - 1, grp * G + h] * LOG2E, far_bias)

    def visit_far(j0, n, slot, ahead=None):
        vjt = jnp.concatenate([v_ref[j0 + t] for t in range(n)], axis=1)
        addend = [far_bias + mask_row(j0 + t) for t in range(n)]
        m_old, l_old, acc_old = m_ref[...], l_ref[...], acc_ref[...]
        m_out, l_out, acc_out = [], [], []
        if ahead is not None:
            k_next = k_ref[0, pl.ds(pl.multiple_of(ahead[0] * BS, BS), 2 * BS), :]
        for h in range(G):
            cs = slice(h * BS, (h + 1) * BS)
            if ahead is not None:
                sbuf_ref[ahead[1], :, cs] = jnp.dot(k_next, qs_ref[:, cs], preferred_element_type=F32)
            s = sbuf_ref[slot, 0:n * BS, cs]
            mx = jnp.max(s[0:BS], axis=0, keepdims=True) + addend[0][:, cs]
            for t in range(1, n):
                mx = jnp.maximum(mx, jnp.max(s[t * BS:(t + 1) * BS], axis=0, keepdims=True) + addend[t][:, cs])
            m_new = jnp.maximum(m_old[:, cs], mx)
            pb = jnp.concatenate([jnp.exp2(s[t * BS:(t + 1) * BS] - (m_new - addend[t][:, cs])).astype(BF16)
                                  for t in range(n)], axis=0)
            pv = jnp.dot(with_ones(vjt[h * DH:(h + 1) * DH, :]), pb, preferred_element_type=F32)
            alpha = jnp.exp2(m_old[:, cs] - m_new)
            m_out.append(m_new)
            l_out.append(alpha * l_old[:, cs] + pv[DH:DH + 1, :])
            acc_out.append(alpha * acc_old[:, cs] + pv[0:DH, :])
        m_ref[...] = jnp.concatenate(m_out, axis=1)
        l_ref[...] = jnp.concatenate(l_out, axis=1)
        acc_ref[...] = jnp.concatenate(acc_out, axis=1)

    n_far = jnp.maximum(i - 1, 0)
    n_pairs = lax.shift_right_logical(n_far, 1)

    def pair(p, slot, look_ahead=True):
        ahead = (2 * jnp.minimum(p + 1, n_pairs - 1), 1 - slot) if look_ahead else None
        visit_far(2 * p, 2, slot, ahead=ahead)

    def far_octet(w, carry):
        for t in range(4):
            pair(4 * w + t, t % 2)
        return carry

    lax.fori_loop(0, lax.shift_right_logical(n_pairs, 2), far_octet, 0)

    @pl.when((n_pairs & 2) != 0)
    def _():
        base = 4 * lax.shift_right_logical(n_pairs, 2)
        pair(base, 0)
        pair(base + 1, 1)

    @pl.when((n_pairs & 1) != 0)
    def _():
        pair(n_pairs - 1, 0, look_ahead=False)

    @pl.when(n_far % 2 == 1)
    def _():
        far_scores(n_far - 1, 1, 1)
        visit_far(n_far - 1, 1, 1)

    out_t = jnp.concatenate([acc_ref[:, h * BS:(h + 1) * BS] / l_ref[:, h * BS:(h + 1) * BS] for h in range(G)],
                            axis=0)
    o_ref[0] = out_t.T.astype(o_ref.dtype)


def _moba(qm_t, km, vm_t, bias, rel_bias, B, S):
    BS, G, DH = MOBA_BLOCK, MOBA_GROUP, MOBA_DH
    W = G * DH
    n_grp = MOBA_HEADS // G
    assert S % BS == 0 and S >= 2 * BS
    NB = S // BS
    NBP = -(-NB // 8) * 8
    return pl.pallas_call(
        _moba_kernel,
        grid=(B, n_grp, NB),
        in_specs=[pl.BlockSpec(memory_space=pltpu.SMEM),
                  pl.BlockSpec((1, W, BS), lambda b, g, i: (b * NB + i, g, 0)),
                  pl.BlockSpec((1, S, W), lambda b, g, i: (b, 0, g)),
                  pl.BlockSpec((NB, W, BS), lambda b, g, i: (b, g, 0)),
                  pl.BlockSpec((1, 2, BS, G * BS), lambda b, g, i: (g, 0, 0, 0))],
        out_specs=pl.BlockSpec((1, BS, W), lambda b, g, i: (b, i, g)),
        out_shape=jax.ShapeDtypeStruct((B, S, MOBA_W), BF16),
        scratch_shapes=[pltpu.VMEM((W, G * BS), BF16), pltpu.VMEM((NBP, W), F32),
                        pltpu.VMEM((8, G * BS), jnp.int32),
                        pltpu.VMEM((1, G * BS), F32), pltpu.VMEM((1, G * BS), F32),
                        pltpu.VMEM((DH, G * BS), F32), pltpu.VMEM((2, 2 * BS, G * BS), F32)],
        compiler_params=_params("parallel", "parallel", "arbitrary"),
        name="moba_attention",
    )(rel_bias.astype(F32), qm_t, km.reshape(B, S, -1), vm_t, bias)


def _mem_kv_kernel(mem_ref, g_ref, w_ref, k_ref, v_ref):
    kv = jnp.dot(_rms(mem_ref[0], g_ref[...]).astype(BF16), w_ref[...], preferred_element_type=F32)
    k_ref[0] = kv[:, :MEM_W].astype(BF16)
    v_ref[0] = kv[:, MEM_W:].astype(BF16)


def _mem_kv(mem, g_mem, w_ckv):
    B, M, D = mem.shape
    spec = pl.BlockSpec((1, M, MEM_W), lambda b: (b, 0, 0))
    return pl.pallas_call(
        _mem_kv_kernel,
        grid=(B,),
        in_specs=[pl.BlockSpec((1, M, D), lambda b: (b, 0, 0)), _full((1, D)), _full((D, 2 * MEM_W))],
        out_specs=[spec, spec],
        out_shape=[jax.ShapeDtypeStruct((B, M, MEM_W), BF16)] * 2,
        compiler_params=_params("parallel"),
        name="memory_kv",
    )(mem, g_mem.reshape(1, D).astype(F32), w_ckv.astype(BF16))


INFO_W0, INFO_W1, INFO_E0, INFO_E1, INFO_R0, INFO_R1 = range(6)
ROUTER_GROUP_LANE0, ROUTER_EXPERT_LANE0 = 0, N_GROUPS


def _mix_kernel(x_ref, og_ref, om_ref, zg_ref, zm_ref, mk_ref, mv_ref, gc_ref, gm_ref,
                wpg, wpm, wout, wcq, wco, wr, br,
                x2_ref, info_ref, cnt_ref, base_ref):
    first = (pl.program_id(0) == 0) & (pl.program_id(1) == 0)

    @pl.when(first)
    def _():
        base_ref[...] = jnp.zeros_like(base_ref)

    def mm(a, w_ref):
        return jnp.dot(a.astype(BF16), w_ref[...], preferred_element_type=F32)

    n_rows = x_ref.shape[1]
    sub = n_rows // MIX_PARTS
    parts = [slice(p * sub, (p + 1) * sub) for p in range(MIX_PARTS)]

    merged = [jax.nn.sigmoid(zg_ref[0, rs, :]) * mm(og_ref[0, rs, :], wpg)
              + jax.nn.sigmoid(zm_ref[0, rs, :]) * mm(om_ref[0, rs, :], wpm) for rs in parts]
    x1 = [x_ref[0, rs, :] + mm(m, wout) for rs, m in zip(parts, merged)]

    qc = [mm(_rms(v, gc_ref[...]), wcq).astype(BF16) for v in x1]

    def mem_attention(q):
        heads = []
        for h in range(MEM_HEADS):
            cs = slice(h * MEM_DH, (h + 1) * MEM_DH)
            s = lax.dot_general(q[:, cs], mk_ref[0, :, cs], (((1,), (1,)), ((), ())),
                                preferred_element_type=F32) * (MEM_DH ** -0.5)
            p = jnp.exp(s - jnp.max(s, axis=-1, keepdims=True))
            o = jnp.dot(p.astype(BF16), mv_ref[0, :, cs], preferred_element_type=F32)
            heads.append(o / jnp.sum(p, axis=-1, keepdims=True))
        return jnp.concatenate(heads, axis=-1)

    attn = [mem_attention(q) for q in qc]
    x2 = [v + mm(a, wco) for v, a in zip(x1, attn)]
    for rs, v in zip(parts, x2):
        x2_ref[0, rs, :] = v

    logits = [mm(_rms(v, gm_ref[...]), wr) + br[...] for v in x2]
    lane = lax.broadcasted_iota(jnp.int32, (sub, LANES), 1)
    is_grp = lane < N_GROUPS
    e_id = lane - ROUTER_EXPERT_LANE0

    def route(lg):
        gl = jnp.where(is_grp, lg, NEG_INF)
        ge = jnp.exp(gl - jnp.max(gl, axis=-1, keepdims=True))
        g_prob = ge / jnp.sum(ge, axis=-1, keepdims=True)
        p_grp = jnp.max(g_prob, axis=-1, keepdims=True)
        grp = jnp.min(jnp.where((g_prob == p_grp) & is_grp, lane, LANES), axis=-1, keepdims=True)
        in_grp = (e_id >= grp * EXPERTS_PER_GROUP) & (e_id < (grp + 1) * EXPERTS_PER_GROUP)
        el = jnp.where(in_grp, lg, NEG_INF)
        ee = jnp.exp(el - jnp.max(el, axis=-1, keepdims=True))
        e_prob = jnp.where(in_grp, ee / jnp.sum(ee, axis=-1, keepdims=True), -1.0)
        p0 = jnp.max(e_prob, axis=-1, keepdims=True)
        e0 = jnp.min(jnp.where(e_prob == p0, e_id, LANES), axis=-1, keepdims=True)
        e_rest = jnp.where(e_id == e0, -1.0, e_prob)
        p1 = jnp.max(e_rest, axis=-1, keepdims=True)
        e1 = jnp.min(jnp.where(e_rest == p1, e_id, LANES), axis=-1, keepdims=True)
        return e0, e1, p_grp * p0 / (p0 + p1), p_grp * p1 / (p0 + p1)

    routed = [route(lg) for lg in logits]

    before = (lax.broadcasted_iota(jnp.int32, (sub, sub), 1)
              < lax.broadcasted_iota(jnp.int32, (sub, sub), 0)).astype(BF16)
    base = base_ref[...]
    for rs, (e0, e1, w0, w1) in zip(parts, routed):
        onehot = ((lane == e0) | (lane == e1)).astype(F32)
        seen = base + jnp.dot(before, onehot.astype(BF16), preferred_element_type=F32)
        r0 = jnp.sum(jnp.where(lane == e0, seen, 0.0), axis=-1, keepdims=True)
        r1 = jnp.sum(jnp.where(lane == e1, seen, 0.0), axis=-1, keepdims=True)
        base = base + jnp.sum(onehot, axis=0, keepdims=True)
        info = jnp.zeros((sub, LANES), F32)
        for ln, val in ((INFO_W0, w0), (INFO_W1, w1), (INFO_E0, e0.astype(F32)), (INFO_E1, e1.astype(F32)),
                        (INFO_R0, r0), (INFO_R1, r1)):
            info = jnp.where(lane == ln, val, info)
        info_ref[0, rs, :] = info
    base_ref[...] = base
    cnt_ref[...] = base


def _mix(x, o_g, o_m, z_g, z_m, mem_k, mem_v, g_cross, g_moe, w_proj_gla, w_proj_moba, w_out, w_cq, w_co,
         w_rg, b_rg, w_re, b_re):
    B, S, D = x.shape
    M = mem_k.shape[1]
    rows = min(MIX_ROWS, S)
    assert S % rows == 0
    pad = LANES - N_GROUPS - N_EXPERTS
    wr = jnp.pad(jnp.concatenate([w_rg, w_re], axis=1), ((0, 0), (0, pad))).astype(BF16)
    br = jnp.pad(jnp.concatenate([b_rg, b_re]), (0, pad)).reshape(1, LANES).astype(F32)
    weights = [w_proj_gla.astype(BF16), w_proj_moba.astype(BF16), w_out.astype(BF16), w_cq.astype(BF16),
               w_co.astype(BF16), wr, br]
    tile = lambda n: pl.BlockSpec((1, rows, n), lambda b, i: (b, i, 0))
    memspec = pl.BlockSpec((1, M, MEM_W), lambda b, i: (b, 0, 0))
    return pl.pallas_call(
        _mix_kernel,
        grid=(B, S // rows),
        in_specs=[tile(D), tile(GLA_V), tile(MOBA_W), tile(D), tile(D), memspec, memspec,
                  _full((1, D)), _full((1, D))] + [_full(w.shape) for w in weights],
        out_specs=[tile(D), tile(LANES), _full((1, LANES))],
        out_shape=[jax.ShapeDtypeStruct((B, S, D), F32), jax.ShapeDtypeStruct((B, S, LANES), F32),
                   jax.ShapeDtypeStruct((1, LANES), F32)],
        scratch_shapes=[pltpu.VMEM((1, LANES), F32)],
        compiler_params=_params("arbitrary", "arbitrary"),
        name="merge_memattn_router",
    )(x, o_g, o_m, z_g.reshape(B, S, D), z_m.reshape(B, S, D), mem_k, mem_v,
      g_cross.reshape(1, D).astype(F32), g_moe.reshape(1, D).astype(F32), *weights)


def _plan_kernel(cnt_ref, info_ref, dest_ref, blk_ref):
    rows = info_ref.shape[0]
    lane1 = lax.broadcasted_iota(jnp.int32, (1, LANES), 1)
    nblk = jnp.floor((cnt_ref[...] + (EXPERT_ROWS - 1)) * (1.0 / EXPERT_ROWS))
    nblk = jnp.where(lane1 < N_EXPERTS, nblk, 0.0)
    hi = jnp.floor(nblk * (1.0 / 256.0))
    lo = nblk - 256.0 * hi
    upto = (lax.broadcasted_iota(jnp.int32, (LANES, LANES), 0)
            <= lax.broadcasted_iota(jnp.int32, (LANES, LANES), 1)).astype(BF16)
    digits = jnp.concatenate([jnp.broadcast_to(hi, (8, LANES)), jnp.broadcast_to(lo, (8, LANES))], axis=0)
    sums = jnp.dot(digits.astype(BF16), upto, preferred_element_type=F32)
    pend = sums[0:1] * 256.0 + sums[8:9]
    pstart_rows = (pend - nblk) * EXPERT_ROWS

    info = info_ref[...]
    lane = lax.broadcasted_iota(jnp.int32, (rows, LANES), 1)

    def field(ln):
        return jnp.sum(jnp.where(lane == ln, info, 0.0), axis=-1, keepdims=True)

    def dest(e, r):
        return jnp.sum(jnp.where(lane == e.astype(jnp.int32), pstart_rows, 0.0), axis=-1, keepdims=True) + r

    d0 = dest(field(INFO_E0), field(INFO_R0))
    d1 = dest(field(INFO_E1), field(INFO_R1))
    cols = jnp.where(lane == 0, d0, jnp.where(lane == 1, d1, 0.0))
    move = dest_ref.shape[2]
    for c in range(dest_ref.shape[0]):
        dest_ref[c] = cols[c * move:(c + 1) * move, :].T[0:8, :].astype(jnp.int32)

    @pl.when(pl.program_id(0) == 0)
    def _():
        n = lax.broadcasted_iota(jnp.int32, (blk_ref.shape[0], LANES), 0).astype(F32)
        done = jnp.where((pend <= n) & (lane1 < N_EXPERTS), 1.0, 0.0)
        e = jnp.minimum(jnp.sum(done, axis=-1, keepdims=True), N_EXPERTS - 1.0)
        blk_ref[...] = e.astype(jnp.int32)


def _plan(counts, info2d, n_blk):
    T = info2d.shape[0]
    move = min(MOVE_ROWS, T)
    rows = min(PLAN_ROWS, T)
    assert T % rows == 0 and rows % move == 0
    n_blk_pad = -(-n_blk // 8) * 8
    return pl.pallas_call(
        _plan_kernel,
        grid=(T // rows,),
        in_specs=[_full((1, LANES)), pl.BlockSpec((rows, LANES), lambda i: (i, 0))],
        out_specs=[pl.BlockSpec((rows // move, 8, move), lambda i: (i, 0, 0)), _full((n_blk_pad, 1))],
        out_shape=[jax.ShapeDtypeStruct((T // move, 8, move), jnp.int32),
                   jax.ShapeDtypeStruct((n_blk_pad, 1), jnp.int32)],
        compiler_params=_params("arbitrary"),
        name="dispatch_plan",
    )(counts, info2d)


def _dispatch_kernel(dest_ref, x_ref, g_ref, xs_in_ref, xs_ref, hbuf_ref, sems):
    del xs_in_ref
    rows, D = x_ref.shape
    i, n = pl.program_id(0), pl.num_programs(0)
    buf = i % 2
    packed = _pack_bf16_pairs(_rms(x_ref[...], g_ref[...]))
    hbuf_ref[buf] = packed.reshape(rows // SUBLANES, SUBLANES, D // 2)

    def row_copy(b, g, k, dst):
        return pltpu.make_async_copy(hbuf_ref.at[b, g, pl.ds(k, 1)], xs_ref.at[pl.ds(dst, 1)], sems.at[b])

    def start(g, k):
        r = g * SUBLANES + k
        row_copy(buf, g, k, dest_ref[0, 0, r]).start()
        row_copy(buf, g, k, dest_ref[0, 1, r]).start()

    def wait_all(b):
        def wait(g, k):
            row_copy(b, g, k, 0).wait()
            row_copy(b, g, k, 0).wait()
        _for_each_row(rows, wait)

    _for_each_row(rows, start)

    @pl.when(i > 0)
    def _():
        wait_all(1 - buf)

    @pl.when(i == n - 1)
    def _():
        wait_all(buf)


def _dispatch(dest, x2d, g_moe, cap):
    T, D = x2d.shape
    rows = dest.shape[2]
    xs0 = jnp.zeros((cap, D // 2), jnp.uint32)
    return pl.pallas_call(
        _dispatch_kernel,
        grid=(T // rows,),
        in_specs=[pl.BlockSpec((1, 8, rows), lambda i: (i, 0, 0), memory_space=pltpu.SMEM),
                  pl.BlockSpec((rows, D), lambda i: (i, 0)), _full((1, D)),
                  pl.BlockSpec(memory_space=pl.ANY)],
        out_specs=pl.BlockSpec(memory_space=pl.ANY),
        out_shape=jax.ShapeDtypeStruct((cap, D // 2), jnp.uint32),
        scratch_shapes=[pltpu.VMEM((2, rows // SUBLANES, SUBLANES, D // 2), jnp.uint32),
                        pltpu.SemaphoreType.DMA((2,))],
        input_output_aliases={3: 0},
        compiler_params=_params("arbitrary"),
        name="moe_dispatch",
    )(dest, x2d, g_moe.reshape(1, D).astype(F32), xs0)


def _expert_kernel(blk_e_ref, xs_ref, wg_ref, wu_ref, wd_ref, y_ref, wg_bf, wu_bf, wd_bf):
    n = pl.program_id(0)
    prev = blk_e_ref[jnp.maximum(n - 1, 0)]

    @pl.when((n == 0) | (blk_e_ref[n] != prev))
    def _():
        wg_bf[...] = wg_ref[0].astype(BF16)
        wu_bf[...] = wu_ref[0].astype(BF16)
        wd_bf[...] = wd_ref[0].astype(BF16)

    sub = xs_ref.shape[0] // EXPERT_PARTS
    parts = [slice(p * sub, (p + 1) * sub) for p in range(EXPERT_PARTS)]
    xb = [_unpack_bf16_pairs(xs_ref[rs, :]).astype(BF16) for rs in parts]
    gate = [jnp.dot(v, wg_bf[...], preferred_element_type=F32) for v in xb]
    up = [jnp.dot(v, wu_bf[...], preferred_element_type=F32) for v in xb]
    hid = [(g * jax.nn.sigmoid(g) * u).astype(BF16) for g, u in zip(gate, up)]
    for rs, hv in zip(parts, hid):
        y_ref[rs, :] = _pack_bf16_pairs(jnp.dot(hv, wd_bf[...], preferred_element_type=F32))


def _experts(blk_e, xs, w_gate, w_up, w_down):
    cap = xs.shape[0]
    _, D, DE = w_gate.shape
    n_blk = cap // EXPERT_ROWS
    rows_spec = pl.BlockSpec((EXPERT_ROWS, D // 2), lambda n, e: (n, 0))
    return pl.pallas_call(
        _expert_kernel,
        grid_spec=pltpu.PrefetchScalarGridSpec(
            num_scalar_prefetch=1,
            grid=(n_blk,),
            in_specs=[rows_spec,
                      pl.BlockSpec((1, D, DE), lambda n, e: (e[n], 0, 0)),
                      pl.BlockSpec((1, D, DE), lambda n, e: (e[n], 0, 0)),
                      pl.BlockSpec((1, DE, D), lambda n, e: (e[n], 0, 0))],
            out_specs=rows_spec,
            scratch_shapes=[pltpu.VMEM((D, DE), BF16), pltpu.VMEM((D, DE), BF16), pltpu.VMEM((DE, D), BF16)]),
        out_shape=jax.ShapeDtypeStruct((cap, D // 2), jnp.uint32),
        compiler_params=_params("arbitrary"),
        name="moe_experts",
    )(blk_e, xs, w_gate, w_up, w_down)


def _combine_kernel(final_norm, dest_ref, dest_next_ref, x_ref, info_ref, g_ref, y_ref, o_ref, ybuf_ref, sems):
    rows, D = x_ref.shape
    i, n = pl.program_id(0), pl.num_programs(0)
    buf = i % 2

    def row_copy(b, g, k, slot, src):
        return pltpu.make_async_copy(y_ref.at[pl.ds(src, 1)], ybuf_ref.at[b, slot, g, pl.ds(k, 1)], sems.at[b])

    def fetch(b, d_ref):
        def start(g, k):
            r = g * SUBLANES + k
            row_copy(b, g, k, 0, d_ref[0, 0, r]).start()
            row_copy(b, g, k, 1, d_ref[0, 1, r]).start()
        _for_each_row(rows, start)

    @pl.when(i == 0)
    def _():
        fetch(0, dest_ref)

    @pl.when(i + 1 < n)
    def _():
        fetch(1 - buf, dest_next_ref)

    def wait(g, k):
        row_copy(buf, g, k, 0, 0).wait()
        row_copy(buf, g, k, 1, 0).wait()

    _for_each_row(rows, wait)
    info = info_ref[...]
    w0 = info[:, INFO_W0:INFO_W0 + 1]
    w1 = info[:, INFO_W1:INFO_W1 + 1]
    y0 = _unpack_bf16_pairs(ybuf_ref[buf, 0].reshape(rows, D // 2))
    y1 = _unpack_bf16_pairs(ybuf_ref[buf, 1].reshape(rows, D // 2))
    out = x_ref[...] + (w0 * y0 + w1 * y1)
    o_ref[...] = _rms(out, g_ref[...]) if final_norm else out


def _combine(dest, x2d, info2d, g_final, y, final_norm):
    T, D = x2d.shape
    rows = dest.shape[2]
    n = T // rows
    return pl.pallas_call(
        functools.partial(_combine_kernel, final_norm),
        grid=(n,),
        in_specs=[pl.BlockSpec((1, 8, rows), lambda i: (i, 0, 0), memory_space=pltpu.SMEM),
                  pl.BlockSpec((1, 8, rows), lambda i: (jnp.minimum(i + 1, n - 1), 0, 0), memory_space=pltpu.SMEM),
                  pl.BlockSpec((rows, D), lambda i: (i, 0)), pl.BlockSpec((rows, LANES), lambda i: (i, 0)),
                  _full((1, D)), pl.BlockSpec(memory_space=pl.ANY)],
        out_specs=pl.BlockSpec((rows, D), lambda i: (i, 0)),
        out_shape=jax.ShapeDtypeStruct((T, D), F32),
        scratch_shapes=[pltpu.VMEM((2, 2, rows // SUBLANES, SUBLANES, D // 2), jnp.uint32),
                        pltpu.SemaphoreType.DMA((2,))],
        compiler_params=_params("arbitrary"),
        name="moe_combine_final_norm",
    )(dest, dest, x2d, info2d, g_final.reshape(1, D).astype(F32), y)


def kernel(x, mem, g_mem, rel_bias, g_mix, w_in, w_alpha_up, b_alpha, g_gla_head, w_proj_gla, w_proj_moba,
           w_out, g_cross, w_cq, w_ckv, w_co, g_moe, w_router_group, b_router_group, w_router_expert,
           b_router_expert, w_exp_gate, w_exp_up, w_exp_down, g_final):
    B, S, D = x.shape
    T = B * S
    depth = g_mix.shape[0]
    n_assign = T * TOPK_IN_GROUP
    n_blk = -(-(n_assign + N_EXPERTS * (EXPERT_ROWS - 1)) // EXPERT_ROWS)
    cap = n_blk * EXPERT_ROWS

    mem_bias = _moba_bias(rel_bias)
    for l in range(depth):
        qk, v_g, r_g, la, q_m, k_m, v_m, z_g, z_m = _project(x.reshape(T, D), g_mix[l], w_in[l], w_alpha_up[l],
                                                             b_alpha[l])
        o_g = _gla(qk, la, v_g, r_g, g_gla_head[l], B, S)
        o_m = _moba(q_m, k_m, v_m, mem_bias, rel_bias, B, S)
        mem_k, mem_v = _mem_kv(mem, g_mem, w_ckv[l])
        x2, info, counts = _mix(x, o_g, o_m, z_g, z_m, mem_k, mem_v, g_cross[l], g_moe[l], w_proj_gla[l],
                                w_proj_moba[l], w_out[l], w_cq[l], w_co[l], w_router_group[l], b_router_group[l],
                                w_router_expert[l], b_router_expert[l])
        x2d, info2d = x2.reshape(T, D), info.reshape(T, LANES)
        dest, blk_e = _plan(counts, info2d, n_blk)
        xs = _dispatch(dest, x2d, g_moe[l], cap)
        y = _experts(blk_e[:n_blk, 0], xs, w_exp_gate[l], w_exp_up[l], w_exp_down[l])
        x = _combine(dest, x2d, info2d, g_final, y, final_norm=(l == depth - 1)).reshape(B, S, D)
    return x
```

```python
import functools
import math

import jax
import jax.numpy as jnp
from jax import lax
from jax.experimental import pallas as pl
from jax.experimental.pallas import tpu as pltpu
from jax.experimental.pallas import tpu_sc as plsc

F32 = jnp.float32
BF16 = jnp.bfloat16
NEG_INF = float("-inf")

EPS = 1e-6
GLA_HEADS, GLA_DK, GLA_DV, GLA_LOWRANK, GLA_TAU, GLA_CHUNK = 4, 64, 128, 16, 16.0, 64
GLA_QK, GLA_V = GLA_HEADS * GLA_DK, GLA_HEADS * GLA_DV
MOBA_HEADS, MOBA_DH, MOBA_BLOCK, MOBA_TOPK = 8, 64, 256, 3
MOBA_W = MOBA_HEADS * MOBA_DH
LOG2E = math.log2(math.e)
MOBA_Q_SCALE = MOBA_DH ** -0.5 * LOG2E
REL_BUCKETS, REL_MAX_DIST = 32, 128
MEM_HEADS, MEM_DH = 4, 128
MEM_W = MEM_HEADS * MEM_DH
N_GROUPS, EXPERTS_PER_GROUP, TOPK_IN_GROUP = 4, 8, 2
N_EXPERTS = N_GROUPS * EXPERTS_PER_GROUP

LANES = 128
SUBLANES = 8
VMEM_LIMIT_BYTES = 56 * 1024 * 1024

PROJ_ROWS = 512
GLA_ROWS = 512
MOBA_GROUP = 4
MIX_ROWS = 512
MIX_PARTS = 2
EXPERT_ROWS = 256
MOVE_ROWS = 512
PLAN_ROWS = 2048
SC_GATHER_ROWS = 128
SC_GATHER_WORDS = 256
EXPERT_PARTS = 2


def _params(*semantics):
    return pltpu.CompilerParams(dimension_semantics=semantics, vmem_limit_bytes=VMEM_LIMIT_BYTES)


def _full(shape):
    return pl.BlockSpec(shape, lambda *_: (0,) * len(shape))


def _rms(x, g):
    return x * lax.rsqrt(jnp.mean(x * x, axis=-1, keepdims=True) + EPS) * g


def _for_each_row(rows, body):
    def trip(g, carry):
        for k in range(SUBLANES):
            body(g, k)
        return carry
    lax.fori_loop(0, rows // SUBLANES, trip, 0)


def _pack_bf16_pairs(x):
    n = x.shape[1] // 2
    bits = pltpu.bitcast(x.astype(BF16).astype(F32), jnp.uint32)
    return bits[:, n:] | (bits[:, :n] >> 16)


def _unpack_bf16_pairs(w):
    lo = pltpu.bitcast(w << 16, F32)
    hi = pltpu.bitcast(w & jnp.uint32(0xFFFF0000), F32)
    return jnp.concatenate([lo, hi], axis=1)


def _proj_kernel(x_ref, g_ref, w_qk, w_v, w_r, w_a, w_up, b_a, w_qm, w_km, w_vm, w_zg, w_zm,
                 o_qk, o_v, o_r, o_la, o_qm, o_km, o_vm, o_zg, o_zm):
    h = _rms(x_ref[...], g_ref[...]).astype(BF16)

    def mm(w_ref):
        return jnp.dot(h, w_ref[...], preferred_element_type=F32)

    o_qk[...] = mm(w_qk)
    o_v[...] = mm(w_v).astype(BF16)
    o_r[...] = mm(w_r)
    a_lr = mm(w_a).astype(BF16)
    pre = jnp.dot(a_lr, w_up[...], preferred_element_type=F32) + b_a[...]
    o_la[...] = jax.nn.log_sigmoid(pre) * (1.0 / GLA_TAU)

    def mm_t(wt_ref):
        return lax.dot_general(wt_ref[...], h, (((1,), (1,)), ((), ())), preferred_element_type=F32)

    def store_blocks(o_ref, val_t):
        for c in range(o_ref.shape[0]):
            o_ref[c] = val_t[:, c * MOBA_BLOCK:(c + 1) * MOBA_BLOCK]

    store_blocks(o_qm, (mm_t(w_qm) * MOBA_Q_SCALE).astype(BF16))
    o_km[...] = mm(w_km).astype(BF16)
    store_blocks(o_vm, mm_t(w_vm).astype(BF16))
    o_zg[...] = mm(w_zg)
    o_zm[...] = mm(w_zm)


def _project(x2d, g_mix, w_in, w_alpha_up, b_alpha):
    T, D = x2d.shape
    rows = min(PROJ_ROWS, T)
    assert T % rows == 0
    splits = (GLA_QK, GLA_QK, GLA_V, GLA_V, GLA_LOWRANK, MOBA_W, MOBA_W, MOBA_W, D, D)
    offs = [0]
    for s in splits:
        offs.append(offs[-1] + s)
    wb = w_in.astype(BF16)
    sec = lambda i, j: wb[:, offs[i]:offs[j]]
    w_a = jnp.pad(sec(4, 5), ((0, 0), (0, LANES - GLA_LOWRANK)))
    w_up = jnp.pad(w_alpha_up.astype(BF16), ((0, LANES - GLA_LOWRANK), (0, 0)))
    weights = [sec(0, 2), sec(2, 3), sec(3, 4), w_a, w_up, b_alpha.reshape(1, GLA_QK).astype(F32),
               sec(5, 6).T, sec(6, 7), sec(7, 8).T, sec(8, 9), sec(9, 10)]
    out_defs = [(2 * GLA_QK, F32, False), (GLA_V, BF16, False), (GLA_V, F32, False), (GLA_QK, F32, False),
                (MOBA_W, BF16, True), (MOBA_W, BF16, False), (MOBA_W, BF16, True), (D, F32, False),
                (D, F32, False)]
    BS = MOBA_BLOCK
    assert rows % BS == 0
    row_spec = lambda n: pl.BlockSpec((rows, n), lambda i: (i, 0))
    blk_spec = lambda n: pl.BlockSpec((rows // BS, n, BS), lambda i: (i, 0, 0))
    return pl.pallas_call(
        _proj_kernel,
        grid=(T // rows,),
        in_specs=[row_spec(D), _full((1, D))] + [_full(w.shape) for w in weights],
        out_specs=[blk_spec(n) if t else row_spec(n) for n, _, t in out_defs],
        out_shape=[jax.ShapeDtypeStruct((T // BS, n, BS) if t else (T, n), dt) for n, dt, t in out_defs],
        compiler_params=_params("parallel"),
        name="norm_in_proj",
    )(x2d, g_mix.reshape(1, D).astype(F32), *weights)


def _gla_kernel(qk_ref, la_ref, v_ref, r_ref, g_ref, o_ref, state_ref, obuf_ref):
    C, H, DK, DV = GLA_CHUNK, GLA_HEADS, GLA_DK, GLA_DV
    rows = qk_ref.shape[1]

    @pl.when(pl.program_id(1) == 0)
    def _():
        state_ref[...] = jnp.zeros_like(state_ref)

    tri = (lax.broadcasted_iota(jnp.int32, (C, C), 0) >= lax.broadcasted_iota(jnp.int32, (C, C), 1)).astype(BF16)
    lane_head = lax.broadcasted_iota(jnp.int32, (1, H * DK), 1) // DK
    head_masks = [(lane_head == h).astype(F32) for h in range(H)]
    stack_row = lax.broadcasted_iota(jnp.int32, (H * C, C), 0) % C
    stack_col = lax.broadcasted_iota(jnp.int32, (H * C, C), 1)
    causal = stack_col <= stack_row
    same_head = (lax.broadcasted_iota(jnp.int32, (H * DV, H * DK), 0) // DV
                 == lax.broadcasted_iota(jnp.int32, (H * DV, H * DK), 1) // DK)
    scale = DK ** -0.5

    def stack(m):
        return jnp.concatenate([m * head_masks[h] for h in range(H)], axis=0).astype(BF16)

    chunks = [slice(c * C, (c + 1) * C) for c in range(rows // C)]

    def cum_log_decay(sl):
        la = la_ref[0, sl, :]
        p1 = la.astype(BF16)
        r1 = la - p1.astype(F32)
        p2 = r1.astype(BF16)
        p3 = (r1 - p2.astype(F32)).astype(BF16)
        s3 = jnp.dot(tri, jnp.concatenate([p1, p2, p3], axis=1), preferred_element_type=F32)
        w = H * DK
        return (s3[:, 0:w] + s3[:, w:2 * w]) + s3[:, 2 * w:3 * w]

    b_all = [cum_log_decay(sl) for sl in chunks]

    qe_all, ke_all, kd_all, qb_all, decay_all = [], [], [], [], []
    for sl, b in zip(chunks, b_all):
        q = qk_ref[0, sl, 0:H * DK] * scale
        k = qk_ref[0, sl, H * DK:2 * H * DK]
        b_last = b[C - 1:C, :]
        b_mid = b[C // 2 - 1:C // 2, :]
        qe_all.append(stack(q * jnp.exp(b - b_mid)))
        ke_all.append((k * jnp.exp(b_mid - b)).astype(BF16))
        kd_all.append((k * jnp.exp(b_last - b)).astype(BF16))
        qb_all.append((q * jnp.exp(b)).astype(BF16))
        decay_all.append(jnp.exp(b_last))

    att_all = [jnp.where(causal, lax.dot_general(qe, ke, (((1,), (1,)), ((), ())), preferred_element_type=F32),
                         0.0).astype(BF16) for qe, ke in zip(qe_all, ke_all)]

    o_intra_all, kv_all = [], []
    for sl, att, kd in zip(chunks, att_all, kd_all):
        v = v_ref[0, sl, :]
        o_intra_all.append(jnp.concatenate(
            [jnp.dot(att[h * C:(h + 1) * C, :], v[:, h * DV:(h + 1) * DV], preferred_element_type=F32)
             for h in range(H)], axis=1))
        kv_t = lax.dot_general(v, kd, (((0,), (0,)), ((), ())), preferred_element_type=F32)
        kv_all.append(jnp.where(same_head, kv_t, 0.0))

    state_t = state_ref[...]
    for sl, qb, decay, kv_t, o_intra in zip(chunks, qb_all, decay_all, kv_all, o_intra_all):
        o_inter = lax.dot_general(qb, state_t.astype(BF16), (((1,), (1,)), ((), ())),
                                  preferred_element_type=F32)
        obuf_ref[sl, :] = o_intra + o_inter
        state_t = decay * state_t + kv_t
    state_ref[...] = state_t

    r = r_ref[0]
    for h in range(H):
        cs = slice(h * DV, (h + 1) * DV)
        y = _rms(obuf_ref[:, cs], g_ref[:, cs])
        rh = r[:, cs]
        o_ref[0, :, cs] = (y * (rh * jax.nn.sigmoid(rh))).astype(BF16)


def _gla(qk, la, v, r, g_head, B, S):
    rows = min(GLA_ROWS, S)
    assert S % rows == 0 and rows % GLA_CHUNK == 0
    spec = lambda n: pl.BlockSpec((1, rows, n), lambda b, i: (b, i, 0))
    return pl.pallas_call(
        _gla_kernel,
        grid=(B, S // rows),
        in_specs=[spec(2 * GLA_QK), spec(GLA_QK), spec(GLA_V), spec(GLA_V), _full((1, GLA_V))],
        out_specs=spec(GLA_V),
        out_shape=jax.ShapeDtypeStruct((B, S, GLA_V), BF16),
        scratch_shapes=[pltpu.VMEM((GLA_V, GLA_QK), F32), pltpu.VMEM((rows, GLA_V), F32)],
        compiler_params=_params("parallel", "arbitrary"),
        name="gla_chunked",
    )(qk.reshape(B, S, -1), la.reshape(B, S, -1), v.reshape(B, S, -1), r.reshape(B, S, -1),
      g_head.reshape(1, GLA_V).astype(F32))


def _t5_bucket(dist):
    n = jnp.maximum(dist, 0)
    max_exact = REL_BUCKETS // 2
    nf = jnp.maximum(n, 1).astype(F32)
    large = max_exact + (jnp.log(nf / max_exact) / math.log(REL_MAX_DIST / max_exact)
                         * (REL_BUCKETS - max_exact)).astype(jnp.int32)
    large = jnp.minimum(large, REL_BUCKETS - 1)
    return jnp.where(n < max_exact, n, large)


def _moba_bias_kernel(rb_ref, o_ref):
    BS, G = MOBA_BLOCK, MOBA_GROUP
    grp, kind = pl.program_id(0), pl.program_id(1)
    d = (lax.broadcasted_iota(jnp.int32, (BS, BS), 1) - lax.broadcasted_iota(jnp.int32, (BS, BS), 0)
         + kind * BS)
    bucket = _t5_bucket(d)
    for h in range(G):
        val = jnp.zeros((BS, BS), F32)
        for bkt in range(REL_BUCKETS):
            val = jnp.where(bucket == bkt, rb_ref[bkt, grp * G + h] * LOG2E, val)
        o_ref[0, 0, :, h * BS:(h + 1) * BS] = jnp.where(d >= 0, val, NEG_INF)


def _moba_bias(rel_bias):
    BS, G = MOBA_BLOCK, MOBA_GROUP
    n_grp = MOBA_HEADS // G
    return pl.pallas_call(
        _moba_bias_kernel,
        grid=(n_grp, 2),
        in_specs=[pl.BlockSpec(memory_space=pltpu.SMEM)],
        out_specs=pl.BlockSpec((1, 1, BS, G * BS), lambda g, k: (g, k, 0, 0)),
        out_shape=jax.ShapeDtypeStruct((n_grp, 2, BS, G * BS), F32),
        compiler_params=_params("parallel", "parallel"),
        name="moba_bias_tables",
    )(rel_bias.astype(F32))


def _moba_kernel(rb_ref, q_ref, k_ref, v_ref, bias_ref, o_ref,
                 qs_ref, kmean_ref, sel_ref, m_ref, l_ref, acc_ref, sbuf_ref):
    BS, G, DH = MOBA_BLOCK, MOBA_GROUP, MOBA_DH
    NBP = kmean_ref.shape[0]
    grp, i = pl.program_id(1), pl.program_id(2)

    @pl.when(i == 0)
    def _():
        S = k_ref.shape[1]
        blk_of_key = lax.broadcasted_iota(jnp.int32, (NBP, S), 1) // BS
        ind = (blk_of_key == lax.broadcasted_iota(jnp.int32, (NBP, S), 0)).astype(BF16)
        kmean_ref[...] = jnp.dot(ind, k_ref[0], preferred_element_type=F32) * (1.0 / BS)

    qt = q_ref[0]
    sub_head = lax.broadcasted_iota(jnp.int32, (G * DH, 1), 0) // DH
    for h in range(G):
        qs_ref[:, h * BS:(h + 1) * BS] = jnp.where(sub_head == h, qt, jnp.zeros_like(qt))

    gate = jnp.dot(kmean_ref[...].astype(BF16), qs_ref[...], preferred_element_type=F32)
    blk = lax.broadcasted_iota(jnp.int32, gate.shape, 0)
    gate = jnp.where(blk < i, gate, NEG_INF)
    for t in range(MOBA_TOPK):
        mx = jnp.max(gate, axis=0, keepdims=True)
        hit = (gate == mx) & (mx > NEG_INF)
        idx = jnp.min(jnp.where(hit, blk, NBP), axis=0, keepdims=True)
        sel_ref[t:t + 1, :] = idx
        gate = jnp.where(blk == idx, NEG_INF, gate)

    def mask_row(j):
        hit = (sel_ref[0:1, :] == j) | (sel_ref[1:2, :] == j) | (sel_ref[2:3, :] == j)
        return jnp.where(hit, 0.0, NEG_INF)

    def with_ones(vt):
        return jnp.concatenate([vt, jnp.ones((8, vt.shape[1]), BF16)], axis=0)

    def far_scores(j0, n, slot):
        kj = k_ref[0, pl.ds(pl.multiple_of(j0 * BS, BS), n * BS), :]
        for h in range(G):
            cs = slice(h * BS, (h + 1) * BS)
            sbuf_ref[slot, 0:n * BS, cs] = jnp.dot(kj, qs_ref[:, cs], preferred_element_type=F32)

    j_prev = jnp.maximum(i - 1, 0)
    k_own = k_ref[0, pl.ds(pl.multiple_of(i * BS, BS), BS), :]
    k_prev = k_ref[0, pl.ds(pl.multiple_of(j_prev * BS, BS), BS), :]
    vt_near = jnp.concatenate([v_ref[i], v_ref[j_prev]], axis=1)
    prev_mask = mask_row(i - 1)
    for h in range(G):
        cs = slice(h * BS, (h + 1) * BS)
        sbuf_ref[1, 0:BS, cs] = jnp.dot(k_own, qs_ref[:, cs], preferred_element_type=F32)
        sbuf_ref[1, BS:2 * BS, cs] = jnp.dot(k_prev, qs_ref[:, cs], preferred_element_type=F32)
    far_scores(0, 2, 0)
    m_out, l_out, acc_out = [], [], []
    for h in range(G):
        cs = slice(h * BS, (h + 1) * BS)
        s_own = sbuf_ref[1, 0:BS, cs] + bias_ref[0, 0, :, cs]
        s_prev = sbuf_ref[1, BS:2 * BS, cs] + (bias_ref[0, 1, :, cs] + prev_mask[:, cs])
        m0 = jnp.maximum(jnp.max(s_own, axis=0, keepdims=True), jnp.max(s_prev, axis=0, keepdims=True))
        pb = jnp.concatenate([jnp.exp2(s_own - m0).astype(BF16), jnp.exp2(s_prev - m0).astype(BF16)], axis=0)
        pv = jnp.dot(with_ones(vt_near[h * DH:(h + 1) * DH, :]), pb, preferred_element_type=F32)
        m_out.append(m0)
        l_out.append(pv[DH:DH + 1, :])
        acc_out.append(pv[0:DH, :])
    m_ref[...] = jnp.concatenate(m_out, axis=1)
    l_ref[...] = jnp.concatenate(l_out, axis=1)
    acc_ref[...] = jnp.concatenate(acc_out, axis=1)

    lane_head = lax.broadcasted_iota(jnp.int32, (1, G * BS), 1) // BS
    far_bias = jnp.zeros((1, G * BS), F32)
    for h in range(G):
        far_bias = jnp.where(lane_head == h, rb_ref[REL_BUCKETS - 1, grp * G + h] * LOG2E, far_bias)

    def visit_far(j0, n, slot, ahead=None):
        vjt = jnp.concatenate([v_ref[j0 + t] for t in range(n)], axis=1)
        addend = [far_bias + mask_row(j0 + t) for t in range(n)]
        m_old, l_old, acc_old = m_ref[...], l_ref[...], acc_ref[...]
        m_out, l_out, acc_out = [], [], []
        if ahead is not None:
            k_next = k_ref[0, pl.ds(pl.multiple_of(ahead[0] * BS, BS), 2 * BS), :]
        for h in range(G):
            cs = slice(h * BS, (h + 1) * BS)
            if ahead is not None:
                sbuf_ref[ahead[1], :, cs] = jnp.dot(k_next, qs_ref[:, cs], preferred_element_type=F32)
            s = sbuf_ref[slot, 0:n * BS, cs]
            mx = jnp.max(s[0:BS], axis=0, keepdims=True) + addend[0][:, cs]
            for t in range(1, n):
                mx = jnp.maximum(mx, jnp.max(s[t * BS:(t + 1) * BS], axis=0, keepdims=True) + addend[t][:, cs])
            m_new = jnp.maximum(m_old[:, cs], mx)
            pb = jnp.concatenate([jnp.exp2(s[t * BS:(t + 1) * BS] - (m_new - addend[t][:, cs])).astype(BF16)
                                  for t in range(n)], axis=0)
            pv = jnp.dot(with_ones(vjt[h * DH:(h + 1) * DH, :]), pb, preferred_element_type=F32)
            alpha = jnp.exp2(m_old[:, cs] - m_new)
            m_out.append(m_new)
            l_out.append(alpha * l_old[:, cs] + pv[DH:DH + 1, :])
            acc_out.append(alpha * acc_old[:, cs] + pv[0:DH, :])
        m_ref[...] = jnp.concatenate(m_out, axis=1)
        l_ref[...] = jnp.concatenate(l_out, axis=1)
        acc_ref[...] = jnp.concatenate(acc_out, axis=1)

    n_far = jnp.maximum(i - 1, 0)
    n_pairs = lax.shift_right_logical(n_far, 1)

    def pair(p, slot, look_ahead=True):
        ahead = (2 * jnp.minimum(p + 1, n_pairs - 1), 1 - slot) if look_ahead else None
        visit_far(2 * p, 2, slot, ahead=ahead)

    def far_octet(w, carry):
        for t in range(4):
            pair(4 * w + t, t % 2)
        return carry

    lax.fori_loop(0, lax.shift_right_logical(n_pairs, 2), far_octet, 0)

    @pl.when((n_pairs & 2) != 0)
    def _():
        base = 4 * lax.shift_right_logical(n_pairs, 2)
        pair(base, 0)
        pair(base + 1, 1)

    @pl.when((n_pairs & 1) != 0)
    def _():
        pair(n_pairs - 1, 0, look_ahead=False)

    @pl.when(n_far % 2 == 1)
    def _():
        far_scores(n_far - 1, 1, 1)
        visit_far(n_far - 1, 1, 1)

    out_t = jnp.concatenate([acc_ref[:, h * BS:(h + 1) * BS] / l_ref[:, h * BS:(h + 1) * BS] for h in range(G)],
                            axis=0)
    o_ref[0] = out_t.T.astype(o_ref.dtype)


def _moba(qm_t, km, vm_t, bias, rel_bias, B, S):
    BS, G, DH = MOBA_BLOCK, MOBA_GROUP, MOBA_DH
    W = G * DH
    n_grp = MOBA_HEADS // G
    assert S % BS == 0 and S >= 2 * BS
    NB = S // BS
    NBP = -(-NB // 8) * 8
    return pl.pallas_call(
        _moba_kernel,
        grid=(B, n_grp, NB),
        in_specs=[pl.BlockSpec(memory_space=pltpu.SMEM),
                  pl.BlockSpec((1, W, BS), lambda b, g, i: (b * NB + i, g, 0)),
                  pl.BlockSpec((1, S, W), lambda b, g, i: (b, 0, g)),
                  pl.BlockSpec((NB, W, BS), lambda b, g, i: (b, g, 0)),
                  pl.BlockSpec((1, 2, BS, G * BS), lambda b, g, i: (g, 0, 0, 0))],
        out_specs=pl.BlockSpec((1, BS, W), lambda b, g, i: (b, i, g)),
        out_shape=jax.ShapeDtypeStruct((B, S, MOBA_W), BF16),
        scratch_shapes=[pltpu.VMEM((W, G * BS), BF16), pltpu.VMEM((NBP, W), F32),
                        pltpu.VMEM((8, G * BS), jnp.int32),
                        pltpu.VMEM((1, G * BS), F32), pltpu.VMEM((1, G * BS), F32),
                        pltpu.VMEM((DH, G * BS), F32), pltpu.VMEM((2, 2 * BS, G * BS), F32)],
        compiler_params=_params("parallel", "parallel", "arbitrary"),
        name="moba_attention",
    )(rel_bias.astype(F32), qm_t, km.reshape(B, S, -1), vm_t, bias)


def _mem_kv_kernel(mem_ref, g_ref, w_ref, k_ref, v_ref):
    kv = jnp.dot(_rms(mem_ref[0], g_ref[...]).astype(BF16), w_ref[...], preferred_element_type=F32)
    k_ref[0] = kv[:, :MEM_W].astype(BF16)
    v_ref[0] = kv[:, MEM_W:].astype(BF16)


def _mem_kv(mem, g_mem, w_ckv):
    B, M, D = mem.shape
    spec = pl.BlockSpec((1, M, MEM_W), lambda b: (b, 0, 0))
    return pl.pallas_call(
        _mem_kv_kernel,
        grid=(B,),
        in_specs=[pl.BlockSpec((1, M, D), lambda b: (b, 0, 0)), _full((1, D)), _full((D, 2 * MEM_W))],
        out_specs=[spec, spec],
        out_shape=[jax.ShapeDtypeStruct((B, M, MEM_W), BF16)] * 2,
        compiler_params=_params("parallel"),
        name="memory_kv",
    )(mem, g_mem.reshape(1, D).astype(F32), w_ckv.astype(BF16))


INFO_W0, INFO_W1, INFO_E0, INFO_E1, INFO_R0, INFO_R1 = range(6)
ROUTER_GROUP_LANE0, ROUTER_EXPERT_LANE0 = 0, N_GROUPS


def _mix_kernel(x_ref, og_ref, om_ref, zg_ref, zm_ref, mk_ref, mv_ref, gc_ref, gm_ref,
                wpg, wpm, wout, wcq, wco, wr, br,
                x2_ref, info_ref, cnt_ref, base_ref):
    first = (pl.program_id(0) == 0) & (pl.program_id(1) == 0)

    @pl.when(first)
    def _():
        base_ref[...] = jnp.zeros_like(base_ref)

    def mm(a, w_ref):
        return jnp.dot(a.astype(BF16), w_ref[...], preferred_element_type=F32)

    n_rows = x_ref.shape[1]
    sub = n_rows // MIX_PARTS
    parts = [slice(p * sub, (p + 1) * sub) for p in range(MIX_PARTS)]

    merged = [jax.nn.sigmoid(zg_ref[0, rs, :]) * mm(og_ref[0, rs, :], wpg)
              + jax.nn.sigmoid(zm_ref[0, rs, :]) * mm(om_ref[0, rs, :], wpm) for rs in parts]
    x1 = [x_ref[0, rs, :] + mm(m, wout) for rs, m in zip(parts, merged)]

    qc = [mm(_rms(v, gc_ref[...]), wcq).astype(BF16) for v in x1]

    def mem_attention(q):
        heads = []
        for h in range(MEM_HEADS):
            cs = slice(h * MEM_DH, (h + 1) * MEM_DH)
            s = lax.dot_general(q[:, cs], mk_ref[0, :, cs], (((1,), (1,)), ((), ())),
                                preferred_element_type=F32) * (MEM_DH ** -0.5)
            p = jnp.exp(s - jnp.max(s, axis=-1, keepdims=True))
            o = jnp.dot(p.astype(BF16), mv_ref[0, :, cs], preferred_element_type=F32)
            heads.append(o / jnp.sum(p, axis=-1, keepdims=True))
        return jnp.concatenate(heads, axis=-1)

    attn = [mem_attention(q) for q in qc]
    x2 = [v + mm(a, wco) for v, a in zip(x1, attn)]
    for rs, v in zip(parts, x2):
        x2_ref[0, rs, :] = v

    logits = [mm(_rms(v, gm_ref[...]), wr) + br[...] for v in x2]
    lane = lax.broadcasted_iota(jnp.int32, (sub, LANES), 1)
    is_grp = lane < N_GROUPS
    e_id = lane - ROUTER_EXPERT_LANE0

    def route(lg):
        gl = jnp.where(is_grp, lg, NEG_INF)
        ge = jnp.exp(gl - jnp.max(gl, axis=-1, keepdims=True))
        g_prob = ge / jnp.sum(ge, axis=-1, keepdims=True)
        p_grp = jnp.max(g_prob, axis=-1, keepdims=True)
        grp = jnp.min(jnp.where((g_prob == p_grp) & is_grp, lane, LANES), axis=-1, keepdims=True)
        in_grp = (e_id >= grp * EXPERTS_PER_GROUP) & (e_id < (grp + 1) * EXPERTS_PER_GROUP)
        el = jnp.where(in_grp, lg, NEG_INF)
        ee = jnp.exp(el - jnp.max(el, axis=-1, keepdims=True))
        e_prob = jnp.where(in_grp, ee / jnp.sum(ee, axis=-1, keepdims=True), -1.0)
        p0 = jnp.max(e_prob, axis=-1, keepdims=True)
        e0 = jnp.min(jnp.where(e_prob == p0, e_id, LANES), axis=-1, keepdims=True)
        e_rest = jnp.where(e_id == e0, -1.0, e_prob)
        p1 = jnp.max(e_rest, axis=-1, keepdims=True)
        e1 = jnp.min(jnp.where(e_rest == p1, e_id, LANES), axis=-1, keepdims=True)
        return e0, e1, p_grp * p0 / (p0 + p1), p_grp * p1 / (p0 + p1)

    routed = [route(lg) for lg in logits]

    before = (lax.broadcasted_iota(jnp.int32, (sub, sub), 1)
              < lax.broadcasted_iota(jnp.int32, (sub, sub), 0)).astype(BF16)
    base = base_ref[...]
    for rs, (e0, e1, w0, w1) in zip(parts, routed):
        onehot = ((lane == e0) | (lane == e1)).astype(F32)
        seen = base + jnp.dot(before, onehot.astype(BF16), preferred_element_type=F32)
        r0 = jnp.sum(jnp.where(lane == e0, seen, 0.0), axis=-1, keepdims=True)
        r1 = jnp.sum(jnp.where(lane == e1, seen, 0.0), axis=-1, keepdims=True)
        base = base + jnp.sum(onehot, axis=0, keepdims=True)
        info = jnp.zeros((sub, LANES), F32)
        for ln, val in ((INFO_W0, w0), (INFO_W1, w1), (INFO_E0, e0.astype(F32)), (INFO_E1, e1.astype(F32)),
                        (INFO_R0, r0), (INFO_R1, r1)):
            info = jnp.where(lane == ln, val, info)
        info_ref[0, rs, :] = info
    base_ref[...] = base
    cnt_ref[...] = base


def _mix(x, o_g, o_m, z_g, z_m, mem_k, mem_v, g_cross, g_moe, w_proj_gla, w_proj_moba, w_out, w_cq, w_co,
         w_rg, b_rg, w_re, b_re):
    B, S, D = x.shape
    M = mem_k.shape[1]
    rows = min(MIX_ROWS, S)
    assert S % rows == 0
    pad = LANES - N_GROUPS - N_EXPERTS
    wr = jnp.pad(jnp.concatenate([w_rg, w_re], axis=1), ((0, 0), (0, pad))).astype(BF16)
    br = jnp.pad(jnp.concatenate([b_rg, b_re]), (0, pad)).reshape(1, LANES).astype(F32)
    weights = [w_proj_gla.astype(BF16), w_proj_moba.astype(BF16), w_out.astype(BF16), w_cq.astype(BF16),
               w_co.astype(BF16), wr, br]
    tile = lambda n: pl.BlockSpec((1, rows, n), lambda b, i: (b, i, 0))
    memspec = pl.BlockSpec((1, M, MEM_W), lambda b, i: (b, 0, 0))
    return pl.pallas_call(
        _mix_kernel,
        grid=(B, S // rows),
        in_specs=[tile(D), tile(GLA_V), tile(MOBA_W), tile(D), tile(D), memspec, memspec,
                  _full((1, D)), _full((1, D))] + [_full(w.shape) for w in weights],
        out_specs=[tile(D), tile(LANES), _full((1, LANES))],
        out_shape=[jax.ShapeDtypeStruct((B, S, D), F32), jax.ShapeDtypeStruct((B, S, LANES), F32),
                   jax.ShapeDtypeStruct((1, LANES), F32)],
        scratch_shapes=[pltpu.VMEM((1, LANES), F32)],
        compiler_params=_params("arbitrary", "arbitrary"),
        name="merge_memattn_router",
    )(x, o_g, o_m, z_g.reshape(B, S, D), z_m.reshape(B, S, D), mem_k, mem_v,
      g_cross.reshape(1, D).astype(F32), g_moe.reshape(1, D).astype(F32), *weights)


def _plan_kernel(cnt_ref, info_ref, dest_ref, blk_ref):
    rows = info_ref.shape[0]
    lane1 = lax.broadcasted_iota(jnp.int32, (1, LANES), 1)
    nblk = jnp.floor((cnt_ref[...] + (EXPERT_ROWS - 1)) * (1.0 / EXPERT_ROWS))
    nblk = jnp.where(lane1 < N_EXPERTS, nblk, 0.0)
    hi = jnp.floor(nblk * (1.0 / 256.0))
    lo = nblk - 256.0 * hi
    upto = (lax.broadcasted_iota(jnp.int32, (LANES, LANES), 0)
            <= lax.broadcasted_iota(jnp.int32, (LANES, LANES), 1)).astype(BF16)
    digits = jnp.concatenate([jnp.broadcast_to(hi, (8, LANES)), jnp.broadcast_to(lo, (8, LANES))], axis=0)
    sums = jnp.dot(digits.astype(BF16), upto, preferred_element_type=F32)
    pend = sums[0:1] * 256.0 + sums[8:9]
    pstart_rows = (pend - nblk) * EXPERT_ROWS

    info = info_ref[...]
    lane = lax.broadcasted_iota(jnp.int32, (rows, LANES), 1)

    def field(ln):
        return jnp.sum(jnp.where(lane == ln, info, 0.0), axis=-1, keepdims=True)

    def dest(e, r):
        return jnp.sum(jnp.where(lane == e.astype(jnp.int32), pstart_rows, 0.0), axis=-1, keepdims=True) + r

    d0 = dest(field(INFO_E0), field(INFO_R0))
    d1 = dest(field(INFO_E1), field(INFO_R1))
    cols = jnp.where(lane == 0, d0, jnp.where(lane == 1, d1, 0.0))
    move = dest_ref.shape[2]
    for c in range(dest_ref.shape[0]):
        dest_ref[c] = cols[c * move:(c + 1) * move, :].T[0:8, :].astype(jnp.int32)

    @pl.when(pl.program_id(0) == 0)
    def _():
        n = lax.broadcasted_iota(jnp.int32, (blk_ref.shape[0], LANES), 0).astype(F32)
        done = jnp.where((pend <= n) & (lane1 < N_EXPERTS), 1.0, 0.0)
        e = jnp.minimum(jnp.sum(done, axis=-1, keepdims=True), N_EXPERTS - 1.0)
        blk_ref[...] = e.astype(jnp.int32)


def _plan(counts, info2d, n_blk):
    T = info2d.shape[0]
    move = min(MOVE_ROWS, T)
    rows = min(PLAN_ROWS, T)
    assert T % rows == 0 and rows % move == 0
    n_blk_pad = -(-n_blk // 8) * 8
    return pl.pallas_call(
        _plan_kernel,
        grid=(T // rows,),
        in_specs=[_full((1, LANES)), pl.BlockSpec((rows, LANES), lambda i: (i, 0))],
        out_specs=[pl.BlockSpec((rows // move, 8, move), lambda i: (i, 0, 0)), _full((n_blk_pad, 1))],
        out_shape=[jax.ShapeDtypeStruct((T // move, 8, move), jnp.int32),
                   jax.ShapeDtypeStruct((n_blk_pad, 1), jnp.int32)],
        compiler_params=_params("arbitrary"),
        name="dispatch_plan",
    )(counts, info2d)


def _dispatch_kernel(dest_ref, x_ref, g_ref, xs_in_ref, xs_ref, hbuf_ref, sems):
    del xs_in_ref
    rows, D = x_ref.shape
    i, n = pl.program_id(0), pl.num_programs(0)
    buf = i % 2
    packed = _pack_bf16_pairs(_rms(x_ref[...], g_ref[...]))
    hbuf_ref[buf] = packed.reshape(rows // SUBLANES, SUBLANES, D // 2)

    def row_copy(b, g, k, dst):
        return pltpu.make_async_copy(hbuf_ref.at[b, g, pl.ds(k, 1)], xs_ref.at[pl.ds(dst, 1)], sems.at[b])

    def start(g, k):
        r = g * SUBLANES + k
        row_copy(buf, g, k, dest_ref[0, 0, r]).start()
        row_copy(buf, g, k, dest_ref[0, 1, r]).start()

    def wait_all(b):
        def wait(g, k):
            row_copy(b, g, k, 0).wait()
            row_copy(b, g, k, 0).wait()
        _for_each_row(rows, wait)

    _for_each_row(rows, start)

    @pl.when(i > 0)
    def _():
        wait_all(1 - buf)

    @pl.when(i == n - 1)
    def _():
        wait_all(buf)


def _dispatch(dest, x2d, g_moe, cap):
    T, D = x2d.shape
    rows = dest.shape[2]
    xs0 = jnp.zeros((cap, D // 2), jnp.uint32)
    return pl.pallas_call(
        _dispatch_kernel,
        grid=(T // rows,),
        in_specs=[pl.BlockSpec((1, 8, rows), lambda i: (i, 0, 0), memory_space=pltpu.SMEM),
                  pl.BlockSpec((rows, D), lambda i: (i, 0)), _full((1, D)),
                  pl.BlockSpec(memory_space=pl.ANY)],
        out_specs=pl.BlockSpec(memory_space=pl.ANY),
        out_shape=jax.ShapeDtypeStruct((cap, D // 2), jnp.uint32),
        scratch_shapes=[pltpu.VMEM((2, rows // SUBLANES, SUBLANES, D // 2), jnp.uint32),
                        pltpu.SemaphoreType.DMA((2,))],
        input_output_aliases={3: 0},
        compiler_params=_params("arbitrary"),
        name="moe_dispatch",
    )(dest, x2d, g_moe.reshape(1, D).astype(F32), xs0)


def _expert_kernel(n_pieces, blk_e_ref, xs_ref, wg_ref, wu_ref, wd_ref, *rest):
    y_refs, (wg_bf, wu_bf, wd_bf) = rest[:n_pieces], rest[n_pieces:]
    n = pl.program_id(0)
    prev = blk_e_ref[jnp.maximum(n - 1, 0)]

    @pl.when((n == 0) | (blk_e_ref[n] != prev))
    def _():
        wg_bf[...] = wg_ref[0].astype(BF16)
        wu_bf[...] = wu_ref[0].astype(BF16)
        wd_bf[...] = wd_ref[0].astype(BF16)

    sub = xs_ref.shape[0] // EXPERT_PARTS
    parts = [slice(p * sub, (p + 1) * sub) for p in range(EXPERT_PARTS)]
    xb = [_unpack_bf16_pairs(xs_ref[rs, :]).astype(BF16) for rs in parts]
    gate = [jnp.dot(v, wg_bf[...], preferred_element_type=F32) for v in xb]
    up = [jnp.dot(v, wu_bf[...], preferred_element_type=F32) for v in xb]
    hid = [(g * jax.nn.sigmoid(g) * u).astype(BF16) for g, u in zip(gate, up)]
    for rs, hv in zip(parts, hid):
        words = _pack_bf16_pairs(jnp.dot(hv, wd_bf[...], preferred_element_type=F32))
        for c, y_ref in enumerate(y_refs):
            y_ref[rs, :] = words[:, c * SC_GATHER_WORDS:(c + 1) * SC_GATHER_WORDS]


def _experts(blk_e, xs, w_gate, w_up, w_down):
    cap = xs.shape[0]
    _, D, DE = w_gate.shape
    n_blk = cap // EXPERT_ROWS
    n_pieces = D // 2 // SC_GATHER_WORDS
    rows_spec = lambda w: pl.BlockSpec((EXPERT_ROWS, w), lambda n, e: (n, 0))
    return pl.pallas_call(
        functools.partial(_expert_kernel, n_pieces),
        grid_spec=pltpu.PrefetchScalarGridSpec(
            num_scalar_prefetch=1,
            grid=(n_blk,),
            in_specs=[rows_spec(D // 2),
                      pl.BlockSpec((1, D, DE), lambda n, e: (e[n], 0, 0)),
                      pl.BlockSpec((1, D, DE), lambda n, e: (e[n], 0, 0)),
                      pl.BlockSpec((1, DE, D), lambda n, e: (e[n], 0, 0))],
            out_specs=[rows_spec(SC_GATHER_WORDS)] * n_pieces,
            scratch_shapes=[pltpu.VMEM((D, DE), BF16), pltpu.VMEM((D, DE), BF16), pltpu.VMEM((DE, D), BF16)]),
        out_shape=[jax.ShapeDtypeStruct((cap, SC_GATHER_WORDS), jnp.uint32)] * n_pieces,
        compiler_params=_params("arbitrary"),
        name="moe_experts",
    )(blk_e, xs, w_gate, w_up, w_down)


def _combine_kernel(final_norm, dest_ref, dest_next_ref, x_ref, info_ref, g_ref, y_ref, o_ref, ybuf_ref, sems):
    rows, D = x_ref.shape
    i, n = pl.program_id(0), pl.num_programs(0)
    buf = i % 2

    def row_copy(b, g, k, slot, src):
        return pltpu.make_async_copy(y_ref.at[pl.ds(src, 1)], ybuf_ref.at[b, slot, g, pl.ds(k, 1)], sems.at[b])

    def fetch(b, d_ref):
        def start(g, k):
            r = g * SUBLANES + k
            row_copy(b, g, k, 0, d_ref[0, 0, r]).start()
            row_copy(b, g, k, 1, d_ref[0, 1, r]).start()
        _for_each_row(rows, start)

    @pl.when(i == 0)
    def _():
        fetch(0, dest_ref)

    @pl.when(i + 1 < n)
    def _():
        fetch(1 - buf, dest_next_ref)

    def wait(g, k):
        row_copy(buf, g, k, 0, 0).wait()
        row_copy(buf, g, k, 1, 0).wait()

    _for_each_row(rows, wait)
    info = info_ref[...]
    w0 = info[:, INFO_W0:INFO_W0 + 1]
    w1 = info[:, INFO_W1:INFO_W1 + 1]
    y0 = _unpack_bf16_pairs(ybuf_ref[buf, 0].reshape(rows, D // 2))
    y1 = _unpack_bf16_pairs(ybuf_ref[buf, 1].reshape(rows, D // 2))
    out = x_ref[...] + (w0 * y0 + w1 * y1)
    o_ref[...] = _rms(out, g_ref[...]) if final_norm else out


def _combine(dest, x2d, info2d, g_final, y, final_norm):
    T, D = x2d.shape
    rows = dest.shape[2]
    n = T // rows
    return pl.pallas_call(
        functools.partial(_combine_kernel, final_norm),
        grid=(n,),
        in_specs=[pl.BlockSpec((1, 8, rows), lambda i: (i, 0, 0), memory_space=pltpu.SMEM),
                  pl.BlockSpec((1, 8, rows), lambda i: (jnp.minimum(i + 1, n - 1), 0, 0), memory_space=pltpu.SMEM),
                  pl.BlockSpec((rows, D), lambda i: (i, 0)), pl.BlockSpec((rows, LANES), lambda i: (i, 0)),
                  _full((1, D)), pl.BlockSpec(memory_space=pl.ANY)],
        out_specs=pl.BlockSpec((rows, D), lambda i: (i, 0)),
        out_shape=jax.ShapeDtypeStruct((T, D), F32),
        scratch_shapes=[pltpu.VMEM((2, 2, rows // SUBLANES, SUBLANES, D // 2), jnp.uint32),
                        pltpu.SemaphoreType.DMA((2,))],
        compiler_params=_params("arbitrary"),
        name="moe_combine_final_norm",
    )(dest, dest, x2d, info2d, g_final.reshape(1, D).astype(F32), y)


def _sc_gather_rows(table, idx):
    M = idx.shape[1]
    W = table.shape[1]
    n_inner = 32
    assert M % (SC_GATHER_ROWS * n_inner) == 0
    n_outer = M // (SC_GATHER_ROWS * n_inner)
    mesh = plsc.VectorSubcoreMesh(core_axis_name="c", subcore_axis_name="s")

    @pl.kernel(out_type=jax.ShapeDtypeStruct((M, W), table.dtype), mesh=mesh, scratch_types=[])
    def gather_kernel(t_hbm, i_hbm, o_hbm):
        def body(i_vmem, o_vmem):
            pltpu.sync_copy(t_hbm.at[i_vmem.at[0]], o_vmem)

        pltpu.emit_pipeline(
            body,
            grid=(n_outer, n_inner),
            in_specs=[pl.BlockSpec((1, SC_GATHER_ROWS), index_map=lambda i, j: (0, i * n_inner + j))],
            out_specs=[pl.BlockSpec((SC_GATHER_ROWS, W), index_map=lambda i, j: (i * n_inner + j, 0))],
            core_axis_name=("c", "s"),
            dimension_semantics=(pltpu.PARALLEL, pltpu.PARALLEL),
        )(i_hbm, o_hbm)

    return gather_kernel(table, idx)


def _combine_dense_kernel(final_norm, n_pieces, *refs):
    yg_refs, (x_ref, info_ref, g_ref, o_ref) = refs[:n_pieces], refs[n_pieces:]
    info = info_ref[...]
    w0 = info[:, INFO_W0:INFO_W0 + 1]
    w1 = info[:, INFO_W1:INFO_W1 + 1]
    y = [_unpack_bf16_pairs(jnp.concatenate([r[slot] for r in yg_refs], axis=1)) for slot in range(TOPK_IN_GROUP)]
    out = x_ref[...] + (w0 * y[0] + w1 * y[1])
    o_ref[...] = _rms(out, g_ref[...]) if final_norm else out


def _combine_dense(yg_pieces, x2d, info2d, g_final, final_norm):
    T, D = x2d.shape
    rows = min(MOVE_ROWS, T)
    n_pieces = len(yg_pieces)
    return pl.pallas_call(
        functools.partial(_combine_dense_kernel, final_norm, n_pieces),
        grid=(T // rows,),
        in_specs=[pl.BlockSpec((TOPK_IN_GROUP, rows, SC_GATHER_WORDS), lambda i: (0, i, 0))] * n_pieces
        + [pl.BlockSpec((rows, D), lambda i: (i, 0)), pl.BlockSpec((rows, LANES), lambda i: (i, 0)),
           _full((1, D))],
        out_specs=pl.BlockSpec((rows, D), lambda i: (i, 0)),
        out_shape=jax.ShapeDtypeStruct((T, D), F32),
        compiler_params=_params("parallel"),
        name="moe_combine_dense_final_norm",
    )(*yg_pieces, x2d, info2d, g_final.reshape(1, D).astype(F32))


def kernel(x, mem, g_mem, rel_bias, g_mix, w_in, w_alpha_up, b_alpha, g_gla_head, w_proj_gla, w_proj_moba,
           w_out, g_cross, w_cq, w_ckv, w_co, g_moe, w_router_group, b_router_group, w_router_expert,
           b_router_expert, w_exp_gate, w_exp_up, w_exp_down, g_final):
    B, S, D = x.shape
    T = B * S
    depth = g_mix.shape[0]
    n_assign = T * TOPK_IN_GROUP
    n_blk = -(-(n_assign + N_EXPERTS * (EXPERT_ROWS - 1)) // EXPERT_ROWS)
    cap = n_blk * EXPERT_ROWS

    mem_bias = _moba_bias(rel_bias)
    for l in range(depth):
        qk, v_g, r_g, la, q_m, k_m, v_m, z_g, z_m = _project(x.reshape(T, D), g_mix[l], w_in[l], w_alpha_up[l],
                                                             b_alpha[l])
        o_g = _gla(qk, la, v_g, r_g, g_gla_head[l], B, S)
        o_m = _moba(q_m, k_m, v_m, mem_bias, rel_bias, B, S)
        mem_k, mem_v = _mem_kv(mem, g_mem, w_ckv[l])
        x2, info, counts = _mix(x, o_g, o_m, z_g, z_m, mem_k, mem_v, g_cross[l], g_moe[l], w_proj_gla[l],
                                w_proj_moba[l], w_out[l], w_cq[l], w_co[l], w_router_group[l], b_router_group[l],
                                w_router_expert[l], b_router_expert[l])
        x2d, info2d = x2.reshape(T, D), info.reshape(T, LANES)
        dest, blk_e = _plan(counts, info2d, n_blk)
        xs = _dispatch(dest, x2d, g_moe[l], cap)
        y_pieces = _experts(blk_e[:n_blk, 0], xs, w_exp_gate[l], w_exp_up[l], w_exp_down[l])
        idx = jnp.transpose(dest[:, 0:TOPK_IN_GROUP, :], (1, 0, 2)).reshape(1, TOPK_IN_GROUP * T)
        yg = [_sc_gather_rows(y, idx).reshape(TOPK_IN_GROUP, T, SC_GATHER_WORDS) for y in y_pieces]
        x = _combine_dense(yg, x2d, info2d, g_final, final_norm=(l == depth - 1)).reshape(B, S, D)
    return x
```

```python
import functools
import math

import jax
import jax.numpy as jnp
from jax import lax
from jax.experimental import pallas as pl
from jax.experimental.pallas import tpu as pltpu
from jax.experimental.pallas import tpu_sc as plsc

F32 = jnp.float32
BF16 = jnp.bfloat16
NEG_INF = float("-inf")

EPS = 1e-6
GLA_HEADS, GLA_DK, GLA_DV, GLA_LOWRANK, GLA_TAU, GLA_CHUNK = 4, 64, 128, 16, 16.0, 64
GLA_QK, GLA_V = GLA_HEADS * GLA_DK, GLA_HEADS * GLA_DV
MOBA_HEADS, MOBA_DH, MOBA_BLOCK, MOBA_TOPK = 8, 64, 256, 3
MOBA_W = MOBA_HEADS * MOBA_DH
LOG2E = math.log2(math.e)
MOBA_Q_SCALE = MOBA_DH ** -0.5 * LOG2E
REL_BUCKETS, REL_MAX_DIST = 32, 128
MEM_HEADS, MEM_DH = 4, 128
MEM_W = MEM_HEADS * MEM_DH
N_GROUPS, EXPERTS_PER_GROUP, TOPK_IN_GROUP = 4, 8, 2
N_EXPERTS = N_GROUPS * EXPERTS_PER_GROUP

LANES = 128
SUBLANES = 8
VMEM_LIMIT_BYTES = 56 * 1024 * 1024

PROJ_ROWS = 512
GLA_ROWS = 512
MOBA_GROUP = 4
MIX_ROWS = 512
MIX_PARTS = 2
EXPERT_ROWS = 256
MOVE_ROWS = 512
PLAN_ROWS = 2048
SC_GATHER_ROWS = 128
SC_GATHER_WORDS = 256
EXPERT_PARTS = 2


def _params(*semantics):
    return pltpu.CompilerParams(dimension_semantics=semantics, vmem_limit_bytes=VMEM_LIMIT_BYTES)


def _full(shape):
    return pl.BlockSpec(shape, lambda *_: (0,) * len(shape))


def _rms(x, g):
    return x * lax.rsqrt(jnp.mean(x * x, axis=-1, keepdims=True) + EPS) * g


def _pack_bf16_pairs(x):
    n = x.shape[1] // 2
    bits = pltpu.bitcast(x.astype(BF16).astype(F32), jnp.uint32)
    return bits[:, n:] | (bits[:, :n] >> 16)


def _unpack_bf16_pairs(w):
    lo = pltpu.bitcast(w << 16, F32)
    hi = pltpu.bitcast(w & jnp.uint32(0xFFFF0000), F32)
    return jnp.concatenate([lo, hi], axis=1)


def _proj_kernel(x_ref, g_ref, w_qk, w_v, w_r, w_a, w_up, b_a, w_qm, w_km, w_vm, w_zg, w_zm,
                 o_qk, o_v, o_r, o_la, o_qm, o_km, o_vm, o_zg, o_zm):
    h = _rms(x_ref[...], g_ref[...]).astype(BF16)

    def mm(w_ref):
        return jnp.dot(h, w_ref[...], preferred_element_type=F32)

    o_qk[...] = mm(w_qk)
    o_v[...] = mm(w_v).astype(BF16)
    o_r[...] = mm(w_r)
    a_lr = mm(w_a).astype(BF16)
    pre = jnp.dot(a_lr, w_up[...], preferred_element_type=F32) + b_a[...]
    o_la[...] = jax.nn.log_sigmoid(pre) * (1.0 / GLA_TAU)

    def mm_t(wt_ref):
        return lax.dot_general(wt_ref[...], h, (((1,), (1,)), ((), ())), preferred_element_type=F32)

    def store_blocks(o_ref, val_t):
        for c in range(o_ref.shape[0]):
            o_ref[c] = val_t[:, c * MOBA_BLOCK:(c + 1) * MOBA_BLOCK]

    store_blocks(o_qm, (mm_t(w_qm) * MOBA_Q_SCALE).astype(BF16))
    o_km[...] = mm(w_km).astype(BF16)
    store_blocks(o_vm, mm_t(w_vm).astype(BF16))
    o_zg[...] = mm(w_zg)
    o_zm[...] = mm(w_zm)


def _project(x2d, g_mix, w_in, w_alpha_up, b_alpha):
    T, D = x2d.shape
    rows = min(PROJ_ROWS, T)
    assert T % rows == 0
    splits = (GLA_QK, GLA_QK, GLA_V, GLA_V, GLA_LOWRANK, MOBA_W, MOBA_W, MOBA_W, D, D)
    offs = [0]
    for s in splits:
        offs.append(offs[-1] + s)
    wb = w_in.astype(BF16)
    sec = lambda i, j: wb[:, offs[i]:offs[j]]
    w_a = jnp.pad(sec(4, 5), ((0, 0), (0, LANES - GLA_LOWRANK)))
    w_up = jnp.pad(w_alpha_up.astype(BF16), ((0, LANES - GLA_LOWRANK), (0, 0)))
    weights = [sec(0, 2), sec(2, 3), sec(3, 4), w_a, w_up, b_alpha.reshape(1, GLA_QK).astype(F32),
               sec(5, 6).T, sec(6, 7), sec(7, 8).T, sec(8, 9), sec(9, 10)]
    out_defs = [(2 * GLA_QK, F32, False), (GLA_V, BF16, False), (GLA_V, F32, False), (GLA_QK, F32, False),
                (MOBA_W, BF16, True), (MOBA_W, BF16, False), (MOBA_W, BF16, True), (D, F32, False),
                (D, F32, False)]
    BS = MOBA_BLOCK
    assert rows % BS == 0
    row_spec = lambda n: pl.BlockSpec((rows, n), lambda i: (i, 0))
    blk_spec = lambda n: pl.BlockSpec((rows // BS, n, BS), lambda i: (i, 0, 0))
    return pl.pallas_call(
        _proj_kernel,
        grid=(T // rows,),
        in_specs=[row_spec(D), _full((1, D))] + [_full(w.shape) for w in weights],
        out_specs=[blk_spec(n) if t else row_spec(n) for n, _, t in out_defs],
        out_shape=[jax.ShapeDtypeStruct((T // BS, n, BS) if t else (T, n), dt) for n, dt, t in out_defs],
        compiler_params=_params("parallel"),
        name="norm_in_proj",
    )(x2d, g_mix.reshape(1, D).astype(F32), *weights)


def _gla_kernel(qk_ref, la_ref, v_ref, r_ref, g_ref, o_ref, state_ref, obuf_ref):
    C, H, DK, DV = GLA_CHUNK, GLA_HEADS, GLA_DK, GLA_DV
    rows = qk_ref.shape[1]

    @pl.when(pl.program_id(1) == 0)
    def _():
        state_ref[...] = jnp.zeros_like(state_ref)

    tri = (lax.broadcasted_iota(jnp.int32, (C, C), 0) >= lax.broadcasted_iota(jnp.int32, (C, C), 1)).astype(BF16)
    lane_head = lax.broadcasted_iota(jnp.int32, (1, H * DK), 1) // DK
    head_masks = [(lane_head == h).astype(F32) for h in range(H)]
    stack_row = lax.broadcasted_iota(jnp.int32, (H * C, C), 0) % C
    stack_col = lax.broadcasted_iota(jnp.int32, (H * C, C), 1)
    causal = stack_col <= stack_row
    same_head = (lax.broadcasted_iota(jnp.int32, (H * DV, H * DK), 0) // DV
                 == lax.broadcasted_iota(jnp.int32, (H * DV, H * DK), 1) // DK)
    scale = DK ** -0.5

    def stack(m):
        return jnp.concatenate([m * head_masks[h] for h in range(H)], axis=0).astype(BF16)

    chunks = [slice(c * C, (c + 1) * C) for c in range(rows // C)]

    def cum_log_decay(sl):
        la = la_ref[0, sl, :]
        p1 = la.astype(BF16)
        r1 = la - p1.astype(F32)
        p2 = r1.astype(BF16)
        p3 = (r1 - p2.astype(F32)).astype(BF16)
        s3 = jnp.dot(tri, jnp.concatenate([p1, p2, p3], axis=1), preferred_element_type=F32)
        w = H * DK
        return (s3[:, 0:w] + s3[:, w:2 * w]) + s3[:, 2 * w:3 * w]

    b_all = [cum_log_decay(sl) for sl in chunks]

    qe_all, ke_all, kd_all, qb_all, decay_all = [], [], [], [], []
    for sl, b in zip(chunks, b_all):
        q = qk_ref[0, sl, 0:H * DK] * scale
        k = qk_ref[0, sl, H * DK:2 * H * DK]
        b_last = b[C - 1:C, :]
        b_mid = b[C // 2 - 1:C // 2, :]
        qe_all.append(stack(q * jnp.exp(b - b_mid)))
        ke_all.append((k * jnp.exp(b_mid - b)).astype(BF16))
        kd_all.append((k * jnp.exp(b_last - b)).astype(BF16))
        qb_all.append((q * jnp.exp(b)).astype(BF16))
        decay_all.append(jnp.exp(b_last))

    att_all = [jnp.where(causal, lax.dot_general(qe, ke, (((1,), (1,)), ((), ())), preferred_element_type=F32),
                         0.0).astype(BF16) for qe, ke in zip(qe_all, ke_all)]

    o_intra_all, kv_all = [], []
    for sl, att, kd in zip(chunks, att_all, kd_all):
        v = v_ref[0, sl, :]
        o_intra_all.append(jnp.concatenate(
            [jnp.dot(att[h * C:(h + 1) * C, :], v[:, h * DV:(h + 1) * DV], preferred_element_type=F32)
             for h in range(H)], axis=1))
        kv_t = lax.dot_general(v, kd, (((0,), (0,)), ((), ())), preferred_element_type=F32)
        kv_all.append(jnp.where(same_head, kv_t, 0.0))

    state_t = state_ref[...]
    for sl, qb, decay, kv_t, o_intra in zip(chunks, qb_all, decay_all, kv_all, o_intra_all):
        o_inter = lax.dot_general(qb, state_t.astype(BF16), (((1,), (1,)), ((), ())),
                                  preferred_element_type=F32)
        obuf_ref[sl, :] = o_intra + o_inter
        state_t = decay * state_t + kv_t
    state_ref[...] = state_t

    r = r_ref[0]
    for h in range(H):
        cs = slice(h * DV, (h + 1) * DV)
        y = _rms(obuf_ref[:, cs], g_ref[:, cs])
        rh = r[:, cs]
        o_ref[0, :, cs] = (y * (rh * jax.nn.sigmoid(rh))).astype(BF16)


def _gla(qk, la, v, r, g_head, B, S):
    rows = min(GLA_ROWS, S)
    assert S % rows == 0 and rows % GLA_CHUNK == 0
    spec = lambda n: pl.BlockSpec((1, rows, n), lambda b, i: (b, i, 0))
    return pl.pallas_call(
        _gla_kernel,
        grid=(B, S // rows),
        in_specs=[spec(2 * GLA_QK), spec(GLA_QK), spec(GLA_V), spec(GLA_V), _full((1, GLA_V))],
        out_specs=spec(GLA_V),
        out_shape=jax.ShapeDtypeStruct((B, S, GLA_V), BF16),
        scratch_shapes=[pltpu.VMEM((GLA_V, GLA_QK), F32), pltpu.VMEM((rows, GLA_V), F32)],
        compiler_params=_params("parallel", "arbitrary"),
        name="gla_chunked",
    )(qk.reshape(B, S, -1), la.reshape(B, S, -1), v.reshape(B, S, -1), r.reshape(B, S, -1),
      g_head.reshape(1, GLA_V).astype(F32))


def _t5_bucket(dist):
    n = jnp.maximum(dist, 0)
    max_exact = REL_BUCKETS // 2
    nf = jnp.maximum(n, 1).astype(F32)
    large = max_exact + (jnp.log(nf / max_exact) / math.log(REL_MAX_DIST / max_exact)
                         * (REL_BUCKETS - max_exact)).astype(jnp.int32)
    large = jnp.minimum(large, REL_BUCKETS - 1)
    return jnp.where(n < max_exact, n, large)


def _moba_bias_kernel(rb_ref, o_ref):
    BS, G = MOBA_BLOCK, MOBA_GROUP
    grp, kind = pl.program_id(0), pl.program_id(1)
    d = (lax.broadcasted_iota(jnp.int32, (BS, BS), 1) - lax.broadcasted_iota(jnp.int32, (BS, BS), 0)
         + kind * BS)
    bucket = _t5_bucket(d)
    for h in range(G):
        val = jnp.zeros((BS, BS), F32)
        for bkt in range(REL_BUCKETS):
            val = jnp.where(bucket == bkt, rb_ref[bkt, grp * G + h] * LOG2E, val)
        o_ref[0, 0, :, h * BS:(h + 1) * BS] = jnp.where(d >= 0, val, NEG_INF)


def _moba_bias(rel_bias):
    BS, G = MOBA_BLOCK, MOBA_GROUP
    n_grp = MOBA_HEADS // G
    return pl.pallas_call(
        _moba_bias_kernel,
        grid=(n_grp, 2),
        in_specs=[pl.BlockSpec(memory_space=pltpu.SMEM)],
        out_specs=pl.BlockSpec((1, 1, BS, G * BS), lambda g, k: (g, k, 0, 0)),
        out_shape=jax.ShapeDtypeStruct((n_grp, 2, BS, G * BS), F32),
        compiler_params=_params("parallel", "parallel"),
        name="moba_bias_tables",
    )(rel_bias.astype(F32))


def _moba_kernel(rb_ref, q_ref, k_ref, v_ref, bias_ref, o_ref,
                 qs_ref, kmean_ref, sel_ref, m_ref, l_ref, acc_ref, sbuf_ref):
    BS, G, DH = MOBA_BLOCK, MOBA_GROUP, MOBA_DH
    NBP = kmean_ref.shape[0]
    grp, i = pl.program_id(1), pl.program_id(2)

    @pl.when(i == 0)
    def _():
        S = k_ref.shape[1]
        blk_of_key = lax.broadcasted_iota(jnp.int32, (NBP, S), 1) // BS
        ind = (blk_of_key == lax.broadcasted_iota(jnp.int32, (NBP, S), 0)).astype(BF16)
        kmean_ref[...] = jnp.dot(ind, k_ref[0], preferred_element_type=F32) * (1.0 / BS)

    qt = q_ref[0]
    sub_head = lax.broadcasted_iota(jnp.int32, (G * DH, 1), 0) // DH
    for h in range(G):
        qs_ref[:, h * BS:(h + 1) * BS] = jnp.where(sub_head == h, qt, jnp.zeros_like(qt))

    gate = jnp.dot(kmean_ref[...].astype(BF16), qs_ref[...], preferred_element_type=F32)
    blk = lax.broadcasted_iota(jnp.int32, gate.shape, 0)
    gate = jnp.where(blk < i, gate, NEG_INF)
    for t in range(MOBA_TOPK):
        mx = jnp.max(gate, axis=0, keepdims=True)
        hit = (gate == mx) & (mx > NEG_INF)
        idx = jnp.min(jnp.where(hit, blk, NBP), axis=0, keepdims=True)
        sel_ref[t:t + 1, :] = idx
        gate = jnp.where(blk == idx, NEG_INF, gate)

    def mask_row(j):
        hit = (sel_ref[0:1, :] == j) | (sel_ref[1:2, :] == j) | (sel_ref[2:3, :] == j)
        return jnp.where(hit, 0.0, NEG_INF)

    def with_ones(vt):
        return jnp.concatenate([vt, jnp.ones((8, vt.shape[1]), BF16)], axis=0)

    def far_scores(j0, n, slot):
        kj = k_ref[0, pl.ds(pl.multiple_of(j0 * BS, BS), n * BS), :]
        for h in range(G):
            cs = slice(h * BS, (h + 1) * BS)
            sbuf_ref[slot, 0:n * BS, cs] = jnp.dot(kj, qs_ref[:, cs], preferred_element_type=F32)

    j_prev = jnp.maximum(i - 1, 0)
    k_own = k_ref[0, pl.ds(pl.multiple_of(i * BS, BS), BS), :]
    k_prev = k_ref[0, pl.ds(pl.multiple_of(j_prev * BS, BS), BS), :]
    vt_near = jnp.concatenate([v_ref[i], v_ref[j_prev]], axis=1)
    prev_mask = mask_row(i - 1)
    for h in range(G):
        cs = slice(h * BS, (h + 1) * BS)
        sbuf_ref[1, 0:BS, cs] = jnp.dot(k_own, qs_ref[:, cs], preferred_element_type=F32)
        sbuf_ref[1, BS:2 * BS, cs] = jnp.dot(k_prev, qs_ref[:, cs], preferred_element_type=F32)
    far_scores(0, 2, 0)
    m_out, l_out, acc_out = [], [], []
    for h in range(G):
        cs = slice(h * BS, (h + 1) * BS)
        s_own = sbuf_ref[1, 0:BS, cs] + bias_ref[0, 0, :, cs]
        s_prev = sbuf_ref[1, BS:2 * BS, cs] + (bias_ref[0, 1, :, cs] + prev_mask[:, cs])
        m0 = jnp.maximum(jnp.max(s_own, axis=0, keepdims=True), jnp.max(s_prev, axis=0, keepdims=True))
        pb = jnp.concatenate([jnp.exp2(s_own - m0).astype(BF16), jnp.exp2(s_prev - m0).astype(BF16)], axis=0)
        pv = jnp.dot(with_ones(vt_near[h * DH:(h + 1) * DH, :]), pb, preferred_element_type=F32)
        m_out.append(m0)
        l_out.append(pv[DH:DH + 1, :])
        acc_out.append(pv[0:DH, :])
    m_ref[...] = jnp.concatenate(m_out, axis=1)
    l_ref[...] = jnp.concatenate(l_out, axis=1)
    acc_ref[...] = jnp.concatenate(acc_out, axis=1)

    lane_head = lax.broadcasted_iota(jnp.int32, (1, G * BS), 1) // BS
    far_bias = jnp.zeros((1, G * BS), F32)
    for h in range(G):
        far_bias = jnp.where(lane_head == h, rb_ref[REL_BUCKETS - 1, grp * G + h] * LOG2E, far_bias)

    def visit_far(j0, n, slot, ahead=None):
        vjt = jnp.concatenate([v_ref[j0 + t] for t in range(n)], axis=1)
        addend = [far_bias + mask_row(j0 + t) for t in range(n)]
        m_old, l_old, acc_old = m_ref[...], l_ref[...], acc_ref[...]
        m_out, l_out, acc_out = [], [], []
        if ahead is not None:
            k_next = k_ref[0, pl.ds(pl.multiple_of(ahead[0] * BS, BS), 2 * BS), :]
        for h in range(G):
            cs = slice(h * BS, (h + 1) * BS)
            if ahead is not None:
                sbuf_ref[ahead[1], :, cs] = jnp.dot(k_next, qs_ref[:, cs], preferred_element_type=F32)
            s = sbuf_ref[slot, 0:n * BS, cs]
            mx = jnp.max(s[0:BS], axis=0, keepdims=True) + addend[0][:, cs]
            for t in range(1, n):
                mx = jnp.maximum(mx, jnp.max(s[t * BS:(t + 1) * BS], axis=0, keepdims=True) + addend[t][:, cs])
            m_new = jnp.maximum(m_old[:, cs], mx)
            pb = jnp.concatenate([jnp.exp2(s[t * BS:(t + 1) * BS] - (m_new - addend[t][:, cs])).astype(BF16)
                                  for t in range(n)], axis=0)
            pv = jnp.dot(with_ones(vjt[h * DH:(h + 1) * DH, :]), pb, preferred_element_type=F32)
            alpha = jnp.exp2(m_old[:, cs] - m_new)
            m_out.append(m_new)
            l_out.append(alpha * l_old[:, cs] + pv[DH:DH + 1, :])
            acc_out.append(alpha * acc_old[:, cs] + pv[0:DH, :])
        m_ref[...] = jnp.concatenate(m_out, axis=1)
        l_ref[...] = jnp.concatenate(l_out, axis=1)
        acc_ref[...] = jnp.concatenate(acc_out, axis=1)

    n_far = jnp.maximum(i - 1, 0)
    n_pairs = lax.shift_right_logical(n_far, 1)

    def pair(p, slot, look_ahead=True):
        ahead = (2 * jnp.minimum(p + 1, n_pairs - 1), 1 - slot) if look_ahead else None
        visit_far(2 * p, 2, slot, ahead=ahead)

    def far_octet(w, carry):
        for t in range(4):
            pair(4 * w + t, t % 2)
        return carry

    lax.fori_loop(0, lax.shift_right_logical(n_pairs, 2), far_octet, 0)

    @pl.when((n_pairs & 2) != 0)
    def _():
        base = 4 * lax.shift_right_logical(n_pairs, 2)
        pair(base, 0)
        pair(base + 1, 1)

    @pl.when((n_pairs & 1) != 0)
    def _():
        pair(n_pairs - 1, 0, look_ahead=False)

    @pl.when(n_far % 2 == 1)
    def _():
        far_scores(n_far - 1, 1, 1)
        visit_far(n_far - 1, 1, 1)

    out_t = jnp.concatenate([acc_ref[:, h * BS:(h + 1) * BS] / l_ref[:, h * BS:(h + 1) * BS] for h in range(G)],
                            axis=0)
    o_ref[0] = out_t.T.astype(o_ref.dtype)


def _moba(qm_t, km, vm_t, bias, rel_bias, B, S):
    BS, G, DH = MOBA_BLOCK, MOBA_GROUP, MOBA_DH
    W = G * DH
    n_grp = MOBA_HEADS // G
    assert S % BS == 0 and S >= 2 * BS
    NB = S // BS
    NBP = -(-NB // 8) * 8
    return pl.pallas_call(
        _moba_kernel,
        grid=(B, n_grp, NB),
        in_specs=[pl.BlockSpec(memory_space=pltpu.SMEM),
                  pl.BlockSpec((1, W, BS), lambda b, g, i: (b * NB + i, g, 0)),
                  pl.BlockSpec((1, S, W), lambda b, g, i: (b, 0, g)),
                  pl.BlockSpec((NB, W, BS), lambda b, g, i: (b, g, 0)),
                  pl.BlockSpec((1, 2, BS, G * BS), lambda b, g, i: (g, 0, 0, 0))],
        out_specs=pl.BlockSpec((1, BS, W), lambda b, g, i: (b, i, g)),
        out_shape=jax.ShapeDtypeStruct((B, S, MOBA_W), BF16),
        scratch_shapes=[pltpu.VMEM((W, G * BS), BF16), pltpu.VMEM((NBP, W), F32),
                        pltpu.VMEM((8, G * BS), jnp.int32),
                        pltpu.VMEM((1, G * BS), F32), pltpu.VMEM((1, G * BS), F32),
                        pltpu.VMEM((DH, G * BS), F32), pltpu.VMEM((2, 2 * BS, G * BS), F32)],
        compiler_params=_params("parallel", "parallel", "arbitrary"),
        name="moba_attention",
    )(rel_bias.astype(F32), qm_t, km.reshape(B, S, -1), vm_t, bias)


def _mem_kv_kernel(mem_ref, g_ref, w_ref, k_ref, v_ref):
    kv = jnp.dot(_rms(mem_ref[0], g_ref[...]).astype(BF16), w_ref[...], preferred_element_type=F32)
    k_ref[0] = kv[:, :MEM_W].astype(BF16)
    v_ref[0] = kv[:, MEM_W:].astype(BF16)


def _mem_kv(mem, g_mem, w_ckv):
    B, M, D = mem.shape
    spec = pl.BlockSpec((1, M, MEM_W), lambda b: (b, 0, 0))
    return pl.pallas_call(
        _mem_kv_kernel,
        grid=(B,),
        in_specs=[pl.BlockSpec((1, M, D), lambda b: (b, 0, 0)), _full((1, D)), _full((D, 2 * MEM_W))],
        out_specs=[spec, spec],
        out_shape=[jax.ShapeDtypeStruct((B, M, MEM_W), BF16)] * 2,
        compiler_params=_params("parallel"),
        name="memory_kv",
    )(mem, g_mem.reshape(1, D).astype(F32), w_ckv.astype(BF16))


INFO_W0, INFO_W1, INFO_E0, INFO_E1, INFO_R0, INFO_R1 = range(6)
BLK_EXPERT, BLK_VALID = range(2)
ROUTER_GROUP_LANE0, ROUTER_EXPERT_LANE0 = 0, N_GROUPS


def _mix_kernel(n_pieces, x_ref, og_ref, om_ref, zg_ref, zm_ref, mk_ref, mv_ref, gc_ref, gm_ref,
                wpg, wpm, wout, wcq, wco, wr, br,
                x2_ref, info_ref, cnt_ref, *rest):
    hp_refs, base_ref = rest[:n_pieces], rest[n_pieces]
    first = (pl.program_id(0) == 0) & (pl.program_id(1) == 0)

    @pl.when(first)
    def _():
        base_ref[...] = jnp.zeros_like(base_ref)

    def mm(a, w_ref):
        return jnp.dot(a.astype(BF16), w_ref[...], preferred_element_type=F32)

    n_rows = x_ref.shape[1]
    sub = n_rows // MIX_PARTS
    parts = [slice(p * sub, (p + 1) * sub) for p in range(MIX_PARTS)]

    merged = [jax.nn.sigmoid(zg_ref[0, rs, :]) * mm(og_ref[0, rs, :], wpg)
              + jax.nn.sigmoid(zm_ref[0, rs, :]) * mm(om_ref[0, rs, :], wpm) for rs in parts]
    x1 = [x_ref[0, rs, :] + mm(m, wout) for rs, m in zip(parts, merged)]

    qc = [mm(_rms(v, gc_ref[...]), wcq).astype(BF16) for v in x1]

    def mem_attention(q):
        heads = []
        for h in range(MEM_HEADS):
            cs = slice(h * MEM_DH, (h + 1) * MEM_DH)
            s = lax.dot_general(q[:, cs], mk_ref[0, :, cs], (((1,), (1,)), ((), ())),
                                preferred_element_type=F32) * (MEM_DH ** -0.5)
            p = jnp.exp(s - jnp.max(s, axis=-1, keepdims=True))
            o = jnp.dot(p.astype(BF16), mv_ref[0, :, cs], preferred_element_type=F32)
            heads.append(o / jnp.sum(p, axis=-1, keepdims=True))
        return jnp.concatenate(heads, axis=-1)

    attn = [mem_attention(q) for q in qc]
    x2 = [v + mm(a, wco) for v, a in zip(x1, attn)]
    for rs, v in zip(parts, x2):
        x2_ref[0, rs, :] = v

    hm = [_rms(v, gm_ref[...]) for v in x2]
    logits = [mm(h, wr) + br[...] for h in hm]
    for rs, h in zip(parts, hm):
        words = _pack_bf16_pairs(h)
        for c, hp_ref in enumerate(hp_refs):
            hp_ref[0, rs, :] = words[:, c * SC_GATHER_WORDS:(c + 1) * SC_GATHER_WORDS]
    lane = lax.broadcasted_iota(jnp.int32, (sub, LANES), 1)
    is_grp = lane < N_GROUPS
    e_id = lane - ROUTER_EXPERT_LANE0

    def route(lg):
        gl = jnp.where(is_grp, lg, NEG_INF)
        ge = jnp.exp(gl - jnp.max(gl, axis=-1, keepdims=True))
        g_prob = ge / jnp.sum(ge, axis=-1, keepdims=True)
        p_grp = jnp.max(g_prob, axis=-1, keepdims=True)
        grp = jnp.min(jnp.where((g_prob == p_grp) & is_grp, lane, LANES), axis=-1, keepdims=True)
        in_grp = (e_id >= grp * EXPERTS_PER_GROUP) & (e_id < (grp + 1) * EXPERTS_PER_GROUP)
        el = jnp.where(in_grp, lg, NEG_INF)
        ee = jnp.exp(el - jnp.max(el, axis=-1, keepdims=True))
        e_prob = jnp.where(in_grp, ee / jnp.sum(ee, axis=-1, keepdims=True), -1.0)
        p0 = jnp.max(e_prob, axis=-1, keepdims=True)
        e0 = jnp.min(jnp.where(e_prob == p0, e_id, LANES), axis=-1, keepdims=True)
        e_rest = jnp.where(e_id == e0, -1.0, e_prob)
        p1 = jnp.max(e_rest, axis=-1, keepdims=True)
        e1 = jnp.min(jnp.where(e_rest == p1, e_id, LANES), axis=-1, keepdims=True)
        return e0, e1, p_grp * p0 / (p0 + p1), p_grp * p1 / (p0 + p1)

    routed = [route(lg) for lg in logits]

    before = (lax.broadcasted_iota(jnp.int32, (sub, sub), 1)
              < lax.broadcasted_iota(jnp.int32, (sub, sub), 0)).astype(BF16)
    base = base_ref[...]
    for rs, (e0, e1, w0, w1) in zip(parts, routed):
        onehot = ((lane == e0) | (lane == e1)).astype(F32)
        seen = base + jnp.dot(before, onehot.astype(BF16), preferred_element_type=F32)
        r0 = jnp.sum(jnp.where(lane == e0, seen, 0.0), axis=-1, keepdims=True)
        r1 = jnp.sum(jnp.where(lane == e1, seen, 0.0), axis=-1, keepdims=True)
        base = base + jnp.sum(onehot, axis=0, keepdims=True)
        info = jnp.zeros((sub, LANES), F32)
        for ln, val in ((INFO_W0, w0), (INFO_W1, w1), (INFO_E0, e0.astype(F32)), (INFO_E1, e1.astype(F32)),
                        (INFO_R0, r0), (INFO_R1, r1)):
            info = jnp.where(lane == ln, val, info)
        info_ref[0, rs, :] = info
    base_ref[...] = base
    cnt_ref[...] = base


def _mix(x, o_g, o_m, z_g, z_m, mem_k, mem_v, g_cross, g_moe, w_proj_gla, w_proj_moba, w_out, w_cq, w_co,
         w_rg, b_rg, w_re, b_re):
    B, S, D = x.shape
    M = mem_k.shape[1]
    rows = min(MIX_ROWS, S)
    assert S % rows == 0
    pad = LANES - N_GROUPS - N_EXPERTS
    wr = jnp.pad(jnp.concatenate([w_rg, w_re], axis=1), ((0, 0), (0, pad))).astype(BF16)
    br = jnp.pad(jnp.concatenate([b_rg, b_re]), (0, pad)).reshape(1, LANES).astype(F32)
    weights = [w_proj_gla.astype(BF16), w_proj_moba.astype(BF16), w_out.astype(BF16), w_cq.astype(BF16),
               w_co.astype(BF16), wr, br]
    tile = lambda n: pl.BlockSpec((1, rows, n), lambda b, i: (b, i, 0))
    memspec = pl.BlockSpec((1, M, MEM_W), lambda b, i: (b, 0, 0))
    n_pieces = D // 2 // SC_GATHER_WORDS
    return pl.pallas_call(
        functools.partial(_mix_kernel, n_pieces),
        grid=(B, S // rows),
        in_specs=[tile(D), tile(GLA_V), tile(MOBA_W), tile(D), tile(D), memspec, memspec,
                  _full((1, D)), _full((1, D))] + [_full(w.shape) for w in weights],
        out_specs=[tile(D), tile(LANES), _full((1, LANES))] + [tile(SC_GATHER_WORDS)] * n_pieces,
        out_shape=[jax.ShapeDtypeStruct((B, S, D), F32), jax.ShapeDtypeStruct((B, S, LANES), F32),
                   jax.ShapeDtypeStruct((1, LANES), F32)]
        + [jax.ShapeDtypeStruct((B, S, SC_GATHER_WORDS), jnp.uint32)] * n_pieces,
        scratch_shapes=[pltpu.VMEM((1, LANES), F32)],
        compiler_params=_params("arbitrary", "arbitrary"),
        name="merge_memattn_router",
    )(x, o_g, o_m, z_g.reshape(B, S, D), z_m.reshape(B, S, D), mem_k, mem_v,
      g_cross.reshape(1, D).astype(F32), g_moe.reshape(1, D).astype(F32), *weights)


def _plan_kernel(cnt_ref, info_ref, dest_ref, blk_ref):
    rows = info_ref.shape[0]
    lane1 = lax.broadcasted_iota(jnp.int32, (1, LANES), 1)
    nblk = jnp.floor((cnt_ref[...] + (EXPERT_ROWS - 1)) * (1.0 / EXPERT_ROWS))
    nblk = jnp.where(lane1 < N_EXPERTS, nblk, 0.0)
    hi = jnp.floor(nblk * (1.0 / 256.0))
    lo = nblk - 256.0 * hi
    upto = (lax.broadcasted_iota(jnp.int32, (LANES, LANES), 0)
            <= lax.broadcasted_iota(jnp.int32, (LANES, LANES), 1)).astype(BF16)
    digits = jnp.concatenate([jnp.broadcast_to(hi, (8, LANES)), jnp.broadcast_to(lo, (8, LANES))], axis=0)
    sums = jnp.dot(digits.astype(BF16), upto, preferred_element_type=F32)
    pend = sums[0:1] * 256.0 + sums[8:9]
    pstart_rows = (pend - nblk) * EXPERT_ROWS

    info = info_ref[...]
    lane = lax.broadcasted_iota(jnp.int32, (rows, LANES), 1)

    def field(ln):
        return jnp.sum(jnp.where(lane == ln, info, 0.0), axis=-1, keepdims=True)

    def dest(e, r):
        return jnp.sum(jnp.where(lane == e.astype(jnp.int32), pstart_rows, 0.0), axis=-1, keepdims=True) + r

    d0 = dest(field(INFO_E0), field(INFO_R0))
    d1 = dest(field(INFO_E1), field(INFO_R1))
    cols = jnp.where(lane == 0, d0, jnp.where(lane == 1, d1, 0.0))
    move = dest_ref.shape[2]
    for c in range(dest_ref.shape[0]):
        dest_ref[c] = cols[c * move:(c + 1) * move, :].T[0:8, :].astype(jnp.int32)

    @pl.when(pl.program_id(0) == 0)
    def _():
        n = lax.broadcasted_iota(jnp.int32, (blk_ref.shape[0], LANES), 0).astype(F32)
        blane = lax.broadcasted_iota(jnp.int32, (blk_ref.shape[0], LANES), 1)
        done = jnp.where((pend <= n) & (lane1 < N_EXPERTS), 1.0, 0.0)
        e = jnp.minimum(jnp.sum(done, axis=-1, keepdims=True), N_EXPERTS - 1.0)
        mine = blane == e.astype(jnp.int32)
        first_blk = jnp.sum(jnp.where(mine, pend - nblk, 0.0), axis=-1, keepdims=True)
        count = jnp.sum(jnp.where(mine, cnt_ref[...], 0.0), axis=-1, keepdims=True)
        valid = jnp.clip(count - EXPERT_ROWS * (n[:, 0:1] - first_blk), 0.0, float(EXPERT_ROWS))
        blk_ref[...] = jnp.where(blane == BLK_EXPERT, e, jnp.where(blane == BLK_VALID, valid, 0.0)).astype(jnp.int32)


def _plan(counts, info2d, n_blk):
    T = info2d.shape[0]
    move = min(MOVE_ROWS, T)
    rows = min(PLAN_ROWS, T)
    assert T % rows == 0 and rows % move == 0
    n_blk_pad = -(-n_blk // 8) * 8
    return pl.pallas_call(
        _plan_kernel,
        grid=(T // rows,),
        in_specs=[_full((1, LANES)), pl.BlockSpec((rows, LANES), lambda i: (i, 0))],
        out_specs=[pl.BlockSpec((rows // move, 8, move), lambda i: (i, 0, 0)), _full((n_blk_pad, LANES))],
        out_shape=[jax.ShapeDtypeStruct((T // move, 8, move), jnp.int32),
                   jax.ShapeDtypeStruct((n_blk_pad, LANES), jnp.int32)],
        compiler_params=_params("arbitrary"),
        name="dispatch_plan",
    )(counts, info2d)


def _sc_windows(n_rows):
    n_inner = 32
    assert n_rows % (SC_GATHER_ROWS * n_inner) == 0
    return n_rows // (SC_GATHER_ROWS * n_inner), n_inner


def _sc_mesh():
    return plsc.VectorSubcoreMesh(core_axis_name="c", subcore_axis_name="s")


def _sc_scatter_rows(src, idx_a, idx_b, n_out):
    T, W = src.shape
    n_outer, n_inner = _sc_windows(T)
    win = lambda i, j: i * n_inner + j

    @pl.kernel(out_type=jax.ShapeDtypeStruct((n_out, W), src.dtype), mesh=_sc_mesh(), scratch_types=[])
    def scatter_kernel(s_hbm, a_hbm, b_hbm, o_hbm):
        def body(s_vmem, a_vmem, b_vmem):
            pltpu.sync_copy(s_vmem, o_hbm.at[a_vmem.at[0]])
            pltpu.sync_copy(s_vmem, o_hbm.at[b_vmem.at[0]])

        pltpu.emit_pipeline(
            body,
            grid=(n_outer, n_inner),
            in_specs=[pl.BlockSpec((SC_GATHER_ROWS, W), index_map=lambda i, j: (win(i, j), 0)),
                      pl.BlockSpec((1, SC_GATHER_ROWS), index_map=lambda i, j: (0, win(i, j))),
                      pl.BlockSpec((1, SC_GATHER_ROWS), index_map=lambda i, j: (0, win(i, j)))],
            out_specs=[],
            core_axis_name=("c", "s"),
            dimension_semantics=(pltpu.PARALLEL, pltpu.PARALLEL),
        )(s_hbm, a_hbm, b_hbm)

    return scatter_kernel(src, idx_a, idx_b)


def _sc_gather_rows(table, idx):
    M = idx.shape[1]
    W = table.shape[1]
    n_outer, n_inner = _sc_windows(M)
    win = lambda i, j: i * n_inner + j

    @pl.kernel(out_type=jax.ShapeDtypeStruct((M, W), table.dtype), mesh=_sc_mesh(), scratch_types=[])
    def gather_kernel(t_hbm, i_hbm, o_hbm):
        def body(i_vmem, o_vmem):
            pltpu.sync_copy(t_hbm.at[i_vmem.at[0]], o_vmem)

        pltpu.emit_pipeline(
            body,
            grid=(n_outer, n_inner),
            in_specs=[pl.BlockSpec((1, SC_GATHER_ROWS), index_map=lambda i, j: (0, win(i, j)))],
            out_specs=[pl.BlockSpec((SC_GATHER_ROWS, W), index_map=lambda i, j: (win(i, j), 0))],
            core_axis_name=("c", "s"),
            dimension_semantics=(pltpu.PARALLEL, pltpu.PARALLEL),
        )(i_hbm, o_hbm)

    return gather_kernel(table, idx)


def _expert_kernel(n_pieces, blk_e_ref, blk_valid_ref, *refs):
    xs_refs, (wg_ref, wu_ref, wd_ref) = refs[:n_pieces], refs[n_pieces:n_pieces + 3]
    y_refs, (wg_bf, wu_bf, wd_bf) = refs[n_pieces + 3:2 * n_pieces + 3], refs[2 * n_pieces + 3:]
    n = pl.program_id(0)
    prev = blk_e_ref[jnp.maximum(n - 1, 0)]

    @pl.when((n == 0) | (blk_e_ref[n] != prev))
    def _():
        wg_bf[...] = wg_ref[0].astype(BF16)
        wu_bf[...] = wu_ref[0].astype(BF16)
        wd_bf[...] = wd_ref[0].astype(BF16)

    sub = xs_refs[0].shape[0] // EXPERT_PARTS
    parts = [slice(p * sub, (p + 1) * sub) for p in range(EXPERT_PARTS)]
    row = lax.broadcasted_iota(jnp.int32, (sub, 1), 0)
    valid = blk_valid_ref[n]

    def load(p, rs):
        words = jnp.concatenate([r[rs, :] for r in xs_refs], axis=1)
        words = jnp.where(row + p * sub < valid, words, jnp.zeros_like(words))
        return _unpack_bf16_pairs(words).astype(BF16)

    xb = [load(p, rs) for p, rs in enumerate(parts)]
    gate = [jnp.dot(v, wg_bf[...], preferred_element_type=F32) for v in xb]
    up = [jnp.dot(v, wu_bf[...], preferred_element_type=F32) for v in xb]
    hid = [(g * jax.nn.sigmoid(g) * u).astype(BF16) for g, u in zip(gate, up)]
    for rs, hv in zip(parts, hid):
        words = _pack_bf16_pairs(jnp.dot(hv, wd_bf[...], preferred_element_type=F32))
        for c, y_ref in enumerate(y_refs):
            y_ref[rs, :] = words[:, c * SC_GATHER_WORDS:(c + 1) * SC_GATHER_WORDS]


def _experts(blk_e, blk_valid, xs_pieces, w_gate, w_up, w_down):
    cap = xs_pieces[0].shape[0]
    _, D, DE = w_gate.shape
    n_blk = cap // EXPERT_ROWS
    n_pieces = len(xs_pieces)
    piece = pl.BlockSpec((EXPERT_ROWS, SC_GATHER_WORDS), lambda n, e, v: (n, 0))
    return pl.pallas_call(
        functools.partial(_expert_kernel, n_pieces),
        grid_spec=pltpu.PrefetchScalarGridSpec(
            num_scalar_prefetch=2,
            grid=(n_blk,),
            in_specs=[piece] * n_pieces
            + [pl.BlockSpec((1, D, DE), lambda n, e, v: (e[n], 0, 0)),
               pl.BlockSpec((1, D, DE), lambda n, e, v: (e[n], 0, 0)),
               pl.BlockSpec((1, DE, D), lambda n, e, v: (e[n], 0, 0))],
            out_specs=[piece] * n_pieces,
            scratch_shapes=[pltpu.VMEM((D, DE), BF16), pltpu.VMEM((D, DE), BF16), pltpu.VMEM((DE, D), BF16)]),
        out_shape=[jax.ShapeDtypeStruct((cap, SC_GATHER_WORDS), jnp.uint32)] * n_pieces,
        compiler_params=_params("arbitrary"),
        name="moe_experts",
    )(blk_e, blk_valid, *xs_pieces, w_gate, w_up, w_down)


def _combine_dense_kernel(final_norm, n_pieces, *refs):
    yg_refs, (x_ref, info_ref, g_ref, o_ref) = refs[:n_pieces], refs[n_pieces:]
    info = info_ref[...]
    w0 = info[:, INFO_W0:INFO_W0 + 1]
    w1 = info[:, INFO_W1:INFO_W1 + 1]
    y = [_unpack_bf16_pairs(jnp.concatenate([r[slot] for r in yg_refs], axis=1)) for slot in range(TOPK_IN_GROUP)]
    out = x_ref[...] + (w0 * y[0] + w1 * y[1])
    o_ref[...] = _rms(out, g_ref[...]) if final_norm else out


def _combine_dense(yg_pieces, x2d, info2d, g_final, final_norm):
    T, D = x2d.shape
    rows = min(MOVE_ROWS, T)
    n_pieces = len(yg_pieces)
    return pl.pallas_call(
        functools.partial(_combine_dense_kernel, final_norm, n_pieces),
        grid=(T // rows,),
        in_specs=[pl.BlockSpec((TOPK_IN_GROUP, rows, SC_GATHER_WORDS), lambda i: (0, i, 0))] * n_pieces
        + [pl.BlockSpec((rows, D), lambda i: (i, 0)), pl.BlockSpec((rows, LANES), lambda i: (i, 0)),
           _full((1, D))],
        out_specs=pl.BlockSpec((rows, D), lambda i: (i, 0)),
        out_shape=jax.ShapeDtypeStruct((T, D), F32),
        compiler_params=_params("parallel"),
        name="moe_combine_dense_final_norm",
    )(*yg_pieces, x2d, info2d, g_final.reshape(1, D).astype(F32))


def kernel(x, mem, g_mem, rel_bias, g_mix, w_in, w_alpha_up, b_alpha, g_gla_head, w_proj_gla, w_proj_moba,
           w_out, g_cross, w_cq, w_ckv, w_co, g_moe, w_router_group, b_router_group, w_router_expert,
           b_router_expert, w_exp_gate, w_exp_up, w_exp_down, g_final):
    B, S, D = x.shape
    T = B * S
    depth = g_mix.shape[0]
    n_assign = T * TOPK_IN_GROUP
    n_blk = -(-(n_assign + N_EXPERTS * (EXPERT_ROWS - 1)) // EXPERT_ROWS)
    cap = n_blk * EXPERT_ROWS

    mem_bias = _moba_bias(rel_bias)
    for l in range(depth):
        qk, v_g, r_g, la, q_m, k_m, v_m, z_g, z_m = _project(x.reshape(T, D), g_mix[l], w_in[l], w_alpha_up[l],
                                                             b_alpha[l])
        o_g = _gla(qk, la, v_g, r_g, g_gla_head[l], B, S)
        o_m = _moba(q_m, k_m, v_m, mem_bias, rel_bias, B, S)
        mem_k, mem_v = _mem_kv(mem, g_mem, w_ckv[l])
        x2, info, counts, *hm_pieces = _mix(x, o_g, o_m, z_g, z_m, mem_k, mem_v, g_cross[l], g_moe[l],
                                            w_proj_gla[l], w_proj_moba[l], w_out[l], w_cq[l], w_co[l],
                                            w_router_group[l], b_router_group[l], w_router_expert[l],
                                            b_router_expert[l])
        x2d, info2d = x2.reshape(T, D), info.reshape(T, LANES)
        dest, blk = _plan(counts, info2d, n_blk)
        idx = jnp.transpose(dest[:, 0:TOPK_IN_GROUP, :], (1, 0, 2)).reshape(1, TOPK_IN_GROUP * T)
        xs_pieces = [_sc_scatter_rows(h.reshape(T, SC_GATHER_WORDS), idx[:, :T], idx[:, T:], cap)
                     for h in hm_pieces]
        y_pieces = _experts(blk[:n_blk, BLK_EXPERT], blk[:n_blk, BLK_VALID], xs_pieces, w_exp_gate[l],
                            w_exp_up[l], w_exp_down[l])
        yg = [_sc_gather_rows(y, idx).reshape(TOPK_IN_GROUP, T, SC_GATHER_WORDS) for y in y_pieces]
        x = _combine_dense(yg, x2d, info2d, g_final, final_norm=(l == depth - 1)).reshape(B, S, D)
    return x
```

```python
import functools
import math

import jax
import jax.numpy as jnp
from jax import lax
from jax.experimental import pallas as pl
from jax.experimental.pallas import tpu as pltpu
from jax.experimental.pallas import tpu_sc as plsc

F32 = jnp.float32
BF16 = jnp.bfloat16
NEG_INF = float("-inf")

EPS = 1e-6
GLA_HEADS, GLA_DK, GLA_DV, GLA_LOWRANK, GLA_TAU, GLA_CHUNK = 4, 64, 128, 16, 16.0, 64
GLA_QK, GLA_V = GLA_HEADS * GLA_DK, GLA_HEADS * GLA_DV
MOBA_HEADS, MOBA_DH, MOBA_BLOCK, MOBA_TOPK = 8, 64, 256, 3
MOBA_W = MOBA_HEADS * MOBA_DH
LOG2E = math.log2(math.e)
MOBA_Q_SCALE = MOBA_DH ** -0.5 * LOG2E
REL_BUCKETS, REL_MAX_DIST = 32, 128
MEM_HEADS, MEM_DH = 4, 128
MEM_W = MEM_HEADS * MEM_DH
N_GROUPS, EXPERTS_PER_GROUP, TOPK_IN_GROUP = 4, 8, 2
N_EXPERTS = N_GROUPS * EXPERTS_PER_GROUP

LANES = 128
SUBLANES = 8
VMEM_LIMIT_BYTES = 56 * 1024 * 1024

PROJ_ROWS = 512
GLA_ROWS = 512
MOBA_GROUP = 4
MIX_ROWS = 512
MIX_PARTS = 2
EXPERT_ROWS = 256
MOVE_ROWS = 512
PLAN_ROWS = 2048
SC_GATHER_ROWS = 128
SC_GATHER_WORDS = 256
EXPERT_PARTS = 2


def _params(*semantics):
    return pltpu.CompilerParams(dimension_semantics=semantics, vmem_limit_bytes=VMEM_LIMIT_BYTES)


def _full(shape):
    return pl.BlockSpec(shape, lambda *_: (0,) * len(shape))


def _rms(x, g):
    return x * lax.rsqrt(jnp.mean(x * x, axis=-1, keepdims=True) + EPS) * g


def _pack_bf16_pairs(x):
    n = x.shape[1] // 2
    bits = pltpu.bitcast(x.astype(BF16).astype(F32), jnp.uint32)
    return bits[:, n:] | (bits[:, :n] >> 16)


def _unpack_bf16_pairs(w):
    lo = pltpu.bitcast(w << 16, F32)
    hi = pltpu.bitcast(w & jnp.uint32(0xFFFF0000), F32)
    return jnp.concatenate([lo, hi], axis=1)


def _proj_kernel(x_ref, g_ref, w_qk, w_v, w_r, w_a, w_up, b_a, w_qm, w_km, w_vm, w_zg, w_zm,
                 o_qk, o_v, o_r, o_la, o_qm, o_km, o_vm, o_zg, o_zm):
    h = _rms(x_ref[...], g_ref[...]).astype(BF16)

    def mm(w_ref):
        return jnp.dot(h, w_ref[...], preferred_element_type=F32)

    o_qk[...] = mm(w_qk)
    o_v[...] = mm(w_v).astype(BF16)
    o_r[...] = mm(w_r).astype(BF16)
    a_lr = mm(w_a).astype(BF16)
    pre = jnp.dot(a_lr, w_up[...], preferred_element_type=F32) + b_a[...]
    o_la[...] = jax.nn.log_sigmoid(pre) * (1.0 / GLA_TAU)

    def mm_t(wt_ref):
        return lax.dot_general(wt_ref[...], h, (((1,), (1,)), ((), ())), preferred_element_type=F32)

    def store_blocks(o_ref, val_t):
        for c in range(o_ref.shape[0]):
            o_ref[c] = val_t[:, c * MOBA_BLOCK:(c + 1) * MOBA_BLOCK]

    store_blocks(o_qm, (mm_t(w_qm) * MOBA_Q_SCALE).astype(BF16))
    o_km[...] = mm(w_km).astype(BF16)
    store_blocks(o_vm, mm_t(w_vm).astype(BF16))
    o_zg[...] = mm(w_zg).astype(BF16)
    o_zm[...] = mm(w_zm).astype(BF16)


def _project(x2d, g_mix, w_in, w_alpha_up, b_alpha):
    T, D = x2d.shape
    rows = min(PROJ_ROWS, T)
    assert T % rows == 0
    splits = (GLA_QK, GLA_QK, GLA_V, GLA_V, GLA_LOWRANK, MOBA_W, MOBA_W, MOBA_W, D, D)
    offs = [0]
    for s in splits:
        offs.append(offs[-1] + s)
    wb = w_in.astype(BF16)
    sec = lambda i, j: wb[:, offs[i]:offs[j]]
    w_a = jnp.pad(sec(4, 5), ((0, 0), (0, LANES - GLA_LOWRANK)))
    w_up = jnp.pad(w_alpha_up.astype(BF16), ((0, LANES - GLA_LOWRANK), (0, 0)))
    weights = [sec(0, 2), sec(2, 3), sec(3, 4), w_a, w_up, b_alpha.reshape(1, GLA_QK).astype(F32),
               sec(5, 6).T, sec(6, 7), sec(7, 8).T, sec(8, 9), sec(9, 10)]
    out_defs = [(2 * GLA_QK, F32, False), (GLA_V, BF16, False), (GLA_V, BF16, False), (GLA_QK, F32, False),
                (MOBA_W, BF16, True), (MOBA_W, BF16, False), (MOBA_W, BF16, True), (D, BF16, False),
                (D, BF16, False)]
    BS = MOBA_BLOCK
    assert rows % BS == 0
    row_spec = lambda n: pl.BlockSpec((rows, n), lambda i: (i, 0))
    blk_spec = lambda n: pl.BlockSpec((rows // BS, n, BS), lambda i: (i, 0, 0))
    return pl.pallas_call(
        _proj_kernel,
        grid=(T // rows,),
        in_specs=[row_spec(D), _full((1, D))] + [_full(w.shape) for w in weights],
        out_specs=[blk_spec(n) if t else row_spec(n) for n, _, t in out_defs],
        out_shape=[jax.ShapeDtypeStruct((T // BS, n, BS) if t else (T, n), dt) for n, dt, t in out_defs],
        compiler_params=_params("parallel"),
        name="norm_in_proj",
    )(x2d, g_mix.reshape(1, D).astype(F32), *weights)


def _gla_kernel(qk_ref, la_ref, v_ref, r_ref, g_ref, o_ref, state_ref, obuf_ref):
    C, H, DK, DV = GLA_CHUNK, GLA_HEADS, GLA_DK, GLA_DV
    rows = qk_ref.shape[1]

    @pl.when(pl.program_id(1) == 0)
    def _():
        state_ref[...] = jnp.zeros_like(state_ref)

    tri = (lax.broadcasted_iota(jnp.int32, (C, C), 0) >= lax.broadcasted_iota(jnp.int32, (C, C), 1)).astype(BF16)
    lane_head = lax.broadcasted_iota(jnp.int32, (1, H * DK), 1) // DK
    head_masks = [(lane_head == h).astype(F32) for h in range(H)]
    stack_row = lax.broadcasted_iota(jnp.int32, (H * C, C), 0) % C
    stack_col = lax.broadcasted_iota(jnp.int32, (H * C, C), 1)
    causal = stack_col <= stack_row
    same_head = (lax.broadcasted_iota(jnp.int32, (H * DV, H * DK), 0) // DV
                 == lax.broadcasted_iota(jnp.int32, (H * DV, H * DK), 1) // DK)
    scale = DK ** -0.5

    def stack(m):
        return jnp.concatenate([m * head_masks[h] for h in range(H)], axis=0).astype(BF16)

    chunks = [slice(c * C, (c + 1) * C) for c in range(rows // C)]

    def cum_log_decay(sl):
        la = la_ref[0, sl, :]
        p1 = la.astype(BF16)
        r1 = la - p1.astype(F32)
        p2 = r1.astype(BF16)
        p3 = (r1 - p2.astype(F32)).astype(BF16)
        s3 = jnp.dot(tri, jnp.concatenate([p1, p2, p3], axis=1), preferred_element_type=F32)
        w = H * DK
        return (s3[:, 0:w] + s3[:, w:2 * w]) + s3[:, 2 * w:3 * w]

    b_all = [cum_log_decay(sl) for sl in chunks]

    qe_all, ke_all, kd_all, qb_all, decay_all = [], [], [], [], []
    for sl, b in zip(chunks, b_all):
        q = qk_ref[0, sl, 0:H * DK] * scale
        k = qk_ref[0, sl, H * DK:2 * H * DK]
        b_last = b[C - 1:C, :]
        b_mid = b[C // 2 - 1:C // 2, :]
        qe_all.append(stack(q * jnp.exp(b - b_mid)))
        ke_all.append((k * jnp.exp(b_mid - b)).astype(BF16))
        kd_all.append((k * jnp.exp(b_last - b)).astype(BF16))
        qb_all.append((q * jnp.exp(b)).astype(BF16))
        decay_all.append(jnp.exp(b_last))

    att_all = [jnp.where(causal, lax.dot_general(qe, ke, (((1,), (1,)), ((), ())), preferred_element_type=F32),
                         0.0).astype(BF16) for qe, ke in zip(qe_all, ke_all)]

    o_intra_all, kv_all = [], []
    for sl, att, kd in zip(chunks, att_all, kd_all):
        v = v_ref[0, sl, :]
        o_intra_all.append(jnp.concatenate(
            [jnp.dot(att[h * C:(h + 1) * C, :], v[:, h * DV:(h + 1) * DV], preferred_element_type=F32)
             for h in range(H)], axis=1))
        kv_t = lax.dot_general(v, kd, (((0,), (0,)), ((), ())), preferred_element_type=F32)
        kv_all.append(jnp.where(same_head, kv_t, 0.0))

    state_t = state_ref[...]
    for sl, qb, decay, kv_t, o_intra in zip(chunks, qb_all, decay_all, kv_all, o_intra_all):
        o_inter = lax.dot_general(qb, state_t.astype(BF16), (((1,), (1,)), ((), ())),
                                  preferred_element_type=F32)
        obuf_ref[sl, :] = o_intra + o_inter
        state_t = decay * state_t + kv_t
    state_ref[...] = state_t

    r = r_ref[0].astype(F32)
    for h in range(H):
        cs = slice(h * DV, (h + 1) * DV)
        y = _rms(obuf_ref[:, cs], g_ref[:, cs])
        rh = r[:, cs]
        o_ref[0, :, cs] = (y * (rh * jax.nn.sigmoid(rh))).astype(BF16)


def _gla(qk, la, v, r, g_head, B, S):
    rows = min(GLA_ROWS, S)
    assert S % rows == 0 and rows % GLA_CHUNK == 0
    spec = lambda n: pl.BlockSpec((1, rows, n), lambda b, i: (b, i, 0))
    return pl.pallas_call(
        _gla_kernel,
        grid=(B, S // rows),
        in_specs=[spec(2 * GLA_QK), spec(GLA_QK), spec(GLA_V), spec(GLA_V), _full((1, GLA_V))],
        out_specs=spec(GLA_V),
        out_shape=jax.ShapeDtypeStruct((B, S, GLA_V), BF16),
        scratch_shapes=[pltpu.VMEM((GLA_V, GLA_QK), F32), pltpu.VMEM((rows, GLA_V), F32)],
        compiler_params=_params("parallel", "arbitrary"),
        name="gla_chunked",
    )(qk.reshape(B, S, -1), la.reshape(B, S, -1), v.reshape(B, S, -1), r.reshape(B, S, -1),
      g_head.reshape(1, GLA_V).astype(F32))


def _t5_bucket(dist):
    n = jnp.maximum(dist, 0)
    max_exact = REL_BUCKETS // 2
    nf = jnp.maximum(n, 1).astype(F32)
    large = max_exact + (jnp.log(nf / max_exact) / math.log(REL_MAX_DIST / max_exact)
                         * (REL_BUCKETS - max_exact)).astype(jnp.int32)
    large = jnp.minimum(large, REL_BUCKETS - 1)
    return jnp.where(n < max_exact, n, large)


def _moba_bias_kernel(rb_ref, o_ref):
    BS, G = MOBA_BLOCK, MOBA_GROUP
    grp, kind = pl.program_id(0), pl.program_id(1)
    d = (lax.broadcasted_iota(jnp.int32, (BS, BS), 1) - lax.broadcasted_iota(jnp.int32, (BS, BS), 0)
         + kind * BS)
    bucket = _t5_bucket(d)
    for h in range(G):
        val = jnp.zeros((BS, BS), F32)
        for bkt in range(REL_BUCKETS):
            val = jnp.where(bucket == bkt, rb_ref[bkt, grp * G + h] * LOG2E, val)
        o_ref[0, 0, :, h * BS:(h + 1) * BS] = jnp.where(d >= 0, val, NEG_INF)


def _moba_bias(rel_bias):
    BS, G = MOBA_BLOCK, MOBA_GROUP
    n_grp = MOBA_HEADS // G
    return pl.pallas_call(
        _moba_bias_kernel,
        grid=(n_grp, 2),
        in_specs=[pl.BlockSpec(memory_space=pltpu.SMEM)],
        out_specs=pl.BlockSpec((1, 1, BS, G * BS), lambda g, k: (g, k, 0, 0)),
        out_shape=jax.ShapeDtypeStruct((n_grp, 2, BS, G * BS), F32),
        compiler_params=_params("parallel", "parallel"),
        name="moba_bias_tables",
    )(rel_bias.astype(F32))


def _moba_kernel(rb_ref, q_ref, k_ref, v_ref, bias_ref, o_ref,
                 qs_ref, kmean_ref, sel_ref, m_ref, l_ref, acc_ref, sbuf_ref):
    BS, G, DH = MOBA_BLOCK, MOBA_GROUP, MOBA_DH
    NBP = kmean_ref.shape[0]
    grp, i = pl.program_id(1), pl.program_id(2)

    @pl.when(i == 0)
    def _():
        S = k_ref.shape[1]
        blk_of_key = lax.broadcasted_iota(jnp.int32, (NBP, S), 1) // BS
        ind = (blk_of_key == lax.broadcasted_iota(jnp.int32, (NBP, S), 0)).astype(BF16)
        kmean_ref[...] = jnp.dot(ind, k_ref[0], preferred_element_type=F32) * (1.0 / BS)

    qt = q_ref[0]
    sub_head = lax.broadcasted_iota(jnp.int32, (G * DH, 1), 0) // DH
    for h in range(G):
        qs_ref[:, h * BS:(h + 1) * BS] = jnp.where(sub_head == h, qt, jnp.zeros_like(qt))

    gate = jnp.dot(kmean_ref[...].astype(BF16), qs_ref[...], preferred_element_type=F32)
    blk = lax.broadcasted_iota(jnp.int32, gate.shape, 0)
    gate = jnp.where(blk < i, gate, NEG_INF)
    for t in range(MOBA_TOPK):
        mx = jnp.max(gate, axis=0, keepdims=True)
        hit = (gate == mx) & (mx > NEG_INF)
        idx = jnp.min(jnp.where(hit, blk, NBP), axis=0, keepdims=True)
        sel_ref[t:t + 1, :] = idx
        gate = jnp.where(blk == idx, NEG_INF, gate)

    def mask_row(j):
        hit = (sel_ref[0:1, :] == j) | (sel_ref[1:2, :] == j) | (sel_ref[2:3, :] == j)
        return jnp.where(hit, 0.0, NEG_INF)

    def with_ones(vt):
        return jnp.concatenate([vt, jnp.ones((8, vt.shape[1]), BF16)], axis=0)

    def far_scores(j0, n, slot):
        kj = k_ref[0, pl.ds(pl.multiple_of(j0 * BS, BS), n * BS), :]
        for h in range(G):
            cs = slice(h * BS, (h + 1) * BS)
            sbuf_ref[slot, 0:n * BS, cs] = jnp.dot(kj, qs_ref[:, cs], preferred_element_type=F32)

    j_prev = jnp.maximum(i - 1, 0)
    k_own = k_ref[0, pl.ds(pl.multiple_of(i * BS, BS), BS), :]
    k_prev = k_ref[0, pl.ds(pl.multiple_of(j_prev * BS, BS), BS), :]
    vt_near = jnp.concatenate([v_ref[i], v_ref[j_prev]], axis=1)
    prev_mask = mask_row(i - 1)
    for h in range(G):
        cs = slice(h * BS, (h + 1) * BS)
        sbuf_ref[1, 0:BS, cs] = jnp.dot(k_own, qs_ref[:, cs], preferred_element_type=F32)
        sbuf_ref[1, BS:2 * BS, cs] = jnp.dot(k_prev, qs_ref[:, cs], preferred_element_type=F32)
    far_scores(0, 2, 0)
    m_out, l_out, acc_out = [], [], []
    for h in range(G):
        cs = slice(h * BS, (h + 1) * BS)
        s_own = sbuf_ref[1, 0:BS, cs] + bias_ref[0, 0, :, cs]
        s_prev = sbuf_ref[1, BS:2 * BS, cs] + (bias_ref[0, 1, :, cs] + prev_mask[:, cs])
        m0 = jnp.maximum(jnp.max(s_own, axis=0, keepdims=True), jnp.max(s_prev, axis=0, keepdims=True))
        pb = jnp.concatenate([jnp.exp2(s_own - m0).astype(BF16), jnp.exp2(s_prev - m0).astype(BF16)], axis=0)
        pv = jnp.dot(with_ones(vt_near[h * DH:(h + 1) * DH, :]), pb, preferred_element_type=F32)
        m_out.append(m0)
        l_out.append(pv[DH:DH + 1, :])
        acc_out.append(pv[0:DH, :])
    m_ref[...] = jnp.concatenate(m_out, axis=1)
    l_ref[...] = jnp.concatenate(l_out, axis=1)
    acc_ref[...] = jnp.concatenate(acc_out, axis=1)

    lane_head = lax.broadcasted_iota(jnp.int32, (1, G * BS), 1) // BS
    far_bias = jnp.zeros((1, G * BS), F32)
    for h in range(G):
        far_bias = jnp.where(lane_head == h, rb_ref[REL_BUCKETS - 1, grp * G + h] * LOG2E, far_bias)

    def visit_far(j0, n, slot, ahead=None):
        vjt = jnp.concatenate([v_ref[j0 + t] for t in range(n)], axis=1)
        addend = [far_bias + mask_row(j0 + t) for t in range(n)]
        m_old, l_old, acc_old = m_ref[...], l_ref[...], acc_ref[...]
        m_out, l_out, acc_out = [], [], []
        if ahead is not None:
            k_next = k_ref[0, pl.ds(pl.multiple_of(ahead[0] * BS, BS), 2 * BS), :]
        for h in range(G):
            cs = slice(h * BS, (h + 1) * BS)
            if ahead is not None:
                sbuf_ref[ahead[1], :, cs] = jnp.dot(k_next, qs_ref[:, cs], preferred_element_type=F32)
            s = sbuf_ref[slot, 0:n * BS, cs]
            mx = jnp.max(s[0:BS], axis=0, keepdims=True) + addend[0][:, cs]
            for t in range(1, n):
                mx = jnp.maximum(mx, jnp.max(s[t * BS:(t + 1) * BS], axis=0, keepdims=True) + addend[t][:, cs])
            m_new = jnp.maximum(m_old[:, cs], mx)
            pb = jnp.concatenate([jnp.exp2(s[t * BS:(t + 1) * BS] - (m_new - addend[t][:, cs])).astype(BF16)
                                  for t in range(n)], axis=0)
            pv = jnp.dot(with_ones(vjt[h * DH:(h + 1) * DH, :]), pb, preferred_element_type=F32)
            alpha = jnp.exp2(m_old[:, cs] - m_new)
            m_out.append(m_new)
            l_out.append(alpha * l_old[:, cs] + pv[DH:DH + 1, :])
            acc_out.append(alpha * acc_old[:, cs] + pv[0:DH, :])
        m_ref[...] = jnp.concatenate(m_out, axis=1)
        l_ref[...] = jnp.concatenate(l_out, axis=1)
        acc_ref[...] = jnp.concatenate(acc_out, axis=1)

    n_far = jnp.maximum(i - 1, 0)
    n_pairs = lax.shift_right_logical(n_far, 1)

    def pair(p, slot, look_ahead=True):
        ahead = (2 * jnp.minimum(p + 1, n_pairs - 1), 1 - slot) if look_ahead else None
        visit_far(2 * p, 2, slot, ahead=ahead)

    def far_octet(w, carry):
        for t in range(4):
            pair(4 * w + t, t % 2)
        return carry

    lax.fori_loop(0, lax.shift_right_logical(n_pairs, 2), far_octet, 0)

    @pl.when((n_pairs & 2) != 0)
    def _():
        base = 4 * lax.shift_right_logical(n_pairs, 2)
        pair(base, 0)
        pair(base + 1, 1)

    @pl.when((n_pairs & 1) != 0)
    def _():
        pair(n_pairs - 1, 0, look_ahead=False)

    @pl.when(n_far % 2 == 1)
    def _():
        far_scores(n_far - 1, 1, 1)
        visit_far(n_far - 1, 1, 1)

    out_t = jnp.concatenate([acc_ref[:, h * BS:(h + 1) * BS] / l_ref[:, h * BS:(h + 1) * BS] for h in range(G)],
                            axis=0)
    o_ref[0] = out_t.T.astype(o_ref.dtype)


def _moba(qm_t, km, vm_t, bias, rel_bias, B, S):
    BS, G, DH = MOBA_BLOCK, MOBA_GROUP, MOBA_DH
    W = G * DH
    n_grp = MOBA_HEADS // G
    assert S % BS == 0 and S >= 2 * BS
    NB = S // BS
    NBP = -(-NB // 8) * 8
    return pl.pallas_call(
        _moba_kernel,
        grid=(B, n_grp, NB),
        in_specs=[pl.BlockSpec(memory_space=pltpu.SMEM),
                  pl.BlockSpec((1, W, BS), lambda b, g, i: (b * NB + i, g, 0)),
                  pl.BlockSpec((1, S, W), lambda b, g, i: (b, 0, g)),
                  pl.BlockSpec((NB, W, BS), lambda b, g, i: (b, g, 0)),
                  pl.BlockSpec((1, 2, BS, G * BS), lambda b, g, i: (g, 0, 0, 0))],
        out_specs=pl.BlockSpec((1, BS, W), lambda b, g, i: (b, i, g)),
        out_shape=jax.ShapeDtypeStruct((B, S, MOBA_W), BF16),
        scratch_shapes=[pltpu.VMEM((W, G * BS), BF16), pltpu.VMEM((NBP, W), F32),
                        pltpu.VMEM((8, G * BS), jnp.int32),
                        pltpu.VMEM((1, G * BS), F32), pltpu.VMEM((1, G * BS), F32),
                        pltpu.VMEM((DH, G * BS), F32), pltpu.VMEM((2, 2 * BS, G * BS), F32)],
        compiler_params=_params("parallel", "parallel", "arbitrary"),
        name="moba_attention",
    )(rel_bias.astype(F32), qm_t, km.reshape(B, S, -1), vm_t, bias)


def _mem_kv_kernel(mem_ref, g_ref, w_ref, k_ref, v_ref):
    kv = jnp.dot(_rms(mem_ref[0], g_ref[...]).astype(BF16), w_ref[...], preferred_element_type=F32)
    k_ref[0] = kv[:, :MEM_W].astype(BF16)
    v_ref[0] = kv[:, MEM_W:].astype(BF16)


def _mem_kv(mem, g_mem, w_ckv):
    B, M, D = mem.shape
    spec = pl.BlockSpec((1, M, MEM_W), lambda b: (b, 0, 0))
    return pl.pallas_call(
        _mem_kv_kernel,
        grid=(B,),
        in_specs=[pl.BlockSpec((1, M, D), lambda b: (b, 0, 0)), _full((1, D)), _full((D, 2 * MEM_W))],
        out_specs=[spec, spec],
        out_shape=[jax.ShapeDtypeStruct((B, M, MEM_W), BF16)] * 2,
        compiler_params=_params("parallel"),
        name="memory_kv",
    )(mem, g_mem.reshape(1, D).astype(F32), w_ckv.astype(BF16))


INFO_W0, INFO_W1, INFO_E0, INFO_E1, INFO_R0, INFO_R1 = range(6)
BLK_EXPERT, BLK_VALID = range(2)
ROUTER_GROUP_LANE0, ROUTER_EXPERT_LANE0 = 0, N_GROUPS


def _mix_kernel(n_pieces, x_ref, og_ref, om_ref, zg_ref, zm_ref, mk_ref, mv_ref, gc_ref, gm_ref,
                wpg, wpm, wout, wcq, wco, wr, br,
                x2_ref, info_ref, cnt_ref, *rest):
    hp_refs, base_ref = rest[:n_pieces], rest[n_pieces]
    first = (pl.program_id(0) == 0) & (pl.program_id(1) == 0)

    @pl.when(first)
    def _():
        base_ref[...] = jnp.zeros_like(base_ref)

    def mm(a, w_ref):
        return jnp.dot(a.astype(BF16), w_ref[...], preferred_element_type=F32)

    n_rows = x_ref.shape[1]
    sub = n_rows // MIX_PARTS
    parts = [slice(p * sub, (p + 1) * sub) for p in range(MIX_PARTS)]

    merged = [jax.nn.sigmoid(zg_ref[0, rs, :].astype(F32)) * mm(og_ref[0, rs, :], wpg)
              + jax.nn.sigmoid(zm_ref[0, rs, :].astype(F32)) * mm(om_ref[0, rs, :], wpm) for rs in parts]
    x1 = [x_ref[0, rs, :] + mm(m, wout) for rs, m in zip(parts, merged)]

    qc = [mm(_rms(v, gc_ref[...]), wcq).astype(BF16) for v in x1]

    def mem_attention(q):
        heads = []
        for h in range(MEM_HEADS):
            cs = slice(h * MEM_DH, (h + 1) * MEM_DH)
            s = lax.dot_general(q[:, cs], mk_ref[0, :, cs], (((1,), (1,)), ((), ())),
                                preferred_element_type=F32) * (MEM_DH ** -0.5)
            p = jnp.exp(s - jnp.max(s, axis=-1, keepdims=True))
            o = jnp.dot(p.astype(BF16), mv_ref[0, :, cs], preferred_element_type=F32)
            heads.append(o / jnp.sum(p, axis=-1, keepdims=True))
        return jnp.concatenate(heads, axis=-1)

    attn = [mem_attention(q) for q in qc]
    x2 = [v + mm(a, wco) for v, a in zip(x1, attn)]
    for rs, v in zip(parts, x2):
        x2_ref[0, rs, :] = v

    hm = [_rms(v, gm_ref[...]) for v in x2]
    logits = [mm(h, wr) + br[...] for h in hm]
    for rs, h in zip(parts, hm):
        words = _pack_bf16_pairs(h)
        for c, hp_ref in enumerate(hp_refs):
            hp_ref[0, rs, :] = words[:, c * SC_GATHER_WORDS:(c + 1) * SC_GATHER_WORDS]
    lane = lax.broadcasted_iota(jnp.int32, (sub, LANES), 1)
    is_grp = lane < N_GROUPS
    e_id = lane - ROUTER_EXPERT_LANE0

    def route(lg):
        gl = jnp.where(is_grp, lg, NEG_INF)
        ge = jnp.exp(gl - jnp.max(gl, axis=-1, keepdims=True))
        g_prob = ge / jnp.sum(ge, axis=-1, keepdims=True)
        p_grp = jnp.max(g_prob, axis=-1, keepdims=True)
        grp = jnp.min(jnp.where((g_prob == p_grp) & is_grp, lane, LANES), axis=-1, keepdims=True)
        in_grp = (e_id >= grp * EXPERTS_PER_GROUP) & (e_id < (grp + 1) * EXPERTS_PER_GROUP)
        el = jnp.where(in_grp, lg, NEG_INF)
        ee = jnp.exp(el - jnp.max(el, axis=-1, keepdims=True))
        e_prob = jnp.where(in_grp, ee / jnp.sum(ee, axis=-1, keepdims=True), -1.0)
        p0 = jnp.max(e_prob, axis=-1, keepdims=True)
        e0 = jnp.min(jnp.where(e_prob == p0, e_id, LANES), axis=-1, keepdims=True)
        e_rest = jnp.where(e_id == e0, -1.0, e_prob)
        p1 = jnp.max(e_rest, axis=-1, keepdims=True)
        e1 = jnp.min(jnp.where(e_rest == p1, e_id, LANES), axis=-1, keepdims=True)
        return e0, e1, p_grp * p0 / (p0 + p1), p_grp * p1 / (p0 + p1)

    routed = [route(lg) for lg in logits]

    before = (lax.broadcasted_iota(jnp.int32, (sub, sub), 1)
              < lax.broadcasted_iota(jnp.int32, (sub, sub), 0)).astype(BF16)
    base = base_ref[...]
    for rs, (e0, e1, w0, w1) in zip(parts, routed):
        onehot = ((lane == e0) | (lane == e1)).astype(F32)
        seen = base + jnp.dot(before, onehot.astype(BF16), preferred_element_type=F32)
        r0 = jnp.sum(jnp.where(lane == e0, seen, 0.0), axis=-1, keepdims=True)
        r1 = jnp.sum(jnp.where(lane == e1, seen, 0.0), axis=-1, keepdims=True)
        base = base + jnp.sum(onehot, axis=0, keepdims=True)
        info = jnp.zeros((sub, LANES), F32)
        for ln, val in ((INFO_W0, w0), (INFO_W1, w1), (INFO_E0, e0.astype(F32)), (INFO_E1, e1.astype(F32)),
                        (INFO_R0, r0), (INFO_R1, r1)):
            info = jnp.where(lane == ln, val, info)
        info_ref[0, rs, :] = info
    base_ref[...] = base
    cnt_ref[...] = base


def _mix(x, o_g, o_m, z_g, z_m, mem_k, mem_v, g_cross, g_moe, w_proj_gla, w_proj_moba, w_out, w_cq, w_co,
         w_rg, b_rg, w_re, b_re):
    B, S, D = x.shape
    M = mem_k.shape[1]
    rows = min(MIX_ROWS, S)
    assert S % rows == 0
    pad = LANES - N_GROUPS - N_EXPERTS
    wr = jnp.pad(jnp.concatenate([w_rg, w_re], axis=1), ((0, 0), (0, pad))).astype(BF16)
    br = jnp.pad(jnp.concatenate([b_rg, b_re]), (0, pad)).reshape(1, LANES).astype(F32)
    weights = [w_proj_gla.astype(BF16), w_proj_moba.astype(BF16), w_out.astype(BF16), w_cq.astype(BF16),
               w_co.astype(BF16), wr, br]
    tile = lambda n: pl.BlockSpec((1, rows, n), lambda b, i: (b, i, 0))
    memspec = pl.BlockSpec((1, M, MEM_W), lambda b, i: (b, 0, 0))
    n_pieces = D // 2 // SC_GATHER_WORDS
    return pl.pallas_call(
        functools.partial(_mix_kernel, n_pieces),
        grid=(B, S // rows),
        in_specs=[tile(D), tile(GLA_V), tile(MOBA_W), tile(D), tile(D), memspec, memspec,
                  _full((1, D)), _full((1, D))] + [_full(w.shape) for w in weights],
        out_specs=[tile(D), tile(LANES), _full((1, LANES))] + [tile(SC_GATHER_WORDS)] * n_pieces,
        out_shape=[jax.ShapeDtypeStruct((B, S, D), F32), jax.ShapeDtypeStruct((B, S, LANES), F32),
                   jax.ShapeDtypeStruct((1, LANES), F32)]
        + [jax.ShapeDtypeStruct((B, S, SC_GATHER_WORDS), jnp.uint32)] * n_pieces,
        scratch_shapes=[pltpu.VMEM((1, LANES), F32)],
        compiler_params=_params("arbitrary", "arbitrary"),
        name="merge_memattn_router",
    )(x, o_g, o_m, z_g.reshape(B, S, D), z_m.reshape(B, S, D), mem_k, mem_v,
      g_cross.reshape(1, D).astype(F32), g_moe.reshape(1, D).astype(F32), *weights)


def _plan_kernel(cnt_ref, info_ref, dest_ref, blk_ref):
    rows = info_ref.shape[0]
    lane1 = lax.broadcasted_iota(jnp.int32, (1, LANES), 1)
    nblk = jnp.floor((cnt_ref[...] + (EXPERT_ROWS - 1)) * (1.0 / EXPERT_ROWS))
    nblk = jnp.where(lane1 < N_EXPERTS, nblk, 0.0)
    hi = jnp.floor(nblk * (1.0 / 256.0))
    lo = nblk - 256.0 * hi
    upto = (lax.broadcasted_iota(jnp.int32, (LANES, LANES), 0)
            <= lax.broadcasted_iota(jnp.int32, (LANES, LANES), 1)).astype(BF16)
    digits = jnp.concatenate([jnp.broadcast_to(hi, (8, LANES)), jnp.broadcast_to(lo, (8, LANES))], axis=0)
    sums = jnp.dot(digits.astype(BF16), upto, preferred_element_type=F32)
    pend = sums[0:1] * 256.0 + sums[8:9]
    pstart_rows = (pend - nblk) * EXPERT_ROWS

    info = info_ref[...]
    lane = lax.broadcasted_iota(jnp.int32, (rows, LANES), 1)

    def field(ln):
        return jnp.sum(jnp.where(lane == ln, info, 0.0), axis=-1, keepdims=True)

    def dest(e, r):
        return jnp.sum(jnp.where(lane == e.astype(jnp.int32), pstart_rows, 0.0), axis=-1, keepdims=True) + r

    d0 = dest(field(INFO_E0), field(INFO_R0))
    d1 = dest(field(INFO_E1), field(INFO_R1))
    cols = jnp.where(lane == 0, d0, jnp.where(lane == 1, d1, 0.0))
    move = dest_ref.shape[2]
    for c in range(dest_ref.shape[0]):
        dest_ref[c] = cols[c * move:(c + 1) * move, :].T[0:8, :].astype(jnp.int32)

    @pl.when(pl.program_id(0) == 0)
    def _():
        n = lax.broadcasted_iota(jnp.int32, (blk_ref.shape[0], LANES), 0).astype(F32)
        blane = lax.broadcasted_iota(jnp.int32, (blk_ref.shape[0], LANES), 1)
        done = jnp.where((pend <= n) & (lane1 < N_EXPERTS), 1.0, 0.0)
        e = jnp.minimum(jnp.sum(done, axis=-1, keepdims=True), N_EXPERTS - 1.0)
        mine = blane == e.astype(jnp.int32)
        first_blk = jnp.sum(jnp.where(mine, pend - nblk, 0.0), axis=-1, keepdims=True)
        count = jnp.sum(jnp.where(mine, cnt_ref[...], 0.0), axis=-1, keepdims=True)
        valid = jnp.clip(count - EXPERT_ROWS * (n[:, 0:1] - first_blk), 0.0, float(EXPERT_ROWS))
        blk_ref[...] = jnp.where(blane == BLK_EXPERT, e, jnp.where(blane == BLK_VALID, valid, 0.0)).astype(jnp.int32)


def _plan(counts, info2d, n_blk):
    T = info2d.shape[0]
    move = min(MOVE_ROWS, T)
    rows = min(PLAN_ROWS, T)
    assert T % rows == 0 and rows % move == 0
    n_blk_pad = -(-n_blk // 8) * 8
    return pl.pallas_call(
        _plan_kernel,
        grid=(T // rows,),
        in_specs=[_full((1, LANES)), pl.BlockSpec((rows, LANES), lambda i: (i, 0))],
        out_specs=[pl.BlockSpec((rows // move, 8, move), lambda i: (i, 0, 0)), _full((n_blk_pad, LANES))],
        out_shape=[jax.ShapeDtypeStruct((T // move, 8, move), jnp.int32),
                   jax.ShapeDtypeStruct((n_blk_pad, LANES), jnp.int32)],
        compiler_params=_params("arbitrary"),
        name="dispatch_plan",
    )(counts, info2d)


def _sc_windows(n_rows):
    n_inner = 32
    assert n_rows % (SC_GATHER_ROWS * n_inner) == 0
    return n_rows // (SC_GATHER_ROWS * n_inner), n_inner


def _sc_mesh():
    return plsc.VectorSubcoreMesh(core_axis_name="c", subcore_axis_name="s")


def _sc_scatter_rows(src, idx_a, idx_b, n_out):
    T, W = src.shape
    n_outer, n_inner = _sc_windows(T)
    win = lambda i, j: i * n_inner + j

    @pl.kernel(out_type=jax.ShapeDtypeStruct((n_out, W), src.dtype), mesh=_sc_mesh(), scratch_types=[])
    def scatter_kernel(s_hbm, a_hbm, b_hbm, o_hbm):
        def body(s_vmem, a_vmem, b_vmem):
            pltpu.sync_copy(s_vmem, o_hbm.at[a_vmem.at[0]])
            pltpu.sync_copy(s_vmem, o_hbm.at[b_vmem.at[0]])

        pltpu.emit_pipeline(
            body,
            grid=(n_outer, n_inner),
            in_specs=[pl.BlockSpec((SC_GATHER_ROWS, W), index_map=lambda i, j: (win(i, j), 0)),
                      pl.BlockSpec((1, SC_GATHER_ROWS), index_map=lambda i, j: (0, win(i, j))),
                      pl.BlockSpec((1, SC_GATHER_ROWS), index_map=lambda i, j: (0, win(i, j)))],
            out_specs=[],
            core_axis_name=("c", "s"),
            dimension_semantics=(pltpu.PARALLEL, pltpu.PARALLEL),
        )(s_hbm, a_hbm, b_hbm)

    return scatter_kernel(src, idx_a, idx_b)


def _sc_gather_rows(table, idx):
    M = idx.shape[1]
    W = table.shape[1]
    n_outer, n_inner = _sc_windows(M)
    win = lambda i, j: i * n_inner + j

    @pl.kernel(out_type=jax.ShapeDtypeStruct((M, W), table.dtype), mesh=_sc_mesh(), scratch_types=[])
    def gather_kernel(t_hbm, i_hbm, o_hbm):
        def body(i_vmem, o_vmem):
            pltpu.sync_copy(t_hbm.at[i_vmem.at[0]], o_vmem)

        pltpu.emit_pipeline(
            body,
            grid=(n_outer, n_inner),
            in_specs=[pl.BlockSpec((1, SC_GATHER_ROWS), index_map=lambda i, j: (0, win(i, j)))],
            out_specs=[pl.BlockSpec((SC_GATHER_ROWS, W), index_map=lambda i, j: (win(i, j), 0))],
            core_axis_name=("c", "s"),
            dimension_semantics=(pltpu.PARALLEL, pltpu.PARALLEL),
        )(i_hbm, o_hbm)

    return gather_kernel(table, idx)


def _expert_kernel(n_pieces, blk_e_ref, blk_valid_ref, *refs):
    xs_refs, (wg_ref, wu_ref, wd_ref) = refs[:n_pieces], refs[n_pieces:n_pieces + 3]
    y_refs, (wg_bf, wu_bf, wd_bf) = refs[n_pieces + 3:2 * n_pieces + 3], refs[2 * n_pieces + 3:]
    n = pl.program_id(0)
    prev = blk_e_ref[jnp.maximum(n - 1, 0)]

    @pl.when((n == 0) | (blk_e_ref[n] != prev))
    def _():
        wg_bf[...] = wg_ref[0].astype(BF16)
        wu_bf[...] = wu_ref[0].astype(BF16)
        wd_bf[...] = wd_ref[0].astype(BF16)

    sub = xs_refs[0].shape[0] // EXPERT_PARTS
    parts = [slice(p * sub, (p + 1) * sub) for p in range(EXPERT_PARTS)]
    row = lax.broadcasted_iota(jnp.int32, (sub, 1), 0)
    valid = blk_valid_ref[n]

    def load(p, rs):
        words = jnp.concatenate([r[rs, :] for r in xs_refs], axis=1)
        words = jnp.where(row + p * sub < valid, words, jnp.zeros_like(words))
        return _unpack_bf16_pairs(words).astype(BF16)

    xb = [load(p, rs) for p, rs in enumerate(parts)]
    gate = [jnp.dot(v, wg_bf[...], preferred_element_type=F32) for v in xb]
    up = [jnp.dot(v, wu_bf[...], preferred_element_type=F32) for v in xb]
    hid = [(g * jax.nn.sigmoid(g) * u).astype(BF16) for g, u in zip(gate, up)]
    for rs, hv in zip(parts, hid):
        words = _pack_bf16_pairs(jnp.dot(hv, wd_bf[...], preferred_element_type=F32))
        for c, y_ref in enumerate(y_refs):
            y_ref[rs, :] = words[:, c * SC_GATHER_WORDS:(c + 1) * SC_GATHER_WORDS]


def _experts(blk_e, blk_valid, xs_pieces, w_gate, w_up, w_down):
    cap = xs_pieces[0].shape[0]
    _, D, DE = w_gate.shape
    n_blk = cap // EXPERT_ROWS
    n_pieces = len(xs_pieces)
    piece = pl.BlockSpec((EXPERT_ROWS, SC_GATHER_WORDS), lambda n, e, v: (n, 0))
    return pl.pallas_call(
        functools.partial(_expert_kernel, n_pieces),
        grid_spec=pltpu.PrefetchScalarGridSpec(
            num_scalar_prefetch=2,
            grid=(n_blk,),
            in_specs=[piece] * n_pieces
            + [pl.BlockSpec((1, D, DE), lambda n, e, v: (e[n], 0, 0)),
               pl.BlockSpec((1, D, DE), lambda n, e, v: (e[n], 0, 0)),
               pl.BlockSpec((1, DE, D), lambda n, e, v: (e[n], 0, 0))],
            out_specs=[piece] * n_pieces,
            scratch_shapes=[pltpu.VMEM((D, DE), BF16), pltpu.VMEM((D, DE), BF16), pltpu.VMEM((DE, D), BF16)]),
        out_shape=[jax.ShapeDtypeStruct((cap, SC_GATHER_WORDS), jnp.uint32)] * n_pieces,
        compiler_params=_params("arbitrary"),
        name="moe_experts",
    )(blk_e, blk_valid, *xs_pieces, w_gate, w_up, w_down)


def _combine_dense_kernel(final_norm, n_pieces, *refs):
    yg_refs, (x_ref, info_ref, g_ref, o_ref) = refs[:n_pieces], refs[n_pieces:]
    info = info_ref[...]
    w0 = info[:, INFO_W0:INFO_W0 + 1]
    w1 = info[:, INFO_W1:INFO_W1 + 1]
    y = [_unpack_bf16_pairs(jnp.concatenate([r[slot] for r in yg_refs], axis=1)) for slot in range(TOPK_IN_GROUP)]
    out = x_ref[...] + (w0 * y[0] + w1 * y[1])
    o_ref[...] = _rms(out, g_ref[...]) if final_norm else out


def _combine_dense(yg_pieces, x2d, info2d, g_final, final_norm):
    T, D = x2d.shape
    rows = min(MOVE_ROWS, T)
    n_pieces = len(yg_pieces)
    return pl.pallas_call(
        functools.partial(_combine_dense_kernel, final_norm, n_pieces),
        grid=(T // rows,),
        in_specs=[pl.BlockSpec((TOPK_IN_GROUP, rows, SC_GATHER_WORDS), lambda i: (0, i, 0))] * n_pieces
        + [pl.BlockSpec((rows, D), lambda i: (i, 0)), pl.BlockSpec((rows, LANES), lambda i: (i, 0)),
           _full((1, D))],
        out_specs=pl.BlockSpec((rows, D), lambda i: (i, 0)),
        out_shape=jax.ShapeDtypeStruct((T, D), F32),
        compiler_params=_params("parallel"),
        name="moe_combine_dense_final_norm",
    )(*yg_pieces, x2d, info2d, g_final.reshape(1, D).astype(F32))


def kernel(x, mem, g_mem, rel_bias, g_mix, w_in, w_alpha_up, b_alpha, g_gla_head, w_proj_gla, w_proj_moba,
           w_out, g_cross, w_cq, w_ckv, w_co, g_moe, w_router_group, b_router_group, w_router_expert,
           b_router_expert, w_exp_gate, w_exp_up, w_exp_down, g_final):
    B, S, D = x.shape
    T = B * S
    depth = g_mix.shape[0]
    n_assign = T * TOPK_IN_GROUP
    n_blk = -(-(n_assign + N_EXPERTS * (EXPERT_ROWS - 1)) // EXPERT_ROWS)
    cap = n_blk * EXPERT_ROWS

    mem_bias = _moba_bias(rel_bias)
    for l in range(depth):
        qk, v_g, r_g, la, q_m, k_m, v_m, z_g, z_m = _project(x.reshape(T, D), g_mix[l], w_in[l], w_alpha_up[l],
                                                             b_alpha[l])
        o_g = _gla(qk, la, v_g, r_g, g_gla_head[l], B, S)
        o_m = _moba(q_m, k_m, v_m, mem_bias, rel_bias, B, S)
        mem_k, mem_v = _mem_kv(mem, g_mem, w_ckv[l])
        x2, info, counts, *hm_pieces = _mix(x, o_g, o_m, z_g, z_m, mem_k, mem_v, g_cross[l], g_moe[l],
                                            w_proj_gla[l], w_proj_moba[l], w_out[l], w_cq[l], w_co[l],
                                            w_router_group[l], b_router_group[l], w_router_expert[l],
                                            b_router_expert[l])
        x2d, info2d = x2.reshape(T, D), info.reshape(T, LANES)
        dest, blk = _plan(counts, info2d, n_blk)
        idx = jnp.transpose(dest[:, 0:TOPK_IN_GROUP, :], (1, 0, 2)).reshape(1, TOPK_IN_GROUP * T)
        xs_pieces = [_sc_scatter_rows(h.reshape(T, SC_GATHER_WORDS), idx[:, :T], idx[:, T:], cap)
                     for h in hm_pieces]
        y_pieces = _experts(blk[:n_blk, BLK_EXPERT], blk[:n_blk, BLK_VALID], xs_pieces, w_exp_gate[l],
                            w_exp_up[l], w_exp_down[l])
        yg = [_sc_gather_rows(y, idx).reshape(TOPK_IN_GROUP, T, SC_GATHER_WORDS) for y in y_pieces]
        x = _combine_dense(yg, x2d, info2d, g_final, final_norm=(l == depth - 1)).reshape(B, S, D)
    return x
```

```python
import functools
import math

import jax
import jax.numpy as jnp
from jax import lax
from jax.experimental import pallas as pl
from jax.experimental.pallas import tpu as pltpu
from jax.experimental.pallas import tpu_sc as plsc

F32 = jnp.float32
BF16 = jnp.bfloat16
NEG_INF = float("-inf")

EPS = 1e-6
GLA_HEADS, GLA_DK, GLA_DV, GLA_LOWRANK, GLA_TAU, GLA_CHUNK = 4, 64, 128, 16, 16.0, 64
GLA_QK, GLA_V = GLA_HEADS * GLA_DK, GLA_HEADS * GLA_DV
MOBA_HEADS, MOBA_DH, MOBA_BLOCK, MOBA_TOPK = 8, 64, 256, 3
MOBA_W = MOBA_HEADS * MOBA_DH
LOG2E = math.log2(math.e)
MOBA_Q_SCALE = MOBA_DH ** -0.5 * LOG2E
REL_BUCKETS, REL_MAX_DIST = 32, 128
MEM_HEADS, MEM_DH = 4, 128
MEM_W = MEM_HEADS * MEM_DH
N_GROUPS, EXPERTS_PER_GROUP, TOPK_IN_GROUP = 4, 8, 2
N_EXPERTS = N_GROUPS * EXPERTS_PER_GROUP

LANES = 128
SUBLANES = 8
VMEM_LIMIT_BYTES = 56 * 1024 * 1024

PROJ_ROWS = 512
GLA_ROWS = 512
MOBA_GROUP = 4
MIX_ROWS = 512
MIX_PARTS = 2
EXPERT_ROWS = 256
MOVE_ROWS = 512
PLAN_ROWS = 2048
SC_GATHER_ROWS = 128
SC_GATHER_WORDS = 256
EXPERT_PARTS = 2


def _params(*semantics):
    return pltpu.CompilerParams(dimension_semantics=semantics, vmem_limit_bytes=VMEM_LIMIT_BYTES)


def _full(shape):
    return pl.BlockSpec(shape, lambda *_: (0,) * len(shape))


def _rms(x, g):
    return x * lax.rsqrt(jnp.mean(x * x, axis=-1, keepdims=True) + EPS) * g


def _pack_bf16_pairs(x):
    n = x.shape[1] // 2
    bits = pltpu.bitcast(x.astype(BF16).astype(F32), jnp.uint32)
    return bits[:, n:] | (bits[:, :n] >> 16)


def _unpack_bf16_pairs(w):
    lo = pltpu.bitcast(w << 16, F32)
    hi = pltpu.bitcast(w & jnp.uint32(0xFFFF0000), F32)
    return jnp.concatenate([lo, hi], axis=1)


def _proj_kernel(x_ref, g_ref, w_qk, w_v, w_r, w_a, w_up, b_a, w_qm, w_km, w_vm, w_zg, w_zm,
                 o_qk, o_v, o_r, o_la, o_qm, o_km, o_vm, o_zg, o_zm):
    h = _rms(x_ref[...], g_ref[...]).astype(BF16)

    def mm(w_ref):
        return jnp.dot(h, w_ref[...], preferred_element_type=F32)

    o_qk[...] = mm(w_qk)
    o_v[...] = mm(w_v).astype(BF16)
    o_r[...] = mm(w_r)
    a_lr = mm(w_a).astype(BF16)
    pre = jnp.dot(a_lr, w_up[...], preferred_element_type=F32) + b_a[...]
    o_la[...] = jax.nn.log_sigmoid(pre) * (1.0 / GLA_TAU)

    def mm_t(wt_ref):
        return lax.dot_general(wt_ref[...], h, (((1,), (1,)), ((), ())), preferred_element_type=F32)

    def store_blocks(o_ref, val_t):
        for c in range(o_ref.shape[0]):
            o_ref[c] = val_t[:, c * MOBA_BLOCK:(c + 1) * MOBA_BLOCK]

    store_blocks(o_qm, (mm_t(w_qm) * MOBA_Q_SCALE).astype(BF16))
    o_km[...] = mm(w_km).astype(BF16)
    store_blocks(o_vm, mm_t(w_vm).astype(BF16))
    o_zg[...] = mm(w_zg)
    o_zm[...] = mm(w_zm)


def _project(x2d, g_mix, w_in, w_alpha_up, b_alpha):
    T, D = x2d.shape
    rows = min(PROJ_ROWS, T)
    assert T % rows == 0
    splits = (GLA_QK, GLA_QK, GLA_V, GLA_V, GLA_LOWRANK, MOBA_W, MOBA_W, MOBA_W, D, D)
    offs = [0]
    for s in splits:
        offs.append(offs[-1] + s)
    wb = w_in.astype(BF16)
    sec = lambda i, j: wb[:, offs[i]:offs[j]]
    w_a = jnp.pad(sec(4, 5), ((0, 0), (0, LANES - GLA_LOWRANK)))
    w_up = jnp.pad(w_alpha_up.astype(BF16), ((0, LANES - GLA_LOWRANK), (0, 0)))
    weights = [sec(0, 2), sec(2, 3), sec(3, 4), w_a, w_up, b_alpha.reshape(1, GLA_QK).astype(F32),
               sec(5, 6).T, sec(6, 7), sec(7, 8).T, sec(8, 9), sec(9, 10)]
    out_defs = [(2 * GLA_QK, F32, False), (GLA_V, BF16, False), (GLA_V, F32, False), (GLA_QK, F32, False),
                (MOBA_W, BF16, True), (MOBA_W, BF16, False), (MOBA_W, BF16, True), (D, F32, False),
                (D, F32, False)]
    BS = MOBA_BLOCK
    assert rows % BS == 0
    row_spec = lambda n: pl.BlockSpec((rows, n), lambda i: (i, 0))
    blk_spec = lambda n: pl.BlockSpec((rows // BS, n, BS), lambda i: (i, 0, 0))
    return pl.pallas_call(
        _proj_kernel,
        grid=(T // rows,),
        in_specs=[row_spec(D), _full((1, D))] + [_full(w.shape) for w in weights],
        out_specs=[blk_spec(n) if t else row_spec(n) for n, _, t in out_defs],
        out_shape=[jax.ShapeDtypeStruct((T // BS, n, BS) if t else (T, n), dt) for n, dt, t in out_defs],
        compiler_params=_params("parallel"),
        name="norm_in_proj",
    )(x2d, g_mix.reshape(1, D).astype(F32), *weights)


def _gla_kernel(qk_ref, la_ref, v_ref, r_ref, g_ref, o_ref, state_ref, obuf_ref):
    C, H, DK, DV = GLA_CHUNK, GLA_HEADS, GLA_DK, GLA_DV
    rows = qk_ref.shape[1]

    @pl.when(pl.program_id(1) == 0)
    def _():
        state_ref[...] = jnp.zeros_like(state_ref)

    tri = (lax.broadcasted_iota(jnp.int32, (C, C), 0) >= lax.broadcasted_iota(jnp.int32, (C, C), 1)).astype(BF16)
    lane_head = lax.broadcasted_iota(jnp.int32, (1, H * DK), 1) // DK
    head_masks = [(lane_head == h).astype(F32) for h in range(H)]
    stack_row = lax.broadcasted_iota(jnp.int32, (H * C, C), 0) % C
    stack_col = lax.broadcasted_iota(jnp.int32, (H * C, C), 1)
    causal = stack_col <= stack_row
    same_head = (lax.broadcasted_iota(jnp.int32, (H * DV, H * DK), 0) // DV
                 == lax.broadcasted_iota(jnp.int32, (H * DV, H * DK), 1) // DK)
    scale = DK ** -0.5

    def stack(m):
        return jnp.concatenate([m * head_masks[h] for h in range(H)], axis=0).astype(BF16)

    chunks = [slice(c * C, (c + 1) * C) for c in range(rows // C)]

    def cum_log_decay(sl):
        la = la_ref[0, sl, :]
        p1 = la.astype(BF16)
        r1 = la - p1.astype(F32)
        p2 = r1.astype(BF16)
        p3 = (r1 - p2.astype(F32)).astype(BF16)
        s3 = jnp.dot(tri, jnp.concatenate([p1, p2, p3], axis=1), preferred_element_type=F32)
        w = H * DK
        return (s3[:, 0:w] + s3[:, w:2 * w]) + s3[:, 2 * w:3 * w]

    b_all = [cum_log_decay(sl) for sl in chunks]

    qe_all, ke_all, kd_all, qb_all, decay_all = [], [], [], [], []
    for sl, b in zip(chunks, b_all):
        q = qk_ref[0, sl, 0:H * DK] * scale
        k = qk_ref[0, sl, H * DK:2 * H * DK]
        b_last = b[C - 1:C, :]
        b_mid = b[C // 2 - 1:C // 2, :]
        qe_all.append(stack(q * jnp.exp(b - b_mid)))
        ke_all.append((k * jnp.exp(b_mid - b)).astype(BF16))
        kd_all.append((k * jnp.exp(b_last - b)).astype(BF16))
        qb_all.append((q * jnp.exp(b)).astype(BF16))
        decay_all.append(jnp.exp(b_last))

    att_all = [jnp.where(causal, lax.dot_general(qe, ke, (((1,), (1,)), ((), ())), preferred_element_type=F32),
                         0.0).astype(BF16) for qe, ke in zip(qe_all, ke_all)]

    o_intra_all, kv_all = [], []
    for sl, att, kd in zip(chunks, att_all, kd_all):
        v = v_ref[0, sl, :]
        o_intra_all.append(jnp.concatenate(
            [jnp.dot(att[h * C:(h + 1) * C, :], v[:, h * DV:(h + 1) * DV], preferred_element_type=F32)
             for h in range(H)], axis=1))
        kv_t = lax.dot_general(v, kd, (((0,), (0,)), ((), ())), preferred_element_type=F32)
        kv_all.append(jnp.where(same_head, kv_t, 0.0))

    state_t = state_ref[...]
    for sl, qb, decay, kv_t, o_intra in zip(chunks, qb_all, decay_all, kv_all, o_intra_all):
        o_inter = lax.dot_general(qb, state_t.astype(BF16), (((1,), (1,)), ((), ())),
                                  preferred_element_type=F32)
        obuf_ref[sl, :] = o_intra + o_inter
        state_t = decay * state_t + kv_t
    state_ref[...] = state_t

    r = r_ref[0]
    for h in range(H):
        cs = slice(h * DV, (h + 1) * DV)
        y = _rms(obuf_ref[:, cs], g_ref[:, cs])
        rh = r[:, cs]
        o_ref[0, :, cs] = (y * (rh * jax.nn.sigmoid(rh))).astype(BF16)


def _gla(qk, la, v, r, g_head, B, S):
    rows = min(GLA_ROWS, S)
    assert S % rows == 0 and rows % GLA_CHUNK == 0
    spec = lambda n: pl.BlockSpec((1, rows, n), lambda b, i: (b, i, 0))
    return pl.pallas_call(
        _gla_kernel,
        grid=(B, S // rows),
        in_specs=[spec(2 * GLA_QK), spec(GLA_QK), spec(GLA_V), spec(GLA_V), _full((1, GLA_V))],
        out_specs=spec(GLA_V),
        out_shape=jax.ShapeDtypeStruct((B, S, GLA_V), BF16),
        scratch_shapes=[pltpu.VMEM((GLA_V, GLA_QK), F32), pltpu.VMEM((rows, GLA_V), F32)],
        compiler_params=_params("parallel", "arbitrary"),
        name="gla_chunked",
    )(qk.reshape(B, S, -1), la.reshape(B, S, -1), v.reshape(B, S, -1), r.reshape(B, S, -1),
      g_head.reshape(1, GLA_V).astype(F32))


def _t5_bucket(dist):
    n = jnp.maximum(dist, 0)
    max_exact = REL_BUCKETS // 2
    nf = jnp.maximum(n, 1).astype(F32)
    large = max_exact + (jnp.log(nf / max_exact) / math.log(REL_MAX_DIST / max_exact)
                         * (REL_BUCKETS - max_exact)).astype(jnp.int32)
    large = jnp.minimum(large, REL_BUCKETS - 1)
    return jnp.where(n < max_exact, n, large)


def _moba_bias_kernel(rb_ref, o_ref):
    BS, G = MOBA_BLOCK, MOBA_GROUP
    grp, kind = pl.program_id(0), pl.program_id(1)
    d = (lax.broadcasted_iota(jnp.int32, (BS, BS), 1) - lax.broadcasted_iota(jnp.int32, (BS, BS), 0)
         + kind * BS)
    bucket = _t5_bucket(d)
    for h in range(G):
        val = jnp.zeros((BS, BS), F32)
        for bkt in range(REL_BUCKETS):
            val = jnp.where(bucket == bkt, rb_ref[bkt, grp * G + h] * LOG2E, val)
        o_ref[0, 0, :, h * BS:(h + 1) * BS] = jnp.where(d >= 0, val, NEG_INF)


def _moba_bias(rel_bias):
    BS, G = MOBA_BLOCK, MOBA_GROUP
    n_grp = MOBA_HEADS // G
    return pl.pallas_call(
        _moba_bias_kernel,
        grid=(n_grp, 2),
        in_specs=[pl.BlockSpec(memory_space=pltpu.SMEM)],
        out_specs=pl.BlockSpec((1, 1, BS, G * BS), lambda g, k: (g, k, 0, 0)),
        out_shape=jax.ShapeDtypeStruct((n_grp, 2, BS, G * BS), F32),
        compiler_params=_params("parallel", "parallel"),
        name="moba_bias_tables",
    )(rel_bias.astype(F32))


def _moba_kernel(rb_ref, q_ref, k_ref, v_ref, bias_ref, o_ref,
                 qs_ref, kmean_ref, sel_ref, m_ref, l_ref, acc_ref, sbuf_ref):
    BS, G, DH = MOBA_BLOCK, MOBA_GROUP, MOBA_DH
    NBP = kmean_ref.shape[0]
    grp, i = pl.program_id(1), pl.program_id(2)

    @pl.when(i == 0)
    def _():
        S = k_ref.shape[1]
        blk_of_key = lax.broadcasted_iota(jnp.int32, (NBP, S), 1) // BS
        ind = (blk_of_key == lax.broadcasted_iota(jnp.int32, (NBP, S), 0)).astype(BF16)
        kmean_ref[...] = jnp.dot(ind, k_ref[0], preferred_element_type=F32) * (1.0 / BS)

    qt = q_ref[0]
    sub_head = lax.broadcasted_iota(jnp.int32, (G * DH, 1), 0) // DH
    for h in range(G):
        qs_ref[:, h * BS:(h + 1) * BS] = jnp.where(sub_head == h, qt, jnp.zeros_like(qt))

    gate = jnp.dot(kmean_ref[...].astype(BF16), qs_ref[...], preferred_element_type=F32)
    blk = lax.broadcasted_iota(jnp.int32, gate.shape, 0)
    gate = jnp.where(blk < i, gate, NEG_INF)
    for t in range(MOBA_TOPK):
        mx = jnp.max(gate, axis=0, keepdims=True)
        hit = (gate == mx) & (mx > NEG_INF)
        idx = jnp.min(jnp.where(hit, blk, NBP), axis=0, keepdims=True)
        sel_ref[t:t + 1, :] = idx
        gate = jnp.where(blk == idx, NEG_INF, gate)

    def mask_row(j):
        hit = (sel_ref[0:1, :] == j) | (sel_ref[1:2, :] == j) | (sel_ref[2:3, :] == j)
        return jnp.where(hit, 0.0, NEG_INF)

    def with_ones(vt):
        return jnp.concatenate([vt, jnp.ones((8, vt.shape[1]), BF16)], axis=0)

    def far_scores(j0, n, slot):
        kj = k_ref[0, pl.ds(pl.multiple_of(j0 * BS, BS), n * BS), :]
        for h in range(G):
            cs = slice(h * BS, (h + 1) * BS)
            sbuf_ref[slot, 0:n * BS, cs] = jnp.dot(kj, qs_ref[:, cs], preferred_element_type=F32)

    j_prev = jnp.maximum(i - 1, 0)
    k_own = k_ref[0, pl.ds(pl.multiple_of(i * BS, BS), BS), :]
    k_prev = k_ref[0, pl.ds(pl.multiple_of(j_prev * BS, BS), BS), :]
    vt_near = jnp.concatenate([v_ref[i], v_ref[j_prev]], axis=1)
    prev_mask = mask_row(i - 1)
    for h in range(G):
        cs = slice(h * BS, (h + 1) * BS)
        sbuf_ref[1, 0:BS, cs] = jnp.dot(k_own, qs_ref[:, cs], preferred_element_type=F32)
        sbuf_ref[1, BS:2 * BS, cs] = jnp.dot(k_prev, qs_ref[:, cs], preferred_element_type=F32)
    far_scores(0, 2, 0)
    m_out, l_out, acc_out = [], [], []
    for h in range(G):
        cs = slice(h * BS, (h + 1) * BS)
        s_own = sbuf_ref[1, 0:BS, cs] + bias_ref[0, 0, :, cs]
        s_prev = sbuf_ref[1, BS:2 * BS, cs] + (bias_ref[0, 1, :, cs] + prev_mask[:, cs])
        m0 = jnp.maximum(jnp.max(s_own, axis=0, keepdims=True), jnp.max(s_prev, axis=0, keepdims=True))
        pb = jnp.concatenate([jnp.exp2(s_own - m0).astype(BF16), jnp.exp2(s_prev - m0).astype(BF16)], axis=0)
        pv = jnp.dot(with_ones(vt_near[h * DH:(h + 1) * DH, :]), pb, preferred_element_type=F32)
        m_out.append(m0)
        l_out.append(pv[DH:DH + 1, :])
        acc_out.append(pv[0:DH, :])
    m_ref[...] = jnp.concatenate(m_out, axis=1)
    l_ref[...] = jnp.concatenate(l_out, axis=1)
    acc_ref[...] = jnp.concatenate(acc_out, axis=1)

    lane_head = lax.broadcasted_iota(jnp.int32, (1, G * BS), 1) // BS
    far_bias = jnp.zeros((1, G * BS), F32)
    for h in range(G):
        far_bias = jnp.where(lane_head == h, rb_ref[REL_BUCKETS - 1, grp * G + h] * LOG2E, far_bias)

    def visit_far(j0, n, slot, ahead=None):
        vjt = jnp.concatenate([v_ref[j0 + t] for t in range(n)], axis=1)
        addend = [far_bias + mask_row(j0 + t) for t in range(n)]
        m_old, l_old, acc_old = m_ref[...], l_ref[...], acc_ref[...]
        m_out, l_out, acc_out = [], [], []
        if ahead is not None:
            k_next = k_ref[0, pl.ds(pl.multiple_of(ahead[0] * BS, BS), 2 * BS), :]
        for h in range(G):
            cs = slice(h * BS, (h + 1) * BS)
            if ahead is not None:
                sbuf_ref[ahead[1], :, cs] = jnp.dot(k_next, qs_ref[:, cs], preferred_element_type=F32)
            s = sbuf_ref[slot, 0:n * BS, cs]
            mx = jnp.max(s[0:BS], axis=0, keepdims=True) + addend[0][:, cs]
            for t in range(1, n):
                mx = jnp.maximum(mx, jnp.max(s[t * BS:(t + 1) * BS], axis=0, keepdims=True) + addend[t][:, cs])
            m_new = jnp.maximum(m_old[:, cs], mx)
            pb = jnp.concatenate([jnp.exp2(s[t * BS:(t + 1) * BS] - (m_new - addend[t][:, cs])).astype(BF16)
                                  for t in range(n)], axis=0)
            pv = jnp.dot(with_ones(vjt[h * DH:(h + 1) * DH, :]), pb, preferred_element_type=F32)
            alpha = jnp.exp2(m_old[:, cs] - m_new)
            m_out.append(m_new)
            l_out.append(alpha * l_old[:, cs] + pv[DH:DH + 1, :])
            acc_out.append(alpha * acc_old[:, cs] + pv[0:DH, :])
        m_ref[...] = jnp.concatenate(m_out, axis=1)
        l_ref[...] = jnp.concatenate(l_out, axis=1)
        acc_ref[...] = jnp.concatenate(acc_out, axis=1)

    n_far = jnp.maximum(i - 1, 0)
    n_pairs = lax.shift_right_logical(n_far, 1)

    def pair(p, slot, look_ahead=True):
        ahead = (2 * jnp.minimum(p + 1, n_pairs - 1), 1 - slot) if look_ahead else None
        visit_far(2 * p, 2, slot, ahead=ahead)

    def far_octet(w, carry):
        for t in range(4):
            pair(4 * w + t, t % 2)
        return carry

    lax.fori_loop(0, lax.shift_right_logical(n_pairs, 2), far_octet, 0)

    @pl.when((n_pairs & 2) != 0)
    def _():
        base = 4 * lax.shift_right_logical(n_pairs, 2)
        pair(base, 0)
        pair(base + 1, 1)

    @pl.when((n_pairs & 1) != 0)
    def _():
        pair(n_pairs - 1, 0, look_ahead=False)

    @pl.when(n_far % 2 == 1)
    def _():
        far_scores(n_far - 1, 1, 1)
        visit_far(n_far - 1, 1, 1)

    out_t = jnp.concatenate([acc_ref[:, h * BS:(h + 1) * BS] / l_ref[:, h * BS:(h + 1) * BS] for h in range(G)],
                            axis=0)
    o_ref[0] = out_t.T.astype(o_ref.dtype)


def _moba(qm_t, km, vm_t, bias, rel_bias, B, S):
    BS, G, DH = MOBA_BLOCK, MOBA_GROUP, MOBA_DH
    W = G * DH
    n_grp = MOBA_HEADS // G
    assert S % BS == 0 and S >= 2 * BS
    NB = S // BS
    NBP = -(-NB // 8) * 8
    return pl.pallas_call(
        _moba_kernel,
        grid=(B, n_grp, NB),
        in_specs=[pl.BlockSpec(memory_space=pltpu.SMEM),
                  pl.BlockSpec((1, W, BS), lambda b, g, i: (b * NB + i, g, 0)),
                  pl.BlockSpec((1, S, W), lambda b, g, i: (b, 0, g)),
                  pl.BlockSpec((NB, W, BS), lambda b, g, i: (b, g, 0)),
                  pl.BlockSpec((1, 2, BS, G * BS), lambda b, g, i: (g, 0, 0, 0))],
        out_specs=pl.BlockSpec((1, BS, W), lambda b, g, i: (b, i, g)),
        out_shape=jax.ShapeDtypeStruct((B, S, MOBA_W), BF16),
        scratch_shapes=[pltpu.VMEM((W, G * BS), BF16), pltpu.VMEM((NBP, W), F32),
                        pltpu.VMEM((8, G * BS), jnp.int32),
                        pltpu.VMEM((1, G * BS), F32), pltpu.VMEM((1, G * BS), F32),
                        pltpu.VMEM((DH, G * BS), F32), pltpu.VMEM((2, 2 * BS, G * BS), F32)],
        compiler_params=_params("parallel", "parallel", "arbitrary"),
        name="moba_attention",
    )(rel_bias.astype(F32), qm_t, km.reshape(B, S, -1), vm_t, bias)


def _mem_kv_kernel(mem_ref, g_ref, w_ref, k_ref, v_ref):
    kv = jnp.dot(_rms(mem_ref[0], g_ref[...]).astype(BF16), w_ref[...], preferred_element_type=F32)
    k_ref[0] = kv[:, :MEM_W].astype(BF16)
    v_ref[0] = kv[:, MEM_W:].astype(BF16)


def _mem_kv(mem, g_mem, w_ckv):
    B, M, D = mem.shape
    spec = pl.BlockSpec((1, M, MEM_W), lambda b: (b, 0, 0))
    return pl.pallas_call(
        _mem_kv_kernel,
        grid=(B,),
        in_specs=[pl.BlockSpec((1, M, D), lambda b: (b, 0, 0)), _full((1, D)), _full((D, 2 * MEM_W))],
        out_specs=[spec, spec],
        out_shape=[jax.ShapeDtypeStruct((B, M, MEM_W), BF16)] * 2,
        compiler_params=_params("parallel"),
        name="memory_kv",
    )(mem, g_mem.reshape(1, D).astype(F32), w_ckv.astype(BF16))


INFO_W0, INFO_W1, INFO_E0, INFO_E1, INFO_R0, INFO_R1 = range(6)
BLK_EXPERT, BLK_VALID = range(2)
ROUTER_GROUP_LANE0, ROUTER_EXPERT_LANE0 = 0, N_GROUPS


def _mix_kernel(n_pieces, x_ref, og_ref, om_ref, zg_ref, zm_ref, mk_ref, mv_ref, gc_ref, gm_ref,
                wpg, wpm, wout, wcq, wco, wr, br,
                x2_ref, info_ref, cnt_ref, *rest):
    hp_refs, base_ref = rest[:n_pieces], rest[n_pieces]
    first = (pl.program_id(0) == 0) & (pl.program_id(1) == 0)

    @pl.when(first)
    def _():
        base_ref[...] = jnp.zeros_like(base_ref)

    def mm(a, w_ref):
        return jnp.dot(a.astype(BF16), w_ref[...], preferred_element_type=F32)

    n_rows = x_ref.shape[1]
    sub = n_rows // MIX_PARTS
    parts = [slice(p * sub, (p + 1) * sub) for p in range(MIX_PARTS)]

    merged = [jax.nn.sigmoid(zg_ref[0, rs, :]) * mm(og_ref[0, rs, :], wpg)
              + jax.nn.sigmoid(zm_ref[0, rs, :]) * mm(om_ref[0, rs, :], wpm) for rs in parts]
    x1 = [x_ref[0, rs, :] + mm(m, wout) for rs, m in zip(parts, merged)]

    qc = [mm(_rms(v, gc_ref[...]), wcq).astype(BF16) for v in x1]

    def mem_attention(q):
        heads = []
        for h in range(MEM_HEADS):
            cs = slice(h * MEM_DH, (h + 1) * MEM_DH)
            s = lax.dot_general(q[:, cs], mk_ref[0, :, cs], (((1,), (1,)), ((), ())),
                                preferred_element_type=F32) * (MEM_DH ** -0.5)
            p = jnp.exp(s - jnp.max(s, axis=-1, keepdims=True))
            o = jnp.dot(p.astype(BF16), mv_ref[0, :, cs], preferred_element_type=F32)
            heads.append(o / jnp.sum(p, axis=-1, keepdims=True))
        return jnp.concatenate(heads, axis=-1)

    attn = [mem_attention(q) for q in qc]
    x2 = [v + mm(a, wco) for v, a in zip(x1, attn)]
    for rs, v in zip(parts, x2):
        x2_ref[0, rs, :] = v

    hm = [_rms(v, gm_ref[...]) for v in x2]
    logits = [mm(h, wr) + br[...] for h in hm]
    for rs, h in zip(parts, hm):
        words = _pack_bf16_pairs(h)
        for c, hp_ref in enumerate(hp_refs):
            hp_ref[0, rs, :] = words[:, c * SC_GATHER_WORDS:(c + 1) * SC_GATHER_WORDS]
    lane = lax.broadcasted_iota(jnp.int32, (sub, LANES), 1)
    is_grp = lane < N_GROUPS
    e_id = lane - ROUTER_EXPERT_LANE0

    def route(lg):
        gl = jnp.where(is_grp, lg, NEG_INF)
        ge = jnp.exp(gl - jnp.max(gl, axis=-1, keepdims=True))
        g_prob = ge / jnp.sum(ge, axis=-1, keepdims=True)
        p_grp = jnp.max(g_prob, axis=-1, keepdims=True)
        grp = jnp.min(jnp.where((g_prob == p_grp) & is_grp, lane, LANES), axis=-1, keepdims=True)
        in_grp = (e_id >= grp * EXPERTS_PER_GROUP) & (e_id < (grp + 1) * EXPERTS_PER_GROUP)
        el = jnp.where(in_grp, lg, NEG_INF)
        ee = jnp.exp(el - jnp.max(el, axis=-1, keepdims=True))
        e_prob = jnp.where(in_grp, ee / jnp.sum(ee, axis=-1, keepdims=True), -1.0)
        p0 = jnp.max(e_prob, axis=-1, keepdims=True)
        e0 = jnp.min(jnp.where(e_prob == p0, e_id, LANES), axis=-1, keepdims=True)
        e_rest = jnp.where(e_id == e0, -1.0, e_prob)
        p1 = jnp.max(e_rest, axis=-1, keepdims=True)
        e1 = jnp.min(jnp.where(e_rest == p1, e_id, LANES), axis=-1, keepdims=True)
        return e0, e1, p_grp * p0 / (p0 + p1), p_grp * p1 / (p0 + p1)

    routed = [route(lg) for lg in logits]

    before = (lax.broadcasted_iota(jnp.int32, (sub, sub), 1)
              < lax.broadcasted_iota(jnp.int32, (sub, sub), 0)).astype(BF16)
    base = base_ref[...]
    for rs, (e0, e1, w0, w1) in zip(parts, routed):
        onehot = ((lane == e0) | (lane == e1)).astype(F32)
        seen = base + jnp.dot(before, onehot.astype(BF16), preferred_element_type=F32)
        r0 = jnp.sum(jnp.where(lane == e0, seen, 0.0), axis=-1, keepdims=True)
        r1 = jnp.sum(jnp.where(lane == e1, seen, 0.0), axis=-1, keepdims=True)
        base = base + jnp.sum(onehot, axis=0, keepdims=True)
        info = jnp.zeros((sub, LANES), F32)
        for ln, val in ((INFO_W0, w0), (INFO_W1, w1), (INFO_E0, e0.astype(F32)), (INFO_E1, e1.astype(F32)),
                        (INFO_R0, r0), (INFO_R1, r1)):
            info = jnp.where(lane == ln, val, info)
        info_ref[0, rs, :] = info
    base_ref[...] = base
    cnt_ref[...] = base


def _mix(x, o_g, o_m, z_g, z_m, mem_k, mem_v, g_cross, g_moe, w_proj_gla, w_proj_moba, w_out, w_cq, w_co,
         w_rg, b_rg, w_re, b_re):
    B, S, D = x.shape
    M = mem_k.shape[1]
    rows = min(MIX_ROWS, S)
    assert S % rows == 0
    pad = LANES - N_GROUPS - N_EXPERTS
    wr = jnp.pad(jnp.concatenate([w_rg, w_re], axis=1), ((0, 0), (0, pad))).astype(BF16)
    br = jnp.pad(jnp.concatenate([b_rg, b_re]), (0, pad)).reshape(1, LANES).astype(F32)
    weights = [w_proj_gla.astype(BF16), w_proj_moba.astype(BF16), w_out.astype(BF16), w_cq.astype(BF16),
               w_co.astype(BF16), wr, br]
    tile = lambda n: pl.BlockSpec((1, rows, n), lambda b, i: (b, i, 0))
    memspec = pl.BlockSpec((1, M, MEM_W), lambda b, i: (b, 0, 0))
    n_pieces = D // 2 // SC_GATHER_WORDS
    return pl.pallas_call(
        functools.partial(_mix_kernel, n_pieces),
        grid=(B, S // rows),
        in_specs=[tile(D), tile(GLA_V), tile(MOBA_W), tile(D), tile(D), memspec, memspec,
                  _full((1, D)), _full((1, D))] + [_full(w.shape) for w in weights],
        out_specs=[tile(D), tile(LANES), _full((1, LANES))] + [tile(SC_GATHER_WORDS)] * n_pieces,
        out_shape=[jax.ShapeDtypeStruct((B, S, D), F32), jax.ShapeDtypeStruct((B, S, LANES), F32),
                   jax.ShapeDtypeStruct((1, LANES), F32)]
        + [jax.ShapeDtypeStruct((B, S, SC_GATHER_WORDS), jnp.uint32)] * n_pieces,
        scratch_shapes=[pltpu.VMEM((1, LANES), F32)],
        compiler_params=_params("arbitrary", "arbitrary"),
        name="merge_memattn_router",
    )(x, o_g, o_m, z_g.reshape(B, S, D), z_m.reshape(B, S, D), mem_k, mem_v,
      g_cross.reshape(1, D).astype(F32), g_moe.reshape(1, D).astype(F32), *weights)


def _plan_kernel(cnt_ref, info_ref, dest_ref, blk_ref):
    rows = info_ref.shape[0]
    lane1 = lax.broadcasted_iota(jnp.int32, (1, LANES), 1)
    nblk = jnp.floor((cnt_ref[...] + (EXPERT_ROWS - 1)) * (1.0 / EXPERT_ROWS))
    nblk = jnp.where(lane1 < N_EXPERTS, nblk, 0.0)
    hi = jnp.floor(nblk * (1.0 / 256.0))
    lo = nblk - 256.0 * hi
    upto = (lax.broadcasted_iota(jnp.int32, (LANES, LANES), 0)
            <= lax.broadcasted_iota(jnp.int32, (LANES, LANES), 1)).astype(BF16)
    digits = jnp.concatenate([jnp.broadcast_to(hi, (8, LANES)), jnp.broadcast_to(lo, (8, LANES))], axis=0)
    sums = jnp.dot(digits.astype(BF16), upto, preferred_element_type=F32)
    pend = sums[0:1] * 256.0 + sums[8:9]
    pstart_rows = (pend - nblk) * EXPERT_ROWS

    info = info_ref[...]
    lane = lax.broadcasted_iota(jnp.int32, (rows, LANES), 1)

    def field(ln):
        return jnp.sum(jnp.where(lane == ln, info, 0.0), axis=-1, keepdims=True)

    def dest(e, r):
        return jnp.sum(jnp.where(lane == e.astype(jnp.int32), pstart_rows, 0.0), axis=-1, keepdims=True) + r

    d0 = dest(field(INFO_E0), field(INFO_R0))
    d1 = dest(field(INFO_E1), field(INFO_R1))
    cols = jnp.where(lane == 0, d0, jnp.where(lane == 1, d1, 0.0))
    dest_ref[...] = cols.T[0:SUBLANES, :].astype(jnp.int32)

    @pl.when(pl.program_id(0) == 0)
    def _():
        n = lax.broadcasted_iota(jnp.int32, (blk_ref.shape[0], LANES), 0).astype(F32)
        blane = lax.broadcasted_iota(jnp.int32, (blk_ref.shape[0], LANES), 1)
        done = jnp.where((pend <= n) & (lane1 < N_EXPERTS), 1.0, 0.0)
        e = jnp.minimum(jnp.sum(done, axis=-1, keepdims=True), N_EXPERTS - 1.0)
        mine = blane == e.astype(jnp.int32)
        first_blk = jnp.sum(jnp.where(mine, pend - nblk, 0.0), axis=-1, keepdims=True)
        count = jnp.sum(jnp.where(mine, cnt_ref[...], 0.0), axis=-1, keepdims=True)
        valid = jnp.clip(count - EXPERT_ROWS * (n[:, 0:1] - first_blk), 0.0, float(EXPERT_ROWS))
        blk_ref[...] = jnp.where(blane == BLK_EXPERT, e, jnp.where(blane == BLK_VALID, valid, 0.0)).astype(jnp.int32)


def _plan(counts, info2d, n_blk):
    T = info2d.shape[0]
    rows = min(PLAN_ROWS, T)
    assert T % rows == 0
    n_blk_pad = -(-n_blk // SUBLANES) * SUBLANES
    return pl.pallas_call(
        _plan_kernel,
        grid=(T // rows,),
        in_specs=[_full((1, LANES)), pl.BlockSpec((rows, LANES), lambda i: (i, 0))],
        out_specs=[pl.BlockSpec((SUBLANES, rows), lambda i: (0, i)), _full((n_blk_pad, LANES))],
        out_shape=[jax.ShapeDtypeStruct((SUBLANES, T), jnp.int32),
                   jax.ShapeDtypeStruct((n_blk_pad, LANES), jnp.int32)],
        compiler_params=_params("arbitrary"),
        name="dispatch_plan",
    )(counts, info2d)


def _sc_windows(n_rows):
    n_inner = 32
    assert n_rows % (SC_GATHER_ROWS * n_inner) == 0
    return n_rows // (SC_GATHER_ROWS * n_inner), n_inner


def _sc_mesh():
    return plsc.VectorSubcoreMesh(core_axis_name="c", subcore_axis_name="s")


def _sc_scatter_rows(src, idx_a, idx_b, n_out):
    T, W = src.shape
    n_outer, n_inner = _sc_windows(T)
    win = lambda i, j: i * n_inner + j

    @pl.kernel(out_type=jax.ShapeDtypeStruct((n_out, W), src.dtype), mesh=_sc_mesh(), scratch_types=[])
    def scatter_kernel(s_hbm, a_hbm, b_hbm, o_hbm):
        def body(s_vmem, a_vmem, b_vmem):
            pltpu.sync_copy(s_vmem, o_hbm.at[a_vmem.at[0]])
            pltpu.sync_copy(s_vmem, o_hbm.at[b_vmem.at[0]])

        pltpu.emit_pipeline(
            body,
            grid=(n_outer, n_inner),
            in_specs=[pl.BlockSpec((SC_GATHER_ROWS, W), index_map=lambda i, j: (win(i, j), 0)),
                      pl.BlockSpec((1, SC_GATHER_ROWS), index_map=lambda i, j: (0, win(i, j))),
                      pl.BlockSpec((1, SC_GATHER_ROWS), index_map=lambda i, j: (0, win(i, j)))],
            out_specs=[],
            core_axis_name=("c", "s"),
            dimension_semantics=(pltpu.PARALLEL, pltpu.PARALLEL),
        )(s_hbm, a_hbm, b_hbm)

    return scatter_kernel(src, idx_a, idx_b)


def _sc_gather_rows(table, idx):
    M = idx.shape[1]
    W = table.shape[1]
    n_outer, n_inner = _sc_windows(M)
    win = lambda i, j: i * n_inner + j

    @pl.kernel(out_type=jax.ShapeDtypeStruct((M, W), table.dtype), mesh=_sc_mesh(), scratch_types=[])
    def gather_kernel(t_hbm, i_hbm, o_hbm):
        def body(i_vmem, o_vmem):
            pltpu.sync_copy(t_hbm.at[i_vmem.at[0]], o_vmem)

        pltpu.emit_pipeline(
            body,
            grid=(n_outer, n_inner),
            in_specs=[pl.BlockSpec((1, SC_GATHER_ROWS), index_map=lambda i, j: (0, win(i, j)))],
            out_specs=[pl.BlockSpec((SC_GATHER_ROWS, W), index_map=lambda i, j: (win(i, j), 0))],
            core_axis_name=("c", "s"),
            dimension_semantics=(pltpu.PARALLEL, pltpu.PARALLEL),
        )(i_hbm, o_hbm)

    return gather_kernel(table, idx)


def _expert_kernel(n_pieces, blk_e_ref, blk_valid_ref, *refs):
    xs_refs, (wg_ref, wu_ref, wd_ref) = refs[:n_pieces], refs[n_pieces:n_pieces + 3]
    y_refs, (wg_bf, wu_bf, wd_bf) = refs[n_pieces + 3:2 * n_pieces + 3], refs[2 * n_pieces + 3:]
    n = pl.program_id(0)
    prev = blk_e_ref[jnp.maximum(n - 1, 0)]

    @pl.when((n == 0) | (blk_e_ref[n] != prev))
    def _():
        wg_bf[...] = wg_ref[0].astype(BF16)
        wu_bf[...] = wu_ref[0].astype(BF16)
        wd_bf[...] = wd_ref[0].astype(BF16)

    valid = blk_valid_ref[n]

    @pl.when(valid == 0)
    def _():
        for y_ref in y_refs:
            y_ref[...] = jnp.zeros_like(y_ref)

    @pl.when(valid > 0)
    def _():
        sub = xs_refs[0].shape[0] // EXPERT_PARTS
        parts = [slice(p * sub, (p + 1) * sub) for p in range(EXPERT_PARTS)]
        row = lax.broadcasted_iota(jnp.int32, (sub, 1), 0)

        def load(p, rs):
            words = jnp.concatenate([r[rs, :] for r in xs_refs], axis=1)
            words = jnp.where(row + p * sub < valid, words, jnp.zeros_like(words))
            return _unpack_bf16_pairs(words).astype(BF16)

        xb = [load(p, rs) for p, rs in enumerate(parts)]
        gate = [jnp.dot(v, wg_bf[...], preferred_element_type=F32) for v in xb]
        up = [jnp.dot(v, wu_bf[...], preferred_element_type=F32) for v in xb]
        hid = [(g * jax.nn.sigmoid(g) * u).astype(BF16) for g, u in zip(gate, up)]
        for rs, hv in zip(parts, hid):
            words = _pack_bf16_pairs(jnp.dot(hv, wd_bf[...], preferred_element_type=F32))
            for c, y_ref in enumerate(y_refs):
                y_ref[rs, :] = words[:, c * SC_GATHER_WORDS:(c + 1) * SC_GATHER_WORDS]


def _experts(blk_e, blk_valid, xs_pieces, w_gate, w_up, w_down):
    cap = xs_pieces[0].shape[0]
    _, D, DE = w_gate.shape
    n_blk = cap // EXPERT_ROWS
    n_pieces = len(xs_pieces)
    piece = pl.BlockSpec((EXPERT_ROWS, SC_GATHER_WORDS), lambda n, e, v: (n, 0))
    return pl.pallas_call(
        functools.partial(_expert_kernel, n_pieces),
        grid_spec=pltpu.PrefetchScalarGridSpec(
            num_scalar_prefetch=2,
            grid=(n_blk,),
            in_specs=[piece] * n_pieces
            + [pl.BlockSpec((1, D, DE), lambda n, e, v: (e[n], 0, 0)),
               pl.BlockSpec((1, D, DE), lambda n, e, v: (e[n], 0, 0)),
               pl.BlockSpec((1, DE, D), lambda n, e, v: (e[n], 0, 0))],
            out_specs=[piece] * n_pieces,
            scratch_shapes=[pltpu.VMEM((D, DE), BF16), pltpu.VMEM((D, DE), BF16), pltpu.VMEM((DE, D), BF16)]),
        out_shape=[jax.ShapeDtypeStruct((cap, SC_GATHER_WORDS), jnp.uint32)] * n_pieces,
        compiler_params=_params("arbitrary"),
        name="moe_experts",
    )(blk_e, blk_valid, *xs_pieces, w_gate, w_up, w_down)


def _combine_dense_kernel(final_norm, n_pieces, *refs):
    yg_refs, (x_ref, info_ref, g_ref, o_ref) = refs[:n_pieces], refs[n_pieces:]
    info = info_ref[...]
    w0 = info[:, INFO_W0:INFO_W0 + 1]
    w1 = info[:, INFO_W1:INFO_W1 + 1]
    y = [_unpack_bf16_pairs(jnp.concatenate([r[slot] for r in yg_refs], axis=1)) for slot in range(TOPK_IN_GROUP)]
    out = x_ref[...] + (w0 * y[0] + w1 * y[1])
    o_ref[...] = _rms(out, g_ref[...]) if final_norm else out


def _combine_dense(yg_pieces, x2d, info2d, g_final, final_norm):
    T, D = x2d.shape
    rows = min(MOVE_ROWS, T)
    n_pieces = len(yg_pieces)
    return pl.pallas_call(
        functools.partial(_combine_dense_kernel, final_norm, n_pieces),
        grid=(T // rows,),
        in_specs=[pl.BlockSpec((TOPK_IN_GROUP, rows, SC_GATHER_WORDS), lambda i: (0, i, 0))] * n_pieces
        + [pl.BlockSpec((rows, D), lambda i: (i, 0)), pl.BlockSpec((rows, LANES), lambda i: (i, 0)),
           _full((1, D))],
        out_specs=pl.BlockSpec((rows, D), lambda i: (i, 0)),
        out_shape=jax.ShapeDtypeStruct((T, D), F32),
        compiler_params=_params("parallel"),
        name="moe_combine_dense_final_norm",
    )(*yg_pieces, x2d, info2d, g_final.reshape(1, D).astype(F32))


def kernel(x, mem, g_mem, rel_bias, g_mix, w_in, w_alpha_up, b_alpha, g_gla_head, w_proj_gla, w_proj_moba,
           w_out, g_cross, w_cq, w_ckv, w_co, g_moe, w_router_group, b_router_group, w_router_expert,
           b_router_expert, w_exp_gate, w_exp_up, w_exp_down, g_final):
    B, S, D = x.shape
    T = B * S
    depth = g_mix.shape[0]
    n_assign = T * TOPK_IN_GROUP
    n_blk = -(-(n_assign + N_EXPERTS * (EXPERT_ROWS - 1)) // EXPERT_ROWS)
    cap = n_blk * EXPERT_ROWS

    mem_bias = _moba_bias(rel_bias)
    for l in range(depth):
        qk, v_g, r_g, la, q_m, k_m, v_m, z_g, z_m = _project(x.reshape(T, D), g_mix[l], w_in[l], w_alpha_up[l],
                                                             b_alpha[l])
        o_g = _gla(qk, la, v_g, r_g, g_gla_head[l], B, S)
        o_m = _moba(q_m, k_m, v_m, mem_bias, rel_bias, B, S)
        mem_k, mem_v = _mem_kv(mem, g_mem, w_ckv[l])
        x2, info, counts, *hm_pieces = _mix(x, o_g, o_m, z_g, z_m, mem_k, mem_v, g_cross[l], g_moe[l],
                                            w_proj_gla[l], w_proj_moba[l], w_out[l], w_cq[l], w_co[l],
                                            w_router_group[l], b_router_group[l], w_router_expert[l],
                                            b_router_expert[l])
        x2d, info2d = x2.reshape(T, D), info.reshape(T, LANES)
        dest, blk = _plan(counts, info2d, n_blk)
        idx = dest[0:TOPK_IN_GROUP].reshape(1, TOPK_IN_GROUP * T)
        xs_pieces = [_sc_scatter_rows(h.reshape(T, SC_GATHER_WORDS), idx[:, :T], idx[:, T:], cap)
                     for h in hm_pieces]
        y_pieces = _experts(blk[:n_blk, BLK_EXPERT], blk[:n_blk, BLK_VALID], xs_pieces, w_exp_gate[l],
                            w_exp_up[l], w_exp_down[l])
        yg = [_sc_gather_rows(y, idx).reshape(TOPK_IN_GROUP, T, SC_GATHER_WORDS) for y in y_pieces]
        x = _combine_dense(yg, x2d, info2d, g_final, final_norm=(l == depth - 1)).reshape(B, S, D)
    return x
```

```python
import functools
import math

import jax
import jax.numpy as jnp
from jax import lax
from jax.experimental import pallas as pl
from jax.experimental.pallas import tpu as pltpu
from jax.experimental.pallas import tpu_sc as plsc

F32 = jnp.float32
BF16 = jnp.bfloat16
NEG_INF = float("-inf")

EPS = 1e-6
GLA_HEADS, GLA_DK, GLA_DV, GLA_LOWRANK, GLA_TAU, GLA_CHUNK = 4, 64, 128, 16, 16.0, 64
GLA_QK, GLA_V = GLA_HEADS * GLA_DK, GLA_HEADS * GLA_DV
MOBA_HEADS, MOBA_DH, MOBA_BLOCK, MOBA_TOPK = 8, 64, 256, 3
MOBA_W = MOBA_HEADS * MOBA_DH
LOG2E = math.log2(math.e)
MOBA_Q_SCALE = MOBA_DH ** -0.5 * LOG2E
REL_BUCKETS, REL_MAX_DIST = 32, 128
MEM_HEADS, MEM_DH = 4, 128
MEM_W = MEM_HEADS * MEM_DH
N_GROUPS, EXPERTS_PER_GROUP, TOPK_IN_GROUP = 4, 8, 2
N_EXPERTS = N_GROUPS * EXPERTS_PER_GROUP

LANES = 128
SUBLANES = 8
VMEM_LIMIT_BYTES = 56 * 1024 * 1024

PROJ_ROWS = 512
GLA_ROWS = 512
MOBA_GROUP = 4
MIX_ROWS = 512
MIX_PARTS = 2
EXPERT_ROWS = 256
MOVE_ROWS = 512
PLAN_ROWS = 2048
SC_GATHER_ROWS = 128
SC_GATHER_WORDS = 256
EXPERT_PARTS = 2


def _params(*semantics):
    return pltpu.CompilerParams(dimension_semantics=semantics, vmem_limit_bytes=VMEM_LIMIT_BYTES)


def _full(shape):
    return pl.BlockSpec(shape, lambda *_: (0,) * len(shape))


def _rms(x, g):
    return x * lax.rsqrt(jnp.mean(x * x, axis=-1, keepdims=True) + EPS) * g


def _pack_bf16_pairs(x):
    n = x.shape[1] // 2
    bits = pltpu.bitcast(x.astype(BF16).astype(F32), jnp.uint32)
    return bits[:, n:] | (bits[:, :n] >> 16)


def _unpack_bf16_pairs(w):
    lo = pltpu.bitcast(w << 16, F32)
    hi = pltpu.bitcast(w & jnp.uint32(0xFFFF0000), F32)
    return jnp.concatenate([lo, hi], axis=1)


def _proj_kernel(x_ref, g_ref, w_qk, w_v, w_r, w_a, w_up, b_a, w_qm, w_km, w_vm, w_zg, w_zm,
                 o_qk, o_v, o_r, o_la, o_qm, o_km, o_vm, o_zg, o_zm):
    h = _rms(x_ref[...], g_ref[...]).astype(BF16)

    def mm(w_ref):
        return jnp.dot(h, w_ref[...], preferred_element_type=F32)

    o_qk[...] = mm(w_qk)
    o_v[...] = mm(w_v).astype(BF16)
    o_r[...] = mm(w_r)
    a_lr = mm(w_a).astype(BF16)
    pre = jnp.dot(a_lr, w_up[...], preferred_element_type=F32) + b_a[...]
    o_la[...] = jax.nn.log_sigmoid(pre) * (1.0 / GLA_TAU)

    def mm_t(wt_ref):
        return lax.dot_general(wt_ref[...], h, (((1,), (1,)), ((), ())), preferred_element_type=F32)

    def store_blocks(o_ref, val_t):
        for c in range(o_ref.shape[0]):
            o_ref[c] = val_t[:, c * MOBA_BLOCK:(c + 1) * MOBA_BLOCK]

    store_blocks(o_qm, (mm_t(w_qm) * MOBA_Q_SCALE).astype(BF16))
    o_km[...] = mm(w_km).astype(BF16)
    store_blocks(o_vm, mm_t(w_vm).astype(BF16))
    o_zg[...] = mm(w_zg)
    o_zm[...] = mm(w_zm)


def _project(x2d, g_mix, w_in, w_alpha_up, b_alpha):
    T, D = x2d.shape
    rows = min(PROJ_ROWS, T)
    assert T % rows == 0
    splits = (GLA_QK, GLA_QK, GLA_V, GLA_V, GLA_LOWRANK, MOBA_W, MOBA_W, MOBA_W, D, D)
    offs = [0]
    for s in splits:
        offs.append(offs[-1] + s)
    wb = w_in.astype(BF16)
    sec = lambda i, j: wb[:, offs[i]:offs[j]]
    w_a = jnp.pad(sec(4, 5), ((0, 0), (0, LANES - GLA_LOWRANK)))
    w_up = jnp.pad(w_alpha_up.astype(BF16), ((0, LANES - GLA_LOWRANK), (0, 0)))
    weights = [sec(0, 2), sec(2, 3), sec(3, 4), w_a, w_up, b_alpha.reshape(1, GLA_QK).astype(F32),
               sec(5, 6).T, sec(6, 7), sec(7, 8).T, sec(8, 9), sec(9, 10)]
    out_defs = [(2 * GLA_QK, F32, False), (GLA_V, BF16, False), (GLA_V, F32, False), (GLA_QK, F32, False),
                (MOBA_W, BF16, True), (MOBA_W, BF16, False), (MOBA_W, BF16, True), (D, F32, False),
                (D, F32, False)]
    BS = MOBA_BLOCK
    assert rows % BS == 0
    row_spec = lambda n: pl.BlockSpec((rows, n), lambda i: (i, 0))
    blk_spec = lambda n: pl.BlockSpec((rows // BS, n, BS), lambda i: (i, 0, 0))
    return pl.pallas_call(
        _proj_kernel,
        grid=(T // rows,),
        in_specs=[row_spec(D), _full((1, D))] + [_full(w.shape) for w in weights],
        out_specs=[blk_spec(n) if t else row_spec(n) for n, _, t in out_defs],
        out_shape=[jax.ShapeDtypeStruct((T // BS, n, BS) if t else (T, n), dt) for n, dt, t in out_defs],
        compiler_params=_params("parallel"),
        name="norm_in_proj",
    )(x2d, g_mix.reshape(1, D).astype(F32), *weights)


def _gla_kernel(qk_ref, la_ref, v_ref, r_ref, g_ref, o_ref, state_ref, obuf_ref):
    C, H, DK, DV = GLA_CHUNK, GLA_HEADS, GLA_DK, GLA_DV
    rows = qk_ref.shape[1]

    @pl.when(pl.program_id(1) == 0)
    def _():
        state_ref[...] = jnp.zeros_like(state_ref)

    tri = (lax.broadcasted_iota(jnp.int32, (C, C), 0) >= lax.broadcasted_iota(jnp.int32, (C, C), 1)).astype(BF16)
    lane_head = lax.broadcasted_iota(jnp.int32, (1, H * DK), 1) // DK
    head_masks = [(lane_head == h).astype(F32) for h in range(H)]
    stack_row = lax.broadcasted_iota(jnp.int32, (H * C, C), 0) % C
    stack_col = lax.broadcasted_iota(jnp.int32, (H * C, C), 1)
    causal = stack_col <= stack_row
    same_head = (lax.broadcasted_iota(jnp.int32, (H * DV, H * DK), 0) // DV
                 == lax.broadcasted_iota(jnp.int32, (H * DV, H * DK), 1) // DK)
    scale = DK ** -0.5

    def stack(m):
        return jnp.concatenate([m * head_masks[h] for h in range(H)], axis=0).astype(BF16)

    chunks = [slice(c * C, (c + 1) * C) for c in range(rows // C)]

    def cum_log_decay(sl):
        la = la_ref[0, sl, :]
        p1 = la.astype(BF16)
        r1 = la - p1.astype(F32)
        p2 = r1.astype(BF16)
        p3 = (r1 - p2.astype(F32)).astype(BF16)
        s3 = jnp.dot(tri, jnp.concatenate([p1, p2, p3], axis=1), preferred_element_type=F32)
        w = H * DK
        return (s3[:, 0:w] + s3[:, w:2 * w]) + s3[:, 2 * w:3 * w]

    b_all = [cum_log_decay(sl) for sl in chunks]

    qe_all, ke_all, kd_all, qb_all, decay_all = [], [], [], [], []
    for sl, b in zip(chunks, b_all):
        q = qk_ref[0, sl, 0:H * DK] * scale
        k = qk_ref[0, sl, H * DK:2 * H * DK]
        b_last = b[C - 1:C, :]
        b_mid = b[C // 2 - 1:C // 2, :]
        qe_all.append(stack(q * jnp.exp(b - b_mid)))
        ke_all.append((k * jnp.exp(b_mid - b)).astype(BF16))
        kd_all.append((k * jnp.exp(b_last - b)).astype(BF16))
        qb_all.append((q * jnp.exp(b)).astype(BF16))
        decay_all.append(jnp.exp(b_last))

    att_all = [jnp.where(causal, lax.dot_general(qe, ke, (((1,), (1,)), ((), ())), preferred_element_type=F32),
                         0.0).astype(BF16) for qe, ke in zip(qe_all, ke_all)]

    o_intra_all, kv_all = [], []
    for sl, att, kd in zip(chunks, att_all, kd_all):
        v = v_ref[0, sl, :]
        o_intra_all.append(jnp.concatenate(
            [jnp.dot(att[h * C:(h + 1) * C, :], v[:, h * DV:(h + 1) * DV], preferred_element_type=F32)
             for h in range(H)], axis=1))
        kv_t = lax.dot_general(v, kd, (((0,), (0,)), ((), ())), preferred_element_type=F32)
        kv_all.append(jnp.where(same_head, kv_t, 0.0))

    state_t = state_ref[...]
    for sl, qb, decay, kv_t, o_intra in zip(chunks, qb_all, decay_all, kv_all, o_intra_all):
        o_inter = lax.dot_general(qb, state_t.astype(BF16), (((1,), (1,)), ((), ())),
                                  preferred_element_type=F32)
        obuf_ref[sl, :] = o_intra + o_inter
        state_t = decay * state_t + kv_t
    state_ref[...] = state_t

    r = r_ref[0]
    for h in range(H):
        cs = slice(h * DV, (h + 1) * DV)
        y = _rms(obuf_ref[:, cs], g_ref[:, cs])
        rh = r[:, cs]
        o_ref[0, :, cs] = (y * (rh * jax.nn.sigmoid(rh))).astype(BF16)


def _gla(qk, la, v, r, g_head, B, S):
    rows = min(GLA_ROWS, S)
    assert S % rows == 0 and rows % GLA_CHUNK == 0
    spec = lambda n: pl.BlockSpec((1, rows, n), lambda b, i: (b, i, 0))
    return pl.pallas_call(
        _gla_kernel,
        grid=(B, S // rows),
        in_specs=[spec(2 * GLA_QK), spec(GLA_QK), spec(GLA_V), spec(GLA_V), _full((1, GLA_V))],
        out_specs=spec(GLA_V),
        out_shape=jax.ShapeDtypeStruct((B, S, GLA_V), BF16),
        scratch_shapes=[pltpu.VMEM((GLA_V, GLA_QK), F32), pltpu.VMEM((rows, GLA_V), F32)],
        compiler_params=_params("parallel", "arbitrary"),
        name="gla_chunked",
    )(qk.reshape(B, S, -1), la.reshape(B, S, -1), v.reshape(B, S, -1), r.reshape(B, S, -1),
      g_head.reshape(1, GLA_V).astype(F32))


def _t5_bucket(dist):
    n = jnp.maximum(dist, 0)
    max_exact = REL_BUCKETS // 2
    nf = jnp.maximum(n, 1).astype(F32)
    large = max_exact + (jnp.log(nf / max_exact) / math.log(REL_MAX_DIST / max_exact)
                         * (REL_BUCKETS - max_exact)).astype(jnp.int32)
    large = jnp.minimum(large, REL_BUCKETS - 1)
    return jnp.where(n < max_exact, n, large)


def _moba_bias_kernel(rb_ref, o_ref):
    BS, G = MOBA_BLOCK, MOBA_GROUP
    grp, kind = pl.program_id(0), pl.program_id(1)
    d = (lax.broadcasted_iota(jnp.int32, (BS, BS), 1) - lax.broadcasted_iota(jnp.int32, (BS, BS), 0)
         + kind * BS)
    bucket = _t5_bucket(d)
    for h in range(G):
        val = jnp.zeros((BS, BS), F32)
        for bkt in range(REL_BUCKETS):
            val = jnp.where(bucket == bkt, rb_ref[bkt, grp * G + h] * LOG2E, val)
        o_ref[0, 0, :, h * BS:(h + 1) * BS] = jnp.where(d >= 0, val, NEG_INF)


def _moba_bias(rel_bias):
    BS, G = MOBA_BLOCK, MOBA_GROUP
    n_grp = MOBA_HEADS // G
    return pl.pallas_call(
        _moba_bias_kernel,
        grid=(n_grp, 2),
        in_specs=[pl.BlockSpec(memory_space=pltpu.SMEM)],
        out_specs=pl.BlockSpec((1, 1, BS, G * BS), lambda g, k: (g, k, 0, 0)),
        out_shape=jax.ShapeDtypeStruct((n_grp, 2, BS, G * BS), F32),
        compiler_params=_params("parallel", "parallel"),
        name="moba_bias_tables",
    )(rel_bias.astype(F32))


def _moba_kernel(rb_ref, q_ref, k_ref, v_ref, bias_ref, o_ref,
                 qs_ref, kmean_ref, sel_ref, m_ref, l_ref, acc_ref, sbuf_ref):
    BS, G, DH = MOBA_BLOCK, MOBA_GROUP, MOBA_DH
    NBP = kmean_ref.shape[0]
    grp, i = pl.program_id(1), pl.program_id(2)

    @pl.when(i == 0)
    def _():
        S = k_ref.shape[1]
        blk_of_key = lax.broadcasted_iota(jnp.int32, (NBP, S), 1) // BS
        ind = (blk_of_key == lax.broadcasted_iota(jnp.int32, (NBP, S), 0)).astype(BF16)
        kmean_ref[...] = jnp.dot(ind, k_ref[0], preferred_element_type=F32) * (1.0 / BS)

    qt = q_ref[0]
    sub_head = lax.broadcasted_iota(jnp.int32, (G * DH, 1), 0) // DH
    for h in range(G):
        qs_ref[:, h * BS:(h + 1) * BS] = jnp.where(sub_head == h, qt, jnp.zeros_like(qt))

    gate = jnp.dot(kmean_ref[...].astype(BF16), qs_ref[...], preferred_element_type=F32)
    blk = lax.broadcasted_iota(jnp.int32, gate.shape, 0)
    gate = jnp.where(blk < i, gate, NEG_INF)
    for t in range(MOBA_TOPK):
        mx = jnp.max(gate, axis=0, keepdims=True)
        hit = (gate == mx) & (mx > NEG_INF)
        idx = jnp.min(jnp.where(hit, blk, NBP), axis=0, keepdims=True)
        sel_ref[t:t + 1, :] = idx
        gate = jnp.where(blk == idx, NEG_INF, gate)

    def mask_row(j):
        hit = (sel_ref[0:1, :] == j) | (sel_ref[1:2, :] == j) | (sel_ref[2:3, :] == j)
        return jnp.where(hit, 0.0, NEG_INF)

    def with_ones(vt):
        return jnp.concatenate([vt, jnp.ones((8, vt.shape[1]), BF16)], axis=0)

    def far_scores(j0, n, slot):
        kj = k_ref[0, pl.ds(pl.multiple_of(j0 * BS, BS), n * BS), :]
        for h in range(G):
            cs = slice(h * BS, (h + 1) * BS)
            sbuf_ref[slot, 0:n * BS, cs] = jnp.dot(kj, qs_ref[:, cs], preferred_element_type=F32)

    j_prev = jnp.maximum(i - 1, 0)
    k_own = k_ref[0, pl.ds(pl.multiple_of(i * BS, BS), BS), :]
    k_prev = k_ref[0, pl.ds(pl.multiple_of(j_prev * BS, BS), BS), :]
    vt_near = jnp.concatenate([v_ref[i], v_ref[j_prev]], axis=1)
    prev_mask = mask_row(i - 1)
    for h in range(G):
        cs = slice(h * BS, (h + 1) * BS)
        sbuf_ref[1, 0:BS, cs] = jnp.dot(k_own, qs_ref[:, cs], preferred_element_type=F32)
        sbuf_ref[1, BS:2 * BS, cs] = jnp.dot(k_prev, qs_ref[:, cs], preferred_element_type=F32)
    far_scores(0, 2, 0)
    m_out, l_out, acc_out = [], [], []
    for h in range(G):
        cs = slice(h * BS, (h + 1) * BS)
        s_own = sbuf_ref[1, 0:BS, cs] + bias_ref[0, 0, :, cs]
        s_prev = sbuf_ref[1, BS:2 * BS, cs] + (bias_ref[0, 1, :, cs] + prev_mask[:, cs])
        m0 = jnp.maximum(jnp.max(s_own, axis=0, keepdims=True), jnp.max(s_prev, axis=0, keepdims=True))
        pb = jnp.concatenate([jnp.exp2(s_own - m0).astype(BF16), jnp.exp2(s_prev - m0).astype(BF16)], axis=0)
        pv = jnp.dot(with_ones(vt_near[h * DH:(h + 1) * DH, :]), pb, preferred_element_type=F32)
        m_out.append(m0)
        l_out.append(pv[DH:DH + 1, :])
        acc_out.append(pv[0:DH, :])
    m_ref[...] = jnp.concatenate(m_out, axis=1)
    l_ref[...] = jnp.concatenate(l_out, axis=1)
    acc_ref[...] = jnp.concatenate(acc_out, axis=1)

    lane_head = lax.broadcasted_iota(jnp.int32, (1, G * BS), 1) // BS
    far_bias = jnp.zeros((1, G * BS), F32)
    for h in range(G):
        far_bias = jnp.where(lane_head == h, rb_ref[REL_BUCKETS - 1, grp * G + h] * LOG2E, far_bias)

    def visit_far(j0, n, slot, ahead=None):
        vjt = jnp.concatenate([v_ref[j0 + t] for t in range(n)], axis=1)
        addend = [far_bias + mask_row(j0 + t) for t in range(n)]
        m_old, l_old, acc_old = m_ref[...], l_ref[...], acc_ref[...]
        m_out, l_out, acc_out = [], [], []
        if ahead is not None:
            k_next = k_ref[0, pl.ds(pl.multiple_of(ahead[0] * BS, BS), 2 * BS), :]
        for h in range(G):
            cs = slice(h * BS, (h + 1) * BS)
            if ahead is not None:
                sbuf_ref[ahead[1], :, cs] = jnp.dot(k_next, qs_ref[:, cs], preferred_element_type=F32)
            s = sbuf_ref[slot, 0:n * BS, cs]
            mx = jnp.max(s[0:BS], axis=0, keepdims=True) + addend[0][:, cs]
            for t in range(1, n):
                mx = jnp.maximum(mx, jnp.max(s[t * BS:(t + 1) * BS], axis=0, keepdims=True) + addend[t][:, cs])
            m_new = jnp.maximum(m_old[:, cs], mx)
            pb = jnp.concatenate([jnp.exp2(s[t * BS:(t + 1) * BS] - (m_new - addend[t][:, cs])).astype(BF16)
                                  for t in range(n)], axis=0)
            pv = jnp.dot(with_ones(vjt[h * DH:(h + 1) * DH, :]), pb, preferred_element_type=F32)
            alpha = jnp.exp2(m_old[:, cs] - m_new)
            m_out.append(m_new)
            l_out.append(alpha * l_old[:, cs] + pv[DH:DH + 1, :])
            acc_out.append(alpha * acc_old[:, cs] + pv[0:DH, :])
        m_ref[...] = jnp.concatenate(m_out, axis=1)
        l_ref[...] = jnp.concatenate(l_out, axis=1)
        acc_ref[...] = jnp.concatenate(acc_out, axis=1)

    n_far = jnp.maximum(i - 1, 0)
    n_pairs = lax.shift_right_logical(n_far, 1)

    def pair(p, slot, look_ahead=True):
        ahead = (2 * jnp.minimum(p + 1, n_pairs - 1), 1 - slot) if look_ahead else None
        visit_far(2 * p, 2, slot, ahead=ahead)

    def far_octet(w, carry):
        for t in range(4):
            pair(4 * w + t, t % 2)
        return carry

    lax.fori_loop(0, lax.shift_right_logical(n_pairs, 2), far_octet, 0)

    @pl.when((n_pairs & 2) != 0)
    def _():
        base = 4 * lax.shift_right_logical(n_pairs, 2)
        pair(base, 0)
        pair(base + 1, 1)

    @pl.when((n_pairs & 1) != 0)
    def _():
        pair(n_pairs - 1, 0, look_ahead=False)

    @pl.when(n_far % 2 == 1)
    def _():
        far_scores(n_far - 1, 1, 1)
        visit_far(n_far - 1, 1, 1)

    out_t = jnp.concatenate([acc_ref[:, h * BS:(h + 1) * BS] / l_ref[:, h * BS:(h + 1) * BS] for h in range(G)],
                            axis=0)
    o_ref[0] = out_t.T.astype(o_ref.dtype)


def _moba(qm_t, km, vm_t, bias, rel_bias, B, S):
    BS, G, DH = MOBA_BLOCK, MOBA_GROUP, MOBA_DH
    W = G * DH
    n_grp = MOBA_HEADS // G
    assert S % BS == 0 and S >= 2 * BS
    NB = S // BS
    NBP = -(-NB // 8) * 8
    return pl.pallas_call(
        _moba_kernel,
        grid=(B, n_grp, NB),
        in_specs=[pl.BlockSpec(memory_space=pltpu.SMEM),
                  pl.BlockSpec((1, W, BS), lambda b, g, i: (b * NB + i, g, 0)),
                  pl.BlockSpec((1, S, W), lambda b, g, i: (b, 0, g)),
                  pl.BlockSpec((NB, W, BS), lambda b, g, i: (b, g, 0)),
                  pl.BlockSpec((1, 2, BS, G * BS), lambda b, g, i: (g, 0, 0, 0))],
        out_specs=pl.BlockSpec((1, BS, W), lambda b, g, i: (b, i, g)),
        out_shape=jax.ShapeDtypeStruct((B, S, MOBA_W), BF16),
        scratch_shapes=[pltpu.VMEM((W, G * BS), BF16), pltpu.VMEM((NBP, W), F32),
                        pltpu.VMEM((8, G * BS), jnp.int32),
                        pltpu.VMEM((1, G * BS), F32), pltpu.VMEM((1, G * BS), F32),
                        pltpu.VMEM((DH, G * BS), F32), pltpu.VMEM((2, 2 * BS, G * BS), F32)],
        compiler_params=_params("parallel", "parallel", "arbitrary"),
        name="moba_attention",
    )(rel_bias.astype(F32), qm_t, km.reshape(B, S, -1), vm_t, bias)


def _mem_kv_kernel(mem_ref, g_ref, w_ref, k_ref, v_ref):
    kv = jnp.dot(_rms(mem_ref[0], g_ref[...]).astype(BF16), w_ref[...], preferred_element_type=F32)
    k_ref[0] = kv[:, :MEM_W].astype(BF16)
    v_ref[0] = kv[:, MEM_W:].astype(BF16)


def _mem_kv(mem, g_mem, w_ckv):
    B, M, D = mem.shape
    spec = pl.BlockSpec((1, M, MEM_W), lambda b: (b, 0, 0))
    return pl.pallas_call(
        _mem_kv_kernel,
        grid=(B,),
        in_specs=[pl.BlockSpec((1, M, D), lambda b: (b, 0, 0)), _full((1, D)), _full((D, 2 * MEM_W))],
        out_specs=[spec, spec],
        out_shape=[jax.ShapeDtypeStruct((B, M, MEM_W), BF16)] * 2,
        compiler_params=_params("parallel"),
        name="memory_kv",
    )(mem, g_mem.reshape(1, D).astype(F32), w_ckv.astype(BF16))


INFO_W0, INFO_W1, INFO_E0, INFO_E1, INFO_R0, INFO_R1 = range(6)
BLK_EXPERT, BLK_VALID = range(2)
ROUTER_GROUP_LANE0, ROUTER_EXPERT_LANE0 = 0, N_GROUPS


def _mix_kernel(n_pieces, x_ref, og_ref, om_ref, zg_ref, zm_ref, mk_ref, mv_ref, gc_ref, gm_ref,
                wpg, wpm, wout, wcq, wco, wr, br,
                x2_ref, info_ref, cnt_ref, *rest):
    hp_refs, base_ref = rest[:n_pieces], rest[n_pieces]
    first = (pl.program_id(0) == 0) & (pl.program_id(1) == 0)

    @pl.when(first)
    def _():
        base_ref[...] = jnp.zeros_like(base_ref)

    def mm(a, w_ref):
        return jnp.dot(a.astype(BF16), w_ref[...], preferred_element_type=F32)

    n_rows = x_ref.shape[1]
    sub = n_rows // MIX_PARTS
    parts = [slice(p * sub, (p + 1) * sub) for p in range(MIX_PARTS)]

    merged = [jax.nn.sigmoid(zg_ref[0, rs, :]) * mm(og_ref[0, rs, :], wpg)
              + jax.nn.sigmoid(zm_ref[0, rs, :]) * mm(om_ref[0, rs, :], wpm) for rs in parts]
    x1 = [x_ref[0, rs, :] + mm(m, wout) for rs, m in zip(parts, merged)]

    qc = [mm(_rms(v, gc_ref[...]), wcq).astype(BF16) for v in x1]

    def mem_attention(q):
        heads = []
        for h in range(MEM_HEADS):
            cs = slice(h * MEM_DH, (h + 1) * MEM_DH)
            s = lax.dot_general(q[:, cs], mk_ref[0, :, cs], (((1,), (1,)), ((), ())),
                                preferred_element_type=F32) * (MEM_DH ** -0.5)
            p = jnp.exp(s - jnp.max(s, axis=-1, keepdims=True))
            o = jnp.dot(p.astype(BF16), mv_ref[0, :, cs], preferred_element_type=F32)
            heads.append(o / jnp.sum(p, axis=-1, keepdims=True))
        return jnp.concatenate(heads, axis=-1)

    attn = [mem_attention(q) for q in qc]
    x2 = [v + mm(a, wco) for v, a in zip(x1, attn)]
    for rs, v in zip(parts, x2):
        x2_ref[0, rs, :] = v

    hm = [_rms(v, gm_ref[...]) for v in x2]
    logits = [lax.dot_general(wr[...], h.astype(BF16), (((1,), (1,)), ((), ())), preferred_element_type=F32)
              + br[...] for h in hm]
    for rs, h in zip(parts, hm):
        words = _pack_bf16_pairs(h)
        for c, hp_ref in enumerate(hp_refs):
            hp_ref[0, rs, :] = words[:, c * SC_GATHER_WORDS:(c + 1) * SC_GATHER_WORDS]
    n_log = -(-(N_GROUPS + N_EXPERTS) // SUBLANES) * SUBLANES
    row_id = lax.broadcasted_iota(jnp.int32, (n_log, sub), 0)
    is_grp = row_id < N_GROUPS
    e_id = row_id - ROUTER_EXPERT_LANE0

    def route(lg):
        gl = jnp.where(is_grp, lg, NEG_INF)
        ge = jnp.exp(gl - jnp.max(gl, axis=0, keepdims=True))
        g_prob = ge / jnp.sum(ge, axis=0, keepdims=True)
        p_grp = jnp.max(g_prob, axis=0, keepdims=True)
        grp = jnp.min(jnp.where((g_prob == p_grp) & is_grp, row_id, LANES), axis=0, keepdims=True)
        in_grp = (e_id >= grp * EXPERTS_PER_GROUP) & (e_id < (grp + 1) * EXPERTS_PER_GROUP)
        el = jnp.where(in_grp, lg, NEG_INF)
        ee = jnp.exp(el - jnp.max(el, axis=0, keepdims=True))
        e_prob = jnp.where(in_grp, ee / jnp.sum(ee, axis=0, keepdims=True), -1.0)
        p0 = jnp.max(e_prob, axis=0, keepdims=True)
        e0 = jnp.min(jnp.where(e_prob == p0, e_id, LANES), axis=0, keepdims=True)
        e_rest = jnp.where(e_id == e0, -1.0, e_prob)
        p1 = jnp.max(e_rest, axis=0, keepdims=True)
        e1 = jnp.min(jnp.where(e_rest == p1, e_id, LANES), axis=0, keepdims=True)
        return e0, e1, p_grp * p0 / (p0 + p1), p_grp * p1 / (p0 + p1)

    routed = [route(lg[0:n_log, :]) for lg in logits]

    expert = lax.broadcasted_iota(jnp.int32, (LANES, sub), 0)
    before = (lax.broadcasted_iota(jnp.int32, (sub, sub), 0)
              < lax.broadcasted_iota(jnp.int32, (sub, sub), 1)).astype(BF16)
    field = lax.broadcasted_iota(jnp.int32, (LANES, sub), 0)
    base = base_ref[...]
    for rs, (e0, e1, w0, w1) in zip(parts, routed):
        onehot = ((expert == e0) | (expert == e1)).astype(F32)
        seen = base + jnp.dot(onehot.astype(BF16), before, preferred_element_type=F32)
        r0 = jnp.sum(jnp.where(expert == e0, seen, 0.0), axis=0, keepdims=True)
        r1 = jnp.sum(jnp.where(expert == e1, seen, 0.0), axis=0, keepdims=True)
        base = base + jnp.sum(onehot, axis=1, keepdims=True)
        info_t = jnp.zeros((LANES, sub), F32)
        for ln, val in ((INFO_W0, w0), (INFO_W1, w1), (INFO_E0, e0.astype(F32)), (INFO_E1, e1.astype(F32)),
                        (INFO_R0, r0), (INFO_R1, r1)):
            info_t = jnp.where(field == ln, val, info_t)
        info_ref[0, rs, :] = info_t.T
    base_ref[...] = base
    cnt_ref[...] = jnp.broadcast_to(base, (LANES, LANES)).T[0:1, :]


def _mix(x, o_g, o_m, z_g, z_m, mem_k, mem_v, g_cross, g_moe, w_proj_gla, w_proj_moba, w_out, w_cq, w_co,
         w_rg, b_rg, w_re, b_re):
    B, S, D = x.shape
    M = mem_k.shape[1]
    rows = min(MIX_ROWS, S)
    assert S % rows == 0
    pad = LANES - N_GROUPS - N_EXPERTS
    wr = jnp.pad(jnp.concatenate([w_rg, w_re], axis=1), ((0, 0), (0, pad))).astype(BF16).T
    br = jnp.pad(jnp.concatenate([b_rg, b_re]), (0, pad)).reshape(LANES, 1).astype(F32)
    weights = [w_proj_gla.astype(BF16), w_proj_moba.astype(BF16), w_out.astype(BF16), w_cq.astype(BF16),
               w_co.astype(BF16), wr, br]
    tile = lambda n: pl.BlockSpec((1, rows, n), lambda b, i: (b, i, 0))
    memspec = pl.BlockSpec((1, M, MEM_W), lambda b, i: (b, 0, 0))
    n_pieces = D // 2 // SC_GATHER_WORDS
    return pl.pallas_call(
        functools.partial(_mix_kernel, n_pieces),
        grid=(B, S // rows),
        in_specs=[tile(D), tile(GLA_V), tile(MOBA_W), tile(D), tile(D), memspec, memspec,
                  _full((1, D)), _full((1, D))] + [_full(w.shape) for w in weights],
        out_specs=[tile(D), tile(LANES), _full((1, LANES))] + [tile(SC_GATHER_WORDS)] * n_pieces,
        out_shape=[jax.ShapeDtypeStruct((B, S, D), F32), jax.ShapeDtypeStruct((B, S, LANES), F32),
                   jax.ShapeDtypeStruct((1, LANES), F32)]
        + [jax.ShapeDtypeStruct((B, S, SC_GATHER_WORDS), jnp.uint32)] * n_pieces,
        scratch_shapes=[pltpu.VMEM((LANES, 1), F32)],
        compiler_params=_params("arbitrary", "arbitrary"),
        name="merge_memattn_router",
    )(x, o_g, o_m, z_g.reshape(B, S, D), z_m.reshape(B, S, D), mem_k, mem_v,
      g_cross.reshape(1, D).astype(F32), g_moe.reshape(1, D).astype(F32), *weights)


def _plan_kernel(cnt_ref, info_ref, dest_ref, blk_ref):
    rows = info_ref.shape[0]
    lane1 = lax.broadcasted_iota(jnp.int32, (1, LANES), 1)
    nblk = jnp.floor((cnt_ref[...] + (EXPERT_ROWS - 1)) * (1.0 / EXPERT_ROWS))
    nblk = jnp.where(lane1 < N_EXPERTS, nblk, 0.0)
    hi = jnp.floor(nblk * (1.0 / 256.0))
    lo = nblk - 256.0 * hi
    upto = (lax.broadcasted_iota(jnp.int32, (LANES, LANES), 0)
            <= lax.broadcasted_iota(jnp.int32, (LANES, LANES), 1)).astype(BF16)
    digits = jnp.concatenate([jnp.broadcast_to(hi, (8, LANES)), jnp.broadcast_to(lo, (8, LANES))], axis=0)
    sums = jnp.dot(digits.astype(BF16), upto, preferred_element_type=F32)
    pend = sums[0:1] * 256.0 + sums[8:9]
    pstart_rows = (pend - nblk) * EXPERT_ROWS

    info = info_ref[...]
    lane = lax.broadcasted_iota(jnp.int32, (rows, LANES), 1)

    def field(ln):
        return jnp.sum(jnp.where(lane == ln, info, 0.0), axis=-1, keepdims=True)

    def dest(e, r):
        return jnp.sum(jnp.where(lane == e.astype(jnp.int32), pstart_rows, 0.0), axis=-1, keepdims=True) + r

    d0 = dest(field(INFO_E0), field(INFO_R0))
    d1 = dest(field(INFO_E1), field(INFO_R1))
    cols = jnp.where(lane == 0, d0, jnp.where(lane == 1, d1, 0.0))
    dest_ref[...] = cols.T[0:SUBLANES, :].astype(jnp.int32)

    @pl.when(pl.program_id(0) == 0)
    def _():
        n = lax.broadcasted_iota(jnp.int32, (blk_ref.shape[0], LANES), 0).astype(F32)
        blane = lax.broadcasted_iota(jnp.int32, (blk_ref.shape[0], LANES), 1)
        done = jnp.where((pend <= n) & (lane1 < N_EXPERTS), 1.0, 0.0)
        e = jnp.minimum(jnp.sum(done, axis=-1, keepdims=True), N_EXPERTS - 1.0)
        mine = blane == e.astype(jnp.int32)
        first_blk = jnp.sum(jnp.where(mine, pend - nblk, 0.0), axis=-1, keepdims=True)
        count = jnp.sum(jnp.where(mine, cnt_ref[...], 0.0), axis=-1, keepdims=True)
        valid = jnp.clip(count - EXPERT_ROWS * (n[:, 0:1] - first_blk), 0.0, float(EXPERT_ROWS))
        blk_ref[...] = jnp.where(blane == BLK_EXPERT, e, jnp.where(blane == BLK_VALID, valid, 0.0)).astype(jnp.int32)


def _plan(counts, info2d, n_blk):
    T = info2d.shape[0]
    rows = min(PLAN_ROWS, T)
    assert T % rows == 0
    n_blk_pad = -(-n_blk // SUBLANES) * SUBLANES
    return pl.pallas_call(
        _plan_kernel,
        grid=(T // rows,),
        in_specs=[_full((1, LANES)), pl.BlockSpec((rows, LANES), lambda i: (i, 0))],
        out_specs=[pl.BlockSpec((SUBLANES, rows), lambda i: (0, i)), _full((n_blk_pad, LANES))],
        out_shape=[jax.ShapeDtypeStruct((SUBLANES, T), jnp.int32),
                   jax.ShapeDtypeStruct((n_blk_pad, LANES), jnp.int32)],
        compiler_params=_params("arbitrary"),
        name="dispatch_plan",
    )(counts, info2d)


def _sc_windows(n_rows):
    n_inner = 32
    assert n_rows % (SC_GATHER_ROWS * n_inner) == 0
    return n_rows // (SC_GATHER_ROWS * n_inner), n_inner


def _sc_mesh():
    return plsc.VectorSubcoreMesh(core_axis_name="c", subcore_axis_name="s")


def _sc_scatter_rows(src, idx_a, idx_b, n_out):
    T, W = src.shape
    n_outer, n_inner = _sc_windows(T)
    win = lambda i, j: i * n_inner + j

    @pl.kernel(out_type=jax.ShapeDtypeStruct((n_out, W), src.dtype), mesh=_sc_mesh(), scratch_types=[])
    def scatter_kernel(s_hbm, a_hbm, b_hbm, o_hbm):
        def body(s_vmem, a_vmem, b_vmem):
            pltpu.sync_copy(s_vmem, o_hbm.at[a_vmem.at[0]])
            pltpu.sync_copy(s_vmem, o_hbm.at[b_vmem.at[0]])

        pltpu.emit_pipeline(
            body,
            grid=(n_outer, n_inner),
            in_specs=[pl.BlockSpec((SC_GATHER_ROWS, W), index_map=lambda i, j: (win(i, j), 0)),
                      pl.BlockSpec((1, SC_GATHER_ROWS), index_map=lambda i, j: (0, win(i, j))),
                      pl.BlockSpec((1, SC_GATHER_ROWS), index_map=lambda i, j: (0, win(i, j)))],
            out_specs=[],
            core_axis_name=("c", "s"),
            dimension_semantics=(pltpu.PARALLEL, pltpu.PARALLEL),
        )(s_hbm, a_hbm, b_hbm)

    return scatter_kernel(src, idx_a, idx_b)


def _sc_gather_rows(table, idx):
    M = idx.shape[1]
    W = table.shape[1]
    n_outer, n_inner = _sc_windows(M)
    win = lambda i, j: i * n_inner + j

    @pl.kernel(out_type=jax.ShapeDtypeStruct((M, W), table.dtype), mesh=_sc_mesh(), scratch_types=[])
    def gather_kernel(t_hbm, i_hbm, o_hbm):
        def body(i_vmem, o_vmem):
            pltpu.sync_copy(t_hbm.at[i_vmem.at[0]], o_vmem)

        pltpu.emit_pipeline(
            body,
            grid=(n_outer, n_inner),
            in_specs=[pl.BlockSpec((1, SC_GATHER_ROWS), index_map=lambda i, j: (0, win(i, j)))],
            out_specs=[pl.BlockSpec((SC_GATHER_ROWS, W), index_map=lambda i, j: (win(i, j), 0))],
            core_axis_name=("c", "s"),
            dimension_semantics=(pltpu.PARALLEL, pltpu.PARALLEL),
        )(i_hbm, o_hbm)

    return gather_kernel(table, idx)


def _expert_kernel(n_pieces, blk_e_ref, blk_valid_ref, *refs):
    xs_refs, (wg_ref, wu_ref, wd_ref) = refs[:n_pieces], refs[n_pieces:n_pieces + 3]
    y_refs, (wg_bf, wu_bf, wd_bf) = refs[n_pieces + 3:2 * n_pieces + 3], refs[2 * n_pieces + 3:]
    n = pl.program_id(0)
    prev = blk_e_ref[jnp.maximum(n - 1, 0)]

    @pl.when((n == 0) | (blk_e_ref[n] != prev))
    def _():
        wg_bf[...] = wg_ref[0].astype(BF16)
        wu_bf[...] = wu_ref[0].astype(BF16)
        wd_bf[...] = wd_ref[0].astype(BF16)

    valid = blk_valid_ref[n]

    @pl.when(valid == 0)
    def _():
        for y_ref in y_refs:
            y_ref[...] = jnp.zeros_like(y_ref)

    @pl.when(valid > 0)
    def _():
        sub = xs_refs[0].shape[0] // EXPERT_PARTS
        parts = [slice(p * sub, (p + 1) * sub) for p in range(EXPERT_PARTS)]
        row = lax.broadcasted_iota(jnp.int32, (sub, 1), 0)

        def load(p, rs):
            words = jnp.concatenate([r[rs, :] for r in xs_refs], axis=1)
            words = jnp.where(row + p * sub < valid, words, jnp.zeros_like(words))
            return _unpack_bf16_pairs(words).astype(BF16)

        xb = [load(p, rs) for p, rs in enumerate(parts)]
        gate = [jnp.dot(v, wg_bf[...], preferred_element_type=F32) for v in xb]
        up = [jnp.dot(v, wu_bf[...], preferred_element_type=F32) for v in xb]
        hid = [(g * jax.nn.sigmoid(g) * u).astype(BF16) for g, u in zip(gate, up)]
        for rs, hv in zip(parts, hid):
            words = _pack_bf16_pairs(jnp.dot(hv, wd_bf[...], preferred_element_type=F32))
            for c, y_ref in enumerate(y_refs):
                y_ref[rs, :] = words[:, c * SC_GATHER_WORDS:(c + 1) * SC_GATHER_WORDS]


def _experts(blk_e, blk_valid, xs_pieces, w_gate, w_up, w_down):
    cap = xs_pieces[0].shape[0]
    _, D, DE = w_gate.shape
    n_blk = cap // EXPERT_ROWS
    n_pieces = len(xs_pieces)
    piece = pl.BlockSpec((EXPERT_ROWS, SC_GATHER_WORDS), lambda n, e, v: (n, 0))
    return pl.pallas_call(
        functools.partial(_expert_kernel, n_pieces),
        grid_spec=pltpu.PrefetchScalarGridSpec(
            num_scalar_prefetch=2,
            grid=(n_blk,),
            in_specs=[piece] * n_pieces
            + [pl.BlockSpec((1, D, DE), lambda n, e, v: (e[n], 0, 0)),
               pl.BlockSpec((1, D, DE), lambda n, e, v: (e[n], 0, 0)),
               pl.BlockSpec((1, DE, D), lambda n, e, v: (e[n], 0, 0))],
            out_specs=[piece] * n_pieces,
            scratch_shapes=[pltpu.VMEM((D, DE), BF16), pltpu.VMEM((D, DE), BF16), pltpu.VMEM((DE, D), BF16)]),
        out_shape=[jax.ShapeDtypeStruct((cap, SC_GATHER_WORDS), jnp.uint32)] * n_pieces,
        compiler_params=_params("arbitrary"),
        name="moe_experts",
    )(blk_e, blk_valid, *xs_pieces, w_gate, w_up, w_down)


def _combine_dense_kernel(final_norm, n_pieces, *refs):
    yg_refs, (x_ref, info_ref, g_ref, o_ref) = refs[:n_pieces], refs[n_pieces:]
    info = info_ref[...]
    w0 = info[:, INFO_W0:INFO_W0 + 1]
    w1 = info[:, INFO_W1:INFO_W1 + 1]
    y = [_unpack_bf16_pairs(jnp.concatenate([r[slot] for r in yg_refs], axis=1)) for slot in range(TOPK_IN_GROUP)]
    out = x_ref[...] + (w0 * y[0] + w1 * y[1])
    o_ref[...] = _rms(out, g_ref[...]) if final_norm else out


def _combine_dense(yg_pieces, x2d, info2d, g_final, final_norm):
    T, D = x2d.shape
    rows = min(MOVE_ROWS, T)
    n_pieces = len(yg_pieces)
    return pl.pallas_call(
        functools.partial(_combine_dense_kernel, final_norm, n_pieces),
        grid=(T // rows,),
        in_specs=[pl.BlockSpec((TOPK_IN_GROUP, rows, SC_GATHER_WORDS), lambda i: (0, i, 0))] * n_pieces
        + [pl.BlockSpec((rows, D), lambda i: (i, 0)), pl.BlockSpec((rows, LANES), lambda i: (i, 0)),
           _full((1, D))],
        out_specs=pl.BlockSpec((rows, D), lambda i: (i, 0)),
        out_shape=jax.ShapeDtypeStruct((T, D), F32),
        compiler_params=_params("parallel"),
        name="moe_combine_dense_final_norm",
    )(*yg_pieces, x2d, info2d, g_final.reshape(1, D).astype(F32))


def kernel(x, mem, g_mem, rel_bias, g_mix, w_in, w_alpha_up, b_alpha, g_gla_head, w_proj_gla, w_proj_moba,
           w_out, g_cross, w_cq, w_ckv, w_co, g_moe, w_router_group, b_router_group, w_router_expert,
           b_router_expert, w_exp_gate, w_exp_up, w_exp_down, g_final):
    B, S, D = x.shape
    T = B * S
    depth = g_mix.shape[0]
    n_assign = T * TOPK_IN_GROUP
    n_blk = -(-(n_assign + N_EXPERTS * (EXPERT_ROWS - 1)) // EXPERT_ROWS)
    cap = n_blk * EXPERT_ROWS

    mem_bias = _moba_bias(rel_bias)
    for l in range(depth):
        qk, v_g, r_g, la, q_m, k_m, v_m, z_g, z_m = _project(x.reshape(T, D), g_mix[l], w_in[l], w_alpha_up[l],
                                                             b_alpha[l])
        o_g = _gla(qk, la, v_g, r_g, g_gla_head[l], B, S)
        o_m = _moba(q_m, k_m, v_m, mem_bias, rel_bias, B, S)
        mem_k, mem_v = _mem_kv(mem, g_mem, w_ckv[l])
        x2, info, counts, *hm_pieces = _mix(x, o_g, o_m, z_g, z_m, mem_k, mem_v, g_cross[l], g_moe[l],
                                            w_proj_gla[l], w_proj_moba[l], w_out[l], w_cq[l], w_co[l],
                                            w_router_group[l], b_router_group[l], w_router_expert[l],
                                            b_router_expert[l])
        x2d, info2d = x2.reshape(T, D), info.reshape(T, LANES)
        dest, blk = _plan(counts, info2d, n_blk)
        idx = dest[0:TOPK_IN_GROUP].reshape(1, TOPK_IN_GROUP * T)
        xs_pieces = [_sc_scatter_rows(h.reshape(T, SC_GATHER_WORDS), idx[:, :T], idx[:, T:], cap)
                     for h in hm_pieces]
        y_pieces = _experts(blk[:n_blk, BLK_EXPERT], blk[:n_blk, BLK_VALID], xs_pieces, w_exp_gate[l],
                            w_exp_up[l], w_exp_down[l])
        yg = [_sc_gather_rows(y, idx).reshape(TOPK_IN_GROUP, T, SC_GATHER_WORDS) for y in y_pieces]
        x = _combine_dense(yg, x2d, info2d, g_final, final_norm=(l == depth - 1)).reshape(B, S, D)
    return x
```

```python
import functools
import math

import jax
import jax.numpy as jnp
from jax import lax
from jax.experimental import pallas as pl
from jax.experimental.pallas import tpu as pltpu
from jax.experimental.pallas import tpu_sc as plsc

F32 = jnp.float32
BF16 = jnp.bfloat16
NEG_INF = float("-inf")

EPS = 1e-6
GLA_HEADS, GLA_DK, GLA_DV, GLA_LOWRANK, GLA_TAU, GLA_CHUNK = 4, 64, 128, 16, 16.0, 64
GLA_QK, GLA_V = GLA_HEADS * GLA_DK, GLA_HEADS * GLA_DV
MOBA_HEADS, MOBA_DH, MOBA_BLOCK, MOBA_TOPK = 8, 64, 256, 3
MOBA_W = MOBA_HEADS * MOBA_DH
LOG2E = math.log2(math.e)
MOBA_Q_SCALE = MOBA_DH ** -0.5 * LOG2E
REL_BUCKETS, REL_MAX_DIST = 32, 128
MEM_HEADS, MEM_DH = 4, 128
MEM_W = MEM_HEADS * MEM_DH
N_GROUPS, EXPERTS_PER_GROUP, TOPK_IN_GROUP = 4, 8, 2
N_EXPERTS = N_GROUPS * EXPERTS_PER_GROUP

LANES = 128
SUBLANES = 8
VMEM_LIMIT_BYTES = 56 * 1024 * 1024

PROJ_ROWS = 512
GLA_ROWS = 512
MOBA_GROUP = 4
MIX_ROWS = 512
MIX_PARTS = 2
EXPERT_ROWS = 256
MOVE_ROWS = 512
PLAN_ROWS = 2048
SC_GATHER_ROWS = 128
SC_GATHER_WORDS = 256
EXPERT_PARTS = 2


def _params(*semantics):
    return pltpu.CompilerParams(dimension_semantics=semantics, vmem_limit_bytes=VMEM_LIMIT_BYTES)


def _full(shape):
    return pl.BlockSpec(shape, lambda *_: (0,) * len(shape))


def _rms(x, g):
    return x * lax.rsqrt(jnp.mean(x * x, axis=-1, keepdims=True) + EPS) * g


def _pack_bf16_pairs(x):
    n = x.shape[1] // 2
    bits = pltpu.bitcast(x.astype(BF16).astype(F32), jnp.uint32)
    return bits[:, n:] | (bits[:, :n] >> 16)


def _unpack_bf16_pairs(w):
    lo = pltpu.bitcast(w << 16, F32)
    hi = pltpu.bitcast(w & jnp.uint32(0xFFFF0000), F32)
    return jnp.concatenate([lo, hi], axis=1)


def _proj_kernel(x_ref, g_ref, w_qk, w_v, w_r, w_a, w_up, b_a, w_qm, w_km, w_vm, w_zg, w_zm,
                 o_qk, o_v, o_r, o_la, o_qm, o_km, o_vm, o_zg, o_zm):
    h = _rms(x_ref[...], g_ref[...]).astype(BF16)

    def mm(w_ref):
        return jnp.dot(h, w_ref[...], preferred_element_type=F32)

    o_qk[...] = mm(w_qk)
    o_v[...] = mm(w_v).astype(BF16)
    o_r[...] = mm(w_r)
    a_lr = mm(w_a).astype(BF16)
    pre = jnp.dot(a_lr, w_up[...], preferred_element_type=F32) + b_a[...]
    o_la[...] = jax.nn.log_sigmoid(pre) * (1.0 / GLA_TAU)

    def mm_t(wt_ref):
        return lax.dot_general(wt_ref[...], h, (((1,), (1,)), ((), ())), preferred_element_type=F32)

    def store_blocks(o_ref, val_t):
        for c in range(o_ref.shape[0]):
            o_ref[c] = val_t[:, c * MOBA_BLOCK:(c + 1) * MOBA_BLOCK]

    store_blocks(o_qm, (mm_t(w_qm) * MOBA_Q_SCALE).astype(BF16))
    o_km[...] = mm(w_km).astype(BF16)
    store_blocks(o_vm, mm_t(w_vm).astype(BF16))
    o_zg[...] = mm(w_zg)
    o_zm[...] = mm(w_zm)


def _project(x2d, g_mix, w_in, w_alpha_up, b_alpha):
    T, D = x2d.shape
    rows = min(PROJ_ROWS, T)
    assert T % rows == 0
    splits = (GLA_QK, GLA_QK, GLA_V, GLA_V, GLA_LOWRANK, MOBA_W, MOBA_W, MOBA_W, D, D)
    offs = [0]
    for s in splits:
        offs.append(offs[-1] + s)
    wb = w_in.astype(BF16)
    sec = lambda i, j: wb[:, offs[i]:offs[j]]
    w_a = jnp.pad(sec(4, 5), ((0, 0), (0, LANES - GLA_LOWRANK)))
    w_up = jnp.pad(w_alpha_up.astype(BF16), ((0, LANES - GLA_LOWRANK), (0, 0)))
    weights = [sec(0, 2), sec(2, 3), sec(3, 4), w_a, w_up, b_alpha.reshape(1, GLA_QK).astype(F32),
               sec(5, 6).T, sec(6, 7), sec(7, 8).T, sec(8, 9), sec(9, 10)]
    out_defs = [(2 * GLA_QK, F32, False), (GLA_V, BF16, False), (GLA_V, F32, False), (GLA_QK, F32, False),
                (MOBA_W, BF16, True), (MOBA_W, BF16, False), (MOBA_W, BF16, True), (D, F32, False),
                (D, F32, False)]
    BS = MOBA_BLOCK
    assert rows % BS == 0
    row_spec = lambda n: pl.BlockSpec((rows, n), lambda i: (i, 0))
    blk_spec = lambda n: pl.BlockSpec((rows // BS, n, BS), lambda i: (i, 0, 0))
    return pl.pallas_call(
        _proj_kernel,
        grid=(T // rows,),
        in_specs=[row_spec(D), _full((1, D))] + [_full(w.shape) for w in weights],
        out_specs=[blk_spec(n) if t else row_spec(n) for n, _, t in out_defs],
        out_shape=[jax.ShapeDtypeStruct((T // BS, n, BS) if t else (T, n), dt) for n, dt, t in out_defs],
        compiler_params=_params("parallel"),
        name="norm_in_proj",
    )(x2d, g_mix.reshape(1, D).astype(F32), *weights)


def _gla_kernel(qk_ref, la_ref, v_ref, r_ref, g_ref, o_ref, state_ref, obuf_ref):
    C, H, DK, DV = GLA_CHUNK, GLA_HEADS, GLA_DK, GLA_DV
    rows = qk_ref.shape[1]

    @pl.when(pl.program_id(1) == 0)
    def _():
        state_ref[...] = jnp.zeros_like(state_ref)

    tri = (lax.broadcasted_iota(jnp.int32, (C, C), 0) >= lax.broadcasted_iota(jnp.int32, (C, C), 1)).astype(BF16)
    lane_head = lax.broadcasted_iota(jnp.int32, (1, H * DK), 1) // DK
    head_masks = [(lane_head == h).astype(F32) for h in range(H)]
    stack_row = lax.broadcasted_iota(jnp.int32, (H * C, C), 0) % C
    stack_col = lax.broadcasted_iota(jnp.int32, (H * C, C), 1)
    causal = stack_col <= stack_row
    same_head = (lax.broadcasted_iota(jnp.int32, (H * DV, H * DK), 0) // DV
                 == lax.broadcasted_iota(jnp.int32, (H * DV, H * DK), 1) // DK)
    scale = DK ** -0.5

    def stack(m):
        return jnp.concatenate([m * head_masks[h] for h in range(H)], axis=0).astype(BF16)

    chunks = [slice(c * C, (c + 1) * C) for c in range(rows // C)]

    def cum_log_decay(sl):
        la = la_ref[0, sl, :]
        p1 = la.astype(BF16)
        r1 = la - p1.astype(F32)
        p2 = r1.astype(BF16)
        p3 = (r1 - p2.astype(F32)).astype(BF16)
        s3 = jnp.dot(tri, jnp.concatenate([p1, p2, p3], axis=1), preferred_element_type=F32)
        w = H * DK
        return (s3[:, 0:w] + s3[:, w:2 * w]) + s3[:, 2 * w:3 * w]

    b_all = [cum_log_decay(sl) for sl in chunks]

    qe_all, ke_all, kd_all, qb_all, decay_all = [], [], [], [], []
    for sl, b in zip(chunks, b_all):
        q = qk_ref[0, sl, 0:H * DK] * scale
        k = qk_ref[0, sl, H * DK:2 * H * DK]
        b_last = b[C - 1:C, :]
        b_mid = b[C // 2 - 1:C // 2, :]
        qe_all.append(stack(q * jnp.exp(b - b_mid)))
        ke_all.append((k * jnp.exp(b_mid - b)).astype(BF16))
        kd_all.append((k * jnp.exp(b_last - b)).astype(BF16))
        qb_all.append((q * jnp.exp(b)).astype(BF16))
        decay_all.append(jnp.exp(b_last))

    att_all = [jnp.where(causal, lax.dot_general(qe, ke, (((1,), (1,)), ((), ())), preferred_element_type=F32),
                         0.0).astype(BF16) for qe, ke in zip(qe_all, ke_all)]

    o_intra_all, kv_all = [], []
    for sl, att, kd in zip(chunks, att_all, kd_all):
        v = v_ref[0, sl, :]
        o_intra_all.append(jnp.concatenate(
            [jnp.dot(att[h * C:(h + 1) * C, :], v[:, h * DV:(h + 1) * DV], preferred_element_type=F32)
             for h in range(H)], axis=1))
        kv_t = lax.dot_general(v, kd, (((0,), (0,)), ((), ())), preferred_element_type=F32)
        kv_all.append(jnp.where(same_head, kv_t, 0.0))

    state_t = state_ref[...]
    for sl, qb, decay, kv_t, o_intra in zip(chunks, qb_all, decay_all, kv_all, o_intra_all):
        o_inter = lax.dot_general(qb, state_t.astype(BF16), (((1,), (1,)), ((), ())),
                                  preferred_element_type=F32)
        obuf_ref[sl, :] = o_intra + o_inter
        state_t = decay * state_t + kv_t
    state_ref[...] = state_t

    r = r_ref[0]
    for h in range(H):
        cs = slice(h * DV, (h + 1) * DV)
        y = _rms(obuf_ref[:, cs], g_ref[:, cs])
        rh = r[:, cs]
        o_ref[0, :, cs] = (y * (rh * jax.nn.sigmoid(rh))).astype(BF16)


def _gla(qk, la, v, r, g_head, B, S):
    rows = min(GLA_ROWS, S)
    assert S % rows == 0 and rows % GLA_CHUNK == 0
    spec = lambda n: pl.BlockSpec((1, rows, n), lambda b, i: (b, i, 0))
    return pl.pallas_call(
        _gla_kernel,
        grid=(B, S // rows),
        in_specs=[spec(2 * GLA_QK), spec(GLA_QK), spec(GLA_V), spec(GLA_V), _full((1, GLA_V))],
        out_specs=spec(GLA_V),
        out_shape=jax.ShapeDtypeStruct((B, S, GLA_V), BF16),
        scratch_shapes=[pltpu.VMEM((GLA_V, GLA_QK), F32), pltpu.VMEM((rows, GLA_V), F32)],
        compiler_params=_params("parallel", "arbitrary"),
        name="gla_chunked",
    )(qk.reshape(B, S, -1), la.reshape(B, S, -1), v.reshape(B, S, -1), r.reshape(B, S, -1),
      g_head.reshape(1, GLA_V).astype(F32))


def _t5_bucket(dist):
    n = jnp.maximum(dist, 0)
    max_exact = REL_BUCKETS // 2
    nf = jnp.maximum(n, 1).astype(F32)
    large = max_exact + (jnp.log(nf / max_exact) / math.log(REL_MAX_DIST / max_exact)
                         * (REL_BUCKETS - max_exact)).astype(jnp.int32)
    large = jnp.minimum(large, REL_BUCKETS - 1)
    return jnp.where(n < max_exact, n, large)


def _moba_bias_kernel(rb_ref, o_ref):
    BS, G = MOBA_BLOCK, MOBA_GROUP
    grp, kind = pl.program_id(0), pl.program_id(1)
    d = (lax.broadcasted_iota(jnp.int32, (BS, BS), 1) - lax.broadcasted_iota(jnp.int32, (BS, BS), 0)
         + kind * BS)
    bucket = _t5_bucket(d)
    for h in range(G):
        val = jnp.zeros((BS, BS), F32)
        for bkt in range(REL_BUCKETS):
            val = jnp.where(bucket == bkt, rb_ref[bkt, grp * G + h] * LOG2E, val)
        o_ref[0, 0, :, h * BS:(h + 1) * BS] = jnp.where(d >= 0, val, NEG_INF)


def _moba_bias(rel_bias):
    BS, G = MOBA_BLOCK, MOBA_GROUP
    n_grp = MOBA_HEADS // G
    return pl.pallas_call(
        _moba_bias_kernel,
        grid=(n_grp, 2),
        in_specs=[pl.BlockSpec(memory_space=pltpu.SMEM)],
        out_specs=pl.BlockSpec((1, 1, BS, G * BS), lambda g, k: (g, k, 0, 0)),
        out_shape=jax.ShapeDtypeStruct((n_grp, 2, BS, G * BS), F32),
        compiler_params=_params("parallel", "parallel"),
        name="moba_bias_tables",
    )(rel_bias.astype(F32))


def _moba_kernel(rb_ref, q_ref, k_ref, v_ref, bias_ref, o_ref,
                 qs_ref, kmean_ref, sel_ref, m_ref, l_ref, acc_ref, sbuf_ref):
    BS, G, DH = MOBA_BLOCK, MOBA_GROUP, MOBA_DH
    NBP = kmean_ref.shape[0]
    grp, i = pl.program_id(1), pl.program_id(2)

    @pl.when(i == 0)
    def _():
        S = k_ref.shape[1]
        blk_of_key = lax.broadcasted_iota(jnp.int32, (NBP, S), 1) // BS
        ind = (blk_of_key == lax.broadcasted_iota(jnp.int32, (NBP, S), 0)).astype(BF16)
        kmean_ref[...] = jnp.dot(ind, k_ref[0], preferred_element_type=F32) * (1.0 / BS)

    qt = q_ref[0]
    sub_head = lax.broadcasted_iota(jnp.int32, (G * DH, 1), 0) // DH
    for h in range(G):
        qs_ref[h] = jnp.where(sub_head == h, qt, jnp.zeros_like(qt))

    kmean = kmean_ref[...].astype(BF16)
    gate = jnp.concatenate([jnp.dot(kmean, qs_ref[h], preferred_element_type=F32) for h in range(G)],
                           axis=1)
    blk = lax.broadcasted_iota(jnp.int32, gate.shape, 0)
    gate = jnp.where(blk < i, gate, NEG_INF)
    for t in range(MOBA_TOPK):
        mx = jnp.max(gate, axis=0, keepdims=True)
        hit = (gate == mx) & (mx > NEG_INF)
        idx = jnp.min(jnp.where(hit, blk, NBP), axis=0, keepdims=True)
        sel_ref[t:t + 1, :] = idx
        gate = jnp.where(blk == idx, NEG_INF, gate)

    def mask_row(j):
        hit = (sel_ref[0:1, :] == j) | (sel_ref[1:2, :] == j) | (sel_ref[2:3, :] == j)
        return jnp.where(hit, 0.0, NEG_INF)

    def with_ones(vt):
        return jnp.concatenate([vt, jnp.ones((8, vt.shape[1]), BF16)], axis=0)

    def far_scores(j0, n, slot):
        kj = k_ref[0, pl.ds(pl.multiple_of(j0 * BS, BS), n * BS), :]
        for h in range(G):
            cs = slice(h * BS, (h + 1) * BS)
            sbuf_ref[slot, h, 0:n * BS, :] = jnp.dot(kj, qs_ref[h], preferred_element_type=F32)

    j_prev = jnp.maximum(i - 1, 0)
    k_own = k_ref[0, pl.ds(pl.multiple_of(i * BS, BS), BS), :]
    k_prev = k_ref[0, pl.ds(pl.multiple_of(j_prev * BS, BS), BS), :]
    vt_near = jnp.concatenate([v_ref[i], v_ref[j_prev]], axis=1)
    prev_mask = mask_row(i - 1)
    for h in range(G):
        cs = slice(h * BS, (h + 1) * BS)
        sbuf_ref[1, h, 0:BS, :] = jnp.dot(k_own, qs_ref[h], preferred_element_type=F32)
        sbuf_ref[1, h, BS:2 * BS, :] = jnp.dot(k_prev, qs_ref[h], preferred_element_type=F32)
    far_scores(0, 2, 0)
    m_out, l_out, acc_out = [], [], []
    for h in range(G):
        cs = slice(h * BS, (h + 1) * BS)
        s_own = sbuf_ref[1, h, 0:BS, :] + bias_ref[0, 0, :, cs]
        s_prev = sbuf_ref[1, h, BS:2 * BS, :] + (bias_ref[0, 1, :, cs] + prev_mask[:, cs])
        m0 = jnp.maximum(jnp.max(s_own, axis=0, keepdims=True), jnp.max(s_prev, axis=0, keepdims=True))
        pb = jnp.concatenate([jnp.exp2(s_own - m0).astype(BF16), jnp.exp2(s_prev - m0).astype(BF16)], axis=0)
        pv = jnp.dot(with_ones(vt_near[h * DH:(h + 1) * DH, :]), pb, preferred_element_type=F32)
        m_out.append(m0)
        l_out.append(pv[DH:DH + 1, :])
        acc_out.append(pv[0:DH, :])
    m_ref[...] = jnp.concatenate(m_out, axis=1)
    l_ref[...] = jnp.concatenate(l_out, axis=1)
    acc_ref[...] = jnp.concatenate(acc_out, axis=1)

    lane_head = lax.broadcasted_iota(jnp.int32, (1, G * BS), 1) // BS
    far_bias = jnp.zeros((1, G * BS), F32)
    for h in range(G):
        far_bias = jnp.where(lane_head == h, rb_ref[REL_BUCKETS - 1, grp * G + h] * LOG2E, far_bias)

    def visit_far(j0, n, slot, ahead=None):
        vjt = jnp.concatenate([v_ref[j0 + t] for t in range(n)], axis=1)
        addend = [far_bias + mask_row(j0 + t) for t in range(n)]
        m_old, l_old, acc_old = m_ref[...], l_ref[...], acc_ref[...]
        m_out, l_out, acc_out = [], [], []
        if ahead is not None:
            k_next = k_ref[0, pl.ds(pl.multiple_of(ahead[0] * BS, BS), 2 * BS), :]
        for h in range(G):
            cs = slice(h * BS, (h + 1) * BS)
            if ahead is not None:
                sbuf_ref[ahead[1], h] = jnp.dot(k_next, qs_ref[h], preferred_element_type=F32)
            s = sbuf_ref[slot, h, 0:n * BS, :]
            mx = jnp.max(s[0:BS], axis=0, keepdims=True) + addend[0][:, cs]
            for t in range(1, n):
                mx = jnp.maximum(mx, jnp.max(s[t * BS:(t + 1) * BS], axis=0, keepdims=True) + addend[t][:, cs])
            m_new = jnp.maximum(m_old[:, cs], mx)
            pb = jnp.concatenate([jnp.exp2(s[t * BS:(t + 1) * BS] - (m_new - addend[t][:, cs])).astype(BF16)
                                  for t in range(n)], axis=0)
            pv = jnp.dot(with_ones(vjt[h * DH:(h + 1) * DH, :]), pb, preferred_element_type=F32)
            alpha = jnp.exp2(m_old[:, cs] - m_new)
            m_out.append(m_new)
            l_out.append(alpha * l_old[:, cs] + pv[DH:DH + 1, :])
            acc_out.append(alpha * acc_old[:, cs] + pv[0:DH, :])
        m_ref[...] = jnp.concatenate(m_out, axis=1)
        l_ref[...] = jnp.concatenate(l_out, axis=1)
        acc_ref[...] = jnp.concatenate(acc_out, axis=1)

    n_far = jnp.maximum(i - 1, 0)
    n_pairs = lax.shift_right_logical(n_far, 1)

    def pair(p, slot, look_ahead=True):
        ahead = (2 * jnp.minimum(p + 1, n_pairs - 1), 1 - slot) if look_ahead else None
        visit_far(2 * p, 2, slot, ahead=ahead)

    def far_octet(w, carry):
        for t in range(4):
            pair(4 * w + t, t % 2)
        return carry

    lax.fori_loop(0, lax.shift_right_logical(n_pairs, 2), far_octet, 0)

    @pl.when((n_pairs & 2) != 0)
    def _():
        base = 4 * lax.shift_right_logical(n_pairs, 2)
        pair(base, 0)
        pair(base + 1, 1)

    @pl.when((n_pairs & 1) != 0)
    def _():
        pair(n_pairs - 1, 0, look_ahead=False)

    @pl.when(n_far % 2 == 1)
    def _():
        far_scores(n_far - 1, 1, 1)
        visit_far(n_far - 1, 1, 1)

    out_t = jnp.concatenate([acc_ref[:, h * BS:(h + 1) * BS] / l_ref[:, h * BS:(h + 1) * BS] for h in range(G)],
                            axis=0)
    o_ref[0] = out_t.T.astype(o_ref.dtype)


def _moba(qm_t, km, vm_t, bias, rel_bias, B, S):
    BS, G, DH = MOBA_BLOCK, MOBA_GROUP, MOBA_DH
    W = G * DH
    n_grp = MOBA_HEADS // G
    assert S % BS == 0 and S >= 2 * BS
    NB = S // BS
    NBP = -(-NB // 8) * 8
    return pl.pallas_call(
        _moba_kernel,
        grid=(B, n_grp, NB),
        in_specs=[pl.BlockSpec(memory_space=pltpu.SMEM),
                  pl.BlockSpec((1, W, BS), lambda b, g, i: (b * NB + i, g, 0)),
                  pl.BlockSpec((1, S, W), lambda b, g, i: (b, 0, g)),
                  pl.BlockSpec((NB, W, BS), lambda b, g, i: (b, g, 0)),
                  pl.BlockSpec((1, 2, BS, G * BS), lambda b, g, i: (g, 0, 0, 0))],
        out_specs=pl.BlockSpec((1, BS, W), lambda b, g, i: (b, i, g)),
        out_shape=jax.ShapeDtypeStruct((B, S, MOBA_W), BF16),
        scratch_shapes=[pltpu.VMEM((G, W, BS), BF16), pltpu.VMEM((NBP, W), F32),
                        pltpu.VMEM((8, G * BS), jnp.int32),
                        pltpu.VMEM((1, G * BS), F32), pltpu.VMEM((1, G * BS), F32),
                        pltpu.VMEM((DH, G * BS), F32), pltpu.VMEM((2, G, 2 * BS, BS), F32)],
        compiler_params=_params("parallel", "parallel", "arbitrary"),
        name="moba_attention",
    )(rel_bias.astype(F32), qm_t, km.reshape(B, S, -1), vm_t, bias)


def _mem_kv_kernel(mem_ref, g_ref, w_ref, k_ref, v_ref):
    kv = jnp.dot(_rms(mem_ref[0], g_ref[...]).astype(BF16), w_ref[...], preferred_element_type=F32)
    k_ref[0] = kv[:, :MEM_W].astype(BF16)
    v_ref[0] = kv[:, MEM_W:].astype(BF16)


def _mem_kv(mem, g_mem, w_ckv):
    B, M, D = mem.shape
    spec = pl.BlockSpec((1, M, MEM_W), lambda b: (b, 0, 0))
    return pl.pallas_call(
        _mem_kv_kernel,
        grid=(B,),
        in_specs=[pl.BlockSpec((1, M, D), lambda b: (b, 0, 0)), _full((1, D)), _full((D, 2 * MEM_W))],
        out_specs=[spec, spec],
        out_shape=[jax.ShapeDtypeStruct((B, M, MEM_W), BF16)] * 2,
        compiler_params=_params("parallel"),
        name="memory_kv",
    )(mem, g_mem.reshape(1, D).astype(F32), w_ckv.astype(BF16))


INFO_W0, INFO_W1, INFO_E0, INFO_E1, INFO_R0, INFO_R1 = range(6)
BLK_EXPERT, BLK_VALID = range(2)
ROUTER_GROUP_LANE0, ROUTER_EXPERT_LANE0 = 0, N_GROUPS


def _mix_kernel(n_pieces, x_ref, og_ref, om_ref, zg_ref, zm_ref, mk_ref, mv_ref, gc_ref, gm_ref,
                wpg, wpm, wout, wcq, wco, wr, br,
                x2_ref, info_ref, cnt_ref, *rest):
    hp_refs, base_ref = rest[:n_pieces], rest[n_pieces]
    first = (pl.program_id(0) == 0) & (pl.program_id(1) == 0)

    @pl.when(first)
    def _():
        base_ref[...] = jnp.zeros_like(base_ref)

    def mm(a, w_ref):
        return jnp.dot(a.astype(BF16), w_ref[...], preferred_element_type=F32)

    n_rows = x_ref.shape[1]
    sub = n_rows // MIX_PARTS
    parts = [slice(p * sub, (p + 1) * sub) for p in range(MIX_PARTS)]

    merged = [jax.nn.sigmoid(zg_ref[0, rs, :]) * mm(og_ref[0, rs, :], wpg)
              + jax.nn.sigmoid(zm_ref[0, rs, :]) * mm(om_ref[0, rs, :], wpm) for rs in parts]
    x1 = [x_ref[0, rs, :] + mm(m, wout) for rs, m in zip(parts, merged)]

    qc = [mm(_rms(v, gc_ref[...]), wcq).astype(BF16) for v in x1]

    def mem_attention(q):
        heads = []
        for h in range(MEM_HEADS):
            cs = slice(h * MEM_DH, (h + 1) * MEM_DH)
            s = lax.dot_general(q[:, cs], mk_ref[0, :, cs], (((1,), (1,)), ((), ())),
                                preferred_element_type=F32) * (MEM_DH ** -0.5)
            p = jnp.exp(s - jnp.max(s, axis=-1, keepdims=True))
            o = jnp.dot(p.astype(BF16), mv_ref[0, :, cs], preferred_element_type=F32)
            heads.append(o / jnp.sum(p, axis=-1, keepdims=True))
        return jnp.concatenate(heads, axis=-1)

    attn = [mem_attention(q) for q in qc]
    x2 = [v + mm(a, wco) for v, a in zip(x1, attn)]
    for rs, v in zip(parts, x2):
        x2_ref[0, rs, :] = v

    hm = [_rms(v, gm_ref[...]) for v in x2]
    logits = [lax.dot_general(wr[...], h.astype(BF16), (((1,), (1,)), ((), ())), preferred_element_type=F32)
              + br[...] for h in hm]
    for rs, h in zip(parts, hm):
        words = _pack_bf16_pairs(h)
        for c, hp_ref in enumerate(hp_refs):
            hp_ref[0, rs, :] = words[:, c * SC_GATHER_WORDS:(c + 1) * SC_GATHER_WORDS]
    n_log = -(-(N_GROUPS + N_EXPERTS) // SUBLANES) * SUBLANES
    row_id = lax.broadcasted_iota(jnp.int32, (n_log, sub), 0)
    is_grp = row_id < N_GROUPS
    e_id = row_id - ROUTER_EXPERT_LANE0

    def route(lg):
        gl = jnp.where(is_grp, lg, NEG_INF)
        ge = jnp.exp(gl - jnp.max(gl, axis=0, keepdims=True))
        g_prob = ge / jnp.sum(ge, axis=0, keepdims=True)
        p_grp = jnp.max(g_prob, axis=0, keepdims=True)
        grp = jnp.min(jnp.where((g_prob == p_grp) & is_grp, row_id, LANES), axis=0, keepdims=True)
        in_grp = (e_id >= grp * EXPERTS_PER_GROUP) & (e_id < (grp + 1) * EXPERTS_PER_GROUP)
        el = jnp.where(in_grp, lg, NEG_INF)
        ee = jnp.exp(el - jnp.max(el, axis=0, keepdims=True))
        e_prob = jnp.where(in_grp, ee / jnp.sum(ee, axis=0, keepdims=True), -1.0)
        p0 = jnp.max(e_prob, axis=0, keepdims=True)
        e0 = jnp.min(jnp.where(e_prob == p0, e_id, LANES), axis=0, keepdims=True)
        e_rest = jnp.where(e_id == e0, -1.0, e_prob)
        p1 = jnp.max(e_rest, axis=0, keepdims=True)
        e1 = jnp.min(jnp.where(e_rest == p1, e_id, LANES), axis=0, keepdims=True)
        return e0, e1, p_grp * p0 / (p0 + p1), p_grp * p1 / (p0 + p1)

    routed = [route(lg[0:n_log, :]) for lg in logits]

    expert = lax.broadcasted_iota(jnp.int32, (LANES, sub), 0)
    before = (lax.broadcasted_iota(jnp.int32, (sub, sub), 0)
              < lax.broadcasted_iota(jnp.int32, (sub, sub), 1)).astype(BF16)
    field = lax.broadcasted_iota(jnp.int32, (LANES, sub), 0)
    base = base_ref[...]
    for rs, (e0, e1, w0, w1) in zip(parts, routed):
        onehot = ((expert == e0) | (expert == e1)).astype(F32)
        seen = base + jnp.dot(onehot.astype(BF16), before, preferred_element_type=F32)
        r0 = jnp.sum(jnp.where(expert == e0, seen, 0.0), axis=0, keepdims=True)
        r1 = jnp.sum(jnp.where(expert == e1, seen, 0.0), axis=0, keepdims=True)
        base = base + jnp.sum(onehot, axis=1, keepdims=True)
        info_t = jnp.zeros((LANES, sub), F32)
        for ln, val in ((INFO_W0, w0), (INFO_W1, w1), (INFO_E0, e0.astype(F32)), (INFO_E1, e1.astype(F32)),
                        (INFO_R0, r0), (INFO_R1, r1)):
            info_t = jnp.where(field == ln, val, info_t)
        info_ref[0, rs, :] = info_t.T
    base_ref[...] = base
    cnt_ref[...] = jnp.broadcast_to(base, (LANES, LANES)).T[0:1, :]


def _mix(x, o_g, o_m, z_g, z_m, mem_k, mem_v, g_cross, g_moe, w_proj_gla, w_proj_moba, w_out, w_cq, w_co,
         w_rg, b_rg, w_re, b_re):
    B, S, D = x.shape
    M = mem_k.shape[1]
    rows = min(MIX_ROWS, S)
    assert S % rows == 0
    pad = LANES - N_GROUPS - N_EXPERTS
    wr = jnp.pad(jnp.concatenate([w_rg, w_re], axis=1), ((0, 0), (0, pad))).astype(BF16).T
    br = jnp.pad(jnp.concatenate([b_rg, b_re]), (0, pad)).reshape(LANES, 1).astype(F32)
    weights = [w_proj_gla.astype(BF16), w_proj_moba.astype(BF16), w_out.astype(BF16), w_cq.astype(BF16),
               w_co.astype(BF16), wr, br]
    tile = lambda n: pl.BlockSpec((1, rows, n), lambda b, i: (b, i, 0))
    memspec = pl.BlockSpec((1, M, MEM_W), lambda b, i: (b, 0, 0))
    n_pieces = D // 2 // SC_GATHER_WORDS
    return pl.pallas_call(
        functools.partial(_mix_kernel, n_pieces),
        grid=(B, S // rows),
        in_specs=[tile(D), tile(GLA_V), tile(MOBA_W), tile(D), tile(D), memspec, memspec,
                  _full((1, D)), _full((1, D))] + [_full(w.shape) for w in weights],
        out_specs=[tile(D), tile(LANES), _full((1, LANES))] + [tile(SC_GATHER_WORDS)] * n_pieces,
        out_shape=[jax.ShapeDtypeStruct((B, S, D), F32), jax.ShapeDtypeStruct((B, S, LANES), F32),
                   jax.ShapeDtypeStruct((1, LANES), F32)]
        + [jax.ShapeDtypeStruct((B, S, SC_GATHER_WORDS), jnp.uint32)] * n_pieces,
        scratch_shapes=[pltpu.VMEM((LANES, 1), F32)],
        compiler_params=_params("arbitrary", "arbitrary"),
        name="merge_memattn_router",
    )(x, o_g, o_m, z_g.reshape(B, S, D), z_m.reshape(B, S, D), mem_k, mem_v,
      g_cross.reshape(1, D).astype(F32), g_moe.reshape(1, D).astype(F32), *weights)


def _plan_kernel(cnt_ref, info_ref, dest_ref, blk_ref):
    rows = info_ref.shape[0]
    lane1 = lax.broadcasted_iota(jnp.int32, (1, LANES), 1)
    nblk = jnp.floor((cnt_ref[...] + (EXPERT_ROWS - 1)) * (1.0 / EXPERT_ROWS))
    nblk = jnp.where(lane1 < N_EXPERTS, nblk, 0.0)
    hi = jnp.floor(nblk * (1.0 / 256.0))
    lo = nblk - 256.0 * hi
    upto = (lax.broadcasted_iota(jnp.int32, (LANES, LANES), 0)
            <= lax.broadcasted_iota(jnp.int32, (LANES, LANES), 1)).astype(BF16)
    digits = jnp.concatenate([jnp.broadcast_to(hi, (8, LANES)), jnp.broadcast_to(lo, (8, LANES))], axis=0)
    sums = jnp.dot(digits.astype(BF16), upto, preferred_element_type=F32)
    pend = sums[0:1] * 256.0 + sums[8:9]
    pstart_rows = (pend - nblk) * EXPERT_ROWS

    info = info_ref[...]
    lane = lax.broadcasted_iota(jnp.int32, (rows, LANES), 1)

    def field(ln):
        return jnp.sum(jnp.where(lane == ln, info, 0.0), axis=-1, keepdims=True)

    def dest(e, r):
        return jnp.sum(jnp.where(lane == e.astype(jnp.int32), pstart_rows, 0.0), axis=-1, keepdims=True) + r

    d0 = dest(field(INFO_E0), field(INFO_R0))
    d1 = dest(field(INFO_E1), field(INFO_R1))
    cols = jnp.where(lane == 0, d0, jnp.where(lane == 1, d1, 0.0))
    dest_ref[...] = cols.T[0:SUBLANES, :].astype(jnp.int32)

    @pl.when(pl.program_id(0) == 0)
    def _():
        n = lax.broadcasted_iota(jnp.int32, (blk_ref.shape[0], LANES), 0).astype(F32)
        blane = lax.broadcasted_iota(jnp.int32, (blk_ref.shape[0], LANES), 1)
        done = jnp.where((pend <= n) & (lane1 < N_EXPERTS), 1.0, 0.0)
        e = jnp.minimum(jnp.sum(done, axis=-1, keepdims=True), N_EXPERTS - 1.0)
        mine = blane == e.astype(jnp.int32)
        first_blk = jnp.sum(jnp.where(mine, pend - nblk, 0.0), axis=-1, keepdims=True)
        count = jnp.sum(jnp.where(mine, cnt_ref[...], 0.0), axis=-1, keepdims=True)
        valid = jnp.clip(count - EXPERT_ROWS * (n[:, 0:1] - first_blk), 0.0, float(EXPERT_ROWS))
        blk_ref[...] = jnp.where(blane == BLK_EXPERT, e, jnp.where(blane == BLK_VALID, valid, 0.0)).astype(jnp.int32)


def _plan(counts, info2d, n_blk):
    T = info2d.shape[0]
    rows = min(PLAN_ROWS, T)
    assert T % rows == 0
    n_blk_pad = -(-n_blk // SUBLANES) * SUBLANES
    return pl.pallas_call(
        _plan_kernel,
        grid=(T // rows,),
        in_specs=[_full((1, LANES)), pl.BlockSpec((rows, LANES), lambda i: (i, 0))],
        out_specs=[pl.BlockSpec((SUBLANES, rows), lambda i: (0, i)), _full((n_blk_pad, LANES))],
        out_shape=[jax.ShapeDtypeStruct((SUBLANES, T), jnp.int32),
                   jax.ShapeDtypeStruct((n_blk_pad, LANES), jnp.int32)],
        compiler_params=_params("arbitrary"),
        name="dispatch_plan",
    )(counts, info2d)


def _sc_windows(n_rows):
    n_inner = 32
    assert n_rows % (SC_GATHER_ROWS * n_inner) == 0
    return n_rows // (SC_GATHER_ROWS * n_inner), n_inner


def _sc_mesh():
    return plsc.VectorSubcoreMesh(core_axis_name="c", subcore_axis_name="s")


def _sc_scatter_rows(src, idx_a, idx_b, n_out):
    T, W = src.shape
    n_outer, n_inner = _sc_windows(T)
    win = lambda i, j: i * n_inner + j

    @pl.kernel(out_type=jax.ShapeDtypeStruct((n_out, W), src.dtype), mesh=_sc_mesh(), scratch_types=[])
    def scatter_kernel(s_hbm, a_hbm, b_hbm, o_hbm):
        def body(s_vmem, a_vmem, b_vmem):
            pltpu.sync_copy(s_vmem, o_hbm.at[a_vmem.at[0]])
            pltpu.sync_copy(s_vmem, o_hbm.at[b_vmem.at[0]])

        pltpu.emit_pipeline(
            body,
            grid=(n_outer, n_inner),
            in_specs=[pl.BlockSpec((SC_GATHER_ROWS, W), index_map=lambda i, j: (win(i, j), 0)),
                      pl.BlockSpec((1, SC_GATHER_ROWS), index_map=lambda i, j: (0, win(i, j))),
                      pl.BlockSpec((1, SC_GATHER_ROWS), index_map=lambda i, j: (0, win(i, j)))],
            out_specs=[],
            core_axis_name=("c", "s"),
            dimension_semantics=(pltpu.PARALLEL, pltpu.PARALLEL),
        )(s_hbm, a_hbm, b_hbm)

    return scatter_kernel(src, idx_a, idx_b)


def _sc_gather_rows(table, idx):
    M = idx.shape[1]
    W = table.shape[1]
    n_outer, n_inner = _sc_windows(M)
    win = lambda i, j: i * n_inner + j

    @pl.kernel(out_type=jax.ShapeDtypeStruct((M, W), table.dtype), mesh=_sc_mesh(), scratch_types=[])
    def gather_kernel(t_hbm, i_hbm, o_hbm):
        def body(i_vmem, o_vmem):
            pltpu.sync_copy(t_hbm.at[i_vmem.at[0]], o_vmem)

        pltpu.emit_pipeline(
            body,
            grid=(n_outer, n_inner),
            in_specs=[pl.BlockSpec((1, SC_GATHER_ROWS), index_map=lambda i, j: (0, win(i, j)))],
            out_specs=[pl.BlockSpec((SC_GATHER_ROWS, W), index_map=lambda i, j: (win(i, j), 0))],
            core_axis_name=("c", "s"),
            dimension_semantics=(pltpu.PARALLEL, pltpu.PARALLEL),
        )(i_hbm, o_hbm)

    return gather_kernel(table, idx)


def _expert_kernel(n_pieces, blk_e_ref, blk_valid_ref, *refs):
    xs_refs, (wg_ref, wu_ref, wd_ref) = refs[:n_pieces], refs[n_pieces:n_pieces + 3]
    y_refs, (wg_bf, wu_bf, wd_bf) = refs[n_pieces + 3:2 * n_pieces + 3], refs[2 * n_pieces + 3:]
    n = pl.program_id(0)
    prev = blk_e_ref[jnp.maximum(n - 1, 0)]

    @pl.when((n == 0) | (blk_e_ref[n] != prev))
    def _():
        wg_bf[...] = wg_ref[0].astype(BF16)
        wu_bf[...] = wu_ref[0].astype(BF16)
        wd_bf[...] = wd_ref[0].astype(BF16)

    valid = blk_valid_ref[n]

    @pl.when(valid == 0)
    def _():
        for y_ref in y_refs:
            y_ref[...] = jnp.zeros_like(y_ref)

    @pl.when(valid > 0)
    def _():
        sub = xs_refs[0].shape[0] // EXPERT_PARTS
        parts = [slice(p * sub, (p + 1) * sub) for p in range(EXPERT_PARTS)]
        row = lax.broadcasted_iota(jnp.int32, (sub, 1), 0)

        def load(p, rs):
            words = jnp.concatenate([r[rs, :] for r in xs_refs], axis=1)
            words = jnp.where(row + p * sub < valid, words, jnp.zeros_like(words))
            return _unpack_bf16_pairs(words).astype(BF16)

        xb = [load(p, rs) for p, rs in enumerate(parts)]
        gate = [jnp.dot(v, wg_bf[...], preferred_element_type=F32) for v in xb]
        up = [jnp.dot(v, wu_bf[...], preferred_element_type=F32) for v in xb]
        hid = [(g * jax.nn.sigmoid(g) * u).astype(BF16) for g, u in zip(gate, up)]
        for rs, hv in zip(parts, hid):
            words = _pack_bf16_pairs(jnp.dot(hv, wd_bf[...], preferred_element_type=F32))
            for c, y_ref in enumerate(y_refs):
                y_ref[rs, :] = words[:, c * SC_GATHER_WORDS:(c + 1) * SC_GATHER_WORDS]


def _experts(blk_e, blk_valid, xs_pieces, w_gate, w_up, w_down):
    cap = xs_pieces[0].shape[0]
    _, D, DE = w_gate.shape
    n_blk = cap // EXPERT_ROWS
    n_pieces = len(xs_pieces)
    piece = pl.BlockSpec((EXPERT_ROWS, SC_GATHER_WORDS), lambda n, e, v: (n, 0))
    return pl.pallas_call(
        functools.partial(_expert_kernel, n_pieces),
        grid_spec=pltpu.PrefetchScalarGridSpec(
            num_scalar_prefetch=2,
            grid=(n_blk,),
            in_specs=[piece] * n_pieces
            + [pl.BlockSpec((1, D, DE), lambda n, e, v: (e[n], 0, 0)),
               pl.BlockSpec((1, D, DE), lambda n, e, v: (e[n], 0, 0)),
               pl.BlockSpec((1, DE, D), lambda n, e, v: (e[n], 0, 0))],
            out_specs=[piece] * n_pieces,
            scratch_shapes=[pltpu.VMEM((D, DE), BF16), pltpu.VMEM((D, DE), BF16), pltpu.VMEM((DE, D), BF16)]),
        out_shape=[jax.ShapeDtypeStruct((cap, SC_GATHER_WORDS), jnp.uint32)] * n_pieces,
        compiler_params=_params("arbitrary"),
        name="moe_experts",
    )(blk_e, blk_valid, *xs_pieces, w_gate, w_up, w_down)


def _combine_dense_kernel(final_norm, n_pieces, *refs):
    yg_refs, (x_ref, info_ref, g_ref, o_ref) = refs[:n_pieces], refs[n_pieces:]
    info = info_ref[...]
    w0 = info[:, INFO_W0:INFO_W0 + 1]
    w1 = info[:, INFO_W1:INFO_W1 + 1]
    y = [_unpack_bf16_pairs(jnp.concatenate([r[slot] for r in yg_refs], axis=1)) for slot in range(TOPK_IN_GROUP)]
    out = x_ref[...] + (w0 * y[0] + w1 * y[1])
    o_ref[...] = _rms(out, g_ref[...]) if final_norm else out


def _combine_dense(yg_pieces, x2d, info2d, g_final, final_norm):
    T, D = x2d.shape
    rows = min(MOVE_ROWS, T)
    n_pieces = len(yg_pieces)
    return pl.pallas_call(
        functools.partial(_combine_dense_kernel, final_norm, n_pieces),
        grid=(T // rows,),
        in_specs=[pl.BlockSpec((TOPK_IN_GROUP, rows, SC_GATHER_WORDS), lambda i: (0, i, 0))] * n_pieces
        + [pl.BlockSpec((rows, D), lambda i: (i, 0)), pl.BlockSpec((rows, LANES), lambda i: (i, 0)),
           _full((1, D))],
        out_specs=pl.BlockSpec((rows, D), lambda i: (i, 0)),
        out_shape=jax.ShapeDtypeStruct((T, D), F32),
        compiler_params=_params("parallel"),
        name="moe_combine_dense_final_norm",
    )(*yg_pieces, x2d, info2d, g_final.reshape(1, D).astype(F32))


def kernel(x, mem, g_mem, rel_bias, g_mix, w_in, w_alpha_up, b_alpha, g_gla_head, w_proj_gla, w_proj_moba,
           w_out, g_cross, w_cq, w_ckv, w_co, g_moe, w_router_group, b_router_group, w_router_expert,
           b_router_expert, w_exp_gate, w_exp_up, w_exp_down, g_final):
    B, S, D = x.shape
    T = B * S
    depth = g_mix.shape[0]
    n_assign = T * TOPK_IN_GROUP
    n_blk = -(-(n_assign + N_EXPERTS * (EXPERT_ROWS - 1)) // EXPERT_ROWS)
    cap = n_blk * EXPERT_ROWS

    mem_bias = _moba_bias(rel_bias)
    for l in range(depth):
        qk, v_g, r_g, la, q_m, k_m, v_m, z_g, z_m = _project(x.reshape(T, D), g_mix[l], w_in[l], w_alpha_up[l],
                                                             b_alpha[l])
        o_g = _gla(qk, la, v_g, r_g, g_gla_head[l], B, S)
        o_m = _moba(q_m, k_m, v_m, mem_bias, rel_bias, B, S)
        mem_k, mem_v = _mem_kv(mem, g_mem, w_ckv[l])
        x2, info, counts, *hm_pieces = _mix(x, o_g, o_m, z_g, z_m, mem_k, mem_v, g_cross[l], g_moe[l],
                                            w_proj_gla[l], w_proj_moba[l], w_out[l], w_cq[l], w_co[l],
                                            w_router_group[l], b_router_group[l], w_router_expert[l],
                                            b_router_expert[l])
        x2d, info2d = x2.reshape(T, D), info.reshape(T, LANES)
        dest, blk = _plan(counts, info2d, n_blk)
        idx = dest[0:TOPK_IN_GROUP].reshape(1, TOPK_IN_GROUP * T)
        xs_pieces = [_sc_scatter_rows(h.reshape(T, SC_GATHER_WORDS), idx[:, :T], idx[:, T:], cap)
                     for h in hm_pieces]
        y_pieces = _experts(blk[:n_blk, BLK_EXPERT], blk[:n_blk, BLK_VALID], xs_pieces, w_exp_gate[l],
                            w_exp_up[l], w_exp_down[l])
        yg = [_sc_gather_rows(y, idx).reshape(TOPK_IN_GROUP, T, SC_GATHER_WORDS) for y in y_pieces]
        x = _combine_dense(yg, x2d, info2d, g_final, final_norm=(l == depth - 1)).reshape(B, S, D)
    return x
```

```python
import functools
import math

import jax
import jax.numpy as jnp
from jax import lax
from jax.experimental import pallas as pl
from jax.experimental.pallas import tpu as pltpu
from jax.experimental.pallas import tpu_sc as plsc

F32 = jnp.float32
BF16 = jnp.bfloat16
NEG_INF = float("-inf")

EPS = 1e-6
GLA_HEADS, GLA_DK, GLA_DV, GLA_LOWRANK, GLA_TAU, GLA_CHUNK = 4, 64, 128, 16, 16.0, 64
GLA_QK, GLA_V = GLA_HEADS * GLA_DK, GLA_HEADS * GLA_DV
MOBA_HEADS, MOBA_DH, MOBA_BLOCK, MOBA_TOPK = 8, 64, 256, 3
MOBA_W = MOBA_HEADS * MOBA_DH
LOG2E = math.log2(math.e)
MOBA_Q_SCALE = MOBA_DH ** -0.5 * LOG2E
REL_BUCKETS, REL_MAX_DIST = 32, 128
MEM_HEADS, MEM_DH = 4, 128
MEM_W = MEM_HEADS * MEM_DH
N_GROUPS, EXPERTS_PER_GROUP, TOPK_IN_GROUP = 4, 8, 2
N_EXPERTS = N_GROUPS * EXPERTS_PER_GROUP

LANES = 128
SUBLANES = 8
VMEM_LIMIT_BYTES = 56 * 1024 * 1024

PROJ_ROWS = 512
GLA_ROWS = 512
MOBA_GROUP = 4
MIX_ROWS = 512
MIX_PARTS = 2
EXPERT_ROWS = 512
MOVE_ROWS = 512
PLAN_ROWS = 2048
SC_GATHER_ROWS = 128
SC_GATHER_WORDS = 256
EXPERT_PARTS = 2


def _params(*semantics):
    return pltpu.CompilerParams(dimension_semantics=semantics, vmem_limit_bytes=VMEM_LIMIT_BYTES)


def _full(shape):
    return pl.BlockSpec(shape, lambda *_: (0,) * len(shape))


def _rms(x, g):
    return x * lax.rsqrt(jnp.mean(x * x, axis=-1, keepdims=True) + EPS) * g


def _pack_bf16_pairs(x):
    n = x.shape[1] // 2
    bits = pltpu.bitcast(x.astype(BF16).astype(F32), jnp.uint32)
    return bits[:, n:] | (bits[:, :n] >> 16)


def _unpack_bf16_pairs(w):
    lo = pltpu.bitcast(w << 16, F32)
    hi = pltpu.bitcast(w & jnp.uint32(0xFFFF0000), F32)
    return jnp.concatenate([lo, hi], axis=1)


def _proj_kernel(x_ref, g_ref, w_qk, w_v, w_r, w_a, w_up, b_a, w_qm, w_km, w_vm, w_zg, w_zm,
                 o_qk, o_v, o_r, o_la, o_qm, o_km, o_vm, o_zg, o_zm):
    h = _rms(x_ref[...], g_ref[...]).astype(BF16)

    def mm(w_ref):
        return jnp.dot(h, w_ref[...], preferred_element_type=F32)

    o_qk[...] = mm(w_qk)
    o_v[...] = mm(w_v).astype(BF16)
    o_r[...] = mm(w_r)
    a_lr = mm(w_a).astype(BF16)
    pre = jnp.dot(a_lr, w_up[...], preferred_element_type=F32) + b_a[...]
    o_la[...] = jax.nn.log_sigmoid(pre) * (1.0 / GLA_TAU)

    def mm_t(wt_ref):
        return lax.dot_general(wt_ref[...], h, (((1,), (1,)), ((), ())), preferred_element_type=F32)

    def store_blocks(o_ref, val_t):
        for c in range(o_ref.shape[0]):
            o_ref[c] = val_t[:, c * MOBA_BLOCK:(c + 1) * MOBA_BLOCK]

    store_blocks(o_qm, (mm_t(w_qm) * MOBA_Q_SCALE).astype(BF16))
    o_km[...] = mm(w_km).astype(BF16)
    store_blocks(o_vm, mm_t(w_vm).astype(BF16))
    o_zg[...] = mm(w_zg)
    o_zm[...] = mm(w_zm)


def _project(x2d, g_mix, w_in, w_alpha_up, b_alpha):
    T, D = x2d.shape
    rows = min(PROJ_ROWS, T)
    assert T % rows == 0
    splits = (GLA_QK, GLA_QK, GLA_V, GLA_V, GLA_LOWRANK, MOBA_W, MOBA_W, MOBA_W, D, D)
    offs = [0]
    for s in splits:
        offs.append(offs[-1] + s)
    wb = w_in.astype(BF16)
    sec = lambda i, j: wb[:, offs[i]:offs[j]]
    w_a = jnp.pad(sec(4, 5), ((0, 0), (0, LANES - GLA_LOWRANK)))
    w_up = jnp.pad(w_alpha_up.astype(BF16), ((0, LANES - GLA_LOWRANK), (0, 0)))
    weights = [sec(0, 2), sec(2, 3), sec(3, 4), w_a, w_up, b_alpha.reshape(1, GLA_QK).astype(F32),
               sec(5, 6).T, sec(6, 7), sec(7, 8).T, sec(8, 9), sec(9, 10)]
    out_defs = [(2 * GLA_QK, F32, False), (GLA_V, BF16, False), (GLA_V, F32, False), (GLA_QK, F32, False),
                (MOBA_W, BF16, True), (MOBA_W, BF16, False), (MOBA_W, BF16, True), (D, F32, False),
                (D, F32, False)]
    BS = MOBA_BLOCK
    assert rows % BS == 0
    row_spec = lambda n: pl.BlockSpec((rows, n), lambda i: (i, 0))
    blk_spec = lambda n: pl.BlockSpec((rows // BS, n, BS), lambda i: (i, 0, 0))
    return pl.pallas_call(
        _proj_kernel,
        grid=(T // rows,),
        in_specs=[row_spec(D), _full((1, D))] + [_full(w.shape) for w in weights],
        out_specs=[blk_spec(n) if t else row_spec(n) for n, _, t in out_defs],
        out_shape=[jax.ShapeDtypeStruct((T // BS, n, BS) if t else (T, n), dt) for n, dt, t in out_defs],
        compiler_params=_params("parallel"),
        name="norm_in_proj",
    )(x2d, g_mix.reshape(1, D).astype(F32), *weights)


def _gla_kernel(qk_ref, la_ref, v_ref, r_ref, g_ref, o_ref, state_ref, obuf_ref):
    C, H, DK, DV = GLA_CHUNK, GLA_HEADS, GLA_DK, GLA_DV
    rows = qk_ref.shape[1]

    @pl.when(pl.program_id(1) == 0)
    def _():
        state_ref[...] = jnp.zeros_like(state_ref)

    tri = (lax.broadcasted_iota(jnp.int32, (C, C), 0) >= lax.broadcasted_iota(jnp.int32, (C, C), 1)).astype(BF16)
    lane_head = lax.broadcasted_iota(jnp.int32, (1, H * DK), 1) // DK
    head_masks = [(lane_head == h).astype(F32) for h in range(H)]
    stack_row = lax.broadcasted_iota(jnp.int32, (H * C, C), 0) % C
    stack_col = lax.broadcasted_iota(jnp.int32, (H * C, C), 1)
    causal = stack_col <= stack_row
    same_head = (lax.broadcasted_iota(jnp.int32, (H * DV, H * DK), 0) // DV
                 == lax.broadcasted_iota(jnp.int32, (H * DV, H * DK), 1) // DK)
    scale = DK ** -0.5

    def stack(m):
        return jnp.concatenate([m * head_masks[h] for h in range(H)], axis=0).astype(BF16)

    chunks = [slice(c * C, (c + 1) * C) for c in range(rows // C)]

    def cum_log_decay(sl):
        la = la_ref[0, sl, :]
        p1 = la.astype(BF16)
        r1 = la - p1.astype(F32)
        p2 = r1.astype(BF16)
        p3 = (r1 - p2.astype(F32)).astype(BF16)
        s3 = jnp.dot(tri, jnp.concatenate([p1, p2, p3], axis=1), preferred_element_type=F32)
        w = H * DK
        return (s3[:, 0:w] + s3[:, w:2 * w]) + s3[:, 2 * w:3 * w]

    b_all = [cum_log_decay(sl) for sl in chunks]

    qe_all, ke_all, kd_all, qb_all, decay_all = [], [], [], [], []
    for sl, b in zip(chunks, b_all):
        q = qk_ref[0, sl, 0:H * DK] * scale
        k = qk_ref[0, sl, H * DK:2 * H * DK]
        b_last = b[C - 1:C, :]
        b_mid = b[C // 2 - 1:C // 2, :]
        qe_all.append(stack(q * jnp.exp(b - b_mid)))
        ke_all.append((k * jnp.exp(b_mid - b)).astype(BF16))
        kd_all.append((k * jnp.exp(b_last - b)).astype(BF16))
        qb_all.append((q * jnp.exp(b)).astype(BF16))
        decay_all.append(jnp.exp(b_last))

    att_all = [jnp.where(causal, lax.dot_general(qe, ke, (((1,), (1,)), ((), ())), preferred_element_type=F32),
                         0.0).astype(BF16) for qe, ke in zip(qe_all, ke_all)]

    o_intra_all, kv_all = [], []
    for sl, att, kd in zip(chunks, att_all, kd_all):
        v = v_ref[0, sl, :]
        o_intra_all.append(jnp.concatenate(
            [jnp.dot(att[h * C:(h + 1) * C, :], v[:, h * DV:(h + 1) * DV], preferred_element_type=F32)
             for h in range(H)], axis=1))
        kv_t = lax.dot_general(v, kd, (((0,), (0,)), ((), ())), preferred_element_type=F32)
        kv_all.append(jnp.where(same_head, kv_t, 0.0))

    state_t = state_ref[...]
    for sl, qb, decay, kv_t, o_intra in zip(chunks, qb_all, decay_all, kv_all, o_intra_all):
        o_inter = lax.dot_general(qb, state_t.astype(BF16), (((1,), (1,)), ((), ())),
                                  preferred_element_type=F32)
        obuf_ref[sl, :] = o_intra + o_inter
        state_t = decay * state_t + kv_t
    state_ref[...] = state_t

    r = r_ref[0]
    for h in range(H):
        cs = slice(h * DV, (h + 1) * DV)
        y = _rms(obuf_ref[:, cs], g_ref[:, cs])
        rh = r[:, cs]
        o_ref[0, :, cs] = (y * (rh * jax.nn.sigmoid(rh))).astype(BF16)


def _gla(qk, la, v, r, g_head, B, S):
    rows = min(GLA_ROWS, S)
    assert S % rows == 0 and rows % GLA_CHUNK == 0
    spec = lambda n: pl.BlockSpec((1, rows, n), lambda b, i: (b, i, 0))
    return pl.pallas_call(
        _gla_kernel,
        grid=(B, S // rows),
        in_specs=[spec(2 * GLA_QK), spec(GLA_QK), spec(GLA_V), spec(GLA_V), _full((1, GLA_V))],
        out_specs=spec(GLA_V),
        out_shape=jax.ShapeDtypeStruct((B, S, GLA_V), BF16),
        scratch_shapes=[pltpu.VMEM((GLA_V, GLA_QK), F32), pltpu.VMEM((rows, GLA_V), F32)],
        compiler_params=_params("parallel", "arbitrary"),
        name="gla_chunked",
    )(qk.reshape(B, S, -1), la.reshape(B, S, -1), v.reshape(B, S, -1), r.reshape(B, S, -1),
      g_head.reshape(1, GLA_V).astype(F32))


def _t5_bucket(dist):
    n = jnp.maximum(dist, 0)
    max_exact = REL_BUCKETS // 2
    nf = jnp.maximum(n, 1).astype(F32)
    large = max_exact + (jnp.log(nf / max_exact) / math.log(REL_MAX_DIST / max_exact)
                         * (REL_BUCKETS - max_exact)).astype(jnp.int32)
    large = jnp.minimum(large, REL_BUCKETS - 1)
    return jnp.where(n < max_exact, n, large)


def _moba_bias_kernel(rb_ref, o_ref):
    BS, G = MOBA_BLOCK, MOBA_GROUP
    grp, kind = pl.program_id(0), pl.program_id(1)
    d = (lax.broadcasted_iota(jnp.int32, (BS, BS), 1) - lax.broadcasted_iota(jnp.int32, (BS, BS), 0)
         + kind * BS)
    bucket = _t5_bucket(d)
    for h in range(G):
        val = jnp.zeros((BS, BS), F32)
        for bkt in range(REL_BUCKETS):
            val = jnp.where(bucket == bkt, rb_ref[bkt, grp * G + h] * LOG2E, val)
        o_ref[0, 0, :, h * BS:(h + 1) * BS] = jnp.where(d >= 0, val, NEG_INF)


def _moba_bias(rel_bias):
    BS, G = MOBA_BLOCK, MOBA_GROUP
    n_grp = MOBA_HEADS // G
    return pl.pallas_call(
        _moba_bias_kernel,
        grid=(n_grp, 2),
        in_specs=[pl.BlockSpec(memory_space=pltpu.SMEM)],
        out_specs=pl.BlockSpec((1, 1, BS, G * BS), lambda g, k: (g, k, 0, 0)),
        out_shape=jax.ShapeDtypeStruct((n_grp, 2, BS, G * BS), F32),
        compiler_params=_params("parallel", "parallel"),
        name="moba_bias_tables",
    )(rel_bias.astype(F32))


def _moba_kernel(rb_ref, q_ref, k_ref, v_ref, bias_ref, o_ref,
                 qs_ref, kmean_ref, sel_ref, m_ref, l_ref, acc_ref, sbuf_ref):
    BS, G, DH = MOBA_BLOCK, MOBA_GROUP, MOBA_DH
    NBP = kmean_ref.shape[0]
    grp, i = pl.program_id(1), pl.program_id(2)

    @pl.when(i == 0)
    def _():
        S = k_ref.shape[1]
        blk_of_key = lax.broadcasted_iota(jnp.int32, (NBP, S), 1) // BS
        ind = (blk_of_key == lax.broadcasted_iota(jnp.int32, (NBP, S), 0)).astype(BF16)
        kmean_ref[...] = jnp.dot(ind, k_ref[0], preferred_element_type=F32) * (1.0 / BS)

    qt = q_ref[0]
    sub_head = lax.broadcasted_iota(jnp.int32, (G * DH, 1), 0) // DH
    for h in range(G):
        qs_ref[h] = jnp.where(sub_head == h, qt, jnp.zeros_like(qt))

    kmean = kmean_ref[...].astype(BF16)
    gate = jnp.concatenate([jnp.dot(kmean, qs_ref[h], preferred_element_type=F32) for h in range(G)],
                           axis=1)
    blk = lax.broadcasted_iota(jnp.int32, gate.shape, 0)
    gate = jnp.where(blk < i, gate, NEG_INF)
    for t in range(MOBA_TOPK):
        mx = jnp.max(gate, axis=0, keepdims=True)
        hit = (gate == mx) & (mx > NEG_INF)
        idx = jnp.min(jnp.where(hit, blk, NBP), axis=0, keepdims=True)
        sel_ref[t:t + 1, :] = idx
        gate = jnp.where(blk == idx, NEG_INF, gate)

    def mask_row(j):
        hit = (sel_ref[0:1, :] == j) | (sel_ref[1:2, :] == j) | (sel_ref[2:3, :] == j)
        return jnp.where(hit, 0.0, NEG_INF)

    def with_ones(vt):
        return jnp.concatenate([vt, jnp.ones((8, vt.shape[1]), BF16)], axis=0)

    def far_scores(j0, n, slot):
        kj = k_ref[0, pl.ds(pl.multiple_of(j0 * BS, BS), n * BS), :]
        for h in range(G):
            cs = slice(h * BS, (h + 1) * BS)
            sbuf_ref[slot, h, 0:n * BS, :] = jnp.dot(kj, qs_ref[h], preferred_element_type=F32)

    j_prev = jnp.maximum(i - 1, 0)
    k_own = k_ref[0, pl.ds(pl.multiple_of(i * BS, BS), BS), :]
    k_prev = k_ref[0, pl.ds(pl.multiple_of(j_prev * BS, BS), BS), :]
    vt_near = jnp.concatenate([v_ref[i], v_ref[j_prev]], axis=1)
    prev_mask = mask_row(i - 1)
    for h in range(G):
        cs = slice(h * BS, (h + 1) * BS)
        sbuf_ref[1, h, 0:BS, :] = jnp.dot(k_own, qs_ref[h], preferred_element_type=F32)
        sbuf_ref[1, h, BS:2 * BS, :] = jnp.dot(k_prev, qs_ref[h], preferred_element_type=F32)
    far_scores(0, 2, 0)
    m_out, l_out, acc_out = [], [], []
    for h in range(G):
        cs = slice(h * BS, (h + 1) * BS)
        s_own = sbuf_ref[1, h, 0:BS, :] + bias_ref[0, 0, :, cs]
        s_prev = sbuf_ref[1, h, BS:2 * BS, :] + (bias_ref[0, 1, :, cs] + prev_mask[:, cs])
        m0 = jnp.maximum(jnp.max(s_own, axis=0, keepdims=True), jnp.max(s_prev, axis=0, keepdims=True))
        pb = jnp.concatenate([jnp.exp2(s_own - m0).astype(BF16), jnp.exp2(s_prev - m0).astype(BF16)], axis=0)
        pv = jnp.dot(with_ones(vt_near[h * DH:(h + 1) * DH, :]), pb, preferred_element_type=F32)
        m_out.append(m0)
        l_out.append(pv[DH:DH + 1, :])
        acc_out.append(pv[0:DH, :])
    m_ref[...] = jnp.concatenate(m_out, axis=1)
    l_ref[...] = jnp.concatenate(l_out, axis=1)
    acc_ref[...] = jnp.concatenate(acc_out, axis=1)

    lane_head = lax.broadcasted_iota(jnp.int32, (1, G * BS), 1) // BS
    far_bias = jnp.zeros((1, G * BS), F32)
    for h in range(G):
        far_bias = jnp.where(lane_head == h, rb_ref[REL_BUCKETS - 1, grp * G + h] * LOG2E, far_bias)

    def visit_far(j0, n, slot, ahead=None):
        vjt = jnp.concatenate([v_ref[j0 + t] for t in range(n)], axis=1)
        addend = [far_bias + mask_row(j0 + t) for t in range(n)]
        m_old, l_old, acc_old = m_ref[...], l_ref[...], acc_ref[...]
        m_out, l_out, acc_out = [], [], []
        if ahead is not None:
            k_next = k_ref[0, pl.ds(pl.multiple_of(ahead[0] * BS, BS), 2 * BS), :]
        for h in range(G):
            cs = slice(h * BS, (h + 1) * BS)
            if ahead is not None:
                sbuf_ref[ahead[1], h] = jnp.dot(k_next, qs_ref[h], preferred_element_type=F32)
            s = sbuf_ref[slot, h, 0:n * BS, :]
            mx = jnp.max(s[0:BS], axis=0, keepdims=True) + addend[0][:, cs]
            for t in range(1, n):
                mx = jnp.maximum(mx, jnp.max(s[t * BS:(t + 1) * BS], axis=0, keepdims=True) + addend[t][:, cs])
            m_new = jnp.maximum(m_old[:, cs], mx)
            pb = jnp.concatenate([jnp.exp2(s[t * BS:(t + 1) * BS] - (m_new - addend[t][:, cs])).astype(BF16)
                                  for t in range(n)], axis=0)
            pv = jnp.dot(with_ones(vjt[h * DH:(h + 1) * DH, :]), pb, preferred_element_type=F32)
            alpha = jnp.exp2(m_old[:, cs] - m_new)
            m_out.append(m_new)
            l_out.append(alpha * l_old[:, cs] + pv[DH:DH + 1, :])
            acc_out.append(alpha * acc_old[:, cs] + pv[0:DH, :])
        m_ref[...] = jnp.concatenate(m_out, axis=1)
        l_ref[...] = jnp.concatenate(l_out, axis=1)
        acc_ref[...] = jnp.concatenate(acc_out, axis=1)

    n_far = jnp.maximum(i - 1, 0)
    n_pairs = lax.shift_right_logical(n_far, 1)

    def pair(p, slot, look_ahead=True):
        ahead = (2 * jnp.minimum(p + 1, n_pairs - 1), 1 - slot) if look_ahead else None
        visit_far(2 * p, 2, slot, ahead=ahead)

    def far_octet(w, carry):
        for t in range(4):
            pair(4 * w + t, t % 2)
        return carry

    lax.fori_loop(0, lax.shift_right_logical(n_pairs, 2), far_octet, 0)

    @pl.when((n_pairs & 2) != 0)
    def _():
        base = 4 * lax.shift_right_logical(n_pairs, 2)
        pair(base, 0)
        pair(base + 1, 1)

    @pl.when((n_pairs & 1) != 0)
    def _():
        pair(n_pairs - 1, 0, look_ahead=False)

    @pl.when(n_far % 2 == 1)
    def _():
        far_scores(n_far - 1, 1, 1)
        visit_far(n_far - 1, 1, 1)

    out_t = jnp.concatenate([acc_ref[:, h * BS:(h + 1) * BS] / l_ref[:, h * BS:(h + 1) * BS] for h in range(G)],
                            axis=0)
    o_ref[0] = out_t.T.astype(o_ref.dtype)


def _moba(qm_t, km, vm_t, bias, rel_bias, B, S):
    BS, G, DH = MOBA_BLOCK, MOBA_GROUP, MOBA_DH
    W = G * DH
    n_grp = MOBA_HEADS // G
    assert S % BS == 0 and S >= 2 * BS
    NB = S // BS
    NBP = -(-NB // 8) * 8
    return pl.pallas_call(
        _moba_kernel,
        grid=(B, n_grp, NB),
        in_specs=[pl.BlockSpec(memory_space=pltpu.SMEM),
                  pl.BlockSpec((1, W, BS), lambda b, g, i: (b * NB + i, g, 0)),
                  pl.BlockSpec((1, S, W), lambda b, g, i: (b, 0, g)),
                  pl.BlockSpec((NB, W, BS), lambda b, g, i: (b, g, 0)),
                  pl.BlockSpec((1, 2, BS, G * BS), lambda b, g, i: (g, 0, 0, 0))],
        out_specs=pl.BlockSpec((1, BS, W), lambda b, g, i: (b, i, g)),
        out_shape=jax.ShapeDtypeStruct((B, S, MOBA_W), BF16),
        scratch_shapes=[pltpu.VMEM((G, W, BS), BF16), pltpu.VMEM((NBP, W), F32),
                        pltpu.VMEM((8, G * BS), jnp.int32),
                        pltpu.VMEM((1, G * BS), F32), pltpu.VMEM((1, G * BS), F32),
                        pltpu.VMEM((DH, G * BS), F32), pltpu.VMEM((2, G, 2 * BS, BS), F32)],
        compiler_params=_params("parallel", "parallel", "arbitrary"),
        name="moba_attention",
    )(rel_bias.astype(F32), qm_t, km.reshape(B, S, -1), vm_t, bias)


def _mem_kv_kernel(mem_ref, g_ref, w_ref, k_ref, v_ref):
    kv = jnp.dot(_rms(mem_ref[0], g_ref[...]).astype(BF16), w_ref[...], preferred_element_type=F32)
    k_ref[0] = kv[:, :MEM_W].astype(BF16)
    v_ref[0] = kv[:, MEM_W:].astype(BF16)


def _mem_kv(mem, g_mem, w_ckv):
    B, M, D = mem.shape
    spec = pl.BlockSpec((1, M, MEM_W), lambda b: (b, 0, 0))
    return pl.pallas_call(
        _mem_kv_kernel,
        grid=(B,),
        in_specs=[pl.BlockSpec((1, M, D), lambda b: (b, 0, 0)), _full((1, D)), _full((D, 2 * MEM_W))],
        out_specs=[spec, spec],
        out_shape=[jax.ShapeDtypeStruct((B, M, MEM_W), BF16)] * 2,
        compiler_params=_params("parallel"),
        name="memory_kv",
    )(mem, g_mem.reshape(1, D).astype(F32), w_ckv.astype(BF16))


INFO_W0, INFO_W1, INFO_E0, INFO_E1, INFO_R0, INFO_R1 = range(6)
BLK_EXPERT, BLK_VALID = range(2)
ROUTER_GROUP_LANE0, ROUTER_EXPERT_LANE0 = 0, N_GROUPS


def _mix_kernel(n_pieces, x_ref, og_ref, om_ref, zg_ref, zm_ref, mk_ref, mv_ref, gc_ref, gm_ref,
                wpg, wpm, wout, wcq, wco, wr, br,
                x2_ref, info_ref, cnt_ref, *rest):
    hp_refs, base_ref = rest[:n_pieces], rest[n_pieces]
    first = (pl.program_id(0) == 0) & (pl.program_id(1) == 0)

    @pl.when(first)
    def _():
        base_ref[...] = jnp.zeros_like(base_ref)

    def mm(a, w_ref):
        return jnp.dot(a.astype(BF16), w_ref[...], preferred_element_type=F32)

    n_rows = x_ref.shape[1]
    sub = n_rows // MIX_PARTS
    parts = [slice(p * sub, (p + 1) * sub) for p in range(MIX_PARTS)]

    merged = [jax.nn.sigmoid(zg_ref[0, rs, :]) * mm(og_ref[0, rs, :], wpg)
              + jax.nn.sigmoid(zm_ref[0, rs, :]) * mm(om_ref[0, rs, :], wpm) for rs in parts]
    x1 = [x_ref[0, rs, :] + mm(m, wout) for rs, m in zip(parts, merged)]

    qc = [mm(_rms(v, gc_ref[...]), wcq).astype(BF16) for v in x1]

    def mem_attention(q):
        heads = []
        for h in range(MEM_HEADS):
            cs = slice(h * MEM_DH, (h + 1) * MEM_DH)
            s = lax.dot_general(q[:, cs], mk_ref[0, :, cs], (((1,), (1,)), ((), ())),
                                preferred_element_type=F32) * (MEM_DH ** -0.5)
            p = jnp.exp(s - jnp.max(s, axis=-1, keepdims=True))
            o = jnp.dot(p.astype(BF16), mv_ref[0, :, cs], preferred_element_type=F32)
            heads.append(o / jnp.sum(p, axis=-1, keepdims=True))
        return jnp.concatenate(heads, axis=-1)

    attn = [mem_attention(q) for q in qc]
    x2 = [v + mm(a, wco) for v, a in zip(x1, attn)]
    for rs, v in zip(parts, x2):
        x2_ref[0, rs, :] = v

    hm = [_rms(v, gm_ref[...]) for v in x2]
    logits = [lax.dot_general(wr[...], h.astype(BF16), (((1,), (1,)), ((), ())), preferred_element_type=F32)
              + br[...] for h in hm]
    for rs, h in zip(parts, hm):
        words = _pack_bf16_pairs(h)
        for c, hp_ref in enumerate(hp_refs):
            hp_ref[0, rs, :] = words[:, c * SC_GATHER_WORDS:(c + 1) * SC_GATHER_WORDS]
    n_log = -(-(N_GROUPS + N_EXPERTS) // SUBLANES) * SUBLANES
    row_id = lax.broadcasted_iota(jnp.int32, (n_log, sub), 0)
    is_grp = row_id < N_GROUPS
    e_id = row_id - ROUTER_EXPERT_LANE0

    def route(lg):
        gl = jnp.where(is_grp, lg, NEG_INF)
        ge = jnp.exp(gl - jnp.max(gl, axis=0, keepdims=True))
        g_prob = ge / jnp.sum(ge, axis=0, keepdims=True)
        p_grp = jnp.max(g_prob, axis=0, keepdims=True)
        grp = jnp.min(jnp.where((g_prob == p_grp) & is_grp, row_id, LANES), axis=0, keepdims=True)
        in_grp = (e_id >= grp * EXPERTS_PER_GROUP) & (e_id < (grp + 1) * EXPERTS_PER_GROUP)
        el = jnp.where(in_grp, lg, NEG_INF)
        ee = jnp.exp(el - jnp.max(el, axis=0, keepdims=True))
        e_prob = jnp.where(in_grp, ee / jnp.sum(ee, axis=0, keepdims=True), -1.0)
        p0 = jnp.max(e_prob, axis=0, keepdims=True)
        e0 = jnp.min(jnp.where(e_prob == p0, e_id, LANES), axis=0, keepdims=True)
        e_rest = jnp.where(e_id == e0, -1.0, e_prob)
        p1 = jnp.max(e_rest, axis=0, keepdims=True)
        e1 = jnp.min(jnp.where(e_rest == p1, e_id, LANES), axis=0, keepdims=True)
        return e0, e1, p_grp * p0 / (p0 + p1), p_grp * p1 / (p0 + p1)

    routed = [route(lg[0:n_log, :]) for lg in logits]

    expert = lax.broadcasted_iota(jnp.int32, (LANES, sub), 0)
    before = (lax.broadcasted_iota(jnp.int32, (sub, sub), 0)
              < lax.broadcasted_iota(jnp.int32, (sub, sub), 1)).astype(BF16)
    field = lax.broadcasted_iota(jnp.int32, (LANES, sub), 0)
    base = base_ref[...]
    for rs, (e0, e1, w0, w1) in zip(parts, routed):
        onehot = ((expert == e0) | (expert == e1)).astype(F32)
        seen = base + jnp.dot(onehot.astype(BF16), before, preferred_element_type=F32)
        r0 = jnp.sum(jnp.where(expert == e0, seen, 0.0), axis=0, keepdims=True)
        r1 = jnp.sum(jnp.where(expert == e1, seen, 0.0), axis=0, keepdims=True)
        base = base + jnp.sum(onehot, axis=1, keepdims=True)
        info_t = jnp.zeros((LANES, sub), F32)
        for ln, val in ((INFO_W0, w0), (INFO_W1, w1), (INFO_E0, e0.astype(F32)), (INFO_E1, e1.astype(F32)),
                        (INFO_R0, r0), (INFO_R1, r1)):
            info_t = jnp.where(field == ln, val, info_t)
        info_ref[0, rs, :] = info_t.T
    base_ref[...] = base
    cnt_ref[...] = jnp.broadcast_to(base, (LANES, LANES)).T[0:1, :]


def _mix(x, o_g, o_m, z_g, z_m, mem_k, mem_v, g_cross, g_moe, w_proj_gla, w_proj_moba, w_out, w_cq, w_co,
         w_rg, b_rg, w_re, b_re):
    B, S, D = x.shape
    M = mem_k.shape[1]
    rows = min(MIX_ROWS, S)
    assert S % rows == 0
    pad = LANES - N_GROUPS - N_EXPERTS
    wr = jnp.pad(jnp.concatenate([w_rg, w_re], axis=1), ((0, 0), (0, pad))).astype(BF16).T
    br = jnp.pad(jnp.concatenate([b_rg, b_re]), (0, pad)).reshape(LANES, 1).astype(F32)
    weights = [w_proj_gla.astype(BF16), w_proj_moba.astype(BF16), w_out.astype(BF16), w_cq.astype(BF16),
               w_co.astype(BF16), wr, br]
    tile = lambda n: pl.BlockSpec((1, rows, n), lambda b, i: (b, i, 0))
    memspec = pl.BlockSpec((1, M, MEM_W), lambda b, i: (b, 0, 0))
    n_pieces = D // 2 // SC_GATHER_WORDS
    return pl.pallas_call(
        functools.partial(_mix_kernel, n_pieces),
        grid=(B, S // rows),
        in_specs=[tile(D), tile(GLA_V), tile(MOBA_W), tile(D), tile(D), memspec, memspec,
                  _full((1, D)), _full((1, D))] + [_full(w.shape) for w in weights],
        out_specs=[tile(D), tile(LANES), _full((1, LANES))] + [tile(SC_GATHER_WORDS)] * n_pieces,
        out_shape=[jax.ShapeDtypeStruct((B, S, D), F32), jax.ShapeDtypeStruct((B, S, LANES), F32),
                   jax.ShapeDtypeStruct((1, LANES), F32)]
        + [jax.ShapeDtypeStruct((B, S, SC_GATHER_WORDS), jnp.uint32)] * n_pieces,
        scratch_shapes=[pltpu.VMEM((LANES, 1), F32)],
        compiler_params=_params("arbitrary", "arbitrary"),
        name="merge_memattn_router",
    )(x, o_g, o_m, z_g.reshape(B, S, D), z_m.reshape(B, S, D), mem_k, mem_v,
      g_cross.reshape(1, D).astype(F32), g_moe.reshape(1, D).astype(F32), *weights)


def _plan_kernel(cnt_ref, info_ref, dest_ref, blk_ref):
    rows = info_ref.shape[0]
    lane1 = lax.broadcasted_iota(jnp.int32, (1, LANES), 1)
    nblk = jnp.floor((cnt_ref[...] + (EXPERT_ROWS - 1)) * (1.0 / EXPERT_ROWS))
    nblk = jnp.where(lane1 < N_EXPERTS, nblk, 0.0)
    hi = jnp.floor(nblk * (1.0 / 256.0))
    lo = nblk - 256.0 * hi
    upto = (lax.broadcasted_iota(jnp.int32, (LANES, LANES), 0)
            <= lax.broadcasted_iota(jnp.int32, (LANES, LANES), 1)).astype(BF16)
    digits = jnp.concatenate([jnp.broadcast_to(hi, (8, LANES)), jnp.broadcast_to(lo, (8, LANES))], axis=0)
    sums = jnp.dot(digits.astype(BF16), upto, preferred_element_type=F32)
    pend = sums[0:1] * 256.0 + sums[8:9]
    pstart_rows = (pend - nblk) * EXPERT_ROWS

    info = info_ref[...]
    lane = lax.broadcasted_iota(jnp.int32, (rows, LANES), 1)

    def field(ln):
        return jnp.sum(jnp.where(lane == ln, info, 0.0), axis=-1, keepdims=True)

    def dest(e, r):
        return jnp.sum(jnp.where(lane == e.astype(jnp.int32), pstart_rows, 0.0), axis=-1, keepdims=True) + r

    d0 = dest(field(INFO_E0), field(INFO_R0))
    d1 = dest(field(INFO_E1), field(INFO_R1))
    cols = jnp.where(lane == 0, d0, jnp.where(lane == 1, d1, 0.0))
    dest_ref[...] = cols.T[0:SUBLANES, :].astype(jnp.int32)

    @pl.when(pl.program_id(0) == 0)
    def _():
        n = lax.broadcasted_iota(jnp.int32, (blk_ref.shape[0], LANES), 0).astype(F32)
        blane = lax.broadcasted_iota(jnp.int32, (blk_ref.shape[0], LANES), 1)
        done = jnp.where((pend <= n) & (lane1 < N_EXPERTS), 1.0, 0.0)
        e = jnp.minimum(jnp.sum(done, axis=-1, keepdims=True), N_EXPERTS - 1.0)
        mine = blane == e.astype(jnp.int32)
        first_blk = jnp.sum(jnp.where(mine, pend - nblk, 0.0), axis=-1, keepdims=True)
        count = jnp.sum(jnp.where(mine, cnt_ref[...], 0.0), axis=-1, keepdims=True)
        valid = jnp.clip(count - EXPERT_ROWS * (n[:, 0:1] - first_blk), 0.0, float(EXPERT_ROWS))
        blk_ref[...] = jnp.where(blane == BLK_EXPERT, e, jnp.where(blane == BLK_VALID, valid, 0.0)).astype(jnp.int32)


def _plan(counts, info2d, n_blk):
    T = info2d.shape[0]
    rows = min(PLAN_ROWS, T)
    assert T % rows == 0
    n_blk_pad = -(-n_blk // SUBLANES) * SUBLANES
    return pl.pallas_call(
        _plan_kernel,
        grid=(T // rows,),
        in_specs=[_full((1, LANES)), pl.BlockSpec((rows, LANES), lambda i: (i, 0))],
        out_specs=[pl.BlockSpec((SUBLANES, rows), lambda i: (0, i)), _full((n_blk_pad, LANES))],
        out_shape=[jax.ShapeDtypeStruct((SUBLANES, T), jnp.int32),
                   jax.ShapeDtypeStruct((n_blk_pad, LANES), jnp.int32)],
        compiler_params=_params("arbitrary"),
        name="dispatch_plan",
    )(counts, info2d)


def _sc_windows(n_rows):
    n_inner = 32
    assert n_rows % (SC_GATHER_ROWS * n_inner) == 0
    return n_rows // (SC_GATHER_ROWS * n_inner), n_inner


def _sc_mesh():
    return plsc.VectorSubcoreMesh(core_axis_name="c", subcore_axis_name="s")


def _sc_scatter_rows(src, idx_a, idx_b, n_out):
    T, W = src.shape
    n_outer, n_inner = _sc_windows(T)
    win = lambda i, j: i * n_inner + j

    @pl.kernel(out_type=jax.ShapeDtypeStruct((n_out, W), src.dtype), mesh=_sc_mesh(), scratch_types=[])
    def scatter_kernel(s_hbm, a_hbm, b_hbm, o_hbm):
        def body(s_vmem, a_vmem, b_vmem):
            pltpu.sync_copy(s_vmem, o_hbm.at[a_vmem.at[0]])
            pltpu.sync_copy(s_vmem, o_hbm.at[b_vmem.at[0]])

        pltpu.emit_pipeline(
            body,
            grid=(n_outer, n_inner),
            in_specs=[pl.BlockSpec((SC_GATHER_ROWS, W), index_map=lambda i, j: (win(i, j), 0)),
                      pl.BlockSpec((1, SC_GATHER_ROWS), index_map=lambda i, j: (0, win(i, j))),
                      pl.BlockSpec((1, SC_GATHER_ROWS), index_map=lambda i, j: (0, win(i, j)))],
            out_specs=[],
            core_axis_name=("c", "s"),
            dimension_semantics=(pltpu.PARALLEL, pltpu.PARALLEL),
        )(s_hbm, a_hbm, b_hbm)

    return scatter_kernel(src, idx_a, idx_b)


def _sc_gather_rows(table, idx):
    M = idx.shape[1]
    W = table.shape[1]
    n_outer, n_inner = _sc_windows(M)
    win = lambda i, j: i * n_inner + j

    @pl.kernel(out_type=jax.ShapeDtypeStruct((M, W), table.dtype), mesh=_sc_mesh(), scratch_types=[])
    def gather_kernel(t_hbm, i_hbm, o_hbm):
        def body(i_vmem, o_vmem):
            pltpu.sync_copy(t_hbm.at[i_vmem.at[0]], o_vmem)

        pltpu.emit_pipeline(
            body,
            grid=(n_outer, n_inner),
            in_specs=[pl.BlockSpec((1, SC_GATHER_ROWS), index_map=lambda i, j: (0, win(i, j)))],
            out_specs=[pl.BlockSpec((SC_GATHER_ROWS, W), index_map=lambda i, j: (win(i, j), 0))],
            core_axis_name=("c", "s"),
            dimension_semantics=(pltpu.PARALLEL, pltpu.PARALLEL),
        )(i_hbm, o_hbm)

    return gather_kernel(table, idx)


def _expert_kernel(n_pieces, blk_e_ref, blk_valid_ref, *refs):
    xs_refs, (wg_ref, wu_ref, wd_ref) = refs[:n_pieces], refs[n_pieces:n_pieces + 3]
    y_refs, (wg_bf, wu_bf, wd_bf) = refs[n_pieces + 3:2 * n_pieces + 3], refs[2 * n_pieces + 3:]
    n = pl.program_id(0)
    prev = blk_e_ref[jnp.maximum(n - 1, 0)]

    @pl.when((n == 0) | (blk_e_ref[n] != prev))
    def _():
        wg_bf[...] = wg_ref[0].astype(BF16)
        wu_bf[...] = wu_ref[0].astype(BF16)
        wd_bf[...] = wd_ref[0].astype(BF16)

    valid = blk_valid_ref[n]

    @pl.when(valid == 0)
    def _():
        for y_ref in y_refs:
            y_ref[...] = jnp.zeros_like(y_ref)

    @pl.when(valid > 0)
    def _():
        sub = xs_refs[0].shape[0] // EXPERT_PARTS
        parts = [slice(p * sub, (p + 1) * sub) for p in range(EXPERT_PARTS)]
        row = lax.broadcasted_iota(jnp.int32, (sub, 1), 0)

        def load(p, rs):
            words = jnp.concatenate([r[rs, :] for r in xs_refs], axis=1)
            words = jnp.where(row + p * sub < valid, words, jnp.zeros_like(words))
            return _unpack_bf16_pairs(words).astype(BF16)

        xb = [load(p, rs) for p, rs in enumerate(parts)]
        gate = [jnp.dot(v, wg_bf[...], preferred_element_type=F32) for v in xb]
        up = [jnp.dot(v, wu_bf[...], preferred_element_type=F32) for v in xb]
        hid = [(g * jax.nn.sigmoid(g) * u).astype(BF16) for g, u in zip(gate, up)]
        for rs, hv in zip(parts, hid):
            words = _pack_bf16_pairs(jnp.dot(hv, wd_bf[...], preferred_element_type=F32))
            for c, y_ref in enumerate(y_refs):
                y_ref[rs, :] = words[:, c * SC_GATHER_WORDS:(c + 1) * SC_GATHER_WORDS]


def _experts(blk_e, blk_valid, xs_pieces, w_gate, w_up, w_down):
    cap = xs_pieces[0].shape[0]
    _, D, DE = w_gate.shape
    n_blk = cap // EXPERT_ROWS
    n_pieces = len(xs_pieces)
    piece = pl.BlockSpec((EXPERT_ROWS, SC_GATHER_WORDS), lambda n, e, v: (n, 0))
    return pl.pallas_call(
        functools.partial(_expert_kernel, n_pieces),
        grid_spec=pltpu.PrefetchScalarGridSpec(
            num_scalar_prefetch=2,
            grid=(n_blk,),
            in_specs=[piece] * n_pieces
            + [pl.BlockSpec((1, D, DE), lambda n, e, v: (e[n], 0, 0)),
               pl.BlockSpec((1, D, DE), lambda n, e, v: (e[n], 0, 0)),
               pl.BlockSpec((1, DE, D), lambda n, e, v: (e[n], 0, 0))],
            out_specs=[piece] * n_pieces,
            scratch_shapes=[pltpu.VMEM((D, DE), BF16), pltpu.VMEM((D, DE), BF16), pltpu.VMEM((DE, D), BF16)]),
        out_shape=[jax.ShapeDtypeStruct((cap, SC_GATHER_WORDS), jnp.uint32)] * n_pieces,
        compiler_params=_params("arbitrary"),
        name="moe_experts",
    )(blk_e, blk_valid, *xs_pieces, w_gate, w_up, w_down)


def _combine_dense_kernel(final_norm, n_pieces, *refs):
    yg_refs, (x_ref, info_ref, g_ref, o_ref) = refs[:n_pieces], refs[n_pieces:]
    info = info_ref[...]
    w0 = info[:, INFO_W0:INFO_W0 + 1]
    w1 = info[:, INFO_W1:INFO_W1 + 1]
    y = [_unpack_bf16_pairs(jnp.concatenate([r[slot] for r in yg_refs], axis=1)) for slot in range(TOPK_IN_GROUP)]
    out = x_ref[...] + (w0 * y[0] + w1 * y[1])
    o_ref[...] = _rms(out, g_ref[...]) if final_norm else out


def _combine_dense(yg_pieces, x2d, info2d, g_final, final_norm):
    T, D = x2d.shape
    rows = min(MOVE_ROWS, T)
    n_pieces = len(yg_pieces)
    return pl.pallas_call(
        functools.partial(_combine_dense_kernel, final_norm, n_pieces),
        grid=(T // rows,),
        in_specs=[pl.BlockSpec((TOPK_IN_GROUP, rows, SC_GATHER_WORDS), lambda i: (0, i, 0))] * n_pieces
        + [pl.BlockSpec((rows, D), lambda i: (i, 0)), pl.BlockSpec((rows, LANES), lambda i: (i, 0)),
           _full((1, D))],
        out_specs=pl.BlockSpec((rows, D), lambda i: (i, 0)),
        out_shape=jax.ShapeDtypeStruct((T, D), F32),
        compiler_params=_params("parallel"),
        name="moe_combine_dense_final_norm",
    )(*yg_pieces, x2d, info2d, g_final.reshape(1, D).astype(F32))


def kernel(x, mem, g_mem, rel_bias, g_mix, w_in, w_alpha_up, b_alpha, g_gla_head, w_proj_gla, w_proj_moba,
           w_out, g_cross, w_cq, w_ckv, w_co, g_moe, w_router_group, b_router_group, w_router_expert,
           b_router_expert, w_exp_gate, w_exp_up, w_exp_down, g_final):
    B, S, D = x.shape
    T = B * S
    depth = g_mix.shape[0]
    n_assign = T * TOPK_IN_GROUP
    n_blk = -(-(n_assign + N_EXPERTS * (EXPERT_ROWS - 1)) // EXPERT_ROWS)
    cap = n_blk * EXPERT_ROWS

    mem_bias = _moba_bias(rel_bias)
    for l in range(depth):
        qk, v_g, r_g, la, q_m, k_m, v_m, z_g, z_m = _project(x.reshape(T, D), g_mix[l], w_in[l], w_alpha_up[l],
                                                             b_alpha[l])
        o_g = _gla(qk, la, v_g, r_g, g_gla_head[l], B, S)
        o_m = _moba(q_m, k_m, v_m, mem_bias, rel_bias, B, S)
        mem_k, mem_v = _mem_kv(mem, g_mem, w_ckv[l])
        x2, info, counts, *hm_pieces = _mix(x, o_g, o_m, z_g, z_m, mem_k, mem_v, g_cross[l], g_moe[l],
                                            w_proj_gla[l], w_proj_moba[l], w_out[l], w_cq[l], w_co[l],
                                            w_router_group[l], b_router_group[l], w_router_expert[l],
                                            b_router_expert[l])
        x2d, info2d = x2.reshape(T, D), info.reshape(T, LANES)
        dest, blk = _plan(counts, info2d, n_blk)
        idx = dest[0:TOPK_IN_GROUP].reshape(1, TOPK_IN_GROUP * T)
        xs_pieces = [_sc_scatter_rows(h.reshape(T, SC_GATHER_WORDS), idx[:, :T], idx[:, T:], cap)
                     for h in hm_pieces]
        y_pieces = _experts(blk[:n_blk, BLK_EXPERT], blk[:n_blk, BLK_VALID], xs_pieces, w_exp_gate[l],
                            w_exp_up[l], w_exp_down[l])
        yg = [_sc_gather_rows(y, idx).reshape(TOPK_IN_GROUP, T, SC_GATHER_WORDS) for y in y_pieces]
        x = _combine_dense(yg, x2d, info2d, g_final, final_norm=(l == depth - 1)).reshape(B, S, D)
    return x
```

```python
import functools
import math

import jax
import jax.numpy as jnp
from jax import lax
from jax.experimental import pallas as pl
from jax.experimental.pallas import tpu as pltpu
from jax.experimental.pallas import tpu_sc as plsc

F32 = jnp.float32
BF16 = jnp.bfloat16
NEG_INF = float("-inf")

EPS = 1e-6
GLA_HEADS, GLA_DK, GLA_DV, GLA_LOWRANK, GLA_TAU, GLA_CHUNK = 4, 64, 128, 16, 16.0, 64
GLA_QK, GLA_V = GLA_HEADS * GLA_DK, GLA_HEADS * GLA_DV
MOBA_HEADS, MOBA_DH, MOBA_BLOCK, MOBA_TOPK = 8, 64, 256, 3
MOBA_W = MOBA_HEADS * MOBA_DH
LOG2E = math.log2(math.e)
MOBA_Q_SCALE = MOBA_DH ** -0.5 * LOG2E
REL_BUCKETS, REL_MAX_DIST = 32, 128
MEM_HEADS, MEM_DH = 4, 128
MEM_W = MEM_HEADS * MEM_DH
N_GROUPS, EXPERTS_PER_GROUP, TOPK_IN_GROUP = 4, 8, 2
N_EXPERTS = N_GROUPS * EXPERTS_PER_GROUP

LANES = 128
SUBLANES = 8
VMEM_LIMIT_BYTES = 56 * 1024 * 1024

PROJ_ROWS = 512
GLA_ROWS = 1024
MOBA_GROUP = 4
MIX_ROWS = 512
MIX_PARTS = 2
EXPERT_ROWS = 512
MOVE_ROWS = 1024
PLAN_ROWS = 2048
SC_GATHER_ROWS = 128
SC_GATHER_WORDS = 256
EXPERT_PARTS = 2


def _params(*semantics):
    return pltpu.CompilerParams(dimension_semantics=semantics, vmem_limit_bytes=VMEM_LIMIT_BYTES)


def _full(shape):
    return pl.BlockSpec(shape, lambda *_: (0,) * len(shape))


def _rms(x, g):
    return x * lax.rsqrt(jnp.mean(x * x, axis=-1, keepdims=True) + EPS) * g


def _pack_bf16_pairs(x):
    n = x.shape[1] // 2
    bits = pltpu.bitcast(x.astype(BF16).astype(F32), jnp.uint32)
    return bits[:, n:] | (bits[:, :n] >> 16)


def _unpack_bf16_pairs(w):
    lo = pltpu.bitcast(w << 16, F32)
    hi = pltpu.bitcast(w & jnp.uint32(0xFFFF0000), F32)
    return jnp.concatenate([lo, hi], axis=1)


def _proj_kernel(x_ref, g_ref, w_qk, w_v, w_r, w_a, w_up, b_a, w_qm, w_km, w_vm, w_zg, w_zm,
                 o_qk, o_v, o_r, o_la, o_qm, o_km, o_vm, o_zg, o_zm):
    h = _rms(x_ref[...], g_ref[...]).astype(BF16)

    def mm(w_ref):
        return jnp.dot(h, w_ref[...], preferred_element_type=F32)

    o_qk[...] = mm(w_qk)
    o_v[...] = mm(w_v).astype(BF16)
    o_r[...] = mm(w_r)
    a_lr = mm(w_a).astype(BF16)
    pre = jnp.dot(a_lr, w_up[...], preferred_element_type=F32) + b_a[...]
    o_la[...] = jax.nn.log_sigmoid(pre) * (1.0 / GLA_TAU)

    def mm_t(wt_ref):
        return lax.dot_general(wt_ref[...], h, (((1,), (1,)), ((), ())), preferred_element_type=F32)

    def store_blocks(o_ref, val_t):
        for c in range(o_ref.shape[0]):
            o_ref[c] = val_t[:, c * MOBA_BLOCK:(c + 1) * MOBA_BLOCK]

    store_blocks(o_qm, (mm_t(w_qm) * MOBA_Q_SCALE).astype(BF16))
    o_km[...] = mm(w_km).astype(BF16)
    store_blocks(o_vm, mm_t(w_vm).astype(BF16))
    o_zg[...] = mm(w_zg)
    o_zm[...] = mm(w_zm)


def _project(x2d, g_mix, w_in, w_alpha_up, b_alpha):
    T, D = x2d.shape
    rows = min(PROJ_ROWS, T)
    assert T % rows == 0
    splits = (GLA_QK, GLA_QK, GLA_V, GLA_V, GLA_LOWRANK, MOBA_W, MOBA_W, MOBA_W, D, D)
    offs = [0]
    for s in splits:
        offs.append(offs[-1] + s)
    wb = w_in.astype(BF16)
    sec = lambda i, j: wb[:, offs[i]:offs[j]]
    w_a = jnp.pad(sec(4, 5), ((0, 0), (0, LANES - GLA_LOWRANK)))
    w_up = jnp.pad(w_alpha_up.astype(BF16), ((0, LANES - GLA_LOWRANK), (0, 0)))
    weights = [sec(0, 2), sec(2, 3), sec(3, 4), w_a, w_up, b_alpha.reshape(1, GLA_QK).astype(F32),
               sec(5, 6).T, sec(6, 7), sec(7, 8).T, sec(8, 9), sec(9, 10)]
    out_defs = [(2 * GLA_QK, F32, False), (GLA_V, BF16, False), (GLA_V, F32, False), (GLA_QK, F32, False),
                (MOBA_W, BF16, True), (MOBA_W, BF16, False), (MOBA_W, BF16, True), (D, F32, False),
                (D, F32, False)]
    BS = MOBA_BLOCK
    assert rows % BS == 0
    row_spec = lambda n: pl.BlockSpec((rows, n), lambda i: (i, 0))
    blk_spec = lambda n: pl.BlockSpec((rows // BS, n, BS), lambda i: (i, 0, 0))
    return pl.pallas_call(
        _proj_kernel,
        grid=(T // rows,),
        in_specs=[row_spec(D), _full((1, D))] + [_full(w.shape) for w in weights],
        out_specs=[blk_spec(n) if t else row_spec(n) for n, _, t in out_defs],
        out_shape=[jax.ShapeDtypeStruct((T // BS, n, BS) if t else (T, n), dt) for n, dt, t in out_defs],
        compiler_params=_params("parallel"),
        name="norm_in_proj",
    )(x2d, g_mix.reshape(1, D).astype(F32), *weights)


def _gla_kernel(qk_ref, la_ref, v_ref, r_ref, g_ref, o_ref, state_ref, obuf_ref):
    C, H, DK, DV = GLA_CHUNK, GLA_HEADS, GLA_DK, GLA_DV
    rows = qk_ref.shape[1]

    @pl.when(pl.program_id(1) == 0)
    def _():
        state_ref[...] = jnp.zeros_like(state_ref)

    tri = (lax.broadcasted_iota(jnp.int32, (C, C), 0) >= lax.broadcasted_iota(jnp.int32, (C, C), 1)).astype(BF16)
    lane_head = lax.broadcasted_iota(jnp.int32, (1, H * DK), 1) // DK
    head_masks = [(lane_head == h).astype(F32) for h in range(H)]
    stack_row = lax.broadcasted_iota(jnp.int32, (H * C, C), 0) % C
    stack_col = lax.broadcasted_iota(jnp.int32, (H * C, C), 1)
    causal = stack_col <= stack_row
    same_head = (lax.broadcasted_iota(jnp.int32, (H * DV, H * DK), 0) // DV
                 == lax.broadcasted_iota(jnp.int32, (H * DV, H * DK), 1) // DK)
    scale = DK ** -0.5

    def stack(m):
        return jnp.concatenate([m * head_masks[h] for h in range(H)], axis=0).astype(BF16)

    chunks = [slice(c * C, (c + 1) * C) for c in range(rows // C)]

    def cum_log_decay(sl):
        la = la_ref[0, sl, :]
        p1 = la.astype(BF16)
        r1 = la - p1.astype(F32)
        p2 = r1.astype(BF16)
        p3 = (r1 - p2.astype(F32)).astype(BF16)
        s3 = jnp.dot(tri, jnp.concatenate([p1, p2, p3], axis=1), preferred_element_type=F32)
        w = H * DK
        return (s3[:, 0:w] + s3[:, w:2 * w]) + s3[:, 2 * w:3 * w]

    b_all = [cum_log_decay(sl) for sl in chunks]

    qe_all, ke_all, kd_all, qb_all, decay_all = [], [], [], [], []
    for sl, b in zip(chunks, b_all):
        q = qk_ref[0, sl, 0:H * DK] * scale
        k = qk_ref[0, sl, H * DK:2 * H * DK]
        b_last = b[C - 1:C, :]
        b_mid = b[C // 2 - 1:C // 2, :]
        qe_all.append(stack(q * jnp.exp(b - b_mid)))
        ke_all.append((k * jnp.exp(b_mid - b)).astype(BF16))
        kd_all.append((k * jnp.exp(b_last - b)).astype(BF16))
        qb_all.append((q * jnp.exp(b)).astype(BF16))
        decay_all.append(jnp.exp(b_last))

    att_all = [jnp.where(causal, lax.dot_general(qe, ke, (((1,), (1,)), ((), ())), preferred_element_type=F32),
                         0.0).astype(BF16) for qe, ke in zip(qe_all, ke_all)]

    o_intra_all, kv_all = [], []
    for sl, att, kd in zip(chunks, att_all, kd_all):
        v = v_ref[0, sl, :]
        o_intra_all.append(jnp.concatenate(
            [jnp.dot(att[h * C:(h + 1) * C, :], v[:, h * DV:(h + 1) * DV], preferred_element_type=F32)
             for h in range(H)], axis=1))
        kv_t = lax.dot_general(v, kd, (((0,), (0,)), ((), ())), preferred_element_type=F32)
        kv_all.append(jnp.where(same_head, kv_t, 0.0))

    state_t = state_ref[...]
    for sl, qb, decay, kv_t, o_intra in zip(chunks, qb_all, decay_all, kv_all, o_intra_all):
        o_inter = lax.dot_general(qb, state_t.astype(BF16), (((1,), (1,)), ((), ())),
                                  preferred_element_type=F32)
        obuf_ref[sl, :] = o_intra + o_inter
        state_t = decay * state_t + kv_t
    state_ref[...] = state_t

    r = r_ref[0]
    for h in range(H):
        cs = slice(h * DV, (h + 1) * DV)
        y = _rms(obuf_ref[:, cs], g_ref[:, cs])
        rh = r[:, cs]
        o_ref[0, :, cs] = (y * (rh * jax.nn.sigmoid(rh))).astype(BF16)


def _gla(qk, la, v, r, g_head, B, S):
    rows = min(GLA_ROWS, S)
    assert S % rows == 0 and rows % GLA_CHUNK == 0
    spec = lambda n: pl.BlockSpec((1, rows, n), lambda b, i: (b, i, 0))
    return pl.pallas_call(
        _gla_kernel,
        grid=(B, S // rows),
        in_specs=[spec(2 * GLA_QK), spec(GLA_QK), spec(GLA_V), spec(GLA_V), _full((1, GLA_V))],
        out_specs=spec(GLA_V),
        out_shape=jax.ShapeDtypeStruct((B, S, GLA_V), BF16),
        scratch_shapes=[pltpu.VMEM((GLA_V, GLA_QK), F32), pltpu.VMEM((rows, GLA_V), F32)],
        compiler_params=_params("parallel", "arbitrary"),
        name="gla_chunked",
    )(qk.reshape(B, S, -1), la.reshape(B, S, -1), v.reshape(B, S, -1), r.reshape(B, S, -1),
      g_head.reshape(1, GLA_V).astype(F32))


def _t5_bucket(dist):
    n = jnp.maximum(dist, 0)
    max_exact = REL_BUCKETS // 2
    nf = jnp.maximum(n, 1).astype(F32)
    large = max_exact + (jnp.log(nf / max_exact) / math.log(REL_MAX_DIST / max_exact)
                         * (REL_BUCKETS - max_exact)).astype(jnp.int32)
    large = jnp.minimum(large, REL_BUCKETS - 1)
    return jnp.where(n < max_exact, n, large)


def _moba_bias_kernel(rb_ref, o_ref):
    BS, G = MOBA_BLOCK, MOBA_GROUP
    grp, kind = pl.program_id(0), pl.program_id(1)
    d = (lax.broadcasted_iota(jnp.int32, (BS, BS), 1) - lax.broadcasted_iota(jnp.int32, (BS, BS), 0)
         + kind * BS)
    bucket = _t5_bucket(d)
    for h in range(G):
        val = jnp.zeros((BS, BS), F32)
        for bkt in range(REL_BUCKETS):
            val = jnp.where(bucket == bkt, rb_ref[bkt, grp * G + h] * LOG2E, val)
        o_ref[0, 0, :, h * BS:(h + 1) * BS] = jnp.where(d >= 0, val, NEG_INF)


def _moba_bias(rel_bias):
    BS, G = MOBA_BLOCK, MOBA_GROUP
    n_grp = MOBA_HEADS // G
    return pl.pallas_call(
        _moba_bias_kernel,
        grid=(n_grp, 2),
        in_specs=[pl.BlockSpec(memory_space=pltpu.SMEM)],
        out_specs=pl.BlockSpec((1, 1, BS, G * BS), lambda g, k: (g, k, 0, 0)),
        out_shape=jax.ShapeDtypeStruct((n_grp, 2, BS, G * BS), F32),
        compiler_params=_params("parallel", "parallel"),
        name="moba_bias_tables",
    )(rel_bias.astype(F32))


def _moba_kernel(rb_ref, q_ref, k_ref, v_ref, bias_ref, o_ref,
                 qs_ref, kmean_ref, sel_ref, m_ref, l_ref, acc_ref, sbuf_ref):
    BS, G, DH = MOBA_BLOCK, MOBA_GROUP, MOBA_DH
    NBP = kmean_ref.shape[0]
    grp, i = pl.program_id(1), pl.program_id(2)

    @pl.when(i == 0)
    def _():
        S = k_ref.shape[1]
        blk_of_key = lax.broadcasted_iota(jnp.int32, (NBP, S), 1) // BS
        ind = (blk_of_key == lax.broadcasted_iota(jnp.int32, (NBP, S), 0)).astype(BF16)
        kmean_ref[...] = jnp.dot(ind, k_ref[0], preferred_element_type=F32) * (1.0 / BS)

    qt = q_ref[0]
    sub_head = lax.broadcasted_iota(jnp.int32, (G * DH, 1), 0) // DH
    for h in range(G):
        qs_ref[h] = jnp.where(sub_head == h, qt, jnp.zeros_like(qt))

    kmean = kmean_ref[...].astype(BF16)
    gate = jnp.concatenate([jnp.dot(kmean, qs_ref[h], preferred_element_type=F32) for h in range(G)],
                           axis=1)
    blk = lax.broadcasted_iota(jnp.int32, gate.shape, 0)
    gate = jnp.where(blk < i, gate, NEG_INF)
    for t in range(MOBA_TOPK):
        mx = jnp.max(gate, axis=0, keepdims=True)
        hit = (gate == mx) & (mx > NEG_INF)
        idx = jnp.min(jnp.where(hit, blk, NBP), axis=0, keepdims=True)
        sel_ref[t:t + 1, :] = idx
        gate = jnp.where(blk == idx, NEG_INF, gate)

    def mask_row(j):
        hit = (sel_ref[0:1, :] == j) | (sel_ref[1:2, :] == j) | (sel_ref[2:3, :] == j)
        return jnp.where(hit, 0.0, NEG_INF)

    def with_ones(vt):
        return jnp.concatenate([vt, jnp.ones((8, vt.shape[1]), BF16)], axis=0)

    def far_scores(j0, n, slot):
        kj = k_ref[0, pl.ds(pl.multiple_of(j0 * BS, BS), n * BS), :]
        for h in range(G):
            cs = slice(h * BS, (h + 1) * BS)
            sbuf_ref[slot, h, 0:n * BS, :] = jnp.dot(kj, qs_ref[h], preferred_element_type=F32)

    j_prev = jnp.maximum(i - 1, 0)
    k_own = k_ref[0, pl.ds(pl.multiple_of(i * BS, BS), BS), :]
    k_prev = k_ref[0, pl.ds(pl.multiple_of(j_prev * BS, BS), BS), :]
    vt_near = jnp.concatenate([v_ref[i], v_ref[j_prev]], axis=1)
    prev_mask = mask_row(i - 1)
    for h in range(G):
        cs = slice(h * BS, (h + 1) * BS)
        sbuf_ref[1, h, 0:BS, :] = jnp.dot(k_own, qs_ref[h], preferred_element_type=F32)
        sbuf_ref[1, h, BS:2 * BS, :] = jnp.dot(k_prev, qs_ref[h], preferred_element_type=F32)
    far_scores(0, 2, 0)
    m_out, l_out, acc_out = [], [], []
    for h in range(G):
        cs = slice(h * BS, (h + 1) * BS)
        s_own = sbuf_ref[1, h, 0:BS, :] + bias_ref[0, 0, :, cs]
        s_prev = sbuf_ref[1, h, BS:2 * BS, :] + (bias_ref[0, 1, :, cs] + prev_mask[:, cs])
        m0 = jnp.maximum(jnp.max(s_own, axis=0, keepdims=True), jnp.max(s_prev, axis=0, keepdims=True))
        pb = jnp.concatenate([jnp.exp2(s_own - m0).astype(BF16), jnp.exp2(s_prev - m0).astype(BF16)], axis=0)
        pv = jnp.dot(with_ones(vt_near[h * DH:(h + 1) * DH, :]), pb, preferred_element_type=F32)
        m_out.append(m0)
        l_out.append(pv[DH:DH + 1, :])
        acc_out.append(pv[0:DH, :])
    m_ref[...] = jnp.concatenate(m_out, axis=1)
    l_ref[...] = jnp.concatenate(l_out, axis=1)
    acc_ref[...] = jnp.concatenate(acc_out, axis=1)

    lane_head = lax.broadcasted_iota(jnp.int32, (1, G * BS), 1) // BS
    far_bias = jnp.zeros((1, G * BS), F32)
    for h in range(G):
        far_bias = jnp.where(lane_head == h, rb_ref[REL_BUCKETS - 1, grp * G + h] * LOG2E, far_bias)

    def visit_far(j0, n, slot, ahead=None):
        vjt = jnp.concatenate([v_ref[j0 + t] for t in range(n)], axis=1)
        addend = [far_bias + mask_row(j0 + t) for t in range(n)]
        m_old, l_old, acc_old = m_ref[...], l_ref[...], acc_ref[...]
        m_out, l_out, acc_out = [], [], []
        if ahead is not None:
            k_next = k_ref[0, pl.ds(pl.multiple_of(ahead[0] * BS, BS), 2 * BS), :]
        for h in range(G):
            cs = slice(h * BS, (h + 1) * BS)
            if ahead is not None:
                sbuf_ref[ahead[1], h] = jnp.dot(k_next, qs_ref[h], preferred_element_type=F32)
            s = sbuf_ref[slot, h, 0:n * BS, :]
            mx = jnp.max(s[0:BS], axis=0, keepdims=True) + addend[0][:, cs]
            for t in range(1, n):
                mx = jnp.maximum(mx, jnp.max(s[t * BS:(t + 1) * BS], axis=0, keepdims=True) + addend[t][:, cs])
            m_new = jnp.maximum(m_old[:, cs], mx)
            pb = jnp.concatenate([jnp.exp2(s[t * BS:(t + 1) * BS] - (m_new - addend[t][:, cs])).astype(BF16)
                                  for t in range(n)], axis=0)
            pv = jnp.dot(with_ones(vjt[h * DH:(h + 1) * DH, :]), pb, preferred_element_type=F32)
            alpha = jnp.exp2(m_old[:, cs] - m_new)
            m_out.append(m_new)
            l_out.append(alpha * l_old[:, cs] + pv[DH:DH + 1, :])
            acc_out.append(alpha * acc_old[:, cs] + pv[0:DH, :])
        m_ref[...] = jnp.concatenate(m_out, axis=1)
        l_ref[...] = jnp.concatenate(l_out, axis=1)
        acc_ref[...] = jnp.concatenate(acc_out, axis=1)

    n_far = jnp.maximum(i - 1, 0)
    n_pairs = lax.shift_right_logical(n_far, 1)

    def pair(p, slot, look_ahead=True):
        ahead = (2 * jnp.minimum(p + 1, n_pairs - 1), 1 - slot) if look_ahead else None
        visit_far(2 * p, 2, slot, ahead=ahead)

    def far_octet(w, carry):
        for t in range(4):
            pair(4 * w + t, t % 2)
        return carry

    lax.fori_loop(0, lax.shift_right_logical(n_pairs, 2), far_octet, 0)

    @pl.when((n_pairs & 2) != 0)
    def _():
        base = 4 * lax.shift_right_logical(n_pairs, 2)
        pair(base, 0)
        pair(base + 1, 1)

    @pl.when((n_pairs & 1) != 0)
    def _():
        pair(n_pairs - 1, 0, look_ahead=False)

    @pl.when(n_far % 2 == 1)
    def _():
        far_scores(n_far - 1, 1, 1)
        visit_far(n_far - 1, 1, 1)

    out_t = jnp.concatenate([acc_ref[:, h * BS:(h + 1) * BS] / l_ref[:, h * BS:(h + 1) * BS] for h in range(G)],
                            axis=0)
    o_ref[0] = out_t.T.astype(o_ref.dtype)


def _moba(qm_t, km, vm_t, bias, rel_bias, B, S):
    BS, G, DH = MOBA_BLOCK, MOBA_GROUP, MOBA_DH
    W = G * DH
    n_grp = MOBA_HEADS // G
    assert S % BS == 0 and S >= 2 * BS
    NB = S // BS
    NBP = -(-NB // 8) * 8
    return pl.pallas_call(
        _moba_kernel,
        grid=(B, n_grp, NB),
        in_specs=[pl.BlockSpec(memory_space=pltpu.SMEM),
                  pl.BlockSpec((1, W, BS), lambda b, g, i: (b * NB + i, g, 0)),
                  pl.BlockSpec((1, S, W), lambda b, g, i: (b, 0, g)),
                  pl.BlockSpec((NB, W, BS), lambda b, g, i: (b, g, 0)),
                  pl.BlockSpec((1, 2, BS, G * BS), lambda b, g, i: (g, 0, 0, 0))],
        out_specs=pl.BlockSpec((1, BS, W), lambda b, g, i: (b, i, g)),
        out_shape=jax.ShapeDtypeStruct((B, S, MOBA_W), BF16),
        scratch_shapes=[pltpu.VMEM((G, W, BS), BF16), pltpu.VMEM((NBP, W), F32),
                        pltpu.VMEM((8, G * BS), jnp.int32),
                        pltpu.VMEM((1, G * BS), F32), pltpu.VMEM((1, G * BS), F32),
                        pltpu.VMEM((DH, G * BS), F32), pltpu.VMEM((2, G, 2 * BS, BS), F32)],
        compiler_params=_params("parallel", "parallel", "arbitrary"),
        name="moba_attention",
    )(rel_bias.astype(F32), qm_t, km.reshape(B, S, -1), vm_t, bias)


def _mem_kv_kernel(mem_ref, g_ref, w_ref, k_ref, v_ref):
    kv = jnp.dot(_rms(mem_ref[0], g_ref[...]).astype(BF16), w_ref[...], preferred_element_type=F32)
    k_ref[0] = kv[:, :MEM_W].astype(BF16)
    v_ref[0] = kv[:, MEM_W:].astype(BF16)


def _mem_kv(mem, g_mem, w_ckv):
    B, M, D = mem.shape
    spec = pl.BlockSpec((1, M, MEM_W), lambda b: (b, 0, 0))
    return pl.pallas_call(
        _mem_kv_kernel,
        grid=(B,),
        in_specs=[pl.BlockSpec((1, M, D), lambda b: (b, 0, 0)), _full((1, D)), _full((D, 2 * MEM_W))],
        out_specs=[spec, spec],
        out_shape=[jax.ShapeDtypeStruct((B, M, MEM_W), BF16)] * 2,
        compiler_params=_params("parallel"),
        name="memory_kv",
    )(mem, g_mem.reshape(1, D).astype(F32), w_ckv.astype(BF16))


INFO_W0, INFO_W1, INFO_E0, INFO_E1, INFO_R0, INFO_R1 = range(6)
BLK_EXPERT, BLK_VALID = range(2)
ROUTER_GROUP_LANE0, ROUTER_EXPERT_LANE0 = 0, N_GROUPS


def _mix_kernel(n_pieces, x_ref, og_ref, om_ref, zg_ref, zm_ref, mk_ref, mv_ref, gc_ref, gm_ref,
                wpg, wpm, wout, wcq, wco, wr, br,
                x2_ref, info_ref, cnt_ref, *rest):
    hp_refs, base_ref = rest[:n_pieces], rest[n_pieces]
    first = (pl.program_id(0) == 0) & (pl.program_id(1) == 0)

    @pl.when(first)
    def _():
        base_ref[...] = jnp.zeros_like(base_ref)

    def mm(a, w_ref):
        return jnp.dot(a.astype(BF16), w_ref[...], preferred_element_type=F32)

    n_rows = x_ref.shape[1]
    sub = n_rows // MIX_PARTS
    parts = [slice(p * sub, (p + 1) * sub) for p in range(MIX_PARTS)]

    merged = [jax.nn.sigmoid(zg_ref[0, rs, :]) * mm(og_ref[0, rs, :], wpg)
              + jax.nn.sigmoid(zm_ref[0, rs, :]) * mm(om_ref[0, rs, :], wpm) for rs in parts]
    x1 = [x_ref[0, rs, :] + mm(m, wout) for rs, m in zip(parts, merged)]

    qc = [mm(_rms(v, gc_ref[...]), wcq).astype(BF16) for v in x1]

    def mem_attention(q):
        heads = []
        for h in range(MEM_HEADS):
            cs = slice(h * MEM_DH, (h + 1) * MEM_DH)
            s = lax.dot_general(q[:, cs], mk_ref[0, :, cs], (((1,), (1,)), ((), ())),
                                preferred_element_type=F32) * (MEM_DH ** -0.5)
            p = jnp.exp(s - jnp.max(s, axis=-1, keepdims=True))
            o = jnp.dot(p.astype(BF16), mv_ref[0, :, cs], preferred_element_type=F32)
            heads.append(o / jnp.sum(p, axis=-1, keepdims=True))
        return jnp.concatenate(heads, axis=-1)

    attn = [mem_attention(q) for q in qc]
    x2 = [v + mm(a, wco) for v, a in zip(x1, attn)]
    for rs, v in zip(parts, x2):
        x2_ref[0, rs, :] = v

    hm = [_rms(v, gm_ref[...]) for v in x2]
    logits = [lax.dot_general(wr[...], h.astype(BF16), (((1,), (1,)), ((), ())), preferred_element_type=F32)
              + br[...] for h in hm]
    for rs, h in zip(parts, hm):
        words = _pack_bf16_pairs(h)
        for c, hp_ref in enumerate(hp_refs):
            hp_ref[0, rs, :] = words[:, c * SC_GATHER_WORDS:(c + 1) * SC_GATHER_WORDS]
    n_log = -(-(N_GROUPS + N_EXPERTS) // SUBLANES) * SUBLANES
    row_id = lax.broadcasted_iota(jnp.int32, (n_log, sub), 0)
    is_grp = row_id < N_GROUPS
    e_id = row_id - ROUTER_EXPERT_LANE0

    def route(lg):
        gl = jnp.where(is_grp, lg, NEG_INF)
        ge = jnp.exp(gl - jnp.max(gl, axis=0, keepdims=True))
        g_prob = ge / jnp.sum(ge, axis=0, keepdims=True)
        p_grp = jnp.max(g_prob, axis=0, keepdims=True)
        grp = jnp.min(jnp.where((g_prob == p_grp) & is_grp, row_id, LANES), axis=0, keepdims=True)
        in_grp = (e_id >= grp * EXPERTS_PER_GROUP) & (e_id < (grp + 1) * EXPERTS_PER_GROUP)
        el = jnp.where(in_grp, lg, NEG_INF)
        ee = jnp.exp(el - jnp.max(el, axis=0, keepdims=True))
        e_prob = jnp.where(in_grp, ee / jnp.sum(ee, axis=0, keepdims=True), -1.0)
        p0 = jnp.max(e_prob, axis=0, keepdims=True)
        e0 = jnp.min(jnp.where(e_prob == p0, e_id, LANES), axis=0, keepdims=True)
        e_rest = jnp.where(e_id == e0, -1.0, e_prob)
        p1 = jnp.max(e_rest, axis=0, keepdims=True)
        e1 = jnp.min(jnp.where(e_rest == p1, e_id, LANES), axis=0, keepdims=True)
        return e0, e1, p_grp * p0 / (p0 + p1), p_grp * p1 / (p0 + p1)

    routed = [route(lg[0:n_log, :]) for lg in logits]

    expert = lax.broadcasted_iota(jnp.int32, (LANES, sub), 0)
    before = (lax.broadcasted_iota(jnp.int32, (sub, sub), 0)
              < lax.broadcasted_iota(jnp.int32, (sub, sub), 1)).astype(BF16)
    field = lax.broadcasted_iota(jnp.int32, (LANES, sub), 0)
    base = base_ref[...]
    for rs, (e0, e1, w0, w1) in zip(parts, routed):
        onehot = ((expert == e0) | (expert == e1)).astype(F32)
        seen = base + jnp.dot(onehot.astype(BF16), before, preferred_element_type=F32)
        r0 = jnp.sum(jnp.where(expert == e0, seen, 0.0), axis=0, keepdims=True)
        r1 = jnp.sum(jnp.where(expert == e1, seen, 0.0), axis=0, keepdims=True)
        base = base + jnp.sum(onehot, axis=1, keepdims=True)
        info_t = jnp.zeros((LANES, sub), F32)
        for ln, val in ((INFO_W0, w0), (INFO_W1, w1), (INFO_E0, e0.astype(F32)), (INFO_E1, e1.astype(F32)),
                        (INFO_R0, r0), (INFO_R1, r1)):
            info_t = jnp.where(field == ln, val, info_t)
        info_ref[0, rs, :] = info_t.T
    base_ref[...] = base
    cnt_ref[...] = jnp.broadcast_to(base, (LANES, LANES)).T[0:1, :]


def _mix(x, o_g, o_m, z_g, z_m, mem_k, mem_v, g_cross, g_moe, w_proj_gla, w_proj_moba, w_out, w_cq, w_co,
         w_rg, b_rg, w_re, b_re):
    B, S, D = x.shape
    M = mem_k.shape[1]
    rows = min(MIX_ROWS, S)
    assert S % rows == 0
    pad = LANES - N_GROUPS - N_EXPERTS
    wr = jnp.pad(jnp.concatenate([w_rg, w_re], axis=1), ((0, 0), (0, pad))).astype(BF16).T
    br = jnp.pad(jnp.concatenate([b_rg, b_re]), (0, pad)).reshape(LANES, 1).astype(F32)
    weights = [w_proj_gla.astype(BF16), w_proj_moba.astype(BF16), w_out.astype(BF16), w_cq.astype(BF16),
               w_co.astype(BF16), wr, br]
    tile = lambda n: pl.BlockSpec((1, rows, n), lambda b, i: (b, i, 0))
    memspec = pl.BlockSpec((1, M, MEM_W), lambda b, i: (b, 0, 0))
    n_pieces = D // 2 // SC_GATHER_WORDS
    return pl.pallas_call(
        functools.partial(_mix_kernel, n_pieces),
        grid=(B, S // rows),
        in_specs=[tile(D), tile(GLA_V), tile(MOBA_W), tile(D), tile(D), memspec, memspec,
                  _full((1, D)), _full((1, D))] + [_full(w.shape) for w in weights],
        out_specs=[tile(D), tile(LANES), _full((1, LANES))] + [tile(SC_GATHER_WORDS)] * n_pieces,
        out_shape=[jax.ShapeDtypeStruct((B, S, D), F32), jax.ShapeDtypeStruct((B, S, LANES), F32),
                   jax.ShapeDtypeStruct((1, LANES), F32)]
        + [jax.ShapeDtypeStruct((B, S, SC_GATHER_WORDS), jnp.uint32)] * n_pieces,
        scratch_shapes=[pltpu.VMEM((LANES, 1), F32)],
        compiler_params=_params("arbitrary", "arbitrary"),
        name="merge_memattn_router",
    )(x, o_g, o_m, z_g.reshape(B, S, D), z_m.reshape(B, S, D), mem_k, mem_v,
      g_cross.reshape(1, D).astype(F32), g_moe.reshape(1, D).astype(F32), *weights)


def _plan_kernel(cnt_ref, info_ref, dest_ref, blk_ref):
    rows = info_ref.shape[0]
    lane1 = lax.broadcasted_iota(jnp.int32, (1, LANES), 1)
    nblk = jnp.floor((cnt_ref[...] + (EXPERT_ROWS - 1)) * (1.0 / EXPERT_ROWS))
    nblk = jnp.where(lane1 < N_EXPERTS, nblk, 0.0)
    hi = jnp.floor(nblk * (1.0 / 256.0))
    lo = nblk - 256.0 * hi
    upto = (lax.broadcasted_iota(jnp.int32, (LANES, LANES), 0)
            <= lax.broadcasted_iota(jnp.int32, (LANES, LANES), 1)).astype(BF16)
    digits = jnp.concatenate([jnp.broadcast_to(hi, (8, LANES)), jnp.broadcast_to(lo, (8, LANES))], axis=0)
    sums = jnp.dot(digits.astype(BF16), upto, preferred_element_type=F32)
    pend = sums[0:1] * 256.0 + sums[8:9]
    pstart_rows = (pend - nblk) * EXPERT_ROWS

    info = info_ref[...]
    lane = lax.broadcasted_iota(jnp.int32, (rows, LANES), 1)

    def field(ln):
        return jnp.sum(jnp.where(lane == ln, info, 0.0), axis=-1, keepdims=True)

    def dest(e, r):
        return jnp.sum(jnp.where(lane == e.astype(jnp.int32), pstart_rows, 0.0), axis=-1, keepdims=True) + r

    d0 = dest(field(INFO_E0), field(INFO_R0))
    d1 = dest(field(INFO_E1), field(INFO_R1))
    cols = jnp.where(lane == 0, d0, jnp.where(lane == 1, d1, 0.0))
    dest_ref[...] = cols.T[0:SUBLANES, :].astype(jnp.int32)

    @pl.when(pl.program_id(0) == 0)
    def _():
        n = lax.broadcasted_iota(jnp.int32, (blk_ref.shape[0], LANES), 0).astype(F32)
        blane = lax.broadcasted_iota(jnp.int32, (blk_ref.shape[0], LANES), 1)
        done = jnp.where((pend <= n) & (lane1 < N_EXPERTS), 1.0, 0.0)
        e = jnp.minimum(jnp.sum(done, axis=-1, keepdims=True), N_EXPERTS - 1.0)
        mine = blane == e.astype(jnp.int32)
        first_blk = jnp.sum(jnp.where(mine, pend - nblk, 0.0), axis=-1, keepdims=True)
        count = jnp.sum(jnp.where(mine, cnt_ref[...], 0.0), axis=-1, keepdims=True)
        valid = jnp.clip(count - EXPERT_ROWS * (n[:, 0:1] - first_blk), 0.0, float(EXPERT_ROWS))
        blk_ref[...] = jnp.where(blane == BLK_EXPERT, e, jnp.where(blane == BLK_VALID, valid, 0.0)).astype(jnp.int32)


def _plan(counts, info2d, n_blk):
    T = info2d.shape[0]
    rows = min(PLAN_ROWS, T)
    assert T % rows == 0
    n_blk_pad = -(-n_blk // SUBLANES) * SUBLANES
    return pl.pallas_call(
        _plan_kernel,
        grid=(T // rows,),
        in_specs=[_full((1, LANES)), pl.BlockSpec((rows, LANES), lambda i: (i, 0))],
        out_specs=[pl.BlockSpec((SUBLANES, rows), lambda i: (0, i)), _full((n_blk_pad, LANES))],
        out_shape=[jax.ShapeDtypeStruct((SUBLANES, T), jnp.int32),
                   jax.ShapeDtypeStruct((n_blk_pad, LANES), jnp.int32)],
        compiler_params=_params("arbitrary"),
        name="dispatch_plan",
    )(counts, info2d)


def _sc_windows(n_rows):
    n_inner = 32
    assert n_rows % (SC_GATHER_ROWS * n_inner) == 0
    return n_rows // (SC_GATHER_ROWS * n_inner), n_inner


def _sc_mesh():
    return plsc.VectorSubcoreMesh(core_axis_name="c", subcore_axis_name="s")


def _sc_scatter_rows(src, idx_a, idx_b, n_out):
    T, W = src.shape
    n_outer, n_inner = _sc_windows(T)
    win = lambda i, j: i * n_inner + j

    @pl.kernel(out_type=jax.ShapeDtypeStruct((n_out, W), src.dtype), mesh=_sc_mesh(), scratch_types=[])
    def scatter_kernel(s_hbm, a_hbm, b_hbm, o_hbm):
        def body(s_vmem, a_vmem, b_vmem):
            pltpu.sync_copy(s_vmem, o_hbm.at[a_vmem.at[0]])
            pltpu.sync_copy(s_vmem, o_hbm.at[b_vmem.at[0]])

        pltpu.emit_pipeline(
            body,
            grid=(n_outer, n_inner),
            in_specs=[pl.BlockSpec((SC_GATHER_ROWS, W), index_map=lambda i, j: (win(i, j), 0)),
                      pl.BlockSpec((1, SC_GATHER_ROWS), index_map=lambda i, j: (0, win(i, j))),
                      pl.BlockSpec((1, SC_GATHER_ROWS), index_map=lambda i, j: (0, win(i, j)))],
            out_specs=[],
            core_axis_name=("c", "s"),
            dimension_semantics=(pltpu.PARALLEL, pltpu.PARALLEL),
        )(s_hbm, a_hbm, b_hbm)

    return scatter_kernel(src, idx_a, idx_b)


def _sc_gather_rows(table, idx):
    M = idx.shape[1]
    W = table.shape[1]
    n_outer, n_inner = _sc_windows(M)
    win = lambda i, j: i * n_inner + j

    @pl.kernel(out_type=jax.ShapeDtypeStruct((M, W), table.dtype), mesh=_sc_mesh(), scratch_types=[])
    def gather_kernel(t_hbm, i_hbm, o_hbm):
        def body(i_vmem, o_vmem):
            pltpu.sync_copy(t_hbm.at[i_vmem.at[0]], o_vmem)

        pltpu.emit_pipeline(
            body,
            grid=(n_outer, n_inner),
            in_specs=[pl.BlockSpec((1, SC_GATHER_ROWS), index_map=lambda i, j: (0, win(i, j)))],
            out_specs=[pl.BlockSpec((SC_GATHER_ROWS, W), index_map=lambda i, j: (win(i, j), 0))],
            core_axis_name=("c", "s"),
            dimension_semantics=(pltpu.PARALLEL, pltpu.PARALLEL),
        )(i_hbm, o_hbm)

    return gather_kernel(table, idx)


def _expert_kernel(n_pieces, blk_e_ref, blk_valid_ref, *refs):
    xs_refs, (wg_ref, wu_ref, wd_ref) = refs[:n_pieces], refs[n_pieces:n_pieces + 3]
    y_refs, (wg_bf, wu_bf, wd_bf) = refs[n_pieces + 3:2 * n_pieces + 3], refs[2 * n_pieces + 3:]
    n = pl.program_id(0)
    prev = blk_e_ref[jnp.maximum(n - 1, 0)]

    @pl.when((n == 0) | (blk_e_ref[n] != prev))
    def _():
        wg_bf[...] = wg_ref[0].astype(BF16)
        wu_bf[...] = wu_ref[0].astype(BF16)
        wd_bf[...] = wd_ref[0].astype(BF16)

    valid = blk_valid_ref[n]

    @pl.when(valid == 0)
    def _():
        for y_ref in y_refs:
            y_ref[...] = jnp.zeros_like(y_ref)

    @pl.when(valid > 0)
    def _():
        sub = xs_refs[0].shape[0] // EXPERT_PARTS
        parts = [slice(p * sub, (p + 1) * sub) for p in range(EXPERT_PARTS)]
        row = lax.broadcasted_iota(jnp.int32, (sub, 1), 0)

        def load(p, rs):
            words = jnp.concatenate([r[rs, :] for r in xs_refs], axis=1)
            words = jnp.where(row + p * sub < valid, words, jnp.zeros_like(words))
            return _unpack_bf16_pairs(words).astype(BF16)

        xb = [load(p, rs) for p, rs in enumerate(parts)]
        gate = [jnp.dot(v, wg_bf[...], preferred_element_type=F32) for v in xb]
        up = [jnp.dot(v, wu_bf[...], preferred_element_type=F32) for v in xb]
        hid = [(g * jax.nn.sigmoid(g) * u).astype(BF16) for g, u in zip(gate, up)]
        for rs, hv in zip(parts, hid):
            words = _pack_bf16_pairs(jnp.dot(hv, wd_bf[...], preferred_element_type=F32))
            for c, y_ref in enumerate(y_refs):
                y_ref[rs, :] = words[:, c * SC_GATHER_WORDS:(c + 1) * SC_GATHER_WORDS]


def _experts(blk_e, blk_valid, xs_pieces, w_gate, w_up, w_down):
    cap = xs_pieces[0].shape[0]
    _, D, DE = w_gate.shape
    n_blk = cap // EXPERT_ROWS
    n_pieces = len(xs_pieces)
    piece = pl.BlockSpec((EXPERT_ROWS, SC_GATHER_WORDS), lambda n, e, v: (n, 0))
    return pl.pallas_call(
        functools.partial(_expert_kernel, n_pieces),
        grid_spec=pltpu.PrefetchScalarGridSpec(
            num_scalar_prefetch=2,
            grid=(n_blk,),
            in_specs=[piece] * n_pieces
            + [pl.BlockSpec((1, D, DE), lambda n, e, v: (e[n], 0, 0)),
               pl.BlockSpec((1, D, DE), lambda n, e, v: (e[n], 0, 0)),
               pl.BlockSpec((1, DE, D), lambda n, e, v: (e[n], 0, 0))],
            out_specs=[piece] * n_pieces,
            scratch_shapes=[pltpu.VMEM((D, DE), BF16), pltpu.VMEM((D, DE), BF16), pltpu.VMEM((DE, D), BF16)]),
        out_shape=[jax.ShapeDtypeStruct((cap, SC_GATHER_WORDS), jnp.uint32)] * n_pieces,
        compiler_params=_params("arbitrary"),
        name="moe_experts",
    )(blk_e, blk_valid, *xs_pieces, w_gate, w_up, w_down)


def _combine_dense_kernel(final_norm, n_pieces, *refs):
    yg_refs, (x_ref, info_ref, g_ref, o_ref) = refs[:n_pieces], refs[n_pieces:]
    info = info_ref[...]
    w0 = info[:, INFO_W0:INFO_W0 + 1]
    w1 = info[:, INFO_W1:INFO_W1 + 1]
    y = [_unpack_bf16_pairs(jnp.concatenate([r[slot] for r in yg_refs], axis=1)) for slot in range(TOPK_IN_GROUP)]
    out = x_ref[...] + (w0 * y[0] + w1 * y[1])
    o_ref[...] = _rms(out, g_ref[...]) if final_norm else out


def _combine_dense(yg_pieces, x2d, info2d, g_final, final_norm):
    T, D = x2d.shape
    rows = min(MOVE_ROWS, T)
    n_pieces = len(yg_pieces)
    return pl.pallas_call(
        functools.partial(_combine_dense_kernel, final_norm, n_pieces),
        grid=(T // rows,),
        in_specs=[pl.BlockSpec((TOPK_IN_GROUP, rows, SC_GATHER_WORDS), lambda i: (0, i, 0))] * n_pieces
        + [pl.BlockSpec((rows, D), lambda i: (i, 0)), pl.BlockSpec((rows, LANES), lambda i: (i, 0)),
           _full((1, D))],
        out_specs=pl.BlockSpec((rows, D), lambda i: (i, 0)),
        out_shape=jax.ShapeDtypeStruct((T, D), F32),
        compiler_params=_params("parallel"),
        name="moe_combine_dense_final_norm",
    )(*yg_pieces, x2d, info2d, g_final.reshape(1, D).astype(F32))


def kernel(x, mem, g_mem, rel_bias, g_mix, w_in, w_alpha_up, b_alpha, g_gla_head, w_proj_gla, w_proj_moba,
           w_out, g_cross, w_cq, w_ckv, w_co, g_moe, w_router_group, b_router_group, w_router_expert,
           b_router_expert, w_exp_gate, w_exp_up, w_exp_down, g_final):
    B, S, D = x.shape
    T = B * S
    depth = g_mix.shape[0]
    n_assign = T * TOPK_IN_GROUP
    n_blk = -(-(n_assign + N_EXPERTS * (EXPERT_ROWS - 1)) // EXPERT_ROWS)
    cap = n_blk * EXPERT_ROWS

    mem_bias = _moba_bias(rel_bias)
    for l in range(depth):
        qk, v_g, r_g, la, q_m, k_m, v_m, z_g, z_m = _project(x.reshape(T, D), g_mix[l], w_in[l], w_alpha_up[l],
                                                             b_alpha[l])
        o_g = _gla(qk, la, v_g, r_g, g_gla_head[l], B, S)
        o_m = _moba(q_m, k_m, v_m, mem_bias, rel_bias, B, S)
        mem_k, mem_v = _mem_kv(mem, g_mem, w_ckv[l])
        x2, info, counts, *hm_pieces = _mix(x, o_g, o_m, z_g, z_m, mem_k, mem_v, g_cross[l], g_moe[l],
                                            w_proj_gla[l], w_proj_moba[l], w_out[l], w_cq[l], w_co[l],
                                            w_router_group[l], b_router_group[l], w_router_expert[l],
                                            b_router_expert[l])
        x2d, info2d = x2.reshape(T, D), info.reshape(T, LANES)
        dest, blk = _plan(counts, info2d, n_blk)
        idx = dest[0:TOPK_IN_GROUP].reshape(1, TOPK_IN_GROUP * T)
        xs_pieces = [_sc_scatter_rows(h.reshape(T, SC_GATHER_WORDS), idx[:, :T], idx[:, T:], cap)
                     for h in hm_pieces]
        y_pieces = _experts(blk[:n_blk, BLK_EXPERT], blk[:n_blk, BLK_VALID], xs_pieces, w_exp_gate[l],
                            w_exp_up[l], w_exp_down[l])
        yg = [_sc_gather_rows(y, idx).reshape(TOPK_IN_GROUP, T, SC_GATHER_WORDS) for y in y_pieces]
        x = _combine_dense(yg, x2d, info2d, g_final, final_norm=(l == depth - 1)).reshape(B, S, D)
    return x
```

```python
import functools
import math

import jax
import jax.numpy as jnp
from jax import lax
from jax.experimental import pallas as pl
from jax.experimental.pallas import tpu as pltpu
from jax.experimental.pallas import tpu_sc as plsc

F32 = jnp.float32
BF16 = jnp.bfloat16
NEG_INF = float("-inf")

EPS = 1e-6
GLA_HEADS, GLA_DK, GLA_DV, GLA_LOWRANK, GLA_TAU, GLA_CHUNK = 4, 64, 128, 16, 16.0, 64
GLA_QK, GLA_V = GLA_HEADS * GLA_DK, GLA_HEADS * GLA_DV
MOBA_HEADS, MOBA_DH, MOBA_BLOCK, MOBA_TOPK = 8, 64, 256, 3
MOBA_W = MOBA_HEADS * MOBA_DH
LOG2E = math.log2(math.e)
MOBA_Q_SCALE = MOBA_DH ** -0.5 * LOG2E
REL_BUCKETS, REL_MAX_DIST = 32, 128
MEM_HEADS, MEM_DH = 4, 128
MEM_W = MEM_HEADS * MEM_DH
N_GROUPS, EXPERTS_PER_GROUP, TOPK_IN_GROUP = 4, 8, 2
N_EXPERTS = N_GROUPS * EXPERTS_PER_GROUP

LANES = 128
SUBLANES = 8
VMEM_LIMIT_BYTES = 56 * 1024 * 1024

PROJ_ROWS = 512
GLA_ROWS = 1024
MOBA_GROUP = 4
MIX_ROWS = 512
MIX_PARTS = 2
EXPERT_ROWS = 768
MOVE_ROWS = 1024
PLAN_ROWS = 2048
SC_GATHER_ROWS = 128
SC_GATHER_WORDS = 256
EXPERT_PARTS = 2


def _params(*semantics):
    return pltpu.CompilerParams(dimension_semantics=semantics, vmem_limit_bytes=VMEM_LIMIT_BYTES)


def _full(shape):
    return pl.BlockSpec(shape, lambda *_: (0,) * len(shape))


def _rms(x, g):
    return x * lax.rsqrt(jnp.mean(x * x, axis=-1, keepdims=True) + EPS) * g


def _pack_bf16_pairs(x):
    n = x.shape[1] // 2
    bits = pltpu.bitcast(x.astype(BF16).astype(F32), jnp.uint32)
    return bits[:, n:] | (bits[:, :n] >> 16)


def _unpack_bf16_pairs(w):
    lo = pltpu.bitcast(w << 16, F32)
    hi = pltpu.bitcast(w & jnp.uint32(0xFFFF0000), F32)
    return jnp.concatenate([lo, hi], axis=1)


def _proj_kernel(x_ref, g_ref, w_qk, w_v, w_r, w_a, w_up, b_a, w_qm, w_km, w_vm, w_zg, w_zm,
                 o_qk, o_v, o_r, o_la, o_qm, o_km, o_vm, o_zg, o_zm):
    h = _rms(x_ref[...], g_ref[...]).astype(BF16)

    def mm(w_ref):
        return jnp.dot(h, w_ref[...], preferred_element_type=F32)

    o_qk[...] = mm(w_qk)
    o_v[...] = mm(w_v).astype(BF16)
    o_r[...] = mm(w_r)
    a_lr = mm(w_a).astype(BF16)
    pre = jnp.dot(a_lr, w_up[...], preferred_element_type=F32) + b_a[...]
    o_la[...] = jax.nn.log_sigmoid(pre) * (1.0 / GLA_TAU)

    def mm_t(wt_ref):
        return lax.dot_general(wt_ref[...], h, (((1,), (1,)), ((), ())), preferred_element_type=F32)

    def store_blocks(o_ref, val_t):
        for c in range(o_ref.shape[0]):
            o_ref[c] = val_t[:, c * MOBA_BLOCK:(c + 1) * MOBA_BLOCK]

    store_blocks(o_qm, (mm_t(w_qm) * MOBA_Q_SCALE).astype(BF16))
    o_km[...] = mm(w_km).astype(BF16)
    store_blocks(o_vm, mm_t(w_vm).astype(BF16))
    o_zg[...] = mm(w_zg)
    o_zm[...] = mm(w_zm)


def _project(x2d, g_mix, w_in, w_alpha_up, b_alpha):
    T, D = x2d.shape
    rows = min(PROJ_ROWS, T)
    assert T % rows == 0
    splits = (GLA_QK, GLA_QK, GLA_V, GLA_V, GLA_LOWRANK, MOBA_W, MOBA_W, MOBA_W, D, D)
    offs = [0]
    for s in splits:
        offs.append(offs[-1] + s)
    wb = w_in.astype(BF16)
    sec = lambda i, j: wb[:, offs[i]:offs[j]]
    w_a = jnp.pad(sec(4, 5), ((0, 0), (0, LANES - GLA_LOWRANK)))
    w_up = jnp.pad(w_alpha_up.astype(BF16), ((0, LANES - GLA_LOWRANK), (0, 0)))
    weights = [sec(0, 2), sec(2, 3), sec(3, 4), w_a, w_up, b_alpha.reshape(1, GLA_QK).astype(F32),
               sec(5, 6).T, sec(6, 7), sec(7, 8).T, sec(8, 9), sec(9, 10)]
    out_defs = [(2 * GLA_QK, F32, False), (GLA_V, BF16, False), (GLA_V, F32, False), (GLA_QK, F32, False),
                (MOBA_W, BF16, True), (MOBA_W, BF16, False), (MOBA_W, BF16, True), (D, F32, False),
                (D, F32, False)]
    BS = MOBA_BLOCK
    assert rows % BS == 0
    row_spec = lambda n: pl.BlockSpec((rows, n), lambda i: (i, 0))
    blk_spec = lambda n: pl.BlockSpec((rows // BS, n, BS), lambda i: (i, 0, 0))
    return pl.pallas_call(
        _proj_kernel,
        grid=(T // rows,),
        in_specs=[row_spec(D), _full((1, D))] + [_full(w.shape) for w in weights],
        out_specs=[blk_spec(n) if t else row_spec(n) for n, _, t in out_defs],
        out_shape=[jax.ShapeDtypeStruct((T // BS, n, BS) if t else (T, n), dt) for n, dt, t in out_defs],
        compiler_params=_params("parallel"),
        name="norm_in_proj",
    )(x2d, g_mix.reshape(1, D).astype(F32), *weights)


def _gla_kernel(qk_ref, la_ref, v_ref, r_ref, g_ref, o_ref, state_ref, obuf_ref):
    C, H, DK, DV = GLA_CHUNK, GLA_HEADS, GLA_DK, GLA_DV
    rows = qk_ref.shape[1]

    @pl.when(pl.program_id(1) == 0)
    def _():
        state_ref[...] = jnp.zeros_like(state_ref)

    tri = (lax.broadcasted_iota(jnp.int32, (C, C), 0) >= lax.broadcasted_iota(jnp.int32, (C, C), 1)).astype(BF16)
    lane_head = lax.broadcasted_iota(jnp.int32, (1, H * DK), 1) // DK
    head_masks = [(lane_head == h).astype(F32) for h in range(H)]
    stack_row = lax.broadcasted_iota(jnp.int32, (H * C, C), 0) % C
    stack_col = lax.broadcasted_iota(jnp.int32, (H * C, C), 1)
    causal = stack_col <= stack_row
    same_head = (lax.broadcasted_iota(jnp.int32, (H * DV, H * DK), 0) // DV
                 == lax.broadcasted_iota(jnp.int32, (H * DV, H * DK), 1) // DK)
    scale = DK ** -0.5

    def stack(m):
        return jnp.concatenate([m * head_masks[h] for h in range(H)], axis=0).astype(BF16)

    chunks = [slice(c * C, (c + 1) * C) for c in range(rows // C)]

    def cum_log_decay(sl):
        la = la_ref[0, sl, :]
        p1 = la.astype(BF16)
        r1 = la - p1.astype(F32)
        p2 = r1.astype(BF16)
        p3 = (r1 - p2.astype(F32)).astype(BF16)
        s3 = jnp.dot(tri, jnp.concatenate([p1, p2, p3], axis=1), preferred_element_type=F32)
        w = H * DK
        return (s3[:, 0:w] + s3[:, w:2 * w]) + s3[:, 2 * w:3 * w]

    b_all = [cum_log_decay(sl) for sl in chunks]

    qe_all, ke_all, kd_all, qb_all, decay_all = [], [], [], [], []
    for sl, b in zip(chunks, b_all):
        q = qk_ref[0, sl, 0:H * DK] * scale
        k = qk_ref[0, sl, H * DK:2 * H * DK]
        b_last = b[C - 1:C, :]
        b_mid = b[C // 2 - 1:C // 2, :]
        qe_all.append(stack(q * jnp.exp(b - b_mid)))
        ke_all.append((k * jnp.exp(b_mid - b)).astype(BF16))
        kd_all.append((k * jnp.exp(b_last - b)).astype(BF16))
        qb_all.append((q * jnp.exp(b)).astype(BF16))
        decay_all.append(jnp.exp(b_last))

    att_all = [jnp.where(causal, lax.dot_general(qe, ke, (((1,), (1,)), ((), ())), preferred_element_type=F32),
                         0.0).astype(BF16) for qe, ke in zip(qe_all, ke_all)]

    o_intra_all, kv_all = [], []
    for sl, att, kd in zip(chunks, att_all, kd_all):
        v = v_ref[0, sl, :]
        o_intra_all.append(jnp.concatenate(
            [jnp.dot(att[h * C:(h + 1) * C, :], v[:, h * DV:(h + 1) * DV], preferred_element_type=F32)
             for h in range(H)], axis=1))
        kv_t = lax.dot_general(v, kd, (((0,), (0,)), ((), ())), preferred_element_type=F32)
        kv_all.append(jnp.where(same_head, kv_t, 0.0))

    state_t = state_ref[...]
    for sl, qb, decay, kv_t, o_intra in zip(chunks, qb_all, decay_all, kv_all, o_intra_all):
        o_inter = lax.dot_general(qb, state_t.astype(BF16), (((1,), (1,)), ((), ())),
                                  preferred_element_type=F32)
        obuf_ref[sl, :] = o_intra + o_inter
        state_t = decay * state_t + kv_t
    state_ref[...] = state_t

    r = r_ref[0]
    for h in range(H):
        cs = slice(h * DV, (h + 1) * DV)
        y = _rms(obuf_ref[:, cs], g_ref[:, cs])
        rh = r[:, cs]
        o_ref[0, :, cs] = (y * (rh * jax.nn.sigmoid(rh))).astype(BF16)


def _gla(qk, la, v, r, g_head, B, S):
    rows = min(GLA_ROWS, S)
    assert S % rows == 0 and rows % GLA_CHUNK == 0
    spec = lambda n: pl.BlockSpec((1, rows, n), lambda b, i: (b, i, 0))
    return pl.pallas_call(
        _gla_kernel,
        grid=(B, S // rows),
        in_specs=[spec(2 * GLA_QK), spec(GLA_QK), spec(GLA_V), spec(GLA_V), _full((1, GLA_V))],
        out_specs=spec(GLA_V),
        out_shape=jax.ShapeDtypeStruct((B, S, GLA_V), BF16),
        scratch_shapes=[pltpu.VMEM((GLA_V, GLA_QK), F32), pltpu.VMEM((rows, GLA_V), F32)],
        compiler_params=_params("parallel", "arbitrary"),
        name="gla_chunked",
    )(qk.reshape(B, S, -1), la.reshape(B, S, -1), v.reshape(B, S, -1), r.reshape(B, S, -1),
      g_head.reshape(1, GLA_V).astype(F32))


def _t5_bucket(dist):
    n = jnp.maximum(dist, 0)
    max_exact = REL_BUCKETS // 2
    nf = jnp.maximum(n, 1).astype(F32)
    large = max_exact + (jnp.log(nf / max_exact) / math.log(REL_MAX_DIST / max_exact)
                         * (REL_BUCKETS - max_exact)).astype(jnp.int32)
    large = jnp.minimum(large, REL_BUCKETS - 1)
    return jnp.where(n < max_exact, n, large)


def _moba_bias_kernel(rb_ref, o_ref):
    BS, G = MOBA_BLOCK, MOBA_GROUP
    grp, kind = pl.program_id(0), pl.program_id(1)
    d = (lax.broadcasted_iota(jnp.int32, (BS, BS), 1) - lax.broadcasted_iota(jnp.int32, (BS, BS), 0)
         + kind * BS)
    bucket = _t5_bucket(d)
    for h in range(G):
        val = jnp.zeros((BS, BS), F32)
        for bkt in range(REL_BUCKETS):
            val = jnp.where(bucket == bkt, rb_ref[bkt, grp * G + h] * LOG2E, val)
        o_ref[0, 0, :, h * BS:(h + 1) * BS] = jnp.where(d >= 0, val, NEG_INF)


def _moba_bias(rel_bias):
    BS, G = MOBA_BLOCK, MOBA_GROUP
    n_grp = MOBA_HEADS // G
    return pl.pallas_call(
        _moba_bias_kernel,
        grid=(n_grp, 2),
        in_specs=[pl.BlockSpec(memory_space=pltpu.SMEM)],
        out_specs=pl.BlockSpec((1, 1, BS, G * BS), lambda g, k: (g, k, 0, 0)),
        out_shape=jax.ShapeDtypeStruct((n_grp, 2, BS, G * BS), F32),
        compiler_params=_params("parallel", "parallel"),
        name="moba_bias_tables",
    )(rel_bias.astype(F32))


def _moba_kernel(rb_ref, q_ref, k_ref, v_ref, bias_ref, o_ref,
                 qs_ref, kmean_ref, sel_ref, m_ref, l_ref, acc_ref, sbuf_ref):
    BS, G, DH = MOBA_BLOCK, MOBA_GROUP, MOBA_DH
    NBP = kmean_ref.shape[0]
    grp, i = pl.program_id(1), pl.program_id(2)

    @pl.when(i == 0)
    def _():
        S = k_ref.shape[1]
        blk_of_key = lax.broadcasted_iota(jnp.int32, (NBP, S), 1) // BS
        ind = (blk_of_key == lax.broadcasted_iota(jnp.int32, (NBP, S), 0)).astype(BF16)
        kmean_ref[...] = jnp.dot(ind, k_ref[0], preferred_element_type=F32) * (1.0 / BS)

    qt = q_ref[0]
    sub_head = lax.broadcasted_iota(jnp.int32, (G * DH, 1), 0) // DH
    for h in range(G):
        qs_ref[h] = jnp.where(sub_head == h, qt, jnp.zeros_like(qt))

    kmean = kmean_ref[...].astype(BF16)
    gate = jnp.concatenate([jnp.dot(kmean, qs_ref[h], preferred_element_type=F32) for h in range(G)],
                           axis=1)
    blk = lax.broadcasted_iota(jnp.int32, gate.shape, 0)
    gate = jnp.where(blk < i, gate, NEG_INF)
    for t in range(MOBA_TOPK):
        mx = jnp.max(gate, axis=0, keepdims=True)
        hit = (gate == mx) & (mx > NEG_INF)
        idx = jnp.min(jnp.where(hit, blk, NBP), axis=0, keepdims=True)
        sel_ref[t:t + 1, :] = idx
        gate = jnp.where(blk == idx, NEG_INF, gate)

    def mask_row(j):
        hit = (sel_ref[0:1, :] == j) | (sel_ref[1:2, :] == j) | (sel_ref[2:3, :] == j)
        return jnp.where(hit, 0.0, NEG_INF)

    def with_ones(vt):
        return jnp.concatenate([vt, jnp.ones((8, vt.shape[1]), BF16)], axis=0)

    def far_scores(j0, n, slot):
        kj = k_ref[0, pl.ds(pl.multiple_of(j0 * BS, BS), n * BS), :]
        for h in range(G):
            cs = slice(h * BS, (h + 1) * BS)
            sbuf_ref[slot, h, 0:n * BS, :] = jnp.dot(kj, qs_ref[h], preferred_element_type=F32)

    j_prev = jnp.maximum(i - 1, 0)
    k_own = k_ref[0, pl.ds(pl.multiple_of(i * BS, BS), BS), :]
    k_prev = k_ref[0, pl.ds(pl.multiple_of(j_prev * BS, BS), BS), :]
    vt_near = jnp.concatenate([v_ref[i], v_ref[j_prev]], axis=1)
    prev_mask = mask_row(i - 1)
    for h in range(G):
        cs = slice(h * BS, (h + 1) * BS)
        sbuf_ref[1, h, 0:BS, :] = jnp.dot(k_own, qs_ref[h], preferred_element_type=F32)
        sbuf_ref[1, h, BS:2 * BS, :] = jnp.dot(k_prev, qs_ref[h], preferred_element_type=F32)
    far_scores(0, 2, 0)
    m_out, l_out, acc_out = [], [], []
    for h in range(G):
        cs = slice(h * BS, (h + 1) * BS)
        s_own = sbuf_ref[1, h, 0:BS, :] + bias_ref[0, 0, :, cs]
        s_prev = sbuf_ref[1, h, BS:2 * BS, :] + (bias_ref[0, 1, :, cs] + prev_mask[:, cs])
        m0 = jnp.maximum(jnp.max(s_own, axis=0, keepdims=True), jnp.max(s_prev, axis=0, keepdims=True))
        pb = jnp.concatenate([jnp.exp2(s_own - m0).astype(BF16), jnp.exp2(s_prev - m0).astype(BF16)], axis=0)
        pv = jnp.dot(with_ones(vt_near[h * DH:(h + 1) * DH, :]), pb, preferred_element_type=F32)
        m_out.append(m0)
        l_out.append(pv[DH:DH + 1, :])
        acc_out.append(pv[0:DH, :])
    m_ref[...] = jnp.concatenate(m_out, axis=1)
    l_ref[...] = jnp.concatenate(l_out, axis=1)
    acc_ref[...] = jnp.concatenate(acc_out, axis=1)

    lane_head = lax.broadcasted_iota(jnp.int32, (1, G * BS), 1) // BS
    far_bias = jnp.zeros((1, G * BS), F32)
    for h in range(G):
        far_bias = jnp.where(lane_head == h, rb_ref[REL_BUCKETS - 1, grp * G + h] * LOG2E, far_bias)

    def visit_far(j0, n, slot, ahead=None):
        vjt = jnp.concatenate([v_ref[j0 + t] for t in range(n)], axis=1)
        addend = [far_bias + mask_row(j0 + t) for t in range(n)]
        m_old, l_old, acc_old = m_ref[...], l_ref[...], acc_ref[...]
        m_out, l_out, acc_out = [], [], []
        if ahead is not None:
            k_next = k_ref[0, pl.ds(pl.multiple_of(ahead[0] * BS, BS), 2 * BS), :]
        for h in range(G):
            cs = slice(h * BS, (h + 1) * BS)
            if ahead is not None:
                sbuf_ref[ahead[1], h] = jnp.dot(k_next, qs_ref[h], preferred_element_type=F32)
            s = sbuf_ref[slot, h, 0:n * BS, :]
            mx = jnp.max(s[0:BS], axis=0, keepdims=True) + addend[0][:, cs]
            for t in range(1, n):
                mx = jnp.maximum(mx, jnp.max(s[t * BS:(t + 1) * BS], axis=0, keepdims=True) + addend[t][:, cs])
            m_new = jnp.maximum(m_old[:, cs], mx)
            pb = jnp.concatenate([jnp.exp2(s[t * BS:(t + 1) * BS] - (m_new - addend[t][:, cs])).astype(BF16)
                                  for t in range(n)], axis=0)
            pv = jnp.dot(with_ones(vjt[h * DH:(h + 1) * DH, :]), pb, preferred_element_type=F32)
            alpha = jnp.exp2(m_old[:, cs] - m_new)
            m_out.append(m_new)
            l_out.append(alpha * l_old[:, cs] + pv[DH:DH + 1, :])
            acc_out.append(alpha * acc_old[:, cs] + pv[0:DH, :])
        m_ref[...] = jnp.concatenate(m_out, axis=1)
        l_ref[...] = jnp.concatenate(l_out, axis=1)
        acc_ref[...] = jnp.concatenate(acc_out, axis=1)

    n_far = jnp.maximum(i - 1, 0)
    n_pairs = lax.shift_right_logical(n_far, 1)

    def pair(p, slot, look_ahead=True):
        ahead = (2 * jnp.minimum(p + 1, n_pairs - 1), 1 - slot) if look_ahead else None
        visit_far(2 * p, 2, slot, ahead=ahead)

    def far_octet(w, carry):
        for t in range(4):
            pair(4 * w + t, t % 2)
        return carry

    lax.fori_loop(0, lax.shift_right_logical(n_pairs, 2), far_octet, 0)

    @pl.when((n_pairs & 2) != 0)
    def _():
        base = 4 * lax.shift_right_logical(n_pairs, 2)
        pair(base, 0)
        pair(base + 1, 1)

    @pl.when((n_pairs & 1) != 0)
    def _():
        pair(n_pairs - 1, 0, look_ahead=False)

    @pl.when(n_far % 2 == 1)
    def _():
        far_scores(n_far - 1, 1, 1)
        visit_far(n_far - 1, 1, 1)

    out_t = jnp.concatenate([acc_ref[:, h * BS:(h + 1) * BS] / l_ref[:, h * BS:(h + 1) * BS] for h in range(G)],
                            axis=0)
    o_ref[0] = out_t.T.astype(o_ref.dtype)


def _moba(qm_t, km, vm_t, bias, rel_bias, B, S):
    BS, G, DH = MOBA_BLOCK, MOBA_GROUP, MOBA_DH
    W = G * DH
    n_grp = MOBA_HEADS // G
    assert S % BS == 0 and S >= 2 * BS
    NB = S // BS
    NBP = -(-NB // 8) * 8
    return pl.pallas_call(
        _moba_kernel,
        grid=(B, n_grp, NB),
        in_specs=[pl.BlockSpec(memory_space=pltpu.SMEM),
                  pl.BlockSpec((1, W, BS), lambda b, g, i: (b * NB + i, g, 0)),
                  pl.BlockSpec((1, S, W), lambda b, g, i: (b, 0, g)),
                  pl.BlockSpec((NB, W, BS), lambda b, g, i: (b, g, 0)),
                  pl.BlockSpec((1, 2, BS, G * BS), lambda b, g, i: (g, 0, 0, 0))],
        out_specs=pl.BlockSpec((1, BS, W), lambda b, g, i: (b, i, g)),
        out_shape=jax.ShapeDtypeStruct((B, S, MOBA_W), BF16),
        scratch_shapes=[pltpu.VMEM((G, W, BS), BF16), pltpu.VMEM((NBP, W), F32),
                        pltpu.VMEM((8, G * BS), jnp.int32),
                        pltpu.VMEM((1, G * BS), F32), pltpu.VMEM((1, G * BS), F32),
                        pltpu.VMEM((DH, G * BS), F32), pltpu.VMEM((2, G, 2 * BS, BS), F32)],
        compiler_params=_params("parallel", "parallel", "arbitrary"),
        name="moba_attention",
    )(rel_bias.astype(F32), qm_t, km.reshape(B, S, -1), vm_t, bias)


def _mem_kv_kernel(mem_ref, g_ref, w_ref, k_ref, v_ref):
    kv = jnp.dot(_rms(mem_ref[0], g_ref[...]).astype(BF16), w_ref[...], preferred_element_type=F32)
    k_ref[0] = kv[:, :MEM_W].astype(BF16)
    v_ref[0] = kv[:, MEM_W:].astype(BF16)


def _mem_kv(mem, g_mem, w_ckv):
    B, M, D = mem.shape
    spec = pl.BlockSpec((1, M, MEM_W), lambda b: (b, 0, 0))
    return pl.pallas_call(
        _mem_kv_kernel,
        grid=(B,),
        in_specs=[pl.BlockSpec((1, M, D), lambda b: (b, 0, 0)), _full((1, D)), _full((D, 2 * MEM_W))],
        out_specs=[spec, spec],
        out_shape=[jax.ShapeDtypeStruct((B, M, MEM_W), BF16)] * 2,
        compiler_params=_params("parallel"),
        name="memory_kv",
    )(mem, g_mem.reshape(1, D).astype(F32), w_ckv.astype(BF16))


INFO_W0, INFO_W1, INFO_E0, INFO_E1, INFO_R0, INFO_R1 = range(6)
BLK_EXPERT, BLK_VALID = range(2)
ROUTER_GROUP_LANE0, ROUTER_EXPERT_LANE0 = 0, N_GROUPS


def _mix_kernel(n_pieces, x_ref, og_ref, om_ref, zg_ref, zm_ref, mk_ref, mv_ref, gc_ref, gm_ref,
                wpg, wpm, wout, wcq, wco, wr, br,
                x2_ref, info_ref, cnt_ref, *rest):
    hp_refs, base_ref = rest[:n_pieces], rest[n_pieces]
    first = (pl.program_id(0) == 0) & (pl.program_id(1) == 0)

    @pl.when(first)
    def _():
        base_ref[...] = jnp.zeros_like(base_ref)

    def mm(a, w_ref):
        return jnp.dot(a.astype(BF16), w_ref[...], preferred_element_type=F32)

    n_rows = x_ref.shape[1]
    sub = n_rows // MIX_PARTS
    parts = [slice(p * sub, (p + 1) * sub) for p in range(MIX_PARTS)]

    merged = [jax.nn.sigmoid(zg_ref[0, rs, :]) * mm(og_ref[0, rs, :], wpg)
              + jax.nn.sigmoid(zm_ref[0, rs, :]) * mm(om_ref[0, rs, :], wpm) for rs in parts]
    x1 = [x_ref[0, rs, :] + mm(m, wout) for rs, m in zip(parts, merged)]

    qc = [mm(_rms(v, gc_ref[...]), wcq).astype(BF16) for v in x1]

    def mem_attention(q):
        heads = []
        for h in range(MEM_HEADS):
            cs = slice(h * MEM_DH, (h + 1) * MEM_DH)
            s = lax.dot_general(q[:, cs], mk_ref[0, :, cs], (((1,), (1,)), ((), ())),
                                preferred_element_type=F32) * (MEM_DH ** -0.5)
            p = jnp.exp(s - jnp.max(s, axis=-1, keepdims=True))
            o = jnp.dot(p.astype(BF16), mv_ref[0, :, cs], preferred_element_type=F32)
            heads.append(o / jnp.sum(p, axis=-1, keepdims=True))
        return jnp.concatenate(heads, axis=-1)

    attn = [mem_attention(q) for q in qc]
    x2 = [v + mm(a, wco) for v, a in zip(x1, attn)]
    for rs, v in zip(parts, x2):
        x2_ref[0, rs, :] = v

    hm = [_rms(v, gm_ref[...]) for v in x2]
    logits = [lax.dot_general(wr[...], h.astype(BF16), (((1,), (1,)), ((), ())), preferred_element_type=F32)
              + br[...] for h in hm]
    for rs, h in zip(parts, hm):
        words = _pack_bf16_pairs(h)
        for c, hp_ref in enumerate(hp_refs):
            hp_ref[0, rs, :] = words[:, c * SC_GATHER_WORDS:(c + 1) * SC_GATHER_WORDS]
    n_log = -(-(N_GROUPS + N_EXPERTS) // SUBLANES) * SUBLANES
    row_id = lax.broadcasted_iota(jnp.int32, (n_log, sub), 0)
    is_grp = row_id < N_GROUPS
    e_id = row_id - ROUTER_EXPERT_LANE0

    def route(lg):
        gl = jnp.where(is_grp, lg, NEG_INF)
        ge = jnp.exp(gl - jnp.max(gl, axis=0, keepdims=True))
        g_prob = ge / jnp.sum(ge, axis=0, keepdims=True)
        p_grp = jnp.max(g_prob, axis=0, keepdims=True)
        grp = jnp.min(jnp.where((g_prob == p_grp) & is_grp, row_id, LANES), axis=0, keepdims=True)
        in_grp = (e_id >= grp * EXPERTS_PER_GROUP) & (e_id < (grp + 1) * EXPERTS_PER_GROUP)
        el = jnp.where(in_grp, lg, NEG_INF)
        ee = jnp.exp(el - jnp.max(el, axis=0, keepdims=True))
        e_prob = jnp.where(in_grp, ee / jnp.sum(ee, axis=0, keepdims=True), -1.0)
        p0 = jnp.max(e_prob, axis=0, keepdims=True)
        e0 = jnp.min(jnp.where(e_prob == p0, e_id, LANES), axis=0, keepdims=True)
        e_rest = jnp.where(e_id == e0, -1.0, e_prob)
        p1 = jnp.max(e_rest, axis=0, keepdims=True)
        e1 = jnp.min(jnp.where(e_rest == p1, e_id, LANES), axis=0, keepdims=True)
        return e0, e1, p_grp * p0 / (p0 + p1), p_grp * p1 / (p0 + p1)

    routed = [route(lg[0:n_log, :]) for lg in logits]

    expert = lax.broadcasted_iota(jnp.int32, (LANES, sub), 0)
    before = (lax.broadcasted_iota(jnp.int32, (sub, sub), 0)
              < lax.broadcasted_iota(jnp.int32, (sub, sub), 1)).astype(BF16)
    field = lax.broadcasted_iota(jnp.int32, (LANES, sub), 0)
    base = base_ref[...]
    for rs, (e0, e1, w0, w1) in zip(parts, routed):
        onehot = ((expert == e0) | (expert == e1)).astype(F32)
        seen = base + jnp.dot(onehot.astype(BF16), before, preferred_element_type=F32)
        r0 = jnp.sum(jnp.where(expert == e0, seen, 0.0), axis=0, keepdims=True)
        r1 = jnp.sum(jnp.where(expert == e1, seen, 0.0), axis=0, keepdims=True)
        base = base + jnp.sum(onehot, axis=1, keepdims=True)
        info_t = jnp.zeros((LANES, sub), F32)
        for ln, val in ((INFO_W0, w0), (INFO_W1, w1), (INFO_E0, e0.astype(F32)), (INFO_E1, e1.astype(F32)),
                        (INFO_R0, r0), (INFO_R1, r1)):
            info_t = jnp.where(field == ln, val, info_t)
        info_ref[0, rs, :] = info_t.T
    base_ref[...] = base
    cnt_ref[...] = jnp.broadcast_to(base, (LANES, LANES)).T[0:1, :]


def _mix(x, o_g, o_m, z_g, z_m, mem_k, mem_v, g_cross, g_moe, w_proj_gla, w_proj_moba, w_out, w_cq, w_co,
         w_rg, b_rg, w_re, b_re):
    B, S, D = x.shape
    M = mem_k.shape[1]
    rows = min(MIX_ROWS, S)
    assert S % rows == 0
    pad = LANES - N_GROUPS - N_EXPERTS
    wr = jnp.pad(jnp.concatenate([w_rg, w_re], axis=1), ((0, 0), (0, pad))).astype(BF16).T
    br = jnp.pad(jnp.concatenate([b_rg, b_re]), (0, pad)).reshape(LANES, 1).astype(F32)
    weights = [w_proj_gla.astype(BF16), w_proj_moba.astype(BF16), w_out.astype(BF16), w_cq.astype(BF16),
               w_co.astype(BF16), wr, br]
    tile = lambda n: pl.BlockSpec((1, rows, n), lambda b, i: (b, i, 0))
    memspec = pl.BlockSpec((1, M, MEM_W), lambda b, i: (b, 0, 0))
    n_pieces = D // 2 // SC_GATHER_WORDS
    return pl.pallas_call(
        functools.partial(_mix_kernel, n_pieces),
        grid=(B, S // rows),
        in_specs=[tile(D), tile(GLA_V), tile(MOBA_W), tile(D), tile(D), memspec, memspec,
                  _full((1, D)), _full((1, D))] + [_full(w.shape) for w in weights],
        out_specs=[tile(D), tile(LANES), _full((1, LANES))] + [tile(SC_GATHER_WORDS)] * n_pieces,
        out_shape=[jax.ShapeDtypeStruct((B, S, D), F32), jax.ShapeDtypeStruct((B, S, LANES), F32),
                   jax.ShapeDtypeStruct((1, LANES), F32)]
        + [jax.ShapeDtypeStruct((B, S, SC_GATHER_WORDS), jnp.uint32)] * n_pieces,
        scratch_shapes=[pltpu.VMEM((LANES, 1), F32)],
        compiler_params=_params("arbitrary", "arbitrary"),
        name="merge_memattn_router",
    )(x, o_g, o_m, z_g.reshape(B, S, D), z_m.reshape(B, S, D), mem_k, mem_v,
      g_cross.reshape(1, D).astype(F32), g_moe.reshape(1, D).astype(F32), *weights)


def _plan_kernel(cnt_ref, info_ref, dest_ref, blk_ref):
    rows = info_ref.shape[0]
    lane1 = lax.broadcasted_iota(jnp.int32, (1, LANES), 1)
    nblk = jnp.floor((cnt_ref[...] + (EXPERT_ROWS - 1)) * (1.0 / EXPERT_ROWS))
    nblk = jnp.where(lane1 < N_EXPERTS, nblk, 0.0)
    hi = jnp.floor(nblk * (1.0 / 256.0))
    lo = nblk - 256.0 * hi
    upto = (lax.broadcasted_iota(jnp.int32, (LANES, LANES), 0)
            <= lax.broadcasted_iota(jnp.int32, (LANES, LANES), 1)).astype(BF16)
    digits = jnp.concatenate([jnp.broadcast_to(hi, (8, LANES)), jnp.broadcast_to(lo, (8, LANES))], axis=0)
    sums = jnp.dot(digits.astype(BF16), upto, preferred_element_type=F32)
    pend = sums[0:1] * 256.0 + sums[8:9]
    pstart_rows = (pend - nblk) * EXPERT_ROWS

    info = info_ref[...]
    lane = lax.broadcasted_iota(jnp.int32, (rows, LANES), 1)

    def field(ln):
        return jnp.sum(jnp.where(lane == ln, info, 0.0), axis=-1, keepdims=True)

    def dest(e, r):
        return jnp.sum(jnp.where(lane == e.astype(jnp.int32), pstart_rows, 0.0), axis=-1, keepdims=True) + r

    d0 = dest(field(INFO_E0), field(INFO_R0))
    d1 = dest(field(INFO_E1), field(INFO_R1))
    cols = jnp.where(lane == 0, d0, jnp.where(lane == 1, d1, 0.0))
    dest_ref[...] = cols.T[0:SUBLANES, :].astype(jnp.int32)

    @pl.when(pl.program_id(0) == 0)
    def _():
        n = lax.broadcasted_iota(jnp.int32, (blk_ref.shape[0], LANES), 0).astype(F32)
        blane = lax.broadcasted_iota(jnp.int32, (blk_ref.shape[0], LANES), 1)
        done = jnp.where((pend <= n) & (lane1 < N_EXPERTS), 1.0, 0.0)
        e = jnp.minimum(jnp.sum(done, axis=-1, keepdims=True), N_EXPERTS - 1.0)
        mine = blane == e.astype(jnp.int32)
        first_blk = jnp.sum(jnp.where(mine, pend - nblk, 0.0), axis=-1, keepdims=True)
        count = jnp.sum(jnp.where(mine, cnt_ref[...], 0.0), axis=-1, keepdims=True)
        valid = jnp.clip(count - EXPERT_ROWS * (n[:, 0:1] - first_blk), 0.0, float(EXPERT_ROWS))
        blk_ref[...] = jnp.where(blane == BLK_EXPERT, e, jnp.where(blane == BLK_VALID, valid, 0.0)).astype(jnp.int32)


def _plan(counts, info2d, n_blk):
    T = info2d.shape[0]
    rows = min(PLAN_ROWS, T)
    assert T % rows == 0
    n_blk_pad = -(-n_blk // SUBLANES) * SUBLANES
    return pl.pallas_call(
        _plan_kernel,
        grid=(T // rows,),
        in_specs=[_full((1, LANES)), pl.BlockSpec((rows, LANES), lambda i: (i, 0))],
        out_specs=[pl.BlockSpec((SUBLANES, rows), lambda i: (0, i)), _full((n_blk_pad, LANES))],
        out_shape=[jax.ShapeDtypeStruct((SUBLANES, T), jnp.int32),
                   jax.ShapeDtypeStruct((n_blk_pad, LANES), jnp.int32)],
        compiler_params=_params("arbitrary"),
        name="dispatch_plan",
    )(counts, info2d)


def _sc_windows(n_rows):
    n_inner = 32
    assert n_rows % (SC_GATHER_ROWS * n_inner) == 0
    return n_rows // (SC_GATHER_ROWS * n_inner), n_inner


def _sc_mesh():
    return plsc.VectorSubcoreMesh(core_axis_name="c", subcore_axis_name="s")


def _sc_scatter_rows(src, idx_a, idx_b, n_out):
    T, W = src.shape
    n_outer, n_inner = _sc_windows(T)
    win = lambda i, j: i * n_inner + j

    @pl.kernel(out_type=jax.ShapeDtypeStruct((n_out, W), src.dtype), mesh=_sc_mesh(), scratch_types=[])
    def scatter_kernel(s_hbm, a_hbm, b_hbm, o_hbm):
        def body(s_vmem, a_vmem, b_vmem):
            pltpu.sync_copy(s_vmem, o_hbm.at[a_vmem.at[0]])
            pltpu.sync_copy(s_vmem, o_hbm.at[b_vmem.at[0]])

        pltpu.emit_pipeline(
            body,
            grid=(n_outer, n_inner),
            in_specs=[pl.BlockSpec((SC_GATHER_ROWS, W), index_map=lambda i, j: (win(i, j), 0)),
                      pl.BlockSpec((1, SC_GATHER_ROWS), index_map=lambda i, j: (0, win(i, j))),
                      pl.BlockSpec((1, SC_GATHER_ROWS), index_map=lambda i, j: (0, win(i, j)))],
            out_specs=[],
            core_axis_name=("c", "s"),
            dimension_semantics=(pltpu.PARALLEL, pltpu.PARALLEL),
        )(s_hbm, a_hbm, b_hbm)

    return scatter_kernel(src, idx_a, idx_b)


def _sc_gather_rows(table, idx):
    M = idx.shape[1]
    W = table.shape[1]
    n_outer, n_inner = _sc_windows(M)
    win = lambda i, j: i * n_inner + j

    @pl.kernel(out_type=jax.ShapeDtypeStruct((M, W), table.dtype), mesh=_sc_mesh(), scratch_types=[])
    def gather_kernel(t_hbm, i_hbm, o_hbm):
        def body(i_vmem, o_vmem):
            pltpu.sync_copy(t_hbm.at[i_vmem.at[0]], o_vmem)

        pltpu.emit_pipeline(
            body,
            grid=(n_outer, n_inner),
            in_specs=[pl.BlockSpec((1, SC_GATHER_ROWS), index_map=lambda i, j: (0, win(i, j)))],
            out_specs=[pl.BlockSpec((SC_GATHER_ROWS, W), index_map=lambda i, j: (win(i, j), 0))],
            core_axis_name=("c", "s"),
            dimension_semantics=(pltpu.PARALLEL, pltpu.PARALLEL),
        )(i_hbm, o_hbm)

    return gather_kernel(table, idx)


def _expert_kernel(n_pieces, blk_e_ref, blk_valid_ref, *refs):
    xs_refs, (wg_ref, wu_ref, wd_ref) = refs[:n_pieces], refs[n_pieces:n_pieces + 3]
    y_refs, (wg_bf, wu_bf, wd_bf) = refs[n_pieces + 3:2 * n_pieces + 3], refs[2 * n_pieces + 3:]
    n = pl.program_id(0)
    prev = blk_e_ref[jnp.maximum(n - 1, 0)]

    @pl.when((n == 0) | (blk_e_ref[n] != prev))
    def _():
        wg_bf[...] = wg_ref[0].astype(BF16)
        wu_bf[...] = wu_ref[0].astype(BF16)
        wd_bf[...] = wd_ref[0].astype(BF16)

    valid = blk_valid_ref[n]

    @pl.when(valid == 0)
    def _():
        for y_ref in y_refs:
            y_ref[...] = jnp.zeros_like(y_ref)

    @pl.when(valid > 0)
    def _():
        sub = xs_refs[0].shape[0] // EXPERT_PARTS
        parts = [slice(p * sub, (p + 1) * sub) for p in range(EXPERT_PARTS)]
        row = lax.broadcasted_iota(jnp.int32, (sub, 1), 0)

        def load(p, rs):
            words = jnp.concatenate([r[rs, :] for r in xs_refs], axis=1)
            words = jnp.where(row + p * sub < valid, words, jnp.zeros_like(words))
            return _unpack_bf16_pairs(words).astype(BF16)

        xb = [load(p, rs) for p, rs in enumerate(parts)]
        gate = [jnp.dot(v, wg_bf[...], preferred_element_type=F32) for v in xb]
        up = [jnp.dot(v, wu_bf[...], preferred_element_type=F32) for v in xb]
        hid = [(g * jax.nn.sigmoid(g) * u).astype(BF16) for g, u in zip(gate, up)]
        for rs, hv in zip(parts, hid):
            words = _pack_bf16_pairs(jnp.dot(hv, wd_bf[...], preferred_element_type=F32))
            for c, y_ref in enumerate(y_refs):
                y_ref[rs, :] = words[:, c * SC_GATHER_WORDS:(c + 1) * SC_GATHER_WORDS]


def _experts(blk_e, blk_valid, xs_pieces, w_gate, w_up, w_down):
    cap = xs_pieces[0].shape[0]
    _, D, DE = w_gate.shape
    n_blk = cap // EXPERT_ROWS
    n_pieces = len(xs_pieces)
    piece = pl.BlockSpec((EXPERT_ROWS, SC_GATHER_WORDS), lambda n, e, v: (n, 0))
    return pl.pallas_call(
        functools.partial(_expert_kernel, n_pieces),
        grid_spec=pltpu.PrefetchScalarGridSpec(
            num_scalar_prefetch=2,
            grid=(n_blk,),
            in_specs=[piece] * n_pieces
            + [pl.BlockSpec((1, D, DE), lambda n, e, v: (e[n], 0, 0)),
               pl.BlockSpec((1, D, DE), lambda n, e, v: (e[n], 0, 0)),
               pl.BlockSpec((1, DE, D), lambda n, e, v: (e[n], 0, 0))],
            out_specs=[piece] * n_pieces,
            scratch_shapes=[pltpu.VMEM((D, DE), BF16), pltpu.VMEM((D, DE), BF16), pltpu.VMEM((DE, D), BF16)]),
        out_shape=[jax.ShapeDtypeStruct((cap, SC_GATHER_WORDS), jnp.uint32)] * n_pieces,
        compiler_params=_params("arbitrary"),
        name="moe_experts",
    )(blk_e, blk_valid, *xs_pieces, w_gate, w_up, w_down)


def _combine_dense_kernel(final_norm, n_pieces, *refs):
    yg_refs, (x_ref, info_ref, g_ref, o_ref) = refs[:n_pieces], refs[n_pieces:]
    info = info_ref[...]
    w0 = info[:, INFO_W0:INFO_W0 + 1]
    w1 = info[:, INFO_W1:INFO_W1 + 1]
    y = [_unpack_bf16_pairs(jnp.concatenate([r[slot] for r in yg_refs], axis=1)) for slot in range(TOPK_IN_GROUP)]
    out = x_ref[...] + (w0 * y[0] + w1 * y[1])
    o_ref[...] = _rms(out, g_ref[...]) if final_norm else out


def _combine_dense(yg_pieces, x2d, info2d, g_final, final_norm):
    T, D = x2d.shape
    rows = min(MOVE_ROWS, T)
    n_pieces = len(yg_pieces)
    return pl.pallas_call(
        functools.partial(_combine_dense_kernel, final_norm, n_pieces),
        grid=(T // rows,),
        in_specs=[pl.BlockSpec((TOPK_IN_GROUP, rows, SC_GATHER_WORDS), lambda i: (0, i, 0))] * n_pieces
        + [pl.BlockSpec((rows, D), lambda i: (i, 0)), pl.BlockSpec((rows, LANES), lambda i: (i, 0)),
           _full((1, D))],
        out_specs=pl.BlockSpec((rows, D), lambda i: (i, 0)),
        out_shape=jax.ShapeDtypeStruct((T, D), F32),
        compiler_params=_params("parallel"),
        name="moe_combine_dense_final_norm",
    )(*yg_pieces, x2d, info2d, g_final.reshape(1, D).astype(F32))


def kernel(x, mem, g_mem, rel_bias, g_mix, w_in, w_alpha_up, b_alpha, g_gla_head, w_proj_gla, w_proj_moba,
           w_out, g_cross, w_cq, w_ckv, w_co, g_moe, w_router_group, b_router_group, w_router_expert,
           b_router_expert, w_exp_gate, w_exp_up, w_exp_down, g_final):
    B, S, D = x.shape
    T = B * S
    depth = g_mix.shape[0]
    n_assign = T * TOPK_IN_GROUP
    n_blk = -(-(n_assign + N_EXPERTS * (EXPERT_ROWS - 1)) // EXPERT_ROWS)
    cap = n_blk * EXPERT_ROWS

    mem_bias = _moba_bias(rel_bias)
    for l in range(depth):
        qk, v_g, r_g, la, q_m, k_m, v_m, z_g, z_m = _project(x.reshape(T, D), g_mix[l], w_in[l], w_alpha_up[l],
                                                             b_alpha[l])
        o_g = _gla(qk, la, v_g, r_g, g_gla_head[l], B, S)
        o_m = _moba(q_m, k_m, v_m, mem_bias, rel_bias, B, S)
        mem_k, mem_v = _mem_kv(mem, g_mem, w_ckv[l])
        x2, info, counts, *hm_pieces = _mix(x, o_g, o_m, z_g, z_m, mem_k, mem_v, g_cross[l], g_moe[l],
                                            w_proj_gla[l], w_proj_moba[l], w_out[l], w_cq[l], w_co[l],
                                            w_router_group[l], b_router_group[l], w_router_expert[l],
                                            b_router_expert[l])
        x2d, info2d = x2.reshape(T, D), info.reshape(T, LANES)
        dest, blk = _plan(counts, info2d, n_blk)
        idx = dest[0:TOPK_IN_GROUP].reshape(1, TOPK_IN_GROUP * T)
        xs_pieces = [_sc_scatter_rows(h.reshape(T, SC_GATHER_WORDS), idx[:, :T], idx[:, T:], cap)
                     for h in hm_pieces]
        y_pieces = _experts(blk[:n_blk, BLK_EXPERT], blk[:n_blk, BLK_VALID], xs_pieces, w_exp_gate[l],
                            w_exp_up[l], w_exp_down[l])
        yg = [_sc_gather_rows(y, idx).reshape(TOPK_IN_GROUP, T, SC_GATHER_WORDS) for y in y_pieces]
        x = _combine_dense(yg, x2d, info2d, g_final, final_norm=(l == depth - 1)).reshape(B, S, D)
    return x
```

```python
import functools
import math

import jax
import jax.numpy as jnp
from jax import lax
from jax.experimental import pallas as pl
from jax.experimental.pallas import tpu as pltpu
from jax.experimental.pallas import tpu_sc as plsc

F32 = jnp.float32
BF16 = jnp.bfloat16
NEG_INF = float("-inf")

EPS = 1e-6
GLA_HEADS, GLA_DK, GLA_DV, GLA_LOWRANK, GLA_TAU, GLA_CHUNK = 4, 64, 128, 16, 16.0, 64
GLA_QK, GLA_V = GLA_HEADS * GLA_DK, GLA_HEADS * GLA_DV
MOBA_HEADS, MOBA_DH, MOBA_BLOCK, MOBA_TOPK = 8, 64, 256, 3
MOBA_W = MOBA_HEADS * MOBA_DH
LOG2E = math.log2(math.e)
MOBA_Q_SCALE = MOBA_DH ** -0.5 * LOG2E
REL_BUCKETS, REL_MAX_DIST = 32, 128
MEM_HEADS, MEM_DH = 4, 128
MEM_W = MEM_HEADS * MEM_DH
N_GROUPS, EXPERTS_PER_GROUP, TOPK_IN_GROUP = 4, 8, 2
N_EXPERTS = N_GROUPS * EXPERTS_PER_GROUP

LANES = 128
SUBLANES = 8
VMEM_LIMIT_BYTES = 56 * 1024 * 1024

PROJ_ROWS = 512
GLA_ROWS = 1024
MOBA_GROUP = 4
MIX_ROWS = 512
MIX_PARTS = 2
EXPERT_ROWS = 768
MOVE_ROWS = 1024
COMBINE_RANGES = 2
PLAN_ROWS = 2048
SC_GATHER_ROWS = 128
SC_GATHER_WORDS = 256
EXPERT_PARTS = 2


def _params(*semantics):
    return pltpu.CompilerParams(dimension_semantics=semantics, vmem_limit_bytes=VMEM_LIMIT_BYTES)


def _full(shape):
    return pl.BlockSpec(shape, lambda *_: (0,) * len(shape))


def _rms(x, g):
    return x * lax.rsqrt(jnp.mean(x * x, axis=-1, keepdims=True) + EPS) * g


def _pack_bf16_pairs(x):
    n = x.shape[1] // 2
    bits = pltpu.bitcast(x.astype(BF16).astype(F32), jnp.uint32)
    return bits[:, n:] | (bits[:, :n] >> 16)


def _unpack_bf16_pairs(w):
    lo = pltpu.bitcast(w << 16, F32)
    hi = pltpu.bitcast(w & jnp.uint32(0xFFFF0000), F32)
    return jnp.concatenate([lo, hi], axis=1)


def _proj_kernel(x_ref, g_ref, w_qk, w_v, w_r, w_a, w_up, b_a, w_qm, w_km, w_vm, w_zg, w_zm,
                 o_qk, o_v, o_r, o_la, o_qm, o_km, o_vm, o_zg, o_zm):
    h = _rms(x_ref[...], g_ref[...]).astype(BF16)

    def mm(w_ref):
        return jnp.dot(h, w_ref[...], preferred_element_type=F32)

    o_qk[...] = mm(w_qk)
    o_v[...] = mm(w_v).astype(BF16)
    o_r[...] = mm(w_r)
    a_lr = mm(w_a).astype(BF16)
    pre = jnp.dot(a_lr, w_up[...], preferred_element_type=F32) + b_a[...]
    o_la[...] = jax.nn.log_sigmoid(pre) * (1.0 / GLA_TAU)

    def mm_t(wt_ref):
        return lax.dot_general(wt_ref[...], h, (((1,), (1,)), ((), ())), preferred_element_type=F32)

    def store_blocks(o_ref, val_t):
        for c in range(o_ref.shape[0]):
            o_ref[c] = val_t[:, c * MOBA_BLOCK:(c + 1) * MOBA_BLOCK]

    store_blocks(o_qm, (mm_t(w_qm) * MOBA_Q_SCALE).astype(BF16))
    o_km[...] = mm(w_km).astype(BF16)
    store_blocks(o_vm, mm_t(w_vm).astype(BF16))
    o_zg[...] = mm(w_zg)
    o_zm[...] = mm(w_zm)


def _project(x2d, g_mix, w_in, w_alpha_up, b_alpha):
    T, D = x2d.shape
    rows = min(PROJ_ROWS, T)
    assert T % rows == 0
    splits = (GLA_QK, GLA_QK, GLA_V, GLA_V, GLA_LOWRANK, MOBA_W, MOBA_W, MOBA_W, D, D)
    offs = [0]
    for s in splits:
        offs.append(offs[-1] + s)
    wb = w_in.astype(BF16)
    sec = lambda i, j: wb[:, offs[i]:offs[j]]
    w_a = jnp.pad(sec(4, 5), ((0, 0), (0, LANES - GLA_LOWRANK)))
    w_up = jnp.pad(w_alpha_up.astype(BF16), ((0, LANES - GLA_LOWRANK), (0, 0)))
    weights = [sec(0, 2), sec(2, 3), sec(3, 4), w_a, w_up, b_alpha.reshape(1, GLA_QK).astype(F32),
               sec(5, 6).T, sec(6, 7), sec(7, 8).T, sec(8, 9), sec(9, 10)]
    out_defs = [(2 * GLA_QK, F32, False), (GLA_V, BF16, False), (GLA_V, F32, False), (GLA_QK, F32, False),
                (MOBA_W, BF16, True), (MOBA_W, BF16, False), (MOBA_W, BF16, True), (D, F32, False),
                (D, F32, False)]
    BS = MOBA_BLOCK
    assert rows % BS == 0
    row_spec = lambda n: pl.BlockSpec((rows, n), lambda i: (i, 0))
    blk_spec = lambda n: pl.BlockSpec((rows // BS, n, BS), lambda i: (i, 0, 0))
    return pl.pallas_call(
        _proj_kernel,
        grid=(T // rows,),
        in_specs=[row_spec(D), _full((1, D))] + [_full(w.shape) for w in weights],
        out_specs=[blk_spec(n) if t else row_spec(n) for n, _, t in out_defs],
        out_shape=[jax.ShapeDtypeStruct((T // BS, n, BS) if t else (T, n), dt) for n, dt, t in out_defs],
        compiler_params=_params("parallel"),
        name="norm_in_proj",
    )(x2d, g_mix.reshape(1, D).astype(F32), *weights)


def _gla_kernel(qk_ref, la_ref, v_ref, r_ref, g_ref, o_ref, state_ref, obuf_ref):
    C, H, DK, DV = GLA_CHUNK, GLA_HEADS, GLA_DK, GLA_DV
    rows = qk_ref.shape[1]

    @pl.when(pl.program_id(1) == 0)
    def _():
        state_ref[...] = jnp.zeros_like(state_ref)

    tri = (lax.broadcasted_iota(jnp.int32, (C, C), 0) >= lax.broadcasted_iota(jnp.int32, (C, C), 1)).astype(BF16)
    lane_head = lax.broadcasted_iota(jnp.int32, (1, H * DK), 1) // DK
    head_masks = [(lane_head == h).astype(F32) for h in range(H)]
    stack_row = lax.broadcasted_iota(jnp.int32, (H * C, C), 0) % C
    stack_col = lax.broadcasted_iota(jnp.int32, (H * C, C), 1)
    causal = stack_col <= stack_row
    same_head = (lax.broadcasted_iota(jnp.int32, (H * DV, H * DK), 0) // DV
                 == lax.broadcasted_iota(jnp.int32, (H * DV, H * DK), 1) // DK)
    scale = DK ** -0.5

    def stack(m):
        return jnp.concatenate([m * head_masks[h] for h in range(H)], axis=0).astype(BF16)

    chunks = [slice(c * C, (c + 1) * C) for c in range(rows // C)]

    def cum_log_decay(sl):
        la = la_ref[0, sl, :]
        p1 = la.astype(BF16)
        r1 = la - p1.astype(F32)
        p2 = r1.astype(BF16)
        p3 = (r1 - p2.astype(F32)).astype(BF16)
        s3 = jnp.dot(tri, jnp.concatenate([p1, p2, p3], axis=1), preferred_element_type=F32)
        w = H * DK
        return (s3[:, 0:w] + s3[:, w:2 * w]) + s3[:, 2 * w:3 * w]

    b_all = [cum_log_decay(sl) for sl in chunks]

    qe_all, ke_all, kd_all, qb_all, decay_all = [], [], [], [], []
    for sl, b in zip(chunks, b_all):
        q = qk_ref[0, sl, 0:H * DK] * scale
        k = qk_ref[0, sl, H * DK:2 * H * DK]
        b_last = b[C - 1:C, :]
        b_mid = b[C // 2 - 1:C // 2, :]
        qe_all.append(stack(q * jnp.exp(b - b_mid)))
        ke_all.append((k * jnp.exp(b_mid - b)).astype(BF16))
        kd_all.append((k * jnp.exp(b_last - b)).astype(BF16))
        qb_all.append((q * jnp.exp(b)).astype(BF16))
        decay_all.append(jnp.exp(b_last))

    att_all = [jnp.where(causal, lax.dot_general(qe, ke, (((1,), (1,)), ((), ())), preferred_element_type=F32),
                         0.0).astype(BF16) for qe, ke in zip(qe_all, ke_all)]

    o_intra_all, kv_all = [], []
    for sl, att, kd in zip(chunks, att_all, kd_all):
        v = v_ref[0, sl, :]
        o_intra_all.append(jnp.concatenate(
            [jnp.dot(att[h * C:(h + 1) * C, :], v[:, h * DV:(h + 1) * DV], preferred_element_type=F32)
             for h in range(H)], axis=1))
        kv_t = lax.dot_general(v, kd, (((0,), (0,)), ((), ())), preferred_element_type=F32)
        kv_all.append(jnp.where(same_head, kv_t, 0.0))

    state_t = state_ref[...]
    for sl, qb, decay, kv_t, o_intra in zip(chunks, qb_all, decay_all, kv_all, o_intra_all):
        o_inter = lax.dot_general(qb, state_t.astype(BF16), (((1,), (1,)), ((), ())),
                                  preferred_element_type=F32)
        obuf_ref[sl, :] = o_intra + o_inter
        state_t = decay * state_t + kv_t
    state_ref[...] = state_t

    r = r_ref[0]
    for h in range(H):
        cs = slice(h * DV, (h + 1) * DV)
        y = _rms(obuf_ref[:, cs], g_ref[:, cs])
        rh = r[:, cs]
        o_ref[0, :, cs] = (y * (rh * jax.nn.sigmoid(rh))).astype(BF16)


def _gla(qk, la, v, r, g_head, B, S):
    rows = min(GLA_ROWS, S)
    assert S % rows == 0 and rows % GLA_CHUNK == 0
    spec = lambda n: pl.BlockSpec((1, rows, n), lambda b, i: (b, i, 0))
    return pl.pallas_call(
        _gla_kernel,
        grid=(B, S // rows),
        in_specs=[spec(2 * GLA_QK), spec(GLA_QK), spec(GLA_V), spec(GLA_V), _full((1, GLA_V))],
        out_specs=spec(GLA_V),
        out_shape=jax.ShapeDtypeStruct((B, S, GLA_V), BF16),
        scratch_shapes=[pltpu.VMEM((GLA_V, GLA_QK), F32), pltpu.VMEM((rows, GLA_V), F32)],
        compiler_params=_params("parallel", "arbitrary"),
        name="gla_chunked",
    )(qk.reshape(B, S, -1), la.reshape(B, S, -1), v.reshape(B, S, -1), r.reshape(B, S, -1),
      g_head.reshape(1, GLA_V).astype(F32))


def _t5_bucket(dist):
    n = jnp.maximum(dist, 0)
    max_exact = REL_BUCKETS // 2
    nf = jnp.maximum(n, 1).astype(F32)
    large = max_exact + (jnp.log(nf / max_exact) / math.log(REL_MAX_DIST / max_exact)
                         * (REL_BUCKETS - max_exact)).astype(jnp.int32)
    large = jnp.minimum(large, REL_BUCKETS - 1)
    return jnp.where(n < max_exact, n, large)


def _moba_bias_kernel(rb_ref, o_ref):
    BS, G = MOBA_BLOCK, MOBA_GROUP
    grp, kind = pl.program_id(0), pl.program_id(1)
    d = (lax.broadcasted_iota(jnp.int32, (BS, BS), 1) - lax.broadcasted_iota(jnp.int32, (BS, BS), 0)
         + kind * BS)
    bucket = _t5_bucket(d)
    for h in range(G):
        val = jnp.zeros((BS, BS), F32)
        for bkt in range(REL_BUCKETS):
            val = jnp.where(bucket == bkt, rb_ref[bkt, grp * G + h] * LOG2E, val)
        o_ref[0, 0, :, h * BS:(h + 1) * BS] = jnp.where(d >= 0, val, NEG_INF)


def _moba_bias(rel_bias):
    BS, G = MOBA_BLOCK, MOBA_GROUP
    n_grp = MOBA_HEADS // G
    return pl.pallas_call(
        _moba_bias_kernel,
        grid=(n_grp, 2),
        in_specs=[pl.BlockSpec(memory_space=pltpu.SMEM)],
        out_specs=pl.BlockSpec((1, 1, BS, G * BS), lambda g, k: (g, k, 0, 0)),
        out_shape=jax.ShapeDtypeStruct((n_grp, 2, BS, G * BS), F32),
        compiler_params=_params("parallel", "parallel"),
        name="moba_bias_tables",
    )(rel_bias.astype(F32))


def _moba_kernel(rb_ref, q_ref, k_ref, v_ref, bias_ref, o_ref,
                 qs_ref, kmean_ref, sel_ref, m_ref, l_ref, acc_ref, sbuf_ref):
    BS, G, DH = MOBA_BLOCK, MOBA_GROUP, MOBA_DH
    NBP = kmean_ref.shape[0]
    grp, i = pl.program_id(1), pl.program_id(2)

    @pl.when(i == 0)
    def _():
        S = k_ref.shape[1]
        blk_of_key = lax.broadcasted_iota(jnp.int32, (NBP, S), 1) // BS
        ind = (blk_of_key == lax.broadcasted_iota(jnp.int32, (NBP, S), 0)).astype(BF16)
        kmean_ref[...] = jnp.dot(ind, k_ref[0], preferred_element_type=F32) * (1.0 / BS)

    qt = q_ref[0]
    sub_head = lax.broadcasted_iota(jnp.int32, (G * DH, 1), 0) // DH
    for h in range(G):
        qs_ref[h] = jnp.where(sub_head == h, qt, jnp.zeros_like(qt))

    kmean = kmean_ref[...].astype(BF16)
    gate = jnp.concatenate([jnp.dot(kmean, qs_ref[h], preferred_element_type=F32) for h in range(G)],
                           axis=1)
    blk = lax.broadcasted_iota(jnp.int32, gate.shape, 0)
    gate = jnp.where(blk < i, gate, NEG_INF)
    for t in range(MOBA_TOPK):
        mx = jnp.max(gate, axis=0, keepdims=True)
        hit = (gate == mx) & (mx > NEG_INF)
        idx = jnp.min(jnp.where(hit, blk, NBP), axis=0, keepdims=True)
        sel_ref[t:t + 1, :] = idx
        gate = jnp.where(blk == idx, NEG_INF, gate)

    def mask_row(j):
        hit = (sel_ref[0:1, :] == j) | (sel_ref[1:2, :] == j) | (sel_ref[2:3, :] == j)
        return jnp.where(hit, 0.0, NEG_INF)

    def with_ones(vt):
        return jnp.concatenate([vt, jnp.ones((8, vt.shape[1]), BF16)], axis=0)

    def far_scores(j0, n, slot):
        kj = k_ref[0, pl.ds(pl.multiple_of(j0 * BS, BS), n * BS), :]
        for h in range(G):
            cs = slice(h * BS, (h + 1) * BS)
            sbuf_ref[slot, h, 0:n * BS, :] = jnp.dot(kj, qs_ref[h], preferred_element_type=F32)

    j_prev = jnp.maximum(i - 1, 0)
    k_own = k_ref[0, pl.ds(pl.multiple_of(i * BS, BS), BS), :]
    k_prev = k_ref[0, pl.ds(pl.multiple_of(j_prev * BS, BS), BS), :]
    vt_near = jnp.concatenate([v_ref[i], v_ref[j_prev]], axis=1)
    prev_mask = mask_row(i - 1)
    for h in range(G):
        cs = slice(h * BS, (h + 1) * BS)
        sbuf_ref[1, h, 0:BS, :] = jnp.dot(k_own, qs_ref[h], preferred_element_type=F32)
        sbuf_ref[1, h, BS:2 * BS, :] = jnp.dot(k_prev, qs_ref[h], preferred_element_type=F32)
    far_scores(0, 2, 0)
    m_out, l_out, acc_out = [], [], []
    for h in range(G):
        cs = slice(h * BS, (h + 1) * BS)
        s_own = sbuf_ref[1, h, 0:BS, :] + bias_ref[0, 0, :, cs]
        s_prev = sbuf_ref[1, h, BS:2 * BS, :] + (bias_ref[0, 1, :, cs] + prev_mask[:, cs])
        m0 = jnp.maximum(jnp.max(s_own, axis=0, keepdims=True), jnp.max(s_prev, axis=0, keepdims=True))
        pb = jnp.concatenate([jnp.exp2(s_own - m0).astype(BF16), jnp.exp2(s_prev - m0).astype(BF16)], axis=0)
        pv = jnp.dot(with_ones(vt_near[h * DH:(h + 1) * DH, :]), pb, preferred_element_type=F32)
        m_out.append(m0)
        l_out.append(pv[DH:DH + 1, :])
        acc_out.append(pv[0:DH, :])
    m_ref[...] = jnp.concatenate(m_out, axis=1)
    l_ref[...] = jnp.concatenate(l_out, axis=1)
    acc_ref[...] = jnp.concatenate(acc_out, axis=1)

    lane_head = lax.broadcasted_iota(jnp.int32, (1, G * BS), 1) // BS
    far_bias = jnp.zeros((1, G * BS), F32)
    for h in range(G):
        far_bias = jnp.where(lane_head == h, rb_ref[REL_BUCKETS - 1, grp * G + h] * LOG2E, far_bias)

    def visit_far(j0, n, slot, ahead=None):
        vjt = jnp.concatenate([v_ref[j0 + t] for t in range(n)], axis=1)
        addend = [far_bias + mask_row(j0 + t) for t in range(n)]
        m_old, l_old, acc_old = m_ref[...], l_ref[...], acc_ref[...]
        m_out, l_out, acc_out = [], [], []
        if ahead is not None:
            k_next = k_ref[0, pl.ds(pl.multiple_of(ahead[0] * BS, BS), 2 * BS), :]
        for h in range(G):
            cs = slice(h * BS, (h + 1) * BS)
            if ahead is not None:
                sbuf_ref[ahead[1], h] = jnp.dot(k_next, qs_ref[h], preferred_element_type=F32)
            s = sbuf_ref[slot, h, 0:n * BS, :]
            mx = jnp.max(s[0:BS], axis=0, keepdims=True) + addend[0][:, cs]
            for t in range(1, n):
                mx = jnp.maximum(mx, jnp.max(s[t * BS:(t + 1) * BS], axis=0, keepdims=True) + addend[t][:, cs])
            m_new = jnp.maximum(m_old[:, cs], mx)
            pb = jnp.concatenate([jnp.exp2(s[t * BS:(t + 1) * BS] - (m_new - addend[t][:, cs])).astype(BF16)
                                  for t in range(n)], axis=0)
            pv = jnp.dot(with_ones(vjt[h * DH:(h + 1) * DH, :]), pb, preferred_element_type=F32)
            alpha = jnp.exp2(m_old[:, cs] - m_new)
            m_out.append(m_new)
            l_out.append(alpha * l_old[:, cs] + pv[DH:DH + 1, :])
            acc_out.append(alpha * acc_old[:, cs] + pv[0:DH, :])
        m_ref[...] = jnp.concatenate(m_out, axis=1)
        l_ref[...] = jnp.concatenate(l_out, axis=1)
        acc_ref[...] = jnp.concatenate(acc_out, axis=1)

    n_far = jnp.maximum(i - 1, 0)
    n_pairs = lax.shift_right_logical(n_far, 1)

    def pair(p, slot, look_ahead=True):
        ahead = (2 * jnp.minimum(p + 1, n_pairs - 1), 1 - slot) if look_ahead else None
        visit_far(2 * p, 2, slot, ahead=ahead)

    def far_octet(w, carry):
        for t in range(4):
            pair(4 * w + t, t % 2)
        return carry

    lax.fori_loop(0, lax.shift_right_logical(n_pairs, 2), far_octet, 0)

    @pl.when((n_pairs & 2) != 0)
    def _():
        base = 4 * lax.shift_right_logical(n_pairs, 2)
        pair(base, 0)
        pair(base + 1, 1)

    @pl.when((n_pairs & 1) != 0)
    def _():
        pair(n_pairs - 1, 0, look_ahead=False)

    @pl.when(n_far % 2 == 1)
    def _():
        far_scores(n_far - 1, 1, 1)
        visit_far(n_far - 1, 1, 1)

    out_t = jnp.concatenate([acc_ref[:, h * BS:(h + 1) * BS] / l_ref[:, h * BS:(h + 1) * BS] for h in range(G)],
                            axis=0)
    o_ref[0] = out_t.T.astype(o_ref.dtype)


def _moba(qm_t, km, vm_t, bias, rel_bias, B, S):
    BS, G, DH = MOBA_BLOCK, MOBA_GROUP, MOBA_DH
    W = G * DH
    n_grp = MOBA_HEADS // G
    assert S % BS == 0 and S >= 2 * BS
    NB = S // BS
    NBP = -(-NB // 8) * 8
    return pl.pallas_call(
        _moba_kernel,
        grid=(B, n_grp, NB),
        in_specs=[pl.BlockSpec(memory_space=pltpu.SMEM),
                  pl.BlockSpec((1, W, BS), lambda b, g, i: (b * NB + i, g, 0)),
                  pl.BlockSpec((1, S, W), lambda b, g, i: (b, 0, g)),
                  pl.BlockSpec((NB, W, BS), lambda b, g, i: (b, g, 0)),
                  pl.BlockSpec((1, 2, BS, G * BS), lambda b, g, i: (g, 0, 0, 0))],
        out_specs=pl.BlockSpec((1, BS, W), lambda b, g, i: (b, i, g)),
        out_shape=jax.ShapeDtypeStruct((B, S, MOBA_W), BF16),
        scratch_shapes=[pltpu.VMEM((G, W, BS), BF16), pltpu.VMEM((NBP, W), F32),
                        pltpu.VMEM((8, G * BS), jnp.int32),
                        pltpu.VMEM((1, G * BS), F32), pltpu.VMEM((1, G * BS), F32),
                        pltpu.VMEM((DH, G * BS), F32), pltpu.VMEM((2, G, 2 * BS, BS), F32)],
        compiler_params=_params("parallel", "parallel", "arbitrary"),
        name="moba_attention",
    )(rel_bias.astype(F32), qm_t, km.reshape(B, S, -1), vm_t, bias)


def _mem_kv_kernel(mem_ref, g_ref, w_ref, k_ref, v_ref):
    kv = jnp.dot(_rms(mem_ref[0], g_ref[...]).astype(BF16), w_ref[...], preferred_element_type=F32)
    k_ref[0] = kv[:, :MEM_W].astype(BF16)
    v_ref[0] = kv[:, MEM_W:].astype(BF16)


def _mem_kv(mem, g_mem, w_ckv):
    B, M, D = mem.shape
    spec = pl.BlockSpec((1, M, MEM_W), lambda b: (b, 0, 0))
    return pl.pallas_call(
        _mem_kv_kernel,
        grid=(B,),
        in_specs=[pl.BlockSpec((1, M, D), lambda b: (b, 0, 0)), _full((1, D)), _full((D, 2 * MEM_W))],
        out_specs=[spec, spec],
        out_shape=[jax.ShapeDtypeStruct((B, M, MEM_W), BF16)] * 2,
        compiler_params=_params("parallel"),
        name="memory_kv",
    )(mem, g_mem.reshape(1, D).astype(F32), w_ckv.astype(BF16))


INFO_W0, INFO_W1, INFO_E0, INFO_E1, INFO_R0, INFO_R1 = range(6)
BLK_EXPERT, BLK_VALID = range(2)
ROUTER_GROUP_LANE0, ROUTER_EXPERT_LANE0 = 0, N_GROUPS


def _mix_kernel(n_pieces, x_ref, og_ref, om_ref, zg_ref, zm_ref, mk_ref, mv_ref, gc_ref, gm_ref,
                wpg, wpm, wout, wcq, wco, wr, br,
                x2_ref, info_ref, cnt_ref, *rest):
    hp_refs, base_ref = rest[:n_pieces], rest[n_pieces]
    first = (pl.program_id(0) == 0) & (pl.program_id(1) == 0)

    @pl.when(first)
    def _():
        base_ref[...] = jnp.zeros_like(base_ref)

    def mm(a, w_ref):
        return jnp.dot(a.astype(BF16), w_ref[...], preferred_element_type=F32)

    n_rows = x_ref.shape[1]
    sub = n_rows // MIX_PARTS
    parts = [slice(p * sub, (p + 1) * sub) for p in range(MIX_PARTS)]

    merged = [jax.nn.sigmoid(zg_ref[0, rs, :]) * mm(og_ref[0, rs, :], wpg)
              + jax.nn.sigmoid(zm_ref[0, rs, :]) * mm(om_ref[0, rs, :], wpm) for rs in parts]
    x1 = [x_ref[0, rs, :] + mm(m, wout) for rs, m in zip(parts, merged)]

    qc = [mm(_rms(v, gc_ref[...]), wcq).astype(BF16) for v in x1]

    def mem_attention(q):
        heads = []
        for h in range(MEM_HEADS):
            cs = slice(h * MEM_DH, (h + 1) * MEM_DH)
            s = lax.dot_general(q[:, cs], mk_ref[0, :, cs], (((1,), (1,)), ((), ())),
                                preferred_element_type=F32) * (MEM_DH ** -0.5)
            p = jnp.exp(s - jnp.max(s, axis=-1, keepdims=True))
            o = jnp.dot(p.astype(BF16), mv_ref[0, :, cs], preferred_element_type=F32)
            heads.append(o / jnp.sum(p, axis=-1, keepdims=True))
        return jnp.concatenate(heads, axis=-1)

    attn = [mem_attention(q) for q in qc]
    x2 = [v + mm(a, wco) for v, a in zip(x1, attn)]
    for rs, v in zip(parts, x2):
        x2_ref[0, rs, :] = v

    hm = [_rms(v, gm_ref[...]) for v in x2]
    logits = [lax.dot_general(wr[...], h.astype(BF16), (((1,), (1,)), ((), ())), preferred_element_type=F32)
              + br[...] for h in hm]
    for rs, h in zip(parts, hm):
        words = _pack_bf16_pairs(h)
        for c, hp_ref in enumerate(hp_refs):
            hp_ref[0, rs, :] = words[:, c * SC_GATHER_WORDS:(c + 1) * SC_GATHER_WORDS]
    n_log = -(-(N_GROUPS + N_EXPERTS) // SUBLANES) * SUBLANES
    row_id = lax.broadcasted_iota(jnp.int32, (n_log, sub), 0)
    is_grp = row_id < N_GROUPS
    e_id = row_id - ROUTER_EXPERT_LANE0

    def route(lg):
        gl = jnp.where(is_grp, lg, NEG_INF)
        ge = jnp.exp(gl - jnp.max(gl, axis=0, keepdims=True))
        g_prob = ge / jnp.sum(ge, axis=0, keepdims=True)
        p_grp = jnp.max(g_prob, axis=0, keepdims=True)
        grp = jnp.min(jnp.where((g_prob == p_grp) & is_grp, row_id, LANES), axis=0, keepdims=True)
        in_grp = (e_id >= grp * EXPERTS_PER_GROUP) & (e_id < (grp + 1) * EXPERTS_PER_GROUP)
        el = jnp.where(in_grp, lg, NEG_INF)
        ee = jnp.exp(el - jnp.max(el, axis=0, keepdims=True))
        e_prob = jnp.where(in_grp, ee / jnp.sum(ee, axis=0, keepdims=True), -1.0)
        p0 = jnp.max(e_prob, axis=0, keepdims=True)
        e0 = jnp.min(jnp.where(e_prob == p0, e_id, LANES), axis=0, keepdims=True)
        e_rest = jnp.where(e_id == e0, -1.0, e_prob)
        p1 = jnp.max(e_rest, axis=0, keepdims=True)
        e1 = jnp.min(jnp.where(e_rest == p1, e_id, LANES), axis=0, keepdims=True)
        return e0, e1, p_grp * p0 / (p0 + p1), p_grp * p1 / (p0 + p1)

    routed = [route(lg[0:n_log, :]) for lg in logits]

    expert = lax.broadcasted_iota(jnp.int32, (LANES, sub), 0)
    before = (lax.broadcasted_iota(jnp.int32, (sub, sub), 0)
              < lax.broadcasted_iota(jnp.int32, (sub, sub), 1)).astype(BF16)
    field = lax.broadcasted_iota(jnp.int32, (LANES, sub), 0)
    base = base_ref[...]
    for rs, (e0, e1, w0, w1) in zip(parts, routed):
        onehot = ((expert == e0) | (expert == e1)).astype(F32)
        seen = base + jnp.dot(onehot.astype(BF16), before, preferred_element_type=F32)
        r0 = jnp.sum(jnp.where(expert == e0, seen, 0.0), axis=0, keepdims=True)
        r1 = jnp.sum(jnp.where(expert == e1, seen, 0.0), axis=0, keepdims=True)
        base = base + jnp.sum(onehot, axis=1, keepdims=True)
        info_t = jnp.zeros((LANES, sub), F32)
        for ln, val in ((INFO_W0, w0), (INFO_W1, w1), (INFO_E0, e0.astype(F32)), (INFO_E1, e1.astype(F32)),
                        (INFO_R0, r0), (INFO_R1, r1)):
            info_t = jnp.where(field == ln, val, info_t)
        info_ref[0, rs, :] = info_t.T
    base_ref[...] = base
    cnt_ref[...] = jnp.broadcast_to(base, (LANES, LANES)).T[0:1, :]


def _mix(x, o_g, o_m, z_g, z_m, mem_k, mem_v, g_cross, g_moe, w_proj_gla, w_proj_moba, w_out, w_cq, w_co,
         w_rg, b_rg, w_re, b_re):
    B, S, D = x.shape
    M = mem_k.shape[1]
    rows = min(MIX_ROWS, S)
    assert S % rows == 0
    pad = LANES - N_GROUPS - N_EXPERTS
    wr = jnp.pad(jnp.concatenate([w_rg, w_re], axis=1), ((0, 0), (0, pad))).astype(BF16).T
    br = jnp.pad(jnp.concatenate([b_rg, b_re]), (0, pad)).reshape(LANES, 1).astype(F32)
    weights = [w_proj_gla.astype(BF16), w_proj_moba.astype(BF16), w_out.astype(BF16), w_cq.astype(BF16),
               w_co.astype(BF16), wr, br]
    tile = lambda n: pl.BlockSpec((1, rows, n), lambda b, i: (b, i, 0))
    memspec = pl.BlockSpec((1, M, MEM_W), lambda b, i: (b, 0, 0))
    n_pieces = D // 2 // SC_GATHER_WORDS
    return pl.pallas_call(
        functools.partial(_mix_kernel, n_pieces),
        grid=(B, S // rows),
        in_specs=[tile(D), tile(GLA_V), tile(MOBA_W), tile(D), tile(D), memspec, memspec,
                  _full((1, D)), _full((1, D))] + [_full(w.shape) for w in weights],
        out_specs=[tile(D), tile(LANES), _full((1, LANES))] + [tile(SC_GATHER_WORDS)] * n_pieces,
        out_shape=[jax.ShapeDtypeStruct((B, S, D), F32), jax.ShapeDtypeStruct((B, S, LANES), F32),
                   jax.ShapeDtypeStruct((1, LANES), F32)]
        + [jax.ShapeDtypeStruct((B, S, SC_GATHER_WORDS), jnp.uint32)] * n_pieces,
        scratch_shapes=[pltpu.VMEM((LANES, 1), F32)],
        compiler_params=_params("arbitrary", "arbitrary"),
        name="merge_memattn_router",
    )(x, o_g, o_m, z_g.reshape(B, S, D), z_m.reshape(B, S, D), mem_k, mem_v,
      g_cross.reshape(1, D).astype(F32), g_moe.reshape(1, D).astype(F32), *weights)


def _plan_kernel(cnt_ref, info_ref, dest_ref, blk_ref):
    rows = info_ref.shape[0]
    lane1 = lax.broadcasted_iota(jnp.int32, (1, LANES), 1)
    nblk = jnp.floor((cnt_ref[...] + (EXPERT_ROWS - 1)) * (1.0 / EXPERT_ROWS))
    nblk = jnp.where(lane1 < N_EXPERTS, nblk, 0.0)
    hi = jnp.floor(nblk * (1.0 / 256.0))
    lo = nblk - 256.0 * hi
    upto = (lax.broadcasted_iota(jnp.int32, (LANES, LANES), 0)
            <= lax.broadcasted_iota(jnp.int32, (LANES, LANES), 1)).astype(BF16)
    digits = jnp.concatenate([jnp.broadcast_to(hi, (8, LANES)), jnp.broadcast_to(lo, (8, LANES))], axis=0)
    sums = jnp.dot(digits.astype(BF16), upto, preferred_element_type=F32)
    pend = sums[0:1] * 256.0 + sums[8:9]
    pstart_rows = (pend - nblk) * EXPERT_ROWS

    info = info_ref[...]
    lane = lax.broadcasted_iota(jnp.int32, (rows, LANES), 1)

    def field(ln):
        return jnp.sum(jnp.where(lane == ln, info, 0.0), axis=-1, keepdims=True)

    def dest(e, r):
        return jnp.sum(jnp.where(lane == e.astype(jnp.int32), pstart_rows, 0.0), axis=-1, keepdims=True) + r

    d0 = dest(field(INFO_E0), field(INFO_R0))
    d1 = dest(field(INFO_E1), field(INFO_R1))
    cols = jnp.where(lane == 0, d0, jnp.where(lane == 1, d1, 0.0))
    dest_ref[...] = cols.T[0:SUBLANES, :].astype(jnp.int32)

    @pl.when(pl.program_id(0) == 0)
    def _():
        n = lax.broadcasted_iota(jnp.int32, (blk_ref.shape[0], LANES), 0).astype(F32)
        blane = lax.broadcasted_iota(jnp.int32, (blk_ref.shape[0], LANES), 1)
        done = jnp.where((pend <= n) & (lane1 < N_EXPERTS), 1.0, 0.0)
        e = jnp.minimum(jnp.sum(done, axis=-1, keepdims=True), N_EXPERTS - 1.0)
        mine = blane == e.astype(jnp.int32)
        first_blk = jnp.sum(jnp.where(mine, pend - nblk, 0.0), axis=-1, keepdims=True)
        count = jnp.sum(jnp.where(mine, cnt_ref[...], 0.0), axis=-1, keepdims=True)
        valid = jnp.clip(count - EXPERT_ROWS * (n[:, 0:1] - first_blk), 0.0, float(EXPERT_ROWS))
        blk_ref[...] = jnp.where(blane == BLK_EXPERT, e, jnp.where(blane == BLK_VALID, valid, 0.0)).astype(jnp.int32)


def _plan(counts, info2d, n_blk):
    T = info2d.shape[0]
    rows = min(PLAN_ROWS, T)
    assert T % rows == 0
    n_blk_pad = -(-n_blk // SUBLANES) * SUBLANES
    return pl.pallas_call(
        _plan_kernel,
        grid=(T // rows,),
        in_specs=[_full((1, LANES)), pl.BlockSpec((rows, LANES), lambda i: (i, 0))],
        out_specs=[pl.BlockSpec((SUBLANES, rows), lambda i: (0, i)), _full((n_blk_pad, LANES))],
        out_shape=[jax.ShapeDtypeStruct((SUBLANES, T), jnp.int32),
                   jax.ShapeDtypeStruct((n_blk_pad, LANES), jnp.int32)],
        compiler_params=_params("arbitrary"),
        name="dispatch_plan",
    )(counts, info2d)


def _sc_windows(n_rows):
    n_inner = 32
    assert n_rows % (SC_GATHER_ROWS * n_inner) == 0
    return n_rows // (SC_GATHER_ROWS * n_inner), n_inner


def _sc_mesh():
    return plsc.VectorSubcoreMesh(core_axis_name="c", subcore_axis_name="s")


def _sc_scatter_rows(src, idx_a, idx_b, n_out):
    T, W = src.shape
    n_outer, n_inner = _sc_windows(T)
    win = lambda i, j: i * n_inner + j

    @pl.kernel(out_type=jax.ShapeDtypeStruct((n_out, W), src.dtype), mesh=_sc_mesh(), scratch_types=[])
    def scatter_kernel(s_hbm, a_hbm, b_hbm, o_hbm):
        def body(s_vmem, a_vmem, b_vmem):
            pltpu.sync_copy(s_vmem, o_hbm.at[a_vmem.at[0]])
            pltpu.sync_copy(s_vmem, o_hbm.at[b_vmem.at[0]])

        pltpu.emit_pipeline(
            body,
            grid=(n_outer, n_inner),
            in_specs=[pl.BlockSpec((SC_GATHER_ROWS, W), index_map=lambda i, j: (win(i, j), 0)),
                      pl.BlockSpec((1, SC_GATHER_ROWS), index_map=lambda i, j: (0, win(i, j))),
                      pl.BlockSpec((1, SC_GATHER_ROWS), index_map=lambda i, j: (0, win(i, j)))],
            out_specs=[],
            core_axis_name=("c", "s"),
            dimension_semantics=(pltpu.PARALLEL, pltpu.PARALLEL),
        )(s_hbm, a_hbm, b_hbm)

    return scatter_kernel(src, idx_a, idx_b)


def _sc_gather_rows(table, idx):
    M = idx.shape[1]
    W = table.shape[1]
    n_outer, n_inner = _sc_windows(M)
    win = lambda i, j: i * n_inner + j

    @pl.kernel(out_type=jax.ShapeDtypeStruct((M, W), table.dtype), mesh=_sc_mesh(), scratch_types=[])
    def gather_kernel(t_hbm, i_hbm, o_hbm):
        def body(i_vmem, o_vmem):
            pltpu.sync_copy(t_hbm.at[i_vmem.at[0]], o_vmem)

        pltpu.emit_pipeline(
            body,
            grid=(n_outer, n_inner),
            in_specs=[pl.BlockSpec((1, SC_GATHER_ROWS), index_map=lambda i, j: (0, win(i, j)))],
            out_specs=[pl.BlockSpec((SC_GATHER_ROWS, W), index_map=lambda i, j: (win(i, j), 0))],
            core_axis_name=("c", "s"),
            dimension_semantics=(pltpu.PARALLEL, pltpu.PARALLEL),
        )(i_hbm, o_hbm)

    return gather_kernel(table, idx)


def _expert_kernel(n_pieces, blk_e_ref, blk_valid_ref, *refs):
    xs_refs, (wg_ref, wu_ref, wd_ref) = refs[:n_pieces], refs[n_pieces:n_pieces + 3]
    y_refs, (wg_bf, wu_bf, wd_bf) = refs[n_pieces + 3:2 * n_pieces + 3], refs[2 * n_pieces + 3:]
    n = pl.program_id(0)
    prev = blk_e_ref[jnp.maximum(n - 1, 0)]

    @pl.when((n == 0) | (blk_e_ref[n] != prev))
    def _():
        wg_bf[...] = wg_ref[0].astype(BF16)
        wu_bf[...] = wu_ref[0].astype(BF16)
        wd_bf[...] = wd_ref[0].astype(BF16)

    valid = blk_valid_ref[n]

    @pl.when(valid == 0)
    def _():
        for y_ref in y_refs:
            y_ref[...] = jnp.zeros_like(y_ref)

    @pl.when(valid > 0)
    def _():
        sub = xs_refs[0].shape[0] // EXPERT_PARTS
        parts = [slice(p * sub, (p + 1) * sub) for p in range(EXPERT_PARTS)]
        row = lax.broadcasted_iota(jnp.int32, (sub, 1), 0)

        def load(p, rs):
            words = jnp.concatenate([r[rs, :] for r in xs_refs], axis=1)
            words = jnp.where(row + p * sub < valid, words, jnp.zeros_like(words))
            return _unpack_bf16_pairs(words).astype(BF16)

        xb = [load(p, rs) for p, rs in enumerate(parts)]
        gate = [jnp.dot(v, wg_bf[...], preferred_element_type=F32) for v in xb]
        up = [jnp.dot(v, wu_bf[...], preferred_element_type=F32) for v in xb]
        hid = [(g * jax.nn.sigmoid(g) * u).astype(BF16) for g, u in zip(gate, up)]
        for rs, hv in zip(parts, hid):
            words = _pack_bf16_pairs(jnp.dot(hv, wd_bf[...], preferred_element_type=F32))
            for c, y_ref in enumerate(y_refs):
                y_ref[rs, :] = words[:, c * SC_GATHER_WORDS:(c + 1) * SC_GATHER_WORDS]


def _experts(blk_e, blk_valid, xs_pieces, w_gate, w_up, w_down):
    cap = xs_pieces[0].shape[0]
    _, D, DE = w_gate.shape
    n_blk = cap // EXPERT_ROWS
    n_pieces = len(xs_pieces)
    piece = pl.BlockSpec((EXPERT_ROWS, SC_GATHER_WORDS), lambda n, e, v: (n, 0))
    return pl.pallas_call(
        functools.partial(_expert_kernel, n_pieces),
        grid_spec=pltpu.PrefetchScalarGridSpec(
            num_scalar_prefetch=2,
            grid=(n_blk,),
            in_specs=[piece] * n_pieces
            + [pl.BlockSpec((1, D, DE), lambda n, e, v: (e[n], 0, 0)),
               pl.BlockSpec((1, D, DE), lambda n, e, v: (e[n], 0, 0)),
               pl.BlockSpec((1, DE, D), lambda n, e, v: (e[n], 0, 0))],
            out_specs=[piece] * n_pieces,
            scratch_shapes=[pltpu.VMEM((D, DE), BF16), pltpu.VMEM((D, DE), BF16), pltpu.VMEM((DE, D), BF16)]),
        out_shape=[jax.ShapeDtypeStruct((cap, SC_GATHER_WORDS), jnp.uint32)] * n_pieces,
        compiler_params=_params("arbitrary"),
        name="moe_experts",
    )(blk_e, blk_valid, *xs_pieces, w_gate, w_up, w_down)


def _combine_dense_kernel(final_norm, n_pieces, *refs):
    yg_refs, (x_ref, info_ref, g_ref), o_ref = refs[:n_pieces], refs[n_pieces:n_pieces + 3], refs[-1]
    info = info_ref[...]
    w0 = info[:, INFO_W0:INFO_W0 + 1]
    w1 = info[:, INFO_W1:INFO_W1 + 1]
    y = [_unpack_bf16_pairs(jnp.concatenate([r[slot] for r in yg_refs], axis=1)) for slot in range(TOPK_IN_GROUP)]
    out = x_ref[...] + (w0 * y[0] + w1 * y[1])
    o_ref[...] = _rms(out, g_ref[...]) if final_norm else out


def _combine_dense(yg_pieces, x2d, info2d, g_final, final_norm, t0, out_so_far):
    T, D = x2d.shape
    Tc = yg_pieces[0].shape[1]
    rows = min(MOVE_ROWS, Tc)
    assert Tc % rows == 0 and t0 % rows == 0
    tile0 = t0 // rows
    n_pieces = len(yg_pieces)
    in_specs = ([pl.BlockSpec((TOPK_IN_GROUP, rows, SC_GATHER_WORDS), lambda i: (0, i, 0))] * n_pieces
                + [pl.BlockSpec((rows, D), lambda i: (i + tile0, 0)),
                   pl.BlockSpec((rows, LANES), lambda i: (i + tile0, 0)), _full((1, D))])
    args = [*yg_pieces, x2d, info2d, g_final.reshape(1, D).astype(F32)]
    aliases = {}
    if out_so_far is not None:
        in_specs.append(pl.BlockSpec(memory_space=pl.ANY))
        args.append(out_so_far)
        aliases = {len(args) - 1: 0}

    def body(*refs):
        _combine_dense_kernel(final_norm, n_pieces, *refs)

    return pl.pallas_call(
        body,
        grid=(Tc // rows,),
        in_specs=in_specs,
        out_specs=pl.BlockSpec((rows, D), lambda i: (i + tile0, 0)),
        out_shape=jax.ShapeDtypeStruct((T, D), F32),
        input_output_aliases=aliases,
        compiler_params=_params("arbitrary"),
        name="moe_combine_dense_final_norm",
    )(*args)


def kernel(x, mem, g_mem, rel_bias, g_mix, w_in, w_alpha_up, b_alpha, g_gla_head, w_proj_gla, w_proj_moba,
           w_out, g_cross, w_cq, w_ckv, w_co, g_moe, w_router_group, b_router_group, w_router_expert,
           b_router_expert, w_exp_gate, w_exp_up, w_exp_down, g_final):
    B, S, D = x.shape
    T = B * S
    depth = g_mix.shape[0]
    n_assign = T * TOPK_IN_GROUP
    n_blk = -(-(n_assign + N_EXPERTS * (EXPERT_ROWS - 1)) // EXPERT_ROWS)
    cap = n_blk * EXPERT_ROWS

    mem_bias = _moba_bias(rel_bias)
    for l in range(depth):
        qk, v_g, r_g, la, q_m, k_m, v_m, z_g, z_m = _project(x.reshape(T, D), g_mix[l], w_in[l], w_alpha_up[l],
                                                             b_alpha[l])
        o_g = _gla(qk, la, v_g, r_g, g_gla_head[l], B, S)
        o_m = _moba(q_m, k_m, v_m, mem_bias, rel_bias, B, S)
        mem_k, mem_v = _mem_kv(mem, g_mem, w_ckv[l])
        x2, info, counts, *hm_pieces = _mix(x, o_g, o_m, z_g, z_m, mem_k, mem_v, g_cross[l], g_moe[l],
                                            w_proj_gla[l], w_proj_moba[l], w_out[l], w_cq[l], w_co[l],
                                            w_router_group[l], b_router_group[l], w_router_expert[l],
                                            b_router_expert[l])
        x2d, info2d = x2.reshape(T, D), info.reshape(T, LANES)
        dest, blk = _plan(counts, info2d, n_blk)
        idx = dest[0:TOPK_IN_GROUP].reshape(1, TOPK_IN_GROUP * T)
        xs_pieces = [_sc_scatter_rows(h.reshape(T, SC_GATHER_WORDS), idx[:, :T], idx[:, T:], cap)
                     for h in hm_pieces]
        y_pieces = _experts(blk[:n_blk, BLK_EXPERT], blk[:n_blk, BLK_VALID], xs_pieces, w_exp_gate[l],
                            w_exp_up[l], w_exp_down[l])
        Tc = T // COMBINE_RANGES
        out = None
        for r in range(COMBINE_RANGES):
            idx_r = dest[0:TOPK_IN_GROUP, r * Tc:(r + 1) * Tc].reshape(1, TOPK_IN_GROUP * Tc)
            yg = [_sc_gather_rows(y, idx_r).reshape(TOPK_IN_GROUP, Tc, SC_GATHER_WORDS) for y in y_pieces]
            out = _combine_dense(yg, x2d, info2d, g_final, l == depth - 1, r * Tc, out)
        x = out.reshape(B, S, D)
    return x
```

```python
import functools
import math

import jax
import jax.numpy as jnp
from jax import lax
from jax.experimental import pallas as pl
from jax.experimental.pallas import tpu as pltpu
from jax.experimental.pallas import tpu_sc as plsc

F32 = jnp.float32
BF16 = jnp.bfloat16
NEG_INF = float("-inf")

EPS = 1e-6
GLA_HEADS, GLA_DK, GLA_DV, GLA_LOWRANK, GLA_TAU, GLA_CHUNK = 4, 64, 128, 16, 16.0, 64
GLA_QK, GLA_V = GLA_HEADS * GLA_DK, GLA_HEADS * GLA_DV
MOBA_HEADS, MOBA_DH, MOBA_BLOCK, MOBA_TOPK = 8, 64, 256, 3
MOBA_W = MOBA_HEADS * MOBA_DH
LOG2E = math.log2(math.e)
MOBA_Q_SCALE = MOBA_DH ** -0.5 * LOG2E
REL_BUCKETS, REL_MAX_DIST = 32, 128
MEM_HEADS, MEM_DH = 4, 128
MEM_W = MEM_HEADS * MEM_DH
N_GROUPS, EXPERTS_PER_GROUP, TOPK_IN_GROUP = 4, 8, 2
N_EXPERTS = N_GROUPS * EXPERTS_PER_GROUP

LANES = 128
SUBLANES = 8
VMEM_LIMIT_BYTES = 56 * 1024 * 1024

PROJ_ROWS = 512
GLA_ROWS = 1024
MOBA_GROUP = 4
MIX_ROWS = 512
MIX_PARTS = 2
EXPERT_ROWS = 768
MOVE_ROWS = 1024
COMBINE_RANGES = 4
PLAN_ROWS = 2048
SC_GATHER_ROWS = 128
SC_GATHER_WORDS = 256
EXPERT_PARTS = 2


def _params(*semantics):
    return pltpu.CompilerParams(dimension_semantics=semantics, vmem_limit_bytes=VMEM_LIMIT_BYTES)


def _full(shape):
    return pl.BlockSpec(shape, lambda *_: (0,) * len(shape))


def _rms(x, g):
    return x * lax.rsqrt(jnp.mean(x * x, axis=-1, keepdims=True) + EPS) * g


def _pack_bf16_pairs(x):
    n = x.shape[1] // 2
    bits = pltpu.bitcast(x.astype(BF16).astype(F32), jnp.uint32)
    return bits[:, n:] | (bits[:, :n] >> 16)


def _unpack_bf16_pairs(w):
    lo = pltpu.bitcast(w << 16, F32)
    hi = pltpu.bitcast(w & jnp.uint32(0xFFFF0000), F32)
    return jnp.concatenate([lo, hi], axis=1)


def _proj_kernel(x_ref, g_ref, w_qk, w_v, w_r, w_a, w_up, b_a, w_qm, w_km, w_vm, w_zg, w_zm,
                 o_qk, o_v, o_r, o_la, o_qm, o_km, o_vm, o_zg, o_zm):
    h = _rms(x_ref[...], g_ref[...]).astype(BF16)

    def mm(w_ref):
        return jnp.dot(h, w_ref[...], preferred_element_type=F32)

    o_qk[...] = mm(w_qk)
    o_v[...] = mm(w_v).astype(BF16)
    o_r[...] = mm(w_r)
    a_lr = mm(w_a).astype(BF16)
    pre = jnp.dot(a_lr, w_up[...], preferred_element_type=F32) + b_a[...]
    o_la[...] = jax.nn.log_sigmoid(pre) * (1.0 / GLA_TAU)

    def mm_t(wt_ref):
        return lax.dot_general(wt_ref[...], h, (((1,), (1,)), ((), ())), preferred_element_type=F32)

    def store_blocks(o_ref, val_t):
        for c in range(o_ref.shape[0]):
            o_ref[c] = val_t[:, c * MOBA_BLOCK:(c + 1) * MOBA_BLOCK]

    store_blocks(o_qm, (mm_t(w_qm) * MOBA_Q_SCALE).astype(BF16))
    o_km[...] = mm(w_km).astype(BF16)
    store_blocks(o_vm, mm_t(w_vm).astype(BF16))
    o_zg[...] = mm(w_zg)
    o_zm[...] = mm(w_zm)


def _project(x2d, g_mix, w_in, w_alpha_up, b_alpha):
    T, D = x2d.shape
    rows = min(PROJ_ROWS, T)
    assert T % rows == 0
    splits = (GLA_QK, GLA_QK, GLA_V, GLA_V, GLA_LOWRANK, MOBA_W, MOBA_W, MOBA_W, D, D)
    offs = [0]
    for s in splits:
        offs.append(offs[-1] + s)
    wb = w_in.astype(BF16)
    sec = lambda i, j: wb[:, offs[i]:offs[j]]
    w_a = jnp.pad(sec(4, 5), ((0, 0), (0, LANES - GLA_LOWRANK)))
    w_up = jnp.pad(w_alpha_up.astype(BF16), ((0, LANES - GLA_LOWRANK), (0, 0)))
    weights = [sec(0, 2), sec(2, 3), sec(3, 4), w_a, w_up, b_alpha.reshape(1, GLA_QK).astype(F32),
               sec(5, 6).T, sec(6, 7), sec(7, 8).T, sec(8, 9), sec(9, 10)]
    out_defs = [(2 * GLA_QK, F32, False), (GLA_V, BF16, False), (GLA_V, F32, False), (GLA_QK, F32, False),
                (MOBA_W, BF16, True), (MOBA_W, BF16, False), (MOBA_W, BF16, True), (D, F32, False),
                (D, F32, False)]
    BS = MOBA_BLOCK
    assert rows % BS == 0
    row_spec = lambda n: pl.BlockSpec((rows, n), lambda i: (i, 0))
    blk_spec = lambda n: pl.BlockSpec((rows // BS, n, BS), lambda i: (i, 0, 0))
    return pl.pallas_call(
        _proj_kernel,
        grid=(T // rows,),
        in_specs=[row_spec(D), _full((1, D))] + [_full(w.shape) for w in weights],
        out_specs=[blk_spec(n) if t else row_spec(n) for n, _, t in out_defs],
        out_shape=[jax.ShapeDtypeStruct((T // BS, n, BS) if t else (T, n), dt) for n, dt, t in out_defs],
        compiler_params=_params("parallel"),
        name="norm_in_proj",
    )(x2d, g_mix.reshape(1, D).astype(F32), *weights)


def _gla_kernel(qk_ref, la_ref, v_ref, r_ref, g_ref, o_ref, state_ref, obuf_ref):
    C, H, DK, DV = GLA_CHUNK, GLA_HEADS, GLA_DK, GLA_DV
    rows = qk_ref.shape[1]

    @pl.when(pl.program_id(1) == 0)
    def _():
        state_ref[...] = jnp.zeros_like(state_ref)

    tri = (lax.broadcasted_iota(jnp.int32, (C, C), 0) >= lax.broadcasted_iota(jnp.int32, (C, C), 1)).astype(BF16)
    lane_head = lax.broadcasted_iota(jnp.int32, (1, H * DK), 1) // DK
    head_masks = [(lane_head == h).astype(F32) for h in range(H)]
    stack_row = lax.broadcasted_iota(jnp.int32, (H * C, C), 0) % C
    stack_col = lax.broadcasted_iota(jnp.int32, (H * C, C), 1)
    causal = stack_col <= stack_row
    same_head = (lax.broadcasted_iota(jnp.int32, (H * DV, H * DK), 0) // DV
                 == lax.broadcasted_iota(jnp.int32, (H * DV, H * DK), 1) // DK)
    scale = DK ** -0.5

    def stack(m):
        return jnp.concatenate([m * head_masks[h] for h in range(H)], axis=0).astype(BF16)

    chunks = [slice(c * C, (c + 1) * C) for c in range(rows // C)]

    def cum_log_decay(sl):
        la = la_ref[0, sl, :]
        p1 = la.astype(BF16)
        r1 = la - p1.astype(F32)
        p2 = r1.astype(BF16)
        p3 = (r1 - p2.astype(F32)).astype(BF16)
        s3 = jnp.dot(tri, jnp.concatenate([p1, p2, p3], axis=1), preferred_element_type=F32)
        w = H * DK
        return (s3[:, 0:w] + s3[:, w:2 * w]) + s3[:, 2 * w:3 * w]

    b_all = [cum_log_decay(sl) for sl in chunks]

    qe_all, ke_all, kd_all, qb_all, decay_all = [], [], [], [], []
    for sl, b in zip(chunks, b_all):
        q = qk_ref[0, sl, 0:H * DK] * scale
        k = qk_ref[0, sl, H * DK:2 * H * DK]
        b_last = b[C - 1:C, :]
        b_mid = b[C // 2 - 1:C // 2, :]
        qe_all.append(stack(q * jnp.exp(b - b_mid)))
        ke_all.append((k * jnp.exp(b_mid - b)).astype(BF16))
        kd_all.append((k * jnp.exp(b_last - b)).astype(BF16))
        qb_all.append((q * jnp.exp(b)).astype(BF16))
        decay_all.append(jnp.exp(b_last))

    att_all = [jnp.where(causal, lax.dot_general(qe, ke, (((1,), (1,)), ((), ())), preferred_element_type=F32),
                         0.0).astype(BF16) for qe, ke in zip(qe_all, ke_all)]

    o_intra_all, kv_all = [], []
    for sl, att, kd in zip(chunks, att_all, kd_all):
        v = v_ref[0, sl, :]
        o_intra_all.append(jnp.concatenate(
            [jnp.dot(att[h * C:(h + 1) * C, :], v[:, h * DV:(h + 1) * DV], preferred_element_type=F32)
             for h in range(H)], axis=1))
        kv_t = lax.dot_general(v, kd, (((0,), (0,)), ((), ())), preferred_element_type=F32)
        kv_all.append(jnp.where(same_head, kv_t, 0.0))

    state_t = state_ref[...]
    for sl, qb, decay, kv_t, o_intra in zip(chunks, qb_all, decay_all, kv_all, o_intra_all):
        o_inter = lax.dot_general(qb, state_t.astype(BF16), (((1,), (1,)), ((), ())),
                                  preferred_element_type=F32)
        obuf_ref[sl, :] = o_intra + o_inter
        state_t = decay * state_t + kv_t
    state_ref[...] = state_t

    r = r_ref[0]
    for h in range(H):
        cs = slice(h * DV, (h + 1) * DV)
        y = _rms(obuf_ref[:, cs], g_ref[:, cs])
        rh = r[:, cs]
        o_ref[0, :, cs] = (y * (rh * jax.nn.sigmoid(rh))).astype(BF16)


def _gla(qk, la, v, r, g_head, B, S):
    rows = min(GLA_ROWS, S)
    assert S % rows == 0 and rows % GLA_CHUNK == 0
    spec = lambda n: pl.BlockSpec((1, rows, n), lambda b, i: (b, i, 0))
    return pl.pallas_call(
        _gla_kernel,
        grid=(B, S // rows),
        in_specs=[spec(2 * GLA_QK), spec(GLA_QK), spec(GLA_V), spec(GLA_V), _full((1, GLA_V))],
        out_specs=spec(GLA_V),
        out_shape=jax.ShapeDtypeStruct((B, S, GLA_V), BF16),
        scratch_shapes=[pltpu.VMEM((GLA_V, GLA_QK), F32), pltpu.VMEM((rows, GLA_V), F32)],
        compiler_params=_params("parallel", "arbitrary"),
        name="gla_chunked",
    )(qk.reshape(B, S, -1), la.reshape(B, S, -1), v.reshape(B, S, -1), r.reshape(B, S, -1),
      g_head.reshape(1, GLA_V).astype(F32))


def _t5_bucket(dist):
    n = jnp.maximum(dist, 0)
    max_exact = REL_BUCKETS // 2
    nf = jnp.maximum(n, 1).astype(F32)
    large = max_exact + (jnp.log(nf / max_exact) / math.log(REL_MAX_DIST / max_exact)
                         * (REL_BUCKETS - max_exact)).astype(jnp.int32)
    large = jnp.minimum(large, REL_BUCKETS - 1)
    return jnp.where(n < max_exact, n, large)


def _moba_bias_kernel(rb_ref, o_ref):
    BS, G = MOBA_BLOCK, MOBA_GROUP
    grp, kind = pl.program_id(0), pl.program_id(1)
    d = (lax.broadcasted_iota(jnp.int32, (BS, BS), 1) - lax.broadcasted_iota(jnp.int32, (BS, BS), 0)
         + kind * BS)
    bucket = _t5_bucket(d)
    for h in range(G):
        val = jnp.zeros((BS, BS), F32)
        for bkt in range(REL_BUCKETS):
            val = jnp.where(bucket == bkt, rb_ref[bkt, grp * G + h] * LOG2E, val)
        o_ref[0, 0, :, h * BS:(h + 1) * BS] = jnp.where(d >= 0, val, NEG_INF)


def _moba_bias(rel_bias):
    BS, G = MOBA_BLOCK, MOBA_GROUP
    n_grp = MOBA_HEADS // G
    return pl.pallas_call(
        _moba_bias_kernel,
        grid=(n_grp, 2),
        in_specs=[pl.BlockSpec(memory_space=pltpu.SMEM)],
        out_specs=pl.BlockSpec((1, 1, BS, G * BS), lambda g, k: (g, k, 0, 0)),
        out_shape=jax.ShapeDtypeStruct((n_grp, 2, BS, G * BS), F32),
        compiler_params=_params("parallel", "parallel"),
        name="moba_bias_tables",
    )(rel_bias.astype(F32))


def _moba_kernel(rb_ref, q_ref, k_ref, v_ref, bias_ref, o_ref,
                 qs_ref, kmean_ref, sel_ref, m_ref, l_ref, acc_ref, sbuf_ref):
    BS, G, DH = MOBA_BLOCK, MOBA_GROUP, MOBA_DH
    NBP = kmean_ref.shape[0]
    grp, i = pl.program_id(1), pl.program_id(2)

    @pl.when(i == 0)
    def _():
        S = k_ref.shape[1]
        blk_of_key = lax.broadcasted_iota(jnp.int32, (NBP, S), 1) // BS
        ind = (blk_of_key == lax.broadcasted_iota(jnp.int32, (NBP, S), 0)).astype(BF16)
        kmean_ref[...] = jnp.dot(ind, k_ref[0], preferred_element_type=F32) * (1.0 / BS)

    qt = q_ref[0]
    sub_head = lax.broadcasted_iota(jnp.int32, (G * DH, 1), 0) // DH
    for h in range(G):
        qs_ref[h] = jnp.where(sub_head == h, qt, jnp.zeros_like(qt))

    kmean = kmean_ref[...].astype(BF16)
    gate = jnp.concatenate([jnp.dot(kmean, qs_ref[h], preferred_element_type=F32) for h in range(G)],
                           axis=1)
    blk = lax.broadcasted_iota(jnp.int32, gate.shape, 0)
    gate = jnp.where(blk < i, gate, NEG_INF)
    for t in range(MOBA_TOPK):
        mx = jnp.max(gate, axis=0, keepdims=True)
        hit = (gate == mx) & (mx > NEG_INF)
        idx = jnp.min(jnp.where(hit, blk, NBP), axis=0, keepdims=True)
        sel_ref[t:t + 1, :] = idx
        gate = jnp.where(blk == idx, NEG_INF, gate)

    def mask_row(j):
        hit = (sel_ref[0:1, :] == j) | (sel_ref[1:2, :] == j) | (sel_ref[2:3, :] == j)
        return jnp.where(hit, 0.0, NEG_INF)

    def with_ones(vt):
        return jnp.concatenate([vt, jnp.ones((8, vt.shape[1]), BF16)], axis=0)

    def far_scores(j0, n, slot):
        kj = k_ref[0, pl.ds(pl.multiple_of(j0 * BS, BS), n * BS), :]
        for h in range(G):
            cs = slice(h * BS, (h + 1) * BS)
            sbuf_ref[slot, h, 0:n * BS, :] = jnp.dot(kj, qs_ref[h], preferred_element_type=F32)

    j_prev = jnp.maximum(i - 1, 0)
    k_own = k_ref[0, pl.ds(pl.multiple_of(i * BS, BS), BS), :]
    k_prev = k_ref[0, pl.ds(pl.multiple_of(j_prev * BS, BS), BS), :]
    vt_near = jnp.concatenate([v_ref[i], v_ref[j_prev]], axis=1)
    prev_mask = mask_row(i - 1)
    for h in range(G):
        cs = slice(h * BS, (h + 1) * BS)
        sbuf_ref[1, h, 0:BS, :] = jnp.dot(k_own, qs_ref[h], preferred_element_type=F32)
        sbuf_ref[1, h, BS:2 * BS, :] = jnp.dot(k_prev, qs_ref[h], preferred_element_type=F32)
    far_scores(0, 2, 0)
    m_out, l_out, acc_out = [], [], []
    for h in range(G):
        cs = slice(h * BS, (h + 1) * BS)
        s_own = sbuf_ref[1, h, 0:BS, :] + bias_ref[0, 0, :, cs]
        s_prev = sbuf_ref[1, h, BS:2 * BS, :] + (bias_ref[0, 1, :, cs] + prev_mask[:, cs])
        m0 = jnp.maximum(jnp.max(s_own, axis=0, keepdims=True), jnp.max(s_prev, axis=0, keepdims=True))
        pb = jnp.concatenate([jnp.exp2(s_own - m0).astype(BF16), jnp.exp2(s_prev - m0).astype(BF16)], axis=0)
        pv = jnp.dot(with_ones(vt_near[h * DH:(h + 1) * DH, :]), pb, preferred_element_type=F32)
        m_out.append(m0)
        l_out.append(pv[DH:DH + 1, :])
        acc_out.append(pv[0:DH, :])
    m_ref[...] = jnp.concatenate(m_out, axis=1)
    l_ref[...] = jnp.concatenate(l_out, axis=1)
    acc_ref[...] = jnp.concatenate(acc_out, axis=1)

    lane_head = lax.broadcasted_iota(jnp.int32, (1, G * BS), 1) // BS
    far_bias = jnp.zeros((1, G * BS), F32)
    for h in range(G):
        far_bias = jnp.where(lane_head == h, rb_ref[REL_BUCKETS - 1, grp * G + h] * LOG2E, far_bias)

    def visit_far(j0, n, slot, ahead=None):
        vjt = jnp.concatenate([v_ref[j0 + t] for t in range(n)], axis=1)
        addend = [far_bias + mask_row(j0 + t) for t in range(n)]
        m_old, l_old, acc_old = m_ref[...], l_ref[...], acc_ref[...]
        m_out, l_out, acc_out = [], [], []
        if ahead is not None:
            k_next = k_ref[0, pl.ds(pl.multiple_of(ahead[0] * BS, BS), 2 * BS), :]
        for h in range(G):
            cs = slice(h * BS, (h + 1) * BS)
            if ahead is not None:
                sbuf_ref[ahead[1], h] = jnp.dot(k_next, qs_ref[h], preferred_element_type=F32)
            s = sbuf_ref[slot, h, 0:n * BS, :]
            mx = jnp.max(s[0:BS], axis=0, keepdims=True) + addend[0][:, cs]
            for t in range(1, n):
                mx = jnp.maximum(mx, jnp.max(s[t * BS:(t + 1) * BS], axis=0, keepdims=True) + addend[t][:, cs])
            m_new = jnp.maximum(m_old[:, cs], mx)
            pb = jnp.concatenate([jnp.exp2(s[t * BS:(t + 1) * BS] - (m_new - addend[t][:, cs])).astype(BF16)
                                  for t in range(n)], axis=0)
            pv = jnp.dot(with_ones(vjt[h * DH:(h + 1) * DH, :]), pb, preferred_element_type=F32)
            alpha = jnp.exp2(m_old[:, cs] - m_new)
            m_out.append(m_new)
            l_out.append(alpha * l_old[:, cs] + pv[DH:DH + 1, :])
            acc_out.append(alpha * acc_old[:, cs] + pv[0:DH, :])
        m_ref[...] = jnp.concatenate(m_out, axis=1)
        l_ref[...] = jnp.concatenate(l_out, axis=1)
        acc_ref[...] = jnp.concatenate(acc_out, axis=1)

    n_far = jnp.maximum(i - 1, 0)
    n_pairs = lax.shift_right_logical(n_far, 1)

    def pair(p, slot, look_ahead=True):
        ahead = (2 * jnp.minimum(p + 1, n_pairs - 1), 1 - slot) if look_ahead else None
        visit_far(2 * p, 2, slot, ahead=ahead)

    def far_octet(w, carry):
        for t in range(4):
            pair(4 * w + t, t % 2)
        return carry

    lax.fori_loop(0, lax.shift_right_logical(n_pairs, 2), far_octet, 0)

    @pl.when((n_pairs & 2) != 0)
    def _():
        base = 4 * lax.shift_right_logical(n_pairs, 2)
        pair(base, 0)
        pair(base + 1, 1)

    @pl.when((n_pairs & 1) != 0)
    def _():
        pair(n_pairs - 1, 0, look_ahead=False)

    @pl.when(n_far % 2 == 1)
    def _():
        far_scores(n_far - 1, 1, 1)
        visit_far(n_far - 1, 1, 1)

    out_t = jnp.concatenate([acc_ref[:, h * BS:(h + 1) * BS] / l_ref[:, h * BS:(h + 1) * BS] for h in range(G)],
                            axis=0)
    o_ref[0] = out_t.T.astype(o_ref.dtype)


def _moba(qm_t, km, vm_t, bias, rel_bias, B, S):
    BS, G, DH = MOBA_BLOCK, MOBA_GROUP, MOBA_DH
    W = G * DH
    n_grp = MOBA_HEADS // G
    assert S % BS == 0 and S >= 2 * BS
    NB = S // BS
    NBP = -(-NB // 8) * 8
    return pl.pallas_call(
        _moba_kernel,
        grid=(B, n_grp, NB),
        in_specs=[pl.BlockSpec(memory_space=pltpu.SMEM),
                  pl.BlockSpec((1, W, BS), lambda b, g, i: (b * NB + i, g, 0)),
                  pl.BlockSpec((1, S, W), lambda b, g, i: (b, 0, g)),
                  pl.BlockSpec((NB, W, BS), lambda b, g, i: (b, g, 0)),
                  pl.BlockSpec((1, 2, BS, G * BS), lambda b, g, i: (g, 0, 0, 0))],
        out_specs=pl.BlockSpec((1, BS, W), lambda b, g, i: (b, i, g)),
        out_shape=jax.ShapeDtypeStruct((B, S, MOBA_W), BF16),
        scratch_shapes=[pltpu.VMEM((G, W, BS), BF16), pltpu.VMEM((NBP, W), F32),
                        pltpu.VMEM((8, G * BS), jnp.int32),
                        pltpu.VMEM((1, G * BS), F32), pltpu.VMEM((1, G * BS), F32),
                        pltpu.VMEM((DH, G * BS), F32), pltpu.VMEM((2, G, 2 * BS, BS), F32)],
        compiler_params=_params("parallel", "parallel", "arbitrary"),
        name="moba_attention",
    )(rel_bias.astype(F32), qm_t, km.reshape(B, S, -1), vm_t, bias)


def _mem_kv_kernel(mem_ref, g_ref, w_ref, k_ref, v_ref):
    kv = jnp.dot(_rms(mem_ref[0], g_ref[...]).astype(BF16), w_ref[...], preferred_element_type=F32)
    k_ref[0] = kv[:, :MEM_W].astype(BF16)
    v_ref[0] = kv[:, MEM_W:].astype(BF16)


def _mem_kv(mem, g_mem, w_ckv):
    B, M, D = mem.shape
    spec = pl.BlockSpec((1, M, MEM_W), lambda b: (b, 0, 0))
    return pl.pallas_call(
        _mem_kv_kernel,
        grid=(B,),
        in_specs=[pl.BlockSpec((1, M, D), lambda b: (b, 0, 0)), _full((1, D)), _full((D, 2 * MEM_W))],
        out_specs=[spec, spec],
        out_shape=[jax.ShapeDtypeStruct((B, M, MEM_W), BF16)] * 2,
        compiler_params=_params("parallel"),
        name="memory_kv",
    )(mem, g_mem.reshape(1, D).astype(F32), w_ckv.astype(BF16))


INFO_W0, INFO_W1, INFO_E0, INFO_E1, INFO_R0, INFO_R1 = range(6)
BLK_EXPERT, BLK_VALID = range(2)
ROUTER_GROUP_LANE0, ROUTER_EXPERT_LANE0 = 0, N_GROUPS


def _mix_kernel(n_pieces, x_ref, og_ref, om_ref, zg_ref, zm_ref, mk_ref, mv_ref, gc_ref, gm_ref,
                wpg, wpm, wout, wcq, wco, wr, br,
                x2_ref, info_ref, cnt_ref, *rest):
    hp_refs, base_ref = rest[:n_pieces], rest[n_pieces]
    first = (pl.program_id(0) == 0) & (pl.program_id(1) == 0)

    @pl.when(first)
    def _():
        base_ref[...] = jnp.zeros_like(base_ref)

    def mm(a, w_ref):
        return jnp.dot(a.astype(BF16), w_ref[...], preferred_element_type=F32)

    n_rows = x_ref.shape[1]
    sub = n_rows // MIX_PARTS
    parts = [slice(p * sub, (p + 1) * sub) for p in range(MIX_PARTS)]

    merged = [jax.nn.sigmoid(zg_ref[0, rs, :]) * mm(og_ref[0, rs, :], wpg)
              + jax.nn.sigmoid(zm_ref[0, rs, :]) * mm(om_ref[0, rs, :], wpm) for rs in parts]
    x1 = [x_ref[0, rs, :] + mm(m, wout) for rs, m in zip(parts, merged)]

    qc = [mm(_rms(v, gc_ref[...]), wcq).astype(BF16) for v in x1]

    def mem_attention(q):
        heads = []
        for h in range(MEM_HEADS):
            cs = slice(h * MEM_DH, (h + 1) * MEM_DH)
            s = lax.dot_general(q[:, cs], mk_ref[0, :, cs], (((1,), (1,)), ((), ())),
                                preferred_element_type=F32) * (MEM_DH ** -0.5)
            p = jnp.exp(s - jnp.max(s, axis=-1, keepdims=True))
            o = jnp.dot(p.astype(BF16), mv_ref[0, :, cs], preferred_element_type=F32)
            heads.append(o / jnp.sum(p, axis=-1, keepdims=True))
        return jnp.concatenate(heads, axis=-1)

    attn = [mem_attention(q) for q in qc]
    x2 = [v + mm(a, wco) for v, a in zip(x1, attn)]
    for rs, v in zip(parts, x2):
        x2_ref[0, rs, :] = v

    hm = [_rms(v, gm_ref[...]) for v in x2]
    logits = [lax.dot_general(wr[...], h.astype(BF16), (((1,), (1,)), ((), ())), preferred_element_type=F32)
              + br[...] for h in hm]
    for rs, h in zip(parts, hm):
        words = _pack_bf16_pairs(h)
        for c, hp_ref in enumerate(hp_refs):
            hp_ref[0, rs, :] = words[:, c * SC_GATHER_WORDS:(c + 1) * SC_GATHER_WORDS]
    n_log = -(-(N_GROUPS + N_EXPERTS) // SUBLANES) * SUBLANES
    row_id = lax.broadcasted_iota(jnp.int32, (n_log, sub), 0)
    is_grp = row_id < N_GROUPS
    e_id = row_id - ROUTER_EXPERT_LANE0

    def route(lg):
        gl = jnp.where(is_grp, lg, NEG_INF)
        ge = jnp.exp(gl - jnp.max(gl, axis=0, keepdims=True))
        g_prob = ge / jnp.sum(ge, axis=0, keepdims=True)
        p_grp = jnp.max(g_prob, axis=0, keepdims=True)
        grp = jnp.min(jnp.where((g_prob == p_grp) & is_grp, row_id, LANES), axis=0, keepdims=True)
        in_grp = (e_id >= grp * EXPERTS_PER_GROUP) & (e_id < (grp + 1) * EXPERTS_PER_GROUP)
        el = jnp.where(in_grp, lg, NEG_INF)
        ee = jnp.exp(el - jnp.max(el, axis=0, keepdims=True))
        e_prob = jnp.where(in_grp, ee / jnp.sum(ee, axis=0, keepdims=True), -1.0)
        p0 = jnp.max(e_prob, axis=0, keepdims=True)
        e0 = jnp.min(jnp.where(e_prob == p0, e_id, LANES), axis=0, keepdims=True)
        e_rest = jnp.where(e_id == e0, -1.0, e_prob)
        p1 = jnp.max(e_rest, axis=0, keepdims=True)
        e1 = jnp.min(jnp.where(e_rest == p1, e_id, LANES), axis=0, keepdims=True)
        return e0, e1, p_grp * p0 / (p0 + p1), p_grp * p1 / (p0 + p1)

    routed = [route(lg[0:n_log, :]) for lg in logits]

    expert = lax.broadcasted_iota(jnp.int32, (LANES, sub), 0)
    before = (lax.broadcasted_iota(jnp.int32, (sub, sub), 0)
              < lax.broadcasted_iota(jnp.int32, (sub, sub), 1)).astype(BF16)
    field = lax.broadcasted_iota(jnp.int32, (LANES, sub), 0)
    base = base_ref[...]
    for rs, (e0, e1, w0, w1) in zip(parts, routed):
        onehot = ((expert == e0) | (expert == e1)).astype(F32)
        seen = base + jnp.dot(onehot.astype(BF16), before, preferred_element_type=F32)
        r0 = jnp.sum(jnp.where(expert == e0, seen, 0.0), axis=0, keepdims=True)
        r1 = jnp.sum(jnp.where(expert == e1, seen, 0.0), axis=0, keepdims=True)
        base = base + jnp.sum(onehot, axis=1, keepdims=True)
        info_t = jnp.zeros((LANES, sub), F32)
        for ln, val in ((INFO_W0, w0), (INFO_W1, w1), (INFO_E0, e0.astype(F32)), (INFO_E1, e1.astype(F32)),
                        (INFO_R0, r0), (INFO_R1, r1)):
            info_t = jnp.where(field == ln, val, info_t)
        info_ref[0, rs, :] = info_t.T
    base_ref[...] = base
    cnt_ref[...] = jnp.broadcast_to(base, (LANES, LANES)).T[0:1, :]


def _mix(x, o_g, o_m, z_g, z_m, mem_k, mem_v, g_cross, g_moe, w_proj_gla, w_proj_moba, w_out, w_cq, w_co,
         w_rg, b_rg, w_re, b_re):
    B, S, D = x.shape
    M = mem_k.shape[1]
    rows = min(MIX_ROWS, S)
    assert S % rows == 0
    pad = LANES - N_GROUPS - N_EXPERTS
    wr = jnp.pad(jnp.concatenate([w_rg, w_re], axis=1), ((0, 0), (0, pad))).astype(BF16).T
    br = jnp.pad(jnp.concatenate([b_rg, b_re]), (0, pad)).reshape(LANES, 1).astype(F32)
    weights = [w_proj_gla.astype(BF16), w_proj_moba.astype(BF16), w_out.astype(BF16), w_cq.astype(BF16),
               w_co.astype(BF16), wr, br]
    tile = lambda n: pl.BlockSpec((1, rows, n), lambda b, i: (b, i, 0))
    memspec = pl.BlockSpec((1, M, MEM_W), lambda b, i: (b, 0, 0))
    n_pieces = D // 2 // SC_GATHER_WORDS
    return pl.pallas_call(
        functools.partial(_mix_kernel, n_pieces),
        grid=(B, S // rows),
        in_specs=[tile(D), tile(GLA_V), tile(MOBA_W), tile(D), tile(D), memspec, memspec,
                  _full((1, D)), _full((1, D))] + [_full(w.shape) for w in weights],
        out_specs=[tile(D), tile(LANES), _full((1, LANES))] + [tile(SC_GATHER_WORDS)] * n_pieces,
        out_shape=[jax.ShapeDtypeStruct((B, S, D), F32), jax.ShapeDtypeStruct((B, S, LANES), F32),
                   jax.ShapeDtypeStruct((1, LANES), F32)]
        + [jax.ShapeDtypeStruct((B, S, SC_GATHER_WORDS), jnp.uint32)] * n_pieces,
        scratch_shapes=[pltpu.VMEM((LANES, 1), F32)],
        compiler_params=_params("arbitrary", "arbitrary"),
        name="merge_memattn_router",
    )(x, o_g, o_m, z_g.reshape(B, S, D), z_m.reshape(B, S, D), mem_k, mem_v,
      g_cross.reshape(1, D).astype(F32), g_moe.reshape(1, D).astype(F32), *weights)


def _plan_kernel(cnt_ref, info_ref, dest_ref, blk_ref):
    rows = info_ref.shape[0]
    lane1 = lax.broadcasted_iota(jnp.int32, (1, LANES), 1)
    nblk = jnp.floor((cnt_ref[...] + (EXPERT_ROWS - 1)) * (1.0 / EXPERT_ROWS))
    nblk = jnp.where(lane1 < N_EXPERTS, nblk, 0.0)
    hi = jnp.floor(nblk * (1.0 / 256.0))
    lo = nblk - 256.0 * hi
    upto = (lax.broadcasted_iota(jnp.int32, (LANES, LANES), 0)
            <= lax.broadcasted_iota(jnp.int32, (LANES, LANES), 1)).astype(BF16)
    digits = jnp.concatenate([jnp.broadcast_to(hi, (8, LANES)), jnp.broadcast_to(lo, (8, LANES))], axis=0)
    sums = jnp.dot(digits.astype(BF16), upto, preferred_element_type=F32)
    pend = sums[0:1] * 256.0 + sums[8:9]
    pstart_rows = (pend - nblk) * EXPERT_ROWS

    info = info_ref[...]
    lane = lax.broadcasted_iota(jnp.int32, (rows, LANES), 1)

    def field(ln):
        return jnp.sum(jnp.where(lane == ln, info, 0.0), axis=-1, keepdims=True)

    def dest(e, r):
        return jnp.sum(jnp.where(lane == e.astype(jnp.int32), pstart_rows, 0.0), axis=-1, keepdims=True) + r

    d0 = dest(field(INFO_E0), field(INFO_R0))
    d1 = dest(field(INFO_E1), field(INFO_R1))
    cols = jnp.where(lane == 0, d0, jnp.where(lane == 1, d1, 0.0))
    dest_ref[...] = cols.T[0:SUBLANES, :].astype(jnp.int32)

    @pl.when(pl.program_id(0) == 0)
    def _():
        n = lax.broadcasted_iota(jnp.int32, (blk_ref.shape[0], LANES), 0).astype(F32)
        blane = lax.broadcasted_iota(jnp.int32, (blk_ref.shape[0], LANES), 1)
        done = jnp.where((pend <= n) & (lane1 < N_EXPERTS), 1.0, 0.0)
        e = jnp.minimum(jnp.sum(done, axis=-1, keepdims=True), N_EXPERTS - 1.0)
        mine = blane == e.astype(jnp.int32)
        first_blk = jnp.sum(jnp.where(mine, pend - nblk, 0.0), axis=-1, keepdims=True)
        count = jnp.sum(jnp.where(mine, cnt_ref[...], 0.0), axis=-1, keepdims=True)
        valid = jnp.clip(count - EXPERT_ROWS * (n[:, 0:1] - first_blk), 0.0, float(EXPERT_ROWS))
        blk_ref[...] = jnp.where(blane == BLK_EXPERT, e, jnp.where(blane == BLK_VALID, valid, 0.0)).astype(jnp.int32)


def _plan(counts, info2d, n_blk):
    T = info2d.shape[0]
    rows = min(PLAN_ROWS, T)
    assert T % rows == 0
    n_blk_pad = -(-n_blk // SUBLANES) * SUBLANES
    return pl.pallas_call(
        _plan_kernel,
        grid=(T // rows,),
        in_specs=[_full((1, LANES)), pl.BlockSpec((rows, LANES), lambda i: (i, 0))],
        out_specs=[pl.BlockSpec((SUBLANES, rows), lambda i: (0, i)), _full((n_blk_pad, LANES))],
        out_shape=[jax.ShapeDtypeStruct((SUBLANES, T), jnp.int32),
                   jax.ShapeDtypeStruct((n_blk_pad, LANES), jnp.int32)],
        compiler_params=_params("arbitrary"),
        name="dispatch_plan",
    )(counts, info2d)


def _sc_windows(n_rows):
    n_inner = 32
    assert n_rows % (SC_GATHER_ROWS * n_inner) == 0
    return n_rows // (SC_GATHER_ROWS * n_inner), n_inner


def _sc_mesh():
    return plsc.VectorSubcoreMesh(core_axis_name="c", subcore_axis_name="s")


def _sc_scatter_rows(src, idx_a, idx_b, n_out):
    T, W = src.shape
    n_outer, n_inner = _sc_windows(T)
    win = lambda i, j: i * n_inner + j

    @pl.kernel(out_type=jax.ShapeDtypeStruct((n_out, W), src.dtype), mesh=_sc_mesh(), scratch_types=[])
    def scatter_kernel(s_hbm, a_hbm, b_hbm, o_hbm):
        def body(s_vmem, a_vmem, b_vmem):
            pltpu.sync_copy(s_vmem, o_hbm.at[a_vmem.at[0]])
            pltpu.sync_copy(s_vmem, o_hbm.at[b_vmem.at[0]])

        pltpu.emit_pipeline(
            body,
            grid=(n_outer, n_inner),
            in_specs=[pl.BlockSpec((SC_GATHER_ROWS, W), index_map=lambda i, j: (win(i, j), 0)),
                      pl.BlockSpec((1, SC_GATHER_ROWS), index_map=lambda i, j: (0, win(i, j))),
                      pl.BlockSpec((1, SC_GATHER_ROWS), index_map=lambda i, j: (0, win(i, j)))],
            out_specs=[],
            core_axis_name=("c", "s"),
            dimension_semantics=(pltpu.PARALLEL, pltpu.PARALLEL),
        )(s_hbm, a_hbm, b_hbm)

    return scatter_kernel(src, idx_a, idx_b)


def _sc_gather_rows(table, idx):
    M = idx.shape[1]
    W = table.shape[1]
    n_outer, n_inner = _sc_windows(M)
    win = lambda i, j: i * n_inner + j

    @pl.kernel(out_type=jax.ShapeDtypeStruct((M, W), table.dtype), mesh=_sc_mesh(), scratch_types=[])
    def gather_kernel(t_hbm, i_hbm, o_hbm):
        def body(i_vmem, o_vmem):
            pltpu.sync_copy(t_hbm.at[i_vmem.at[0]], o_vmem)

        pltpu.emit_pipeline(
            body,
            grid=(n_outer, n_inner),
            in_specs=[pl.BlockSpec((1, SC_GATHER_ROWS), index_map=lambda i, j: (0, win(i, j)))],
            out_specs=[pl.BlockSpec((SC_GATHER_ROWS, W), index_map=lambda i, j: (win(i, j), 0))],
            core_axis_name=("c", "s"),
            dimension_semantics=(pltpu.PARALLEL, pltpu.PARALLEL),
        )(i_hbm, o_hbm)

    return gather_kernel(table, idx)


def _expert_kernel(n_pieces, blk_e_ref, blk_valid_ref, *refs):
    xs_refs, (wg_ref, wu_ref, wd_ref) = refs[:n_pieces], refs[n_pieces:n_pieces + 3]
    y_refs, (wg_bf, wu_bf, wd_bf) = refs[n_pieces + 3:2 * n_pieces + 3], refs[2 * n_pieces + 3:]
    n = pl.program_id(0)
    prev = blk_e_ref[jnp.maximum(n - 1, 0)]

    @pl.when((n == 0) | (blk_e_ref[n] != prev))
    def _():
        wg_bf[...] = wg_ref[0].astype(BF16)
        wu_bf[...] = wu_ref[0].astype(BF16)
        wd_bf[...] = wd_ref[0].astype(BF16)

    valid = blk_valid_ref[n]

    @pl.when(valid == 0)
    def _():
        for y_ref in y_refs:
            y_ref[...] = jnp.zeros_like(y_ref)

    @pl.when(valid > 0)
    def _():
        sub = xs_refs[0].shape[0] // EXPERT_PARTS
        parts = [slice(p * sub, (p + 1) * sub) for p in range(EXPERT_PARTS)]
        row = lax.broadcasted_iota(jnp.int32, (sub, 1), 0)

        def load(p, rs):
            words = jnp.concatenate([r[rs, :] for r in xs_refs], axis=1)
            words = jnp.where(row + p * sub < valid, words, jnp.zeros_like(words))
            return _unpack_bf16_pairs(words).astype(BF16)

        xb = [load(p, rs) for p, rs in enumerate(parts)]
        gate = [jnp.dot(v, wg_bf[...], preferred_element_type=F32) for v in xb]
        up = [jnp.dot(v, wu_bf[...], preferred_element_type=F32) for v in xb]
        hid = [(g * jax.nn.sigmoid(g) * u).astype(BF16) for g, u in zip(gate, up)]
        for rs, hv in zip(parts, hid):
            words = _pack_bf16_pairs(jnp.dot(hv, wd_bf[...], preferred_element_type=F32))
            for c, y_ref in enumerate(y_refs):
                y_ref[rs, :] = words[:, c * SC_GATHER_WORDS:(c + 1) * SC_GATHER_WORDS]


def _experts(blk_e, blk_valid, xs_pieces, w_gate, w_up, w_down):
    cap = xs_pieces[0].shape[0]
    _, D, DE = w_gate.shape
    n_blk = cap // EXPERT_ROWS
    n_pieces = len(xs_pieces)
    piece = pl.BlockSpec((EXPERT_ROWS, SC_GATHER_WORDS), lambda n, e, v: (n, 0))
    return pl.pallas_call(
        functools.partial(_expert_kernel, n_pieces),
        grid_spec=pltpu.PrefetchScalarGridSpec(
            num_scalar_prefetch=2,
            grid=(n_blk,),
            in_specs=[piece] * n_pieces
            + [pl.BlockSpec((1, D, DE), lambda n, e, v: (e[n], 0, 0)),
               pl.BlockSpec((1, D, DE), lambda n, e, v: (e[n], 0, 0)),
               pl.BlockSpec((1, DE, D), lambda n, e, v: (e[n], 0, 0))],
            out_specs=[piece] * n_pieces,
            scratch_shapes=[pltpu.VMEM((D, DE), BF16), pltpu.VMEM((D, DE), BF16), pltpu.VMEM((DE, D), BF16)]),
        out_shape=[jax.ShapeDtypeStruct((cap, SC_GATHER_WORDS), jnp.uint32)] * n_pieces,
        compiler_params=_params("arbitrary"),
        name="moe_experts",
    )(blk_e, blk_valid, *xs_pieces, w_gate, w_up, w_down)


def _combine_dense_kernel(final_norm, n_pieces, *refs):
    yg_refs, (x_ref, info_ref, g_ref), o_ref = refs[:n_pieces], refs[n_pieces:n_pieces + 3], refs[-1]
    info = info_ref[...]
    w0 = info[:, INFO_W0:INFO_W0 + 1]
    w1 = info[:, INFO_W1:INFO_W1 + 1]
    y = [_unpack_bf16_pairs(jnp.concatenate([r[slot] for r in yg_refs], axis=1)) for slot in range(TOPK_IN_GROUP)]
    out = x_ref[...] + (w0 * y[0] + w1 * y[1])
    o_ref[...] = _rms(out, g_ref[...]) if final_norm else out


def _combine_dense(yg_pieces, x2d, info2d, g_final, final_norm, t0, out_so_far):
    T, D = x2d.shape
    Tc = yg_pieces[0].shape[1]
    rows = min(MOVE_ROWS, Tc)
    assert Tc % rows == 0 and t0 % rows == 0
    tile0 = t0 // rows
    n_pieces = len(yg_pieces)
    in_specs = ([pl.BlockSpec((TOPK_IN_GROUP, rows, SC_GATHER_WORDS), lambda i: (0, i, 0))] * n_pieces
                + [pl.BlockSpec((rows, D), lambda i: (i + tile0, 0)),
                   pl.BlockSpec((rows, LANES), lambda i: (i + tile0, 0)), _full((1, D))])
    args = [*yg_pieces, x2d, info2d, g_final.reshape(1, D).astype(F32)]
    aliases = {}
    if out_so_far is not None:
        in_specs.append(pl.BlockSpec(memory_space=pl.ANY))
        args.append(out_so_far)
        aliases = {len(args) - 1: 0}

    def body(*refs):
        _combine_dense_kernel(final_norm, n_pieces, *refs)

    return pl.pallas_call(
        body,
        grid=(Tc // rows,),
        in_specs=in_specs,
        out_specs=pl.BlockSpec((rows, D), lambda i: (i + tile0, 0)),
        out_shape=jax.ShapeDtypeStruct((T, D), F32),
        input_output_aliases=aliases,
        compiler_params=_params("arbitrary"),
        name="moe_combine_dense_final_norm",
    )(*args)


def kernel(x, mem, g_mem, rel_bias, g_mix, w_in, w_alpha_up, b_alpha, g_gla_head, w_proj_gla, w_proj_moba,
           w_out, g_cross, w_cq, w_ckv, w_co, g_moe, w_router_group, b_router_group, w_router_expert,
           b_router_expert, w_exp_gate, w_exp_up, w_exp_down, g_final):
    B, S, D = x.shape
    T = B * S
    depth = g_mix.shape[0]
    n_assign = T * TOPK_IN_GROUP
    n_blk = -(-(n_assign + N_EXPERTS * (EXPERT_ROWS - 1)) // EXPERT_ROWS)
    cap = n_blk * EXPERT_ROWS

    mem_bias = _moba_bias(rel_bias)
    for l in range(depth):
        qk, v_g, r_g, la, q_m, k_m, v_m, z_g, z_m = _project(x.reshape(T, D), g_mix[l], w_in[l], w_alpha_up[l],
                                                             b_alpha[l])
        o_g = _gla(qk, la, v_g, r_g, g_gla_head[l], B, S)
        o_m = _moba(q_m, k_m, v_m, mem_bias, rel_bias, B, S)
        mem_k, mem_v = _mem_kv(mem, g_mem, w_ckv[l])
        x2, info, counts, *hm_pieces = _mix(x, o_g, o_m, z_g, z_m, mem_k, mem_v, g_cross[l], g_moe[l],
                                            w_proj_gla[l], w_proj_moba[l], w_out[l], w_cq[l], w_co[l],
                                            w_router_group[l], b_router_group[l], w_router_expert[l],
                                            b_router_expert[l])
        x2d, info2d = x2.reshape(T, D), info.reshape(T, LANES)
        dest, blk = _plan(counts, info2d, n_blk)
        idx = dest[0:TOPK_IN_GROUP].reshape(1, TOPK_IN_GROUP * T)
        xs_pieces = [_sc_scatter_rows(h.reshape(T, SC_GATHER_WORDS), idx[:, :T], idx[:, T:], cap)
                     for h in hm_pieces]
        y_pieces = _experts(blk[:n_blk, BLK_EXPERT], blk[:n_blk, BLK_VALID], xs_pieces, w_exp_gate[l],
                            w_exp_up[l], w_exp_down[l])
        Tc = T // COMBINE_RANGES
        out = None
        for r in range(COMBINE_RANGES):
            idx_r = dest[0:TOPK_IN_GROUP, r * Tc:(r + 1) * Tc].reshape(1, TOPK_IN_GROUP * Tc)
            yg = [_sc_gather_rows(y, idx_r).reshape(TOPK_IN_GROUP, Tc, SC_GATHER_WORDS) for y in y_pieces]
            out = _combine_dense(yg, x2d, info2d, g_final, l == depth - 1, r * Tc, out)
        x = out.reshape(B, S, D)
    return x
```
